```python
import math
import jax, jax.numpy as jnp
from jax import lax
import numpy as np

D_MODEL = 1024
BATCH = 2
SEQ = 8192
DEPTH = 2

HEAD_DIM = 64
N_Q_HEADS = 8
N_KV_HEADS = 2
GQA_GROUP = N_Q_HEADS // N_KV_HEADS
ATTN_WIDTH = N_Q_HEADS * HEAD_DIM
KV_WIDTH = N_KV_HEADS * HEAD_DIM
QKV_COLS = ATTN_WIDTH + 2 * KV_WIDTH
CONV_WIDTH = D_MODEL - ATTN_WIDTH
IN_COLS = QKV_COLS + 3 * CONV_WIDTH
D_FF = 2752
BLOCK = 128
WINDOW = 128
N_BUCKETS = 32
MAX_DISTANCE = 128
GRID_W = 64
ROPE_THETA = 10000.0
SHORT_K = 3
HYENA_EMB = 33
HYENA_FILTER_WIDTH = 64
HYENA_MIN_DECAY = math.log(1e-2) / 0.3
HYENA_MAX_DECAY = math.log(1e-2) / 1.5
EPS = 1e-6
NEG_INF = -1e30
N_EVEN = (DEPTH + 1) // 2
N_ODD = DEPTH // 2

kernel_name = 'hybrid_swa_shortconv_axialattn_hyena_macaron'


def rmsnorm(x, g):
    xf = x.astype(jnp.float32)
    y = xf * lax.rsqrt(jnp.mean(xf * xf, axis=-1, keepdims=True) + EPS)
    return (y * g.astype(jnp.float32)).astype(x.dtype)


def adaln_in(x, g, m, i):
    return rmsnorm(x, g) * (1 + m[:, i, 1][:, None]) + m[:, i, 0][:, None]


def swiglu(h, w1, w3, w2):
    return (jax.nn.silu(h @ w1) * (h @ w3)) @ w2


def short_conv(u, w, b):
    up = jnp.pad(u, ((0, 0), (1, 1), (0, 0)))
    return up[:, :-2] * w[0] + up[:, 1:-1] * w[1] + up[:, 2:] * w[2] + b


def t5_bucket(rel):
    half = N_BUCKETS // 2
    max_exact = half // 2
    n = jnp.abs(rel)
    large = max_exact + (jnp.log(jnp.maximum(n, 1).astype(jnp.float32) / max_exact)
                         / math.log(MAX_DISTANCE / max_exact) * (half - max_exact)).astype(jnp.int32)
    large = jnp.minimum(large, half - 1)
    return jnp.where(rel > 0, half, 0) + jnp.where(n < max_exact, n, large)


def split_qkv(h, qk_g):
    bsz, S, _ = h.shape
    q = h[..., :ATTN_WIDTH].reshape(bsz, S, N_Q_HEADS, HEAD_DIM)
    k = h[..., ATTN_WIDTH:ATTN_WIDTH + KV_WIDTH].reshape(bsz, S, N_KV_HEADS, HEAD_DIM)
    v = h[..., ATTN_WIDTH + KV_WIDTH:QKV_COLS].reshape(bsz, S, N_KV_HEADS, HEAD_DIM)
    return rmsnorm(q, qk_g[0]), rmsnorm(k, qk_g[1]), v


def windowed_attention(q, k, v, sink, rel_table):
    bsz, S = q.shape[0], q.shape[1]
    nb = S // BLOCK
    qb = q.reshape(bsz, nb, BLOCK, N_KV_HEADS, GQA_GROUP, HEAD_DIM)
    pad = ((0, 0), (BLOCK, BLOCK), (0, 0), (0, 0))
    kp = jnp.pad(k, pad).reshape(bsz, nb + 2, BLOCK, N_KV_HEADS, HEAD_DIM)
    vp = jnp.pad(v, pad).reshape(bsz, nb + 2, BLOCK, N_KV_HEADS, HEAD_DIM)
    kb = jnp.concatenate([kp[:, :-2], kp[:, 1:-1], kp[:, 2:]], axis=2)
    vb = jnp.concatenate([vp[:, :-2], vp[:, 1:-1], vp[:, 2:]], axis=2)
    s = jnp.einsum('bnqhgd,bnkhd->bnhgqk', qb, kb).astype(jnp.float32) * HEAD_DIM ** -0.5
    koff = jnp.arange(3 * BLOCK) - BLOCK
    rel = koff[None, :] - jnp.arange(BLOCK)[:, None]
    bias = rel_table.astype(jnp.float32)[t5_bucket(rel)]
    bias = jnp.transpose(bias, (2, 0, 1)).reshape(N_KV_HEADS, GQA_GROUP, BLOCK, 3 * BLOCK)
    kpos = jnp.arange(nb)[:, None] * BLOCK + koff[None, :]
    valid = (jnp.abs(rel) <= WINDOW)[None] & ((kpos >= 0) & (kpos < S))[:, None, :]
    s = jnp.where(valid[None, :, None, None], s + bias, NEG_INF)
    sk = sink.astype(jnp.float32).reshape(N_KV_HEADS, GQA_GROUP)[:, :, None, None]
    mx = jnp.maximum(jnp.max(s, axis=-1, keepdims=True), sk)
    p = jnp.exp(s - mx)
    den = jnp.sum(p, axis=-1, keepdims=True) + jnp.exp(sk - mx)
    o = jnp.einsum('bnhgqk,bnkhd->bnqhgd', (p / den).astype(v.dtype), vb)
    return o.reshape(bsz, S, ATTN_WIDTH)


def axial_rope(x):
    S = x.shape[1]
    rows = S // GRID_W
    row, col = jnp.meshgrid(jnp.arange(rows), jnp.arange(GRID_W), indexing='ij')
    row = row.reshape(-1).astype(jnp.float32)
    col = col.reshape(-1).astype(jnp.float32)
    half = HEAD_DIM // 2
    inv = ROPE_THETA ** (-jnp.arange(0, half, 2, dtype=jnp.float32) / half)
    ang = jnp.concatenate([row[:, None] * inv, col[:, None] * inv], axis=-1)
    cos = jnp.cos(ang)[None, :, None, :]
    sin = jnp.sin(ang)[None, :, None, :]
    xf = x.astype(jnp.float32)
    x1, x2 = xf[..., 0::2], xf[..., 1::2]
    out = jnp.stack([x1 * cos - x2 * sin, x1 * sin + x2 * cos], axis=-1).reshape(x.shape)
    return out.astype(x.dtype)


def blocked_dense_attention(q, k, v):
    bsz, S = q.shape[0], q.shape[1]
    nb = S // BLOCK
    qb = jnp.moveaxis(q.reshape(bsz, nb, BLOCK, N_KV_HEADS, GQA_GROUP, HEAD_DIM), 1, 0)
    scale = HEAD_DIM ** -0.5

    def attend(qblk):
        s = jnp.einsum('bqhgd,bkhd->bhgqk', qblk, k).astype(jnp.float32) * scale
        p = jax.nn.softmax(s, axis=-1).astype(v.dtype)
        return jnp.einsum('bhgqk,bkhd->bqhgd', p, v)

    o = lax.map(attend, qb)
    return jnp.moveaxis(o, 0, 1).reshape(bsz, S, ATTN_WIDTH)


def hyena_filter(L, w1, b1, w2, b2, w3, b3, w4, freq):
    f32 = jnp.float32
    t = jnp.linspace(0.0, 1.0, L, dtype=f32)[:, None]
    bands = (HYENA_EMB - 1) // 2
    w = (2.0 * math.pi / L) * jnp.arange(L, dtype=f32)[:, None]
    fr = jnp.linspace(1e-4, bands - 1, bands, dtype=f32)[None, :]
    z = jnp.concatenate([t, jnp.cos(fr * w), -jnp.sin(fr * w)], axis=-1)
    fq = freq.astype(f32)
    h = jnp.sin(fq * (z @ w1.astype(f32) + b1.astype(f32)))
    h = jnp.sin(fq * (h @ w2.astype(f32) + b2.astype(f32)))
    h = jnp.sin(fq * (h @ w3.astype(f32) + b3.astype(f32)))
    h = (h @ w4.astype(f32)).reshape(L, 2, CONV_WIDTH)
    deltas = jnp.abs(jnp.linspace(HYENA_MIN_DECAY, HYENA_MAX_DECAY, CONV_WIDTH, dtype=f32))
    h = h * jnp.exp(-t[:, :, None] * deltas)
    kfull = jnp.concatenate([h[:, 0], jnp.zeros((1, CONV_WIDTH), f32), h[:0:-1, 1]], axis=0)
    return kfull * lax.rsqrt(jnp.sum(kfull * kfull, axis=0, keepdims=True) + EPS)


def fft_long_conv(u, kfull):
    L = u.shape[1]
    uf = jnp.fft.rfft(u.astype(jnp.float32), n=2 * L, axis=1)
    kf = jnp.fft.rfft(kfull, n=2 * L, axis=0)
    return jnp.fft.irfft(uf * kf[None], n=2 * L, axis=1)[:, :L].astype(u.dtype)


def mixer_ab(h, qk_g, sink, conv_w, conv_b, rel_table):
    q, k, v = split_qkv(h, qk_g)
    att = windowed_attention(q, k, v, sink, rel_table)
    gb, gc, u = jnp.split(h[..., QKV_COLS:], 3, axis=-1)
    conv = gb * short_conv(gc * u, conv_w, conv_b)
    return jnp.concatenate([att, conv], axis=-1)


def mixer_cd(h, qk_g, conv_w, conv_b, f_w1, f_b1, f_w2, f_b2, f_w3, f_b3, f_w4, f_freq, skip):
    q, k, v = split_qkv(h, qk_g)
    att = blocked_dense_attention(axial_rope(q), axial_rope(k), v)
    u = short_conv(h[..., QKV_COLS:], conv_w, conv_b)
    x0, x1, vh = jnp.split(u, 3, axis=-1)
    kfull = hyena_filter(h.shape[1], f_w1, f_b1, f_w2, f_b2, f_w3, f_b3, f_w4, f_freq)
    z = x1 * vh
    y = x0 * (fft_long_conv(z, kfull) + skip * z)
    return jnp.concatenate([att, y], axis=-1)


def setup_inputs(seed: int = 0) -> dict:
    key = jax.random.key(seed)
    ks = iter(jax.random.split(key, 32))

    def nrm(shape, s):
        return s * jax.random.normal(next(ks), shape, jnp.float32)

    D = D_MODEL
    FW = HYENA_FILTER_WIDTH
    return {
        'x': nrm((BATCH, SEQ, D), 1.0),
        'c': nrm((BATCH, D), 1.0),
        'rel_table': nrm((N_BUCKETS, N_Q_HEADS), 0.5),
        'norm_g': 1.0 + nrm((DEPTH, 3, D), 0.02),
        'w_mod': nrm((DEPTH, D, 9 * D), 0.5 * D ** -0.5),
        'b_mod': nrm((DEPTH, 9 * D), 0.02),
        'w_in': nrm((DEPTH, D, IN_COLS), D ** -0.5),
        'w_out': nrm((DEPTH, D, D), D ** -0.5),
        'ffn_w1': nrm((DEPTH, 2, D, D_FF), D ** -0.5),
        'ffn_w3': nrm((DEPTH, 2, D, D_FF), D ** -0.5),
        'ffn_w2': nrm((DEPTH, 2, D_FF, D), D_FF ** -0.5),
        'a_qk_g': 1.0 + nrm((N_EVEN, 2, HEAD_DIM), 0.02),
        'a_sink': nrm((N_EVEN, N_Q_HEADS), 0.5),
        'b_conv_w': nrm((N_EVEN, SHORT_K, CONV_WIDTH), SHORT_K ** -0.5),
        'b_conv_b': nrm((N_EVEN, CONV_WIDTH), 0.02),
        'c_qk_g': 1.0 + nrm((N_ODD, 2, HEAD_DIM), 0.02),
        'd_conv_w': nrm((N_ODD, SHORT_K, 3 * CONV_WIDTH), SHORT_K ** -0.5),
        'd_conv_b': nrm((N_ODD, 3 * CONV_WIDTH), 0.02),
        'd_f_w1': nrm((N_ODD, HYENA_EMB, FW), HYENA_EMB ** -0.5),
        'd_f_b1': nrm((N_ODD, FW), 0.02),
        'd_f_w2': nrm((N_ODD, FW, FW), FW ** -0.5),
        'd_f_b2': nrm((N_ODD, FW), 0.02),
        'd_f_w3': nrm((N_ODD, FW, FW), FW ** -0.5),
        'd_f_b3': nrm((N_ODD, FW), 0.02),
        'd_f_w4': nrm((N_ODD, FW, 2 * CONV_WIDTH), FW ** -0.5),
        'd_f_freq': 1.0 + nrm((N_ODD, FW), 0.02),
        'd_skip': nrm((N_ODD, CONV_WIDTH), 0.1),
    }


def reference(x, c, rel_table, norm_g, w_mod, b_mod, w_in, w_out, ffn_w1, ffn_w3, ffn_w2,
              a_qk_g, a_sink, b_conv_w, b_conv_b, c_qk_g, d_conv_w, d_conv_b,
              d_f_w1, d_f_b1, d_f_w2, d_f_b2, d_f_w3, d_f_b3, d_f_w4, d_f_freq, d_skip):
    bsz = x.shape[0]
    cond = jax.nn.silu(c)
    for l in range(DEPTH):
        m = (cond @ w_mod[l] + b_mod[l]).reshape(bsz, 3, 3, D_MODEL)
        x = x + 0.5 * m[:, 0, 2][:, None] * swiglu(adaln_in(x, norm_g[l, 0], m, 0),
                                                   ffn_w1[l, 0], ffn_w3[l, 0], ffn_w2[l, 0])
        h = adaln_in(x, norm_g[l, 1], m, 1) @ w_in[l]
        j = l // 2
        if l % 2 == 0:
            mix = mixer_ab(h, a_qk_g[j], a_sink[j], b_conv_w[j], b_conv_b[j], rel_table)
        else:
            mix = mixer_cd(h, c_qk_g[j], d_conv_w[j], d_conv_b[j], d_f_w1[j], d_f_b1[j],
                           d_f_w2[j], d_f_b2[j], d_f_w3[j], d_f_b3[j], d_f_w4[j], d_f_freq[j], d_skip[j])
        x = x + m[:, 1, 2][:, None] * (mix @ w_out[l])
        x = x + 0.5 * m[:, 2, 2][:, None] * swiglu(adaln_in(x, norm_g[l, 2], m, 2),
                                                   ffn_w1[l, 1], ffn_w3[l, 1], ffn_w2[l, 1])
    return x
```

```python
import functools
import math

import jax
import jax.numpy as jnp
import numpy as np
from jax import lax
from jax.experimental import pallas as pl
from jax.experimental.pallas import tpu as pltpu

D_MODEL = 1024
HEAD_DIM = 64
N_Q_HEADS = 8
N_KV_HEADS = 2
GQA_GROUP = N_Q_HEADS // N_KV_HEADS
ATTN_WIDTH = N_Q_HEADS * HEAD_DIM
KV_WIDTH = N_KV_HEADS * HEAD_DIM
QKV_COLS = ATTN_WIDTH + 2 * KV_WIDTH
CONV_WIDTH = D_MODEL - ATTN_WIDTH
IN_COLS = QKV_COLS + 3 * CONV_WIDTH
D_FF = 2752
BLOCK = 128
WINDOW = 128
N_BUCKETS = 32
MAX_DISTANCE = 128
GRID_W = 64
ROPE_THETA = 10000.0
HYENA_EMB = 33
HYENA_FILTER_WIDTH = 64
HYENA_EMB_PAD = 64
HYENA_MIN_DECAY = math.log(1e-2) / 0.3
HYENA_MAX_DECAY = math.log(1e-2) / 1.5
EPS = 1e-6
NEG_INF = -1e30

LANES = 128
MXU_DIM = 256
FF_PAD = 2816
FF_CHUNK = 256
VMEM_LIMIT = 56 * 1024 * 1024
DFT_N2 = 128

BF16 = jnp.bfloat16
F32 = jnp.float32


def _cparams(sem):
    return pltpu.CompilerParams(dimension_semantics=sem, vmem_limit_bytes=VMEM_LIMIT)


def _resident(shape):
    nd = len(shape)
    return pl.BlockSpec(shape, lambda *_: (0,) * nd, pipeline_mode=pl.Buffered(1))


def _adaln(x, g, scale, shift):
    y = x * lax.rsqrt(jnp.mean(x * x, axis=-1, keepdims=True) + EPS) * g
    return y * (1.0 + scale) + shift


def _mod_kernel(c_ref, w_ref, b_ref, o_ref):
    w = w_ref[0]
    for b in range(c_ref.shape[0]):
        cc = c_ref[b]
        cond = cc * jax.nn.sigmoid(cc)
        o_ref[0, b:b + 1, :] = jnp.sum(w * cond, axis=0, keepdims=True) + b_ref[0]


def _mod_call(c, w_mod, b_mod):
    depth, d, n = w_mod.shape
    bsz = c.shape[0]
    tn = 512
    return pl.pallas_call(
        _mod_kernel,
        out_shape=jax.ShapeDtypeStruct((depth, bsz, n), F32),
        grid=(depth, n // tn),
        in_specs=[
            pl.BlockSpec((bsz, d, 1), lambda l, j: (0, 0, 0)),
            pl.BlockSpec((1, d, tn), lambda l, j: (l, 0, j)),
            pl.BlockSpec((1, 1, tn), lambda l, j: (l, 0, j)),
        ],
        out_specs=pl.BlockSpec((1, bsz, tn), lambda l, j: (l, 0, j)),
        compiler_params=_cparams(("arbitrary", "arbitrary")),
        name="adaln_modulation",
    )(c.reshape(bsz, d, 1), w_mod, b_mod.reshape(depth, 1, n))


def _ffn_kernel(x_ref, g_ref, shift_ref, scale_ref, gate_ref, w1_ref, w3_ref, w2_ref, o_ref):
    x = x_ref[...]
    h = _adaln(x, g_ref[...], scale_ref[0], shift_ref[0]).astype(BF16)
    acc = jnp.zeros(x.shape, F32)
    for c in range(FF_PAD // FF_CHUNK):
        cols = slice(c * FF_CHUNK, (c + 1) * FF_CHUNK)
        a = jnp.dot(h, w1_ref[:, cols], preferred_element_type=F32)
        b = jnp.dot(h, w3_ref[:, cols], preferred_element_type=F32)
        act = (a * jax.nn.sigmoid(a) * b).astype(BF16)
        acc = acc + jnp.dot(act, w2_ref[cols, :], preferred_element_type=F32)
    o_ref[...] = x + (0.5 * gate_ref[0]) * acc


def _ffn_call(x, g, shift, scale, gate, w1, w3, w2, rows_per_batch):
    m, d = x.shape
    tm = 512
    tpb = rows_per_batch // tm
    vec = pl.BlockSpec((1, 1, d), lambda i: (i // tpb, 0, 0))
    return pl.pallas_call(
        _ffn_kernel,
        out_shape=jax.ShapeDtypeStruct((m, d), F32),
        grid=(m // tm,),
        in_specs=[
            pl.BlockSpec((tm, d), lambda i: (i, 0)),
            _resident((1, d)),
            vec, vec, vec,
            _resident((d, FF_PAD)), _resident((d, FF_PAD)), _resident((FF_PAD, d)),
        ],
        out_specs=pl.BlockSpec((tm, d), lambda i: (i, 0)),
        compiler_params=_cparams(("arbitrary",)),
        name="adaln_swiglu_ffn",
    )(x, g, shift, scale, gate, w1, w3, w2)


def _group_norm_scale(v, e_ref):
    sq = v * v
    hi = sq.astype(BF16)
    lo = (sq - hi.astype(F32)).astype(BF16)
    e = e_ref[0:v.shape[1], 0:v.shape[1]]
    ss = jnp.dot(hi, e, preferred_element_type=F32) + jnp.dot(lo, e, preferred_element_type=F32)
    return lax.rsqrt(ss * (1.0 / HEAD_DIM) + EPS)


def _rope128(v, cos, sin_signed, even_lane):
    partner = jnp.where(even_lane, pltpu.roll(v, LANES - 1, 1), pltpu.roll(v, 1, 1))
    return v * cos + partner * sin_signed


def _inproj_kernel(rope, x_ref, g_ref, shift_ref, scale_ref, w_ref, qg_ref, kg_ref, e_ref, *rest):
    if rope:
        cos_ref, sin_ref, q_ref, k_ref, v_ref, u_ref = rest
    else:
        q_ref, k_ref, v_ref, u_ref = rest
    h = _adaln(x_ref[...], g_ref[...], scale_ref[0], shift_ref[0]).astype(BF16)
    q = jnp.dot(h, w_ref[:, 0:ATTN_WIDTH], preferred_element_type=F32)
    k = jnp.dot(h, w_ref[:, ATTN_WIDTH:ATTN_WIDTH + KV_WIDTH], preferred_element_type=F32)
    q = q * _group_norm_scale(q, e_ref) * qg_ref[...]
    k = k * _group_norm_scale(k, e_ref) * kg_ref[...]
    if rope:
        cos = cos_ref[...]
        sin = sin_ref[...]
        even = (lax.broadcasted_iota(jnp.int32, cos.shape, 1) % 2) == 0
        q = jnp.concatenate(
            [_rope128(q[:, j * LANES:(j + 1) * LANES], cos, sin, even) for j in range(ATTN_WIDTH // LANES)],
            axis=1)
        k = _rope128(k, cos, sin, even)
    q_ref[...] = (q * HEAD_DIM ** -0.5).astype(BF16)
    k_ref[...] = k.astype(BF16)
    v_ref[...] = jnp.dot(h, w_ref[:, ATTN_WIDTH + KV_WIDTH:QKV_COLS], preferred_element_type=F32).astype(BF16)
    u_ref[...] = jnp.dot(h, w_ref[:, QKV_COLS:IN_COLS], preferred_element_type=F32)


def _inproj_call(x, g, shift, scale, w_in, qg, kg, e, rope_tabs, rows_per_batch):
    m, d = x.shape
    tm = 512
    tpb = rows_per_batch // tm
    vec = pl.BlockSpec((1, 1, d), lambda i: (i // tpb, 0, 0))
    in_specs = [
        pl.BlockSpec((tm, d), lambda i: (i, 0)),
        _resident((1, d)), vec, vec,
        _resident((d, IN_COLS)),
        _resident((1, ATTN_WIDTH)), _resident((1, KV_WIDTH)),
        _resident((ATTN_WIDTH, ATTN_WIDTH)),
    ]
    args = [x, g, shift, scale, w_in, qg, kg, e]
    if rope_tabs is not None:
        tab = pl.BlockSpec((tm, LANES), lambda i: (i % tpb, 0))
        in_specs += [tab, tab]
        args += list(rope_tabs)
    row = lambda w: pl.BlockSpec((tm, w), lambda i: (i, 0))
    return pl.pallas_call(
        functools.partial(_inproj_kernel, rope_tabs is not None),
        out_shape=(
            jax.ShapeDtypeStruct((m, ATTN_WIDTH), BF16),
            jax.ShapeDtypeStruct((m, KV_WIDTH), BF16),
            jax.ShapeDtypeStruct((m, KV_WIDTH), BF16),
            jax.ShapeDtypeStruct((m, 3 * CONV_WIDTH), F32),
        ),
        grid=(m // tm,),
        in_specs=in_specs,
        out_specs=(row(ATTN_WIDTH), row(KV_WIDTH), row(KV_WIDTH), row(3 * CONV_WIDTH)),
        compiler_params=_cparams(("arbitrary",)),
        name="adaln_in_projection",
    )(*args)


_T5_STEPS = (12, 16, 23, 32, 46, 64, 91)


def _bias_kernel(tab_ref, o_ref):
    qi = lax.broadcasted_iota(jnp.int32, (BLOCK, 3 * BLOCK), 0)
    kj = lax.broadcasted_iota(jnp.int32, (BLOCK, 3 * BLOCK), 1)
    rel = kj - BLOCK - qi
    n = jnp.abs(rel)
    half = N_BUCKETS // 2
    max_exact = half // 2
    large = jnp.full(n.shape, max_exact, jnp.int32)
    for t in _T5_STEPS:
        large = large + (n >= t).astype(jnp.int32)
    bucket = jnp.where(rel > 0, half, 0) + jnp.where(n < max_exact, n, large)
    for h in range(N_Q_HEADS):
        bias = jnp.zeros(n.shape, F32)
        for b in range(N_BUCKETS):
            bias = jnp.where(bucket == b, tab_ref[b, h], bias)
        o_ref[h] = jnp.where(n <= WINDOW, bias, NEG_INF)


def _bias_call(rel_table):
    return pl.pallas_call(
        _bias_kernel,
        out_shape=jax.ShapeDtypeStruct((N_Q_HEADS, BLOCK, 3 * BLOCK), F32),
        in_specs=[pl.BlockSpec(memory_space=pltpu.SMEM)],
        name="t5_bias_tile",
    )(rel_table)


def _mixer_ab_kernel(tq, tpb, x_ref, q_ref, kc_ref, kp_ref, kn_ref, vc_ref, vp_ref, vn_ref,
                     uc_ref, up_ref, un_ref, bias_ref, sink_ref, cw_ref, cb_ref, wo_ref, gate_ref,
                     o_ref, kbuf, vbuf, att):
    i = pl.program_id(0)
    first = (i % tpb) == 0
    last = (i % tpb) == tpb - 1
    nblk = tq // BLOCK
    kbuf[0:BLOCK] = kp_ref[...]
    kbuf[BLOCK:BLOCK + tq] = kc_ref[...]
    kbuf[BLOCK + tq:] = kn_ref[...]
    vbuf[0:BLOCK] = vp_ref[...]
    vbuf[BLOCK:BLOCK + tq] = vc_ref[...]
    vbuf[BLOCK + tq:] = vn_ref[...]
    col = lax.broadcasted_iota(jnp.int32, (BLOCK, 3 * BLOCK), 1)
    for n in range(nblk):
        rows = slice(n * BLOCK, (n + 1) * BLOCK)
        krows = slice(n * BLOCK, (n + 3) * BLOCK)
        for h in range(N_Q_HEADS):
            g = h // GQA_GROUP
            kv_cols = slice(g * HEAD_DIM, (g + 1) * HEAD_DIM)
            s = lax.dot_general(q_ref[rows, h * HEAD_DIM:(h + 1) * HEAD_DIM], kbuf[krows, kv_cols],
                                (((1,), (1,)), ((), ())), preferred_element_type=F32)
            s = s + bias_ref[h]
            if n == 0:
                s = jnp.where(jnp.logical_and(first, col < BLOCK), NEG_INF, s)
            if n == nblk - 1:
                s = jnp.where(jnp.logical_and(last, col >= 2 * BLOCK), NEG_INF, s)
            sk = sink_ref[h]
            mx = jnp.maximum(jnp.max(s, axis=-1, keepdims=True), sk)
            p = jnp.exp(s - mx)
            den = jnp.sum(p, axis=-1, keepdims=True) + jnp.exp(sk - mx)
            o = jnp.dot(p.astype(BF16), vbuf[krows, kv_cols], preferred_element_type=F32)
            att[rows, h * HEAD_DIM:(h + 1) * HEAD_DIM] = o / den

    cw = cw_ref[...]
    gb = uc_ref[:, 0:CONV_WIDTH]
    p = uc_ref[:, CONV_WIDTH:2 * CONV_WIDTH] * uc_ref[:, 2 * CONV_WIDTH:]
    p_before = jnp.where(first, 0.0, up_ref[7:8, CONV_WIDTH:2 * CONV_WIDTH] * up_ref[7:8, 2 * CONV_WIDTH:])
    p_after = jnp.where(last, 0.0, un_ref[0:1, CONV_WIDTH:2 * CONV_WIDTH] * un_ref[0:1, 2 * CONV_WIDTH:])
    r = lax.broadcasted_iota(jnp.int32, p.shape, 0)
    pm1 = jnp.where(r == 0, p_before, pltpu.roll(p, 1, 0))
    pp1 = jnp.where(r == tq - 1, p_after, pltpu.roll(p, tq - 1, 0))
    conv = gb * (cw[0:1] * pm1 + cw[1:2] * p + cw[2:3] * pp1 + cb_ref[...])

    y = jnp.dot(att[...].astype(BF16), wo_ref[0:ATTN_WIDTH, :], preferred_element_type=F32)
    y = y + jnp.dot(conv.astype(BF16), wo_ref[ATTN_WIDTH:, :], preferred_element_type=F32)
    o_ref[...] = x_ref[...] + gate_ref[0] * y


def _mixer_ab_call(x, q, k, v, u, bias, sink, conv_w, conv_b, w_out, gate, rows_per_batch):
    m, d = x.shape
    tq = 512
    tpb = rows_per_batch // tq
    r = tq // BLOCK
    nb = m // BLOCK
    n8 = m // 8
    cur = lambda w: pl.BlockSpec((tq, w), lambda i: (i, 0))
    prev_blk = pl.BlockSpec((BLOCK, KV_WIDTH), lambda i: (jnp.maximum(i * r - 1, 0), 0))
    next_blk = pl.BlockSpec((BLOCK, KV_WIDTH), lambda i: (jnp.minimum((i + 1) * r, nb - 1), 0))
    uw = 3 * CONV_WIDTH
    return pl.pallas_call(
        functools.partial(_mixer_ab_kernel, tq, tpb),
        out_shape=jax.ShapeDtypeStruct((m, d), F32),
        grid=(m // tq,),
        in_specs=[
            cur(d), cur(ATTN_WIDTH),
            cur(KV_WIDTH), prev_blk, next_blk,
            cur(KV_WIDTH), prev_blk, next_blk,
            cur(uw),
            pl.BlockSpec((8, uw), lambda i: (jnp.maximum(i * (tq // 8) - 1, 0), 0)),
            pl.BlockSpec((8, uw), lambda i: (jnp.minimum((i + 1) * (tq // 8), n8 - 1), 0)),
            _resident((N_Q_HEADS, BLOCK, 3 * BLOCK)),
            pl.BlockSpec(memory_space=pltpu.SMEM),
            _resident((3, CONV_WIDTH)), _resident((1, CONV_WIDTH)),
            _resident((d, d)),
            pl.BlockSpec((1, 1, d), lambda i: (i // tpb, 0, 0)),
        ],
        out_specs=cur(d),
        scratch_shapes=[
            pltpu.VMEM((tq + 2 * BLOCK, KV_WIDTH), BF16),
            pltpu.VMEM((tq + 2 * BLOCK, KV_WIDTH), BF16),
            pltpu.VMEM((tq, ATTN_WIDTH), F32),
        ],
        compiler_params=_cparams(("arbitrary",)),
        name="windowed_attn_shortconv_outproj",
    )(x, q, k, k, k, v, v, v, u, u, u, bias, sink, conv_w, conv_b, w_out, gate)


def _dense_attn_kernel(tq, tk, q_ref, k_ref, v_ref, o_ref, qs):
    seq = k_ref.shape[0]
    for h in range(N_Q_HEADS):
        g, j = divmod(h, GQA_GROUP)
        qs[g, j * tq:(j + 1) * tq, :] = q_ref[:, h * HEAD_DIM:(h + 1) * HEAD_DIM]
    rows = GQA_GROUP * tq
    for g in range(N_KV_HEADS):
        kv_cols = slice(g * HEAD_DIM, (g + 1) * HEAD_DIM)
        qg = qs[g]

        def body(c, carry):
            m, l, acc = carry
            start = pl.multiple_of(c * tk, tk)
            kc = k_ref[pl.ds(start, tk), kv_cols]
            vc = v_ref[pl.ds(start, tk), kv_cols]
            s = lax.dot_general(qg, kc, (((1,), (1,)), ((), ())), preferred_element_type=F32)
            m_new = jnp.maximum(m, jnp.max(s, axis=-1, keepdims=True))
            alpha = jnp.exp(m - m_new)
            p = jnp.exp(s - m_new)
            l = alpha * l + jnp.sum(p, axis=-1, keepdims=True)
            acc = alpha * acc + jnp.dot(p.astype(BF16), vc, preferred_element_type=F32)
            return m_new, l, acc

        init = (jnp.full((rows, 1), -jnp.inf, F32), jnp.zeros((rows, 1), F32),
                jnp.zeros((rows, HEAD_DIM), F32))
        _, l, acc = lax.fori_loop(0, seq // tk, body, init)
        o = acc / l
        for j in range(GQA_GROUP):
            h = g * GQA_GROUP + j
            o_ref[:, h * HEAD_DIM:(h + 1) * HEAD_DIM] = o[j * tq:(j + 1) * tq].astype(o_ref.dtype)


def _dense_attn_call(q, k, v, bsz, seq):
    tq, tk = 256, 512
    nq = seq // tq
    return pl.pallas_call(
        functools.partial(_dense_attn_kernel, tq, tk),
        out_shape=jax.ShapeDtypeStruct((bsz * seq, ATTN_WIDTH), BF16),
        grid=(bsz, nq),
        in_specs=[
            pl.BlockSpec((tq, ATTN_WIDTH), lambda b, i: (b * nq + i, 0)),
            pl.BlockSpec((seq, KV_WIDTH), lambda b, i: (b, 0)),
            pl.BlockSpec((seq, KV_WIDTH), lambda b, i: (b, 0)),
        ],
        out_specs=pl.BlockSpec((tq, ATTN_WIDTH), lambda b, i: (b * nq + i, 0)),
        scratch_shapes=[pltpu.VMEM((N_KV_HEADS, GQA_GROUP * tq, HEAD_DIM), BF16)],
        compiler_params=_cparams(("arbitrary", "arbitrary")),
        name="dense_gqa_attention",
    )(q, k, v)


def _hyena_pre_kernel(tm, tpb, uc_ref, up_ref, un_ref, cw_ref, cb_ref, x0_ref, z_ref):
    i = pl.program_id(0)
    first = (i % tpb) == 0
    last = (i % tpb) == tpb - 1
    u = uc_ref[...]
    before = jnp.where(first, 0.0, up_ref[7:8, :])
    after = jnp.where(last, 0.0, un_ref[0:1, :])
    r = lax.broadcasted_iota(jnp.int32, u.shape, 0)
    um1 = jnp.where(r == 0, before, pltpu.roll(u, 1, 0))
    up1 = jnp.where(r == tm - 1, after, pltpu.roll(u, tm - 1, 0))
    cw = cw_ref[...]
    t = cw[0:1] * um1 + cw[1:2] * u + cw[2:3] * up1 + cb_ref[...]
    x0_ref[...] = t[:, 0:CONV_WIDTH]
    z_ref[...] = t[:, CONV_WIDTH:2 * CONV_WIDTH] * t[:, 2 * CONV_WIDTH:]


def _hyena_pre_call(u, conv_w, conv_b, rows_per_batch):
    m, uw = u.shape
    tm = 512
    tpb = rows_per_batch // tm
    n8 = m // 8
    out = pl.BlockSpec((tm, CONV_WIDTH), lambda i: (i, 0))
    return pl.pallas_call(
        functools.partial(_hyena_pre_kernel, tm, tpb),
        out_shape=(jax.ShapeDtypeStruct((m, CONV_WIDTH), F32), jax.ShapeDtypeStruct((m, CONV_WIDTH), F32)),
        grid=(m // tm,),
        in_specs=[
            pl.BlockSpec((tm, uw), lambda i: (i, 0)),
            pl.BlockSpec((8, uw), lambda i: (jnp.maximum(i * (tm // 8) - 1, 0), 0)),
            pl.BlockSpec((8, uw), lambda i: (jnp.minimum((i + 1) * (tm // 8), n8 - 1), 0)),
            _resident((3, uw)), _resident((1, uw)),
        ],
        out_specs=(out, out),
        compiler_params=_cparams(("arbitrary",)),
        name="hyena_short_conv",
    )(u, u, u, conv_w, conv_b)


def _hp_dot(a, b):
    return jnp.dot(a, b, preferred_element_type=F32, precision=lax.Precision.HIGHEST)


def _filter_trunk_kernel(z_ref, w1_ref, b1_ref, w2_ref, b2_ref, w3_ref, b3_ref, fq_ref, o_ref):
    fq = fq_ref[...]
    h = jnp.sin(fq * (_hp_dot(z_ref[...], w1_ref[...]) + b1_ref[...]))
    h = jnp.sin(fq * (_hp_dot(h, w2_ref[...]) + b2_ref[...]))
    o_ref[...] = jnp.sin(fq * (_hp_dot(h, w3_ref[...]) + b3_ref[...]))


def _filter_trunk_call(zfeat, w1, b1, w2, b2, w3, b3, freq):
    rows = zfeat.shape[0]
    tr = min(rows, 2048)
    fw = HYENA_FILTER_WIDTH
    return pl.pallas_call(
        _filter_trunk_kernel,
        out_shape=jax.ShapeDtypeStruct((rows, fw), F32),
        grid=(rows // tr,),
        in_specs=[
            pl.BlockSpec((tr, HYENA_EMB_PAD), lambda i: (i, 0)),
            _resident((HYENA_EMB_PAD, fw)), _resident((1, fw)),
            _resident((fw, fw)), _resident((1, fw)),
            _resident((fw, fw)), _resident((1, fw)),
            _resident((1, fw)),
        ],
        out_specs=pl.BlockSpec((tr, fw), lambda i: (i, 0)),
        compiler_params=_cparams(("arbitrary",)),
        name="hyena_filter_trunk",
    )(zfeat, w1, b1, w2, b2, w3, b3, freq)


def _dft_stage1(src_ref, tab_ref, re_ref, im_ref, n_rows, k1p):
    def body(n2, carry):
        xs = src_ref[pl.ds(n2, n_rows, stride=DFT_N2), :].astype(BF16)
        res = jnp.dot(tab_ref[n2], xs, preferred_element_type=F32)
        re_ref[pl.ds(n2, k1p, stride=DFT_N2), :] = res[0:k1p]
        im_ref[pl.ds(n2, k1p, stride=DFT_N2), :] = res[k1p:]
        return carry

    lax.fori_loop(0, DFT_N2, body, 0)


def _filter_spec_kernel(seq, k1p, h_ref, w4f_ref, w4b_ref, dl_ref, e1_ref, f2_ref,
                        kr_ref, ki_ref, kfull):
    n = 2 * seq
    chunk = min(seq, 1024)
    dl = dl_ref[...]

    def fill(c, ss):
        r0 = pl.multiple_of(c * chunk, chunk)
        hc = h_ref[pl.ds(r0, chunk), :].astype(BF16)
        fwd = jnp.dot(hc, w4f_ref[...], preferred_element_type=F32)
        bwd = jnp.dot(hc, w4b_ref[...], preferred_element_type=F32)
        rows = r0 + lax.broadcasted_iota(jnp.int32, (chunk, 1), 0)
        lag = jnp.where(rows < seq, rows, n - rows).astype(F32)
        t = lag * (1.0 / (seq - 1))
        val = jnp.where(rows < seq, fwd, bwd) * jnp.exp(-t * dl)
        val = jnp.where(rows == seq, 0.0, val)
        kfull[pl.ds(r0, chunk), :] = val
        return ss + jnp.sum(val * val, axis=0, keepdims=True)

    ss = lax.fori_loop(0, n // chunk, fill, jnp.zeros((1, dl.shape[1]), F32))
    norm = lax.rsqrt(ss + EPS)

    _dft_stage1(kfull, e1_ref, kr_ref, ki_ref, n // DFT_N2, k1p)

    f2 = f2_ref[...]

    def stage2(a, carry):
        rows = pl.ds(pl.multiple_of(a * DFT_N2, DFT_N2), DFT_N2)
        slab = jnp.concatenate([kr_ref[rows, :], ki_ref[rows, :]], axis=0).astype(BF16)
        xf = jnp.dot(f2, slab, preferred_element_type=F32)
        kr_ref[rows, :] = xf[0:DFT_N2] * norm
        ki_ref[rows, :] = xf[DFT_N2:] * norm
        return carry

    lax.fori_loop(0, k1p, stage2, 0)


def _filter_spec_call(h3, w4, deltas, e1f, f2f, seq):
    n = 2 * seq
    k1p = e1f.shape[1] // 2
    ct = LANES
    nct = CONV_WIDTH // ct
    fw = HYENA_FILTER_WIDTH
    spec_rows = k1p * DFT_N2
    out = pl.BlockSpec((spec_rows, ct), lambda j: (0, j))
    return pl.pallas_call(
        functools.partial(_filter_spec_kernel, seq, k1p),
        out_shape=(jax.ShapeDtypeStruct((spec_rows, CONV_WIDTH), F32),
                   jax.ShapeDtypeStruct((spec_rows, CONV_WIDTH), F32)),
        grid=(nct,),
        in_specs=[
            _resident((n, fw)),
            pl.BlockSpec((fw, ct), lambda j: (0, j)),
            pl.BlockSpec((fw, ct), lambda j: (0, nct + j)),
            pl.BlockSpec((1, ct), lambda j: (0, j)),
            _resident(e1f.shape),
            _resident((2 * DFT_N2, 2 * DFT_N2)),
        ],
        out_specs=(out, out),
        scratch_shapes=[pltpu.VMEM((n, ct), F32)],
        compiler_params=_cparams(("arbitrary",)),
        name="hyena_filter_spectrum",
    )(h3, w4, w4, deltas, e1f, f2f)


def _long_conv_kernel(seq, k1p, z_ref, x0_ref, kr_ref, ki_ref, skip_ref, e1_ref, f2f_ref, f2i_ref,
                      einv_ref, o_ref, br, bi):
    n1h = seq // DFT_N2
    _dft_stage1(z_ref, e1_ref, br, bi, n1h, k1p)

    f2f = f2f_ref[...]
    f2i = f2i_ref[...]

    def spectral(a, carry):
        r0 = pl.multiple_of(a * DFT_N2, DFT_N2)
        rows = pl.ds(r0, DFT_N2)
        slab = jnp.concatenate([br[rows, :], bi[rows, :]], axis=0).astype(BF16)
        xf = jnp.dot(f2f, slab, preferred_element_type=F32)
        xr, xi = xf[0:DFT_N2], xf[DFT_N2:]
        kr, ki = kr_ref[rows, :], ki_ref[rows, :]
        y = jnp.concatenate([xr * kr - xi * ki, xr * ki + xi * kr], axis=0).astype(BF16)
        d = jnp.dot(f2i, y, preferred_element_type=F32)
        br[rows, :] = d[0:DFT_N2]
        bi[rows, :] = d[DFT_N2:]
        return carry

    lax.fori_loop(0, k1p, spectral, 0)

    skip = skip_ref[...]

    def synth(n2, carry):
        ds_spec = pl.ds(n2, k1p, stride=DFT_N2)
        d = jnp.concatenate([br[ds_spec, :], bi[ds_spec, :]], axis=0).astype(BF16)
        conv = jnp.dot(einv_ref[n2], d, preferred_element_type=F32)
        ds_seq = pl.ds(n2, n1h, stride=DFT_N2)
        o_ref[ds_seq, :] = x0_ref[ds_seq, :] * (conv + skip * z_ref[ds_seq, :])
        return carry

    lax.fori_loop(0, DFT_N2, synth, 0)


def _long_conv_call(z, x0, kr, ki, skip, e1, f2f, f2i, einv, bsz, seq):
    ct = LANES
    nct = CONV_WIDTH // ct
    k1p = e1.shape[1] // 2
    spec_rows = k1p * DFT_N2
    big = lambda: pl.BlockSpec((seq, ct), lambda j, b: (b, j), pipeline_mode=pl.Buffered(1))
    spec = lambda: pl.BlockSpec((spec_rows, ct), lambda j, b: (0, j), pipeline_mode=pl.Buffered(1))
    return pl.pallas_call(
        functools.partial(_long_conv_kernel, seq, k1p),
        out_shape=jax.ShapeDtypeStruct((bsz * seq, CONV_WIDTH), F32),
        grid=(nct, bsz),
        in_specs=[
            big(), big(), spec(), spec(),
            pl.BlockSpec((1, ct), lambda j, b: (0, j)),
            _resident(e1.shape),
            _resident((2 * DFT_N2, 2 * DFT_N2)), _resident((2 * DFT_N2, 2 * DFT_N2)),
            _resident(einv.shape),
        ],
        out_specs=pl.BlockSpec((seq, ct), lambda j, b: (b, j)),
        scratch_shapes=[pltpu.VMEM((spec_rows, ct), F32), pltpu.VMEM((spec_rows, ct), F32)],
        compiler_params=_cparams(("arbitrary", "arbitrary")),
        name="hyena_long_conv",
    )(z, x0, kr, ki, skip, e1, f2f, f2i, einv)


def _outproj_kernel(x_ref, att_ref, y_ref, wo_ref, gate_ref, o_ref):
    y = jnp.dot(att_ref[...], wo_ref[0:ATTN_WIDTH, :], preferred_element_type=F32)
    y = y + jnp.dot(y_ref[...].astype(BF16), wo_ref[ATTN_WIDTH:, :], preferred_element_type=F32)
    o_ref[...] = x_ref[...] + gate_ref[0] * y


def _outproj_call(x, att, y, w_out, gate, rows_per_batch):
    m, d = x.shape
    tm = 512
    tpb = rows_per_batch // tm
    row = lambda w: pl.BlockSpec((tm, w), lambda i: (i, 0))
    return pl.pallas_call(
        _outproj_kernel,
        out_shape=jax.ShapeDtypeStruct((m, d), F32),
        grid=(m // tm,),
        in_specs=[row(d), row(ATTN_WIDTH), row(CONV_WIDTH), _resident((d, d)),
                  pl.BlockSpec((1, 1, d), lambda i: (i // tpb, 0, 0))],
        out_specs=row(d),
        compiler_params=_cparams(("arbitrary",)),
        name="mixer_out_projection",
    )(x, att, y, w_out, gate)


def _rope_tables(seq):
    t = jnp.arange(seq)
    row = (t // GRID_W).astype(F32)
    col = (t % GRID_W).astype(F32)
    half = HEAD_DIM // 2
    inv = ROPE_THETA ** (-jnp.arange(0, half, 2, dtype=F32) / half)
    ang = jnp.concatenate([row[:, None] * inv, col[:, None] * inv], axis=-1)
    ang = jnp.repeat(ang, 2, axis=-1)
    sign = jnp.where(jnp.arange(HEAD_DIM) % 2 == 0, -1.0, 1.0).astype(F32)
    reps = LANES // HEAD_DIM
    return jnp.tile(jnp.cos(ang), (1, reps)), jnp.tile(jnp.sin(ang) * sign, (1, reps))


def _dft_tables(seq):
    n = 2 * seq
    n1 = n // DFT_N2
    k1 = n1 // 2 + 1
    k1p = -(-k1 // 16) * 16
    a = jnp.arange(k1p)
    live = (a < k1)
    pos = jnp.arange(n)
    theta = ((a[:, None] * pos[None, :]) % n).astype(F32) * (2.0 * math.pi / n)
    cos = jnp.where(live[:, None], jnp.cos(theta), 0.0)
    sin = jnp.where(live[:, None], jnp.sin(theta), 0.0)
    ana = jnp.concatenate([cos, -sin], axis=0).reshape(2 * k1p, n1, DFT_N2)
    e1_full = jnp.transpose(ana, (2, 0, 1)).astype(BF16)
    e1_half = e1_full[:, :, :n1 // 2]
    wgt = jnp.where((a == 0) | (a == n1 // 2), 1.0, 2.0) / n
    syn = jnp.concatenate([cos * wgt[:, None], -sin * wgt[:, None]], axis=0)
    syn = syn[:, :seq].reshape(2 * k1p, n1 // 2, DFT_N2)
    einv = jnp.transpose(syn, (2, 1, 0)).astype(BF16)
    kk = jnp.arange(DFT_N2)
    phi = ((kk[:, None] * kk[None, :]) % DFT_N2).astype(F32) * (2.0 * math.pi / DFT_N2)
    c2, s2 = jnp.cos(phi), jnp.sin(phi)
    f2f = jnp.block([[c2, s2], [-s2, c2]]).astype(BF16)
    f2i = jnp.block([[c2, -s2], [s2, c2]]).astype(BF16)
    return e1_full, e1_half, einv, f2f, f2i


def _filter_features(seq):
    t = jnp.linspace(0.0, 1.0, seq, dtype=F32)[:, None]
    bands = (HYENA_EMB - 1) // 2
    w = (2.0 * math.pi / seq) * jnp.arange(seq, dtype=F32)[:, None]
    fr = jnp.linspace(1e-4, bands - 1, bands, dtype=F32)[None, :]
    z = jnp.concatenate([t, jnp.cos(fr * w), -jnp.sin(fr * w)], axis=-1)
    back = jnp.concatenate([z[:1], z[:0:-1]], axis=0)
    feats = jnp.concatenate([z, back], axis=0)
    return jnp.pad(feats, ((0, 0), (0, HYENA_EMB_PAD - HYENA_EMB)))


def kernel(x, c, rel_table, norm_g, w_mod, b_mod, w_in, w_out, ffn_w1, ffn_w3, ffn_w2, a_qk_g, a_sink,
           b_conv_w, b_conv_b, c_qk_g, d_conv_w, d_conv_b, d_f_w1, d_f_b1, d_f_w2, d_f_b2, d_f_w3,
           d_f_b3, d_f_w4, d_f_freq, d_skip):
    bsz, seq, d = x.shape
    depth = w_mod.shape[0]
    m_rows = bsz * seq
    pad_f = FF_PAD - D_FF

    mod = _mod_call(c, w_mod, b_mod).reshape(depth, bsz, 3, 3, 1, d)
    e_blk = jnp.kron(jnp.eye(N_Q_HEADS, dtype=F32), jnp.ones((HEAD_DIM, HEAD_DIM), F32)).astype(BF16)

    def ffn(xf, l, which, sub):
        w1 = jnp.pad(ffn_w1[l, which].astype(BF16), ((0, 0), (0, pad_f)))
        w3 = jnp.pad(ffn_w3[l, which].astype(BF16), ((0, 0), (0, pad_f)))
        w2 = jnp.pad(ffn_w2[l, which].astype(BF16), ((0, pad_f), (0, 0)))
        return _ffn_call(xf, norm_g[l, sub][None], mod[l, :, sub, 0], mod[l, :, sub, 1], mod[l, :, sub, 2],
                         w1, w3, w2, seq)

    xf = x.reshape(m_rows, d)
    for l in range(depth):
        j = l // 2
        xf = ffn(xf, l, 0, 0)
        even = l % 2 == 0
        qk_g = a_qk_g[j] if even else c_qk_g[j]
        q, k, v, u = _inproj_call(
            xf, norm_g[l, 1][None], mod[l, :, 1, 0], mod[l, :, 1, 1], w_in[l].astype(BF16),
            jnp.tile(qk_g[0], N_Q_HEADS)[None], jnp.tile(qk_g[1], N_KV_HEADS)[None], e_blk,
            None if even else _rope_tables(seq), seq)
        wo = w_out[l].astype(BF16)
        gate = mod[l, :, 1, 2]
        if even:
            bias = _bias_call(rel_table)
            xf = _mixer_ab_call(xf, q, k, v, u, bias, a_sink[j], b_conv_w[j], b_conv_b[j][None], wo, gate, seq)
        else:
            att = _dense_attn_call(q, k, v, bsz, seq)
            x0, z = _hyena_pre_call(u, d_conv_w[j], d_conv_b[j][None], seq)
            e1_full, e1_half, einv, f2f, f2i = _dft_tables(seq)
            zfeat = _filter_features(seq)
            fw1 = jnp.pad(d_f_w1[j], ((0, HYENA_EMB_PAD - HYENA_EMB), (0, 0)))
            h3 = _filter_trunk_call(zfeat, fw1, d_f_b1[j][None], d_f_w2[j], d_f_b2[j][None],
                                    d_f_w3[j], d_f_b3[j][None], d_f_freq[j][None])
            deltas = jnp.abs(jnp.linspace(HYENA_MIN_DECAY, HYENA_MAX_DECAY, CONV_WIDTH, dtype=F32))[None]
            kr, ki = _filter_spec_call(h3, d_f_w4[j].astype(BF16), deltas, e1_full, f2f, seq)
            y = _long_conv_call(z, x0, kr, ki, d_skip[j][None], e1_half, f2f, f2i, einv, bsz, seq)
            xf = _outproj_call(xf, att, y, wo, gate, seq)
        xf = ffn(xf, l, 1, 2)
    return xf.reshape(bsz, seq, d)
```

```python
import functools
import math

import jax
import jax.numpy as jnp
import numpy as np
from jax import lax
from jax.experimental import pallas as pl
from jax.experimental.pallas import tpu as pltpu

D_MODEL = 1024
HEAD_DIM = 64
N_Q_HEADS = 8
N_KV_HEADS = 2
GQA_GROUP = N_Q_HEADS // N_KV_HEADS
ATTN_WIDTH = N_Q_HEADS * HEAD_DIM
KV_WIDTH = N_KV_HEADS * HEAD_DIM
QKV_COLS = ATTN_WIDTH + 2 * KV_WIDTH
CONV_WIDTH = D_MODEL - ATTN_WIDTH
IN_COLS = QKV_COLS + 3 * CONV_WIDTH
D_FF = 2752
BLOCK = 128
WINDOW = 128
N_BUCKETS = 32
MAX_DISTANCE = 128
GRID_W = 64
ROPE_THETA = 10000.0
HYENA_EMB = 33
HYENA_FILTER_WIDTH = 64
HYENA_EMB_PAD = 64
HYENA_MIN_DECAY = math.log(1e-2) / 0.3
HYENA_MAX_DECAY = math.log(1e-2) / 1.5
EPS = 1e-6
NEG_INF = -1e30

LANES = 128
MXU_DIM = 256
FF_PAD = 2816
FF_CHUNK = 256
VMEM_LIMIT = 56 * 1024 * 1024
DFT_N2 = 128

BF16 = jnp.bfloat16
F32 = jnp.float32


def _cparams(sem):
    return pltpu.CompilerParams(dimension_semantics=sem, vmem_limit_bytes=VMEM_LIMIT)


def _resident(shape):
    nd = len(shape)
    return pl.BlockSpec(shape, lambda *_: (0,) * nd, pipeline_mode=pl.Buffered(1))


def _adaln(x, g, scale, shift):
    y = x * lax.rsqrt(jnp.mean(x * x, axis=-1, keepdims=True) + EPS) * g
    return y * (1.0 + scale) + shift


def _mod_kernel(c_ref, w_ref, b_ref, o_ref):
    w = w_ref[0]
    for b in range(c_ref.shape[0]):
        cc = c_ref[b]
        cond = cc * jax.nn.sigmoid(cc)
        o_ref[0, b:b + 1, :] = jnp.sum(w * cond, axis=0, keepdims=True) + b_ref[0]


def _mod_call(c, w_mod, b_mod):
    depth, d, n = w_mod.shape
    bsz = c.shape[0]
    tn = 512
    return pl.pallas_call(
        _mod_kernel,
        out_shape=jax.ShapeDtypeStruct((depth, bsz, n), F32),
        grid=(depth, n // tn),
        in_specs=[
            pl.BlockSpec((bsz, d, 1), lambda l, j: (0, 0, 0)),
            pl.BlockSpec((1, d, tn), lambda l, j: (l, 0, j)),
            pl.BlockSpec((1, 1, tn), lambda l, j: (l, 0, j)),
        ],
        out_specs=pl.BlockSpec((1, bsz, tn), lambda l, j: (l, 0, j)),
        compiler_params=_cparams(("arbitrary", "arbitrary")),
        name="adaln_modulation",
    )(c.reshape(bsz, d, 1), w_mod, b_mod.reshape(depth, 1, n))


def _ffn_kernel(x_ref, g_ref, shift_ref, scale_ref, gate_ref, w1_ref, w3_ref, w2_ref, o_ref):
    x = x_ref[...]
    h = _adaln(x, g_ref[...], scale_ref[0], shift_ref[0]).astype(BF16)
    acc = jnp.zeros(x.shape, F32)
    for c in range(FF_PAD // FF_CHUNK):
        cols = slice(c * FF_CHUNK, (c + 1) * FF_CHUNK)
        a = jnp.dot(h, w1_ref[:, cols], preferred_element_type=F32)
        b = jnp.dot(h, w3_ref[:, cols], preferred_element_type=F32)
        act = (a * jax.nn.sigmoid(a) * b).astype(BF16)
        acc = acc + jnp.dot(act, w2_ref[cols, :], preferred_element_type=F32)
    o_ref[...] = x + (0.5 * gate_ref[0]) * acc


def _ffn_call(x, g, shift, scale, gate, w1, w3, w2, rows_per_batch):
    m, d = x.shape
    tm = 512
    tpb = rows_per_batch // tm
    vec = pl.BlockSpec((1, 1, d), lambda i: (i // tpb, 0, 0))
    return pl.pallas_call(
        _ffn_kernel,
        out_shape=jax.ShapeDtypeStruct((m, d), F32),
        grid=(m // tm,),
        in_specs=[
            pl.BlockSpec((tm, d), lambda i: (i, 0)),
            _resident((1, d)),
            vec, vec, vec,
            _resident((d, FF_PAD)), _resident((d, FF_PAD)), _resident((FF_PAD, d)),
        ],
        out_specs=pl.BlockSpec((tm, d), lambda i: (i, 0)),
        compiler_params=_cparams(("arbitrary",)),
        name="adaln_swiglu_ffn",
    )(x, g, shift, scale, gate, w1, w3, w2)


def _group_norm_scale(v, e_ref):
    sq = v * v
    hi = sq.astype(BF16)
    lo = (sq - hi.astype(F32)).astype(BF16)
    e = e_ref[0:v.shape[1], 0:v.shape[1]]
    ss = jnp.dot(hi, e, preferred_element_type=F32) + jnp.dot(lo, e, preferred_element_type=F32)
    return lax.rsqrt(ss * (1.0 / HEAD_DIM) + EPS)


def _rope128(v, cos, sin_signed, even_lane):
    partner = jnp.where(even_lane, pltpu.roll(v, LANES - 1, 1), pltpu.roll(v, 1, 1))
    return v * cos + partner * sin_signed


def _inproj_kernel(rope, x_ref, g_ref, shift_ref, scale_ref, w_ref, qg_ref, kg_ref, e_ref, *rest):
    if rope:
        cos_ref, sin_ref, q_ref, k_ref, v_ref, u_ref = rest
    else:
        q_ref, k_ref, v_ref, u_ref = rest
    h = _adaln(x_ref[...], g_ref[...], scale_ref[0], shift_ref[0]).astype(BF16)
    q = jnp.dot(h, w_ref[:, 0:ATTN_WIDTH], preferred_element_type=F32)
    k = jnp.dot(h, w_ref[:, ATTN_WIDTH:ATTN_WIDTH + KV_WIDTH], preferred_element_type=F32)
    q = q * _group_norm_scale(q, e_ref) * qg_ref[...]
    k = k * _group_norm_scale(k, e_ref) * kg_ref[...]
    if rope:
        cos = cos_ref[...]
        sin = sin_ref[...]
        even = (lax.broadcasted_iota(jnp.int32, cos.shape, 1) % 2) == 0
        q = jnp.concatenate(
            [_rope128(q[:, j * LANES:(j + 1) * LANES], cos, sin, even) for j in range(ATTN_WIDTH // LANES)],
            axis=1)
        k = _rope128(k, cos, sin, even)
    q_ref[...] = (q * HEAD_DIM ** -0.5).astype(BF16)
    v = jnp.dot(h, w_ref[:, ATTN_WIDTH + KV_WIDTH:QKV_COLS], preferred_element_type=F32)
    if rope:
        k_ref[0] = k.T.astype(BF16)
        low = lax.broadcasted_iota(jnp.int32, v.shape, 1) < HEAD_DIM
        v_ref[...] = jnp.concatenate(
            [jnp.where(low, v, 1.0), jnp.where(low, pltpu.roll(v, HEAD_DIM, 1), 1.0)], axis=1).astype(BF16)
    else:
        k_ref[...] = k.astype(BF16)
        v_ref[...] = v.astype(BF16)
    u_ref[...] = jnp.dot(h, w_ref[:, QKV_COLS:IN_COLS], preferred_element_type=F32)


def _inproj_call(x, g, shift, scale, w_in, qg, kg, e, rope_tabs, rows_per_batch):
    m, d = x.shape
    tm = 512
    tpb = rows_per_batch // tm
    vec = pl.BlockSpec((1, 1, d), lambda i: (i // tpb, 0, 0))
    in_specs = [
        pl.BlockSpec((tm, d), lambda i: (i, 0)),
        _resident((1, d)), vec, vec,
        _resident((d, IN_COLS)),
        _resident((1, ATTN_WIDTH)), _resident((1, KV_WIDTH)),
        _resident((ATTN_WIDTH, ATTN_WIDTH)),
    ]
    args = [x, g, shift, scale, w_in, qg, kg, e]
    if rope_tabs is not None:
        tab = pl.BlockSpec((tm, LANES), lambda i: (i % tpb, 0))
        in_specs += [tab, tab]
        args += list(rope_tabs)
    row = lambda w: pl.BlockSpec((tm, w), lambda i: (i, 0))
    if rope_tabs is not None:
        tk = _dense_key_chunk(rows_per_batch)
        per = tk // tm
        k_shape, k_spec = (m // tk, KV_WIDTH, tk), pl.BlockSpec((1, KV_WIDTH, tm), lambda i: (i // per, 0, i % per))
        v_shape, v_spec = (m, 2 * KV_WIDTH), row(2 * KV_WIDTH)
    else:
        k_shape, k_spec = (m, KV_WIDTH), row(KV_WIDTH)
        v_shape, v_spec = (m, KV_WIDTH), row(KV_WIDTH)
    return pl.pallas_call(
        functools.partial(_inproj_kernel, rope_tabs is not None),
        out_shape=(
            jax.ShapeDtypeStruct((m, ATTN_WIDTH), BF16),
            jax.ShapeDtypeStruct(k_shape, BF16),
            jax.ShapeDtypeStruct(v_shape, BF16),
            jax.ShapeDtypeStruct((m, 3 * CONV_WIDTH), F32),
        ),
        grid=(m // tm,),
        in_specs=in_specs,
        out_specs=(row(ATTN_WIDTH), k_spec, v_spec, row(3 * CONV_WIDTH)),
        compiler_params=_cparams(("arbitrary",)),
        name="adaln_in_projection",
    )(*args)


_T5_STEPS = (12, 16, 23, 32, 46, 64, 91)


def _bias_kernel(tab_ref, o_ref):
    qi = lax.broadcasted_iota(jnp.int32, (BLOCK, 3 * BLOCK), 0)
    kj = lax.broadcasted_iota(jnp.int32, (BLOCK, 3 * BLOCK), 1)
    rel = kj - BLOCK - qi
    n = jnp.abs(rel)
    half = N_BUCKETS // 2
    max_exact = half // 2
    large = jnp.full(n.shape, max_exact, jnp.int32)
    for t in _T5_STEPS:
        large = large + (n >= t).astype(jnp.int32)
    bucket = jnp.where(rel > 0, half, 0) + jnp.where(n < max_exact, n, large)
    for h in range(N_Q_HEADS):
        bias = jnp.zeros(n.shape, F32)
        for b in range(N_BUCKETS):
            bias = jnp.where(bucket == b, tab_ref[b, h], bias)
        o_ref[h] = jnp.where(n <= WINDOW, bias, NEG_INF)


def _bias_call(rel_table):
    return pl.pallas_call(
        _bias_kernel,
        out_shape=jax.ShapeDtypeStruct((N_Q_HEADS, BLOCK, 3 * BLOCK), F32),
        in_specs=[pl.BlockSpec(memory_space=pltpu.SMEM)],
        name="t5_bias_tile",
    )(rel_table)


def _mixer_ab_kernel(tq, tpb, x_ref, q_ref, kc_ref, kp_ref, kn_ref, vc_ref, vp_ref, vn_ref,
                     uc_ref, up_ref, un_ref, bias_ref, sink_ref, cw_ref, cb_ref, wo_ref, gate_ref,
                     o_ref, kbuf, vbuf, att):
    i = pl.program_id(0)
    first = (i % tpb) == 0
    last = (i % tpb) == tpb - 1
    nblk = tq // BLOCK
    kbuf[0:BLOCK] = kp_ref[...]
    kbuf[BLOCK:BLOCK + tq] = kc_ref[...]
    kbuf[BLOCK + tq:] = kn_ref[...]
    vbuf[0:BLOCK] = vp_ref[...]
    vbuf[BLOCK:BLOCK + tq] = vc_ref[...]
    vbuf[BLOCK + tq:] = vn_ref[...]
    col = lax.broadcasted_iota(jnp.int32, (BLOCK, 3 * BLOCK), 1)
    for n in range(nblk):
        rows = slice(n * BLOCK, (n + 1) * BLOCK)
        krows = slice(n * BLOCK, (n + 3) * BLOCK)
        for h in range(N_Q_HEADS):
            g = h // GQA_GROUP
            kv_cols = slice(g * HEAD_DIM, (g + 1) * HEAD_DIM)
            s = lax.dot_general(q_ref[rows, h * HEAD_DIM:(h + 1) * HEAD_DIM], kbuf[krows, kv_cols],
                                (((1,), (1,)), ((), ())), preferred_element_type=F32)
            s = s + bias_ref[h]
            if n == 0:
                s = jnp.where(jnp.logical_and(first, col < BLOCK), NEG_INF, s)
            if n == nblk - 1:
                s = jnp.where(jnp.logical_and(last, col >= 2 * BLOCK), NEG_INF, s)
            sk = sink_ref[h]
            mx = jnp.maximum(jnp.max(s, axis=-1, keepdims=True), sk)
            p = jnp.exp(s - mx)
            den = jnp.sum(p, axis=-1, keepdims=True) + jnp.exp(sk - mx)
            o = jnp.dot(p.astype(BF16), vbuf[krows, kv_cols], preferred_element_type=F32)
            att[rows, h * HEAD_DIM:(h + 1) * HEAD_DIM] = o / den

    cw = cw_ref[...]
    gb = uc_ref[:, 0:CONV_WIDTH]
    p = uc_ref[:, CONV_WIDTH:2 * CONV_WIDTH] * uc_ref[:, 2 * CONV_WIDTH:]
    p_before = jnp.where(first, 0.0, up_ref[7:8, CONV_WIDTH:2 * CONV_WIDTH] * up_ref[7:8, 2 * CONV_WIDTH:])
    p_after = jnp.where(last, 0.0, un_ref[0:1, CONV_WIDTH:2 * CONV_WIDTH] * un_ref[0:1, 2 * CONV_WIDTH:])
    r = lax.broadcasted_iota(jnp.int32, p.shape, 0)
    pm1 = jnp.where(r == 0, p_before, pltpu.roll(p, 1, 0))
    pp1 = jnp.where(r == tq - 1, p_after, pltpu.roll(p, tq - 1, 0))
    conv = gb * (cw[0:1] * pm1 + cw[1:2] * p + cw[2:3] * pp1 + cb_ref[...])

    y = jnp.dot(att[...].astype(BF16), wo_ref[0:ATTN_WIDTH, :], preferred_element_type=F32)
    y = y + jnp.dot(conv.astype(BF16), wo_ref[ATTN_WIDTH:, :], preferred_element_type=F32)
    o_ref[...] = x_ref[...] + gate_ref[0] * y


def _mixer_ab_call(x, q, k, v, u, bias, sink, conv_w, conv_b, w_out, gate, rows_per_batch):
    m, d = x.shape
    tq = 512
    tpb = rows_per_batch // tq
    r = tq // BLOCK
    nb = m // BLOCK
    n8 = m // 8
    cur = lambda w: pl.BlockSpec((tq, w), lambda i: (i, 0))
    prev_blk = pl.BlockSpec((BLOCK, KV_WIDTH), lambda i: (jnp.maximum(i * r - 1, 0), 0))
    next_blk = pl.BlockSpec((BLOCK, KV_WIDTH), lambda i: (jnp.minimum((i + 1) * r, nb - 1), 0))
    uw = 3 * CONV_WIDTH
    return pl.pallas_call(
        functools.partial(_mixer_ab_kernel, tq, tpb),
        out_shape=jax.ShapeDtypeStruct((m, d), F32),
        grid=(m // tq,),
        in_specs=[
            cur(d), cur(ATTN_WIDTH),
            cur(KV_WIDTH), prev_blk, next_blk,
            cur(KV_WIDTH), prev_blk, next_blk,
            cur(uw),
            pl.BlockSpec((8, uw), lambda i: (jnp.maximum(i * (tq // 8) - 1, 0), 0)),
            pl.BlockSpec((8, uw), lambda i: (jnp.minimum((i + 1) * (tq // 8), n8 - 1), 0)),
            _resident((N_Q_HEADS, BLOCK, 3 * BLOCK)),
            pl.BlockSpec(memory_space=pltpu.SMEM),
            _resident((3, CONV_WIDTH)), _resident((1, CONV_WIDTH)),
            _resident((d, d)),
            pl.BlockSpec((1, 1, d), lambda i: (i // tpb, 0, 0)),
        ],
        out_specs=cur(d),
        scratch_shapes=[
            pltpu.VMEM((tq + 2 * BLOCK, KV_WIDTH), BF16),
            pltpu.VMEM((tq + 2 * BLOCK, KV_WIDTH), BF16),
            pltpu.VMEM((tq, ATTN_WIDTH), F32),
        ],
        compiler_params=_cparams(("arbitrary",)),
        name="windowed_attn_shortconv_outproj",
    )(x, q, k, k, k, v, v, v, u, u, u, bias, sink, conv_w, conv_b, w_out, gate)


def _dense_key_chunk(seq):
    return min(seq, 1024)


def _dense_attn_kernel(tq, tk, q_ref, k_ref, v_ref, o_ref, qs, s_bufs, p_bufs, rmax_bufs, alpha_bufs,
                       m_ref, acc_ref):
    nc = k_ref.shape[0]
    groups = range(N_KV_HEADS)
    for h in range(N_Q_HEADS):
        g, j = divmod(h, GQA_GROUP)
        qs[g, j * tq:(j + 1) * tq, :] = q_ref[:, h * HEAD_DIM:(h + 1) * HEAD_DIM]
    for g in groups:
        m_ref[g][...] = jnp.full(m_ref[g].shape, -jnp.inf, F32)
        acc_ref[g][...] = jnp.zeros(acc_ref[g].shape, F32)

    def scores(g, c, par):
        s = jnp.dot(qs[g], k_ref[c, g * HEAD_DIM:(g + 1) * HEAD_DIM, :], preferred_element_type=F32)
        s_bufs[g][par][...] = s
        rmax_bufs[g][par][...] = jnp.broadcast_to(jnp.max(s, axis=-1, keepdims=True), rmax_bufs[g][par].shape)

    def softmax(g, par):
        m_old = m_ref[g][...]
        m_new = jnp.maximum(m_old, rmax_bufs[g][par][...])
        alpha_bufs[g][par][...] = jnp.exp(m_old - m_new)
        m_ref[g][...] = m_new
        p = jnp.exp(s_bufs[g][par][...] - jnp.tile(m_new, (1, tk // LANES)))
        p_bufs[g][par][...] = p.astype(BF16)

    def weighted_values(g, c, par):
        start = c * tk if isinstance(c, int) else pl.multiple_of(c * tk, tk)
        vc = v_ref[pl.ds(start, tk), g * LANES:(g + 1) * LANES]
        pv = jnp.dot(p_bufs[g][par][...], vc, preferred_element_type=F32)
        acc_ref[g][...] = alpha_bufs[g][par][...] * acc_ref[g][...] + pv

    def step(t, par):
        static = isinstance(t, int)
        for g in groups:
            weighted_values(g, t, par)
            if not static or t + 1 < nc:
                softmax(g, 1 - par)
            if not static or t + 2 < nc:
                scores(g, t + 2, par)

    for g in groups:
        scores(g, 0, 0)
    for g in groups:
        if nc > 1:
            scores(g, 1, 1)
        softmax(g, 0)

    n_pairs = max(nc - 2, 0) // 2

    def steady(tt, carry):
        step(2 * tt, 0)
        step(2 * tt + 1, 1)
        return carry

    lax.fori_loop(0, n_pairs, steady, 0)
    for t in range(2 * n_pairs, nc):
        step(t, t % 2)

    for g in groups:
        acc = acc_ref[g][...]
        o = acc / pltpu.roll(acc, HEAD_DIM, 1)
        for j in range(GQA_GROUP):
            h = g * GQA_GROUP + j
            o_ref[:, h * HEAD_DIM:(h + 1) * HEAD_DIM] = o[j * tq:(j + 1) * tq, 0:HEAD_DIM].astype(o_ref.dtype)


def _dense_attn_call(q, kt, v1, bsz, seq):
    tq, tk = 128, _dense_key_chunk(seq)
    nq = seq // tq
    nc = seq // tk
    rows = GQA_GROUP * tq
    per_group = lambda shape, dtype: [pltpu.VMEM(shape, dtype) for _ in range(N_KV_HEADS)]
    pair = lambda shape, dtype: [[pltpu.VMEM(shape, dtype), pltpu.VMEM(shape, dtype)]
                                 for _ in range(N_KV_HEADS)]
    return pl.pallas_call(
        functools.partial(_dense_attn_kernel, tq, tk),
        out_shape=jax.ShapeDtypeStruct((bsz * seq, ATTN_WIDTH), BF16),
        grid=(bsz, nq),
        in_specs=[
            pl.BlockSpec((tq, ATTN_WIDTH), lambda b, i: (b * nq + i, 0)),
            pl.BlockSpec((nc, KV_WIDTH, tk), lambda b, i: (b, 0, 0)),
            pl.BlockSpec((seq, 2 * KV_WIDTH), lambda b, i: (b, 0)),
        ],
        out_specs=pl.BlockSpec((tq, ATTN_WIDTH), lambda b, i: (b * nq + i, 0)),
        scratch_shapes=[
            pltpu.VMEM((N_KV_HEADS, rows, HEAD_DIM), BF16),
            pair((rows, tk), F32), pair((rows, tk), BF16), pair((rows, LANES), F32), pair((rows, LANES), F32),
            per_group((rows, LANES), F32), per_group((rows, LANES), F32),
        ],
        compiler_params=_cparams(("arbitrary", "arbitrary")),
        name="dense_gqa_attention",
    )(q, kt, v1)


def _hyena_pre_kernel(tm, tpb, uc_ref, up_ref, un_ref, cw_ref, cb_ref, x0_ref, z_ref):
    i = pl.program_id(0)
    first = (i % tpb) == 0
    last = (i % tpb) == tpb - 1
    u = uc_ref[...]
    before = jnp.where(first, 0.0, up_ref[7:8, :])
    after = jnp.where(last, 0.0, un_ref[0:1, :])
    r = lax.broadcasted_iota(jnp.int32, u.shape, 0)
    um1 = jnp.where(r == 0, before, pltpu.roll(u, 1, 0))
    up1 = jnp.where(r == tm - 1, after, pltpu.roll(u, tm - 1, 0))
    cw = cw_ref[...]
    t = cw[0:1] * um1 + cw[1:2] * u + cw[2:3] * up1 + cb_ref[...]
    x0_ref[...] = t[:, 0:CONV_WIDTH]
    z_ref[...] = t[:, CONV_WIDTH:2 * CONV_WIDTH] * t[:, 2 * CONV_WIDTH:]


def _hyena_pre_call(u, conv_w, conv_b, rows_per_batch):
    m, uw = u.shape
    tm = 512
    tpb = rows_per_batch // tm
    n8 = m // 8
    out = pl.BlockSpec((tm, CONV_WIDTH), lambda i: (i, 0))
    return pl.pallas_call(
        functools.partial(_hyena_pre_kernel, tm, tpb),
        out_shape=(jax.ShapeDtypeStruct((m, CONV_WIDTH), F32), jax.ShapeDtypeStruct((m, CONV_WIDTH), F32)),
        grid=(m // tm,),
        in_specs=[
            pl.BlockSpec((tm, uw), lambda i: (i, 0)),
            pl.BlockSpec((8, uw), lambda i: (jnp.maximum(i * (tm // 8) - 1, 0), 0)),
            pl.BlockSpec((8, uw), lambda i: (jnp.minimum((i + 1) * (tm // 8), n8 - 1), 0)),
            _resident((3, uw)), _resident((1, uw)),
        ],
        out_specs=(out, out),
        compiler_params=_cparams(("arbitrary",)),
        name="hyena_short_conv",
    )(u, u, u, conv_w, conv_b)


def _hp_dot(a, b):
    return jnp.dot(a, b, preferred_element_type=F32, precision=lax.Precision.HIGHEST)


def _filter_trunk_kernel(z_ref, w1_ref, b1_ref, w2_ref, b2_ref, w3_ref, b3_ref, fq_ref, o_ref):
    fq = fq_ref[...]
    h = jnp.sin(fq * (_hp_dot(z_ref[...], w1_ref[...]) + b1_ref[...]))
    h = jnp.sin(fq * (_hp_dot(h, w2_ref[...]) + b2_ref[...]))
    o_ref[...] = jnp.sin(fq * (_hp_dot(h, w3_ref[...]) + b3_ref[...]))


def _filter_trunk_call(zfeat, w1, b1, w2, b2, w3, b3, freq):
    rows = zfeat.shape[0]
    tr = min(rows, 2048)
    fw = HYENA_FILTER_WIDTH
    return pl.pallas_call(
        _filter_trunk_kernel,
        out_shape=jax.ShapeDtypeStruct((rows, fw), F32),
        grid=(rows // tr,),
        in_specs=[
            pl.BlockSpec((tr, HYENA_EMB_PAD), lambda i: (i, 0)),
            _resident((HYENA_EMB_PAD, fw)), _resident((1, fw)),
            _resident((fw, fw)), _resident((1, fw)),
            _resident((fw, fw)), _resident((1, fw)),
            _resident((1, fw)),
        ],
        out_specs=pl.BlockSpec((tr, fw), lambda i: (i, 0)),
        compiler_params=_cparams(("arbitrary",)),
        name="hyena_filter_trunk",
    )(zfeat, w1, b1, w2, b2, w3, b3, freq)


def _dft_stage1(src_ref, tab_ref, re_ref, im_ref, n_rows, k1p):
    def body(n2, carry):
        xs = src_ref[pl.ds(n2, n_rows, stride=DFT_N2), :].astype(BF16)
        res = jnp.dot(tab_ref[n2], xs, preferred_element_type=F32)
        re_ref[pl.ds(n2, k1p, stride=DFT_N2), :] = res[0:k1p]
        im_ref[pl.ds(n2, k1p, stride=DFT_N2), :] = res[k1p:]
        return carry

    lax.fori_loop(0, DFT_N2, body, 0, unroll=4)


def _filter_spec_kernel(seq, k1p, h_ref, w4f_ref, w4b_ref, dl_ref, e1_ref, f2_ref,
                        kr_ref, ki_ref, kfull):
    n = 2 * seq
    chunk = min(seq, 1024)
    dl = dl_ref[...]

    def fill(c, ss):
        r0 = pl.multiple_of(c * chunk, chunk)
        hc = h_ref[pl.ds(r0, chunk), :].astype(BF16)
        fwd = jnp.dot(hc, w4f_ref[...], preferred_element_type=F32)
        bwd = jnp.dot(hc, w4b_ref[...], preferred_element_type=F32)
        rows = r0 + lax.broadcasted_iota(jnp.int32, (chunk, 1), 0)
        lag = jnp.where(rows < seq, rows, n - rows).astype(F32)
        t = lag * (1.0 / (seq - 1))
        val = jnp.where(rows < seq, fwd, bwd) * jnp.exp(-t * dl)
        val = jnp.where(rows == seq, 0.0, val)
        kfull[pl.ds(r0, chunk), :] = val
        return ss + jnp.sum(val * val, axis=0, keepdims=True)

    ss = lax.fori_loop(0, n // chunk, fill, jnp.zeros((1, dl.shape[1]), F32))
    norm = lax.rsqrt(ss + EPS)

    _dft_stage1(kfull, e1_ref, kr_ref, ki_ref, n // DFT_N2, k1p)

    f2 = f2_ref[...]

    def stage2(a, carry):
        rows = pl.ds(pl.multiple_of(a * DFT_N2, DFT_N2), DFT_N2)
        slab = jnp.concatenate([kr_ref[rows, :], ki_ref[rows, :]], axis=0).astype(BF16)
        xf = jnp.dot(f2, slab, preferred_element_type=F32)
        kr_ref[rows, :] = xf[0:DFT_N2] * norm
        ki_ref[rows, :] = xf[DFT_N2:] * norm
        return carry

    lax.fori_loop(0, seq // DFT_N2 + 1, stage2, 0, unroll=2)


def _filter_spec_call(h3, w4, deltas, e1f, f2f, seq):
    n = 2 * seq
    k1p = e1f.shape[1] // 2
    ct = LANES
    nct = CONV_WIDTH // ct
    fw = HYENA_FILTER_WIDTH
    spec_rows = k1p * DFT_N2
    out = pl.BlockSpec((spec_rows, ct), lambda j: (0, j))
    return pl.pallas_call(
        functools.partial(_filter_spec_kernel, seq, k1p),
        out_shape=(jax.ShapeDtypeStruct((spec_rows, CONV_WIDTH), F32),
                   jax.ShapeDtypeStruct((spec_rows, CONV_WIDTH), F32)),
        grid=(nct,),
        in_specs=[
            _resident((n, fw)),
            pl.BlockSpec((fw, ct), lambda j: (0, j)),
            pl.BlockSpec((fw, ct), lambda j: (0, nct + j)),
            pl.BlockSpec((1, ct), lambda j: (0, j)),
            _resident(e1f.shape),
            _resident((2 * DFT_N2, 2 * DFT_N2)),
        ],
        out_specs=(out, out),
        scratch_shapes=[pltpu.VMEM((n, ct), F32)],
        compiler_params=_cparams(("arbitrary",)),
        name="hyena_filter_spectrum",
    )(h3, w4, w4, deltas, e1f, f2f)


def _long_conv_kernel(seq, k1p, z_ref, x0_ref, kr_ref, ki_ref, skip_ref, e1_ref, f2f_ref, f2i_ref,
                      einv_ref, o_ref, br, bi):
    n1h = seq // DFT_N2
    _dft_stage1(z_ref, e1_ref, br, bi, n1h, k1p)

    f2f = f2f_ref[...]
    f2i = f2i_ref[...]

    def spectral(a, carry):
        r0 = pl.multiple_of(a * DFT_N2, DFT_N2)
        rows = pl.ds(r0, DFT_N2)
        slab = jnp.concatenate([br[rows, :], bi[rows, :]], axis=0).astype(BF16)
        xf = jnp.dot(f2f, slab, preferred_element_type=F32)
        xr, xi = xf[0:DFT_N2], xf[DFT_N2:]
        kr, ki = kr_ref[rows, :], ki_ref[rows, :]
        y = jnp.concatenate([xr * kr - xi * ki, xr * ki + xi * kr], axis=0).astype(BF16)
        d = jnp.dot(f2i, y, preferred_element_type=F32)
        br[rows, :] = d[0:DFT_N2]
        bi[rows, :] = d[DFT_N2:]
        return carry

    lax.fori_loop(0, n1h + 1, spectral, 0, unroll=2)

    skip = skip_ref[...]

    def synth(n2, carry):
        ds_spec = pl.ds(n2, k1p, stride=DFT_N2)
        d = jnp.concatenate([br[ds_spec, :], bi[ds_spec, :]], axis=0).astype(BF16)
        conv = jnp.dot(einv_ref[n2], d, preferred_element_type=F32)
        ds_seq = pl.ds(n2, n1h, stride=DFT_N2)
        o_ref[ds_seq, :] = x0_ref[ds_seq, :] * (conv + skip * z_ref[ds_seq, :])
        return carry

    lax.fori_loop(0, DFT_N2, synth, 0, unroll=4)


def _long_conv_call(z, x0, kr, ki, skip, e1, f2f, f2i, einv, bsz, seq):
    ct = LANES
    nct = CONV_WIDTH // ct
    k1p = e1.shape[1] // 2
    spec_rows = k1p * DFT_N2
    big = lambda: pl.BlockSpec((seq, ct), lambda j, b: (b, j), pipeline_mode=pl.Buffered(1))
    spec = lambda: pl.BlockSpec((spec_rows, ct), lambda j, b: (0, j), pipeline_mode=pl.Buffered(1))
    return pl.pallas_call(
        functools.partial(_long_conv_kernel, seq, k1p),
        out_shape=jax.ShapeDtypeStruct((bsz * seq, CONV_WIDTH), F32),
        grid=(nct, bsz),
        in_specs=[
            big(), big(), spec(), spec(),
            pl.BlockSpec((1, ct), lambda j, b: (0, j)),
            _resident(e1.shape),
            _resident((2 * DFT_N2, 2 * DFT_N2)), _resident((2 * DFT_N2, 2 * DFT_N2)),
            _resident(einv.shape),
        ],
        out_specs=pl.BlockSpec((seq, ct), lambda j, b: (b, j)),
        scratch_shapes=[pltpu.VMEM((spec_rows, ct), F32), pltpu.VMEM((spec_rows, ct), F32)],
        compiler_params=_cparams(("arbitrary", "arbitrary")),
        name="hyena_long_conv",
    )(z, x0, kr, ki, skip, e1, f2f, f2i, einv)


def _outproj_kernel(x_ref, att_ref, y_ref, wo_ref, gate_ref, o_ref):
    y = jnp.dot(att_ref[...], wo_ref[0:ATTN_WIDTH, :], preferred_element_type=F32)
    y = y + jnp.dot(y_ref[...].astype(BF16), wo_ref[ATTN_WIDTH:, :], preferred_element_type=F32)
    o_ref[...] = x_ref[...] + gate_ref[0] * y


def _outproj_call(x, att, y, w_out, gate, rows_per_batch):
    m, d = x.shape
    tm = 512
    tpb = rows_per_batch // tm
    row = lambda w: pl.BlockSpec((tm, w), lambda i: (i, 0))
    return pl.pallas_call(
        _outproj_kernel,
        out_shape=jax.ShapeDtypeStruct((m, d), F32),
        grid=(m // tm,),
        in_specs=[row(d), row(ATTN_WIDTH), row(CONV_WIDTH), _resident((d, d)),
                  pl.BlockSpec((1, 1, d), lambda i: (i // tpb, 0, 0))],
        out_specs=row(d),
        compiler_params=_cparams(("arbitrary",)),
        name="mixer_out_projection",
    )(x, att, y, w_out, gate)


def _rope_tables(seq):
    t = jnp.arange(seq)
    row = (t // GRID_W).astype(F32)
    col = (t % GRID_W).astype(F32)
    half = HEAD_DIM // 2
    inv = ROPE_THETA ** (-jnp.arange(0, half, 2, dtype=F32) / half)
    ang = jnp.concatenate([row[:, None] * inv, col[:, None] * inv], axis=-1)
    ang = jnp.repeat(ang, 2, axis=-1)
    sign = jnp.where(jnp.arange(HEAD_DIM) % 2 == 0, -1.0, 1.0).astype(F32)
    reps = LANES // HEAD_DIM
    return jnp.tile(jnp.cos(ang), (1, reps)), jnp.tile(jnp.sin(ang) * sign, (1, reps))


def _dft_tables(seq):
    n = 2 * seq
    n1 = n // DFT_N2
    k1 = n1 // 2 + 1
    k1p = -(-k1 // 16) * 16
    a = jnp.arange(k1p)
    live = (a < k1)
    pos = jnp.arange(n)
    theta = ((a[:, None] * pos[None, :]) % n).astype(F32) * (2.0 * math.pi / n)
    cos = jnp.where(live[:, None], jnp.cos(theta), 0.0)
    sin = jnp.where(live[:, None], jnp.sin(theta), 0.0)
    ana = jnp.concatenate([cos, -sin], axis=0).reshape(2 * k1p, n1, DFT_N2)
    e1_full = jnp.transpose(ana, (2, 0, 1)).astype(BF16)
    e1_half = e1_full[:, :, :n1 // 2]
    wgt = jnp.where((a == 0) | (a == n1 // 2), 1.0, 2.0) / n
    syn = jnp.concatenate([cos * wgt[:, None], -sin * wgt[:, None]], axis=0)
    syn = syn[:, :seq].reshape(2 * k1p, n1 // 2, DFT_N2)
    einv = jnp.transpose(syn, (2, 1, 0)).astype(BF16)
    kk = jnp.arange(DFT_N2)
    phi = ((kk[:, None] * kk[None, :]) % DFT_N2).astype(F32) * (2.0 * math.pi / DFT_N2)
    c2, s2 = jnp.cos(phi), jnp.sin(phi)
    f2f = jnp.block([[c2, s2], [-s2, c2]]).astype(BF16)
    f2i = jnp.block([[c2, -s2], [s2, c2]]).astype(BF16)
    return e1_full, e1_half, einv, f2f, f2i


def _filter_features(seq):
    t = jnp.linspace(0.0, 1.0, seq, dtype=F32)[:, None]
    bands = (HYENA_EMB - 1) // 2
    w = (2.0 * math.pi / seq) * jnp.arange(seq, dtype=F32)[:, None]
    fr = jnp.linspace(1e-4, bands - 1, bands, dtype=F32)[None, :]
    z = jnp.concatenate([t, jnp.cos(fr * w), -jnp.sin(fr * w)], axis=-1)
    back = jnp.concatenate([z[:1], z[:0:-1]], axis=0)
    feats = jnp.concatenate([z, back], axis=0)
    return jnp.pad(feats, ((0, 0), (0, HYENA_EMB_PAD - HYENA_EMB)))


def kernel(x, c, rel_table, norm_g, w_mod, b_mod, w_in, w_out, ffn_w1, ffn_w3, ffn_w2, a_qk_g, a_sink,
           b_conv_w, b_conv_b, c_qk_g, d_conv_w, d_conv_b, d_f_w1, d_f_b1, d_f_w2, d_f_b2, d_f_w3,
           d_f_b3, d_f_w4, d_f_freq, d_skip):
    bsz, seq, d = x.shape
    depth = w_mod.shape[0]
    m_rows = bsz * seq
    pad_f = FF_PAD - D_FF

    mod = _mod_call(c, w_mod, b_mod).reshape(depth, bsz, 3, 3, 1, d)
    e_blk = jnp.kron(jnp.eye(N_Q_HEADS, dtype=F32), jnp.ones((HEAD_DIM, HEAD_DIM), F32)).astype(BF16)

    def ffn(xf, l, which, sub):
        w1 = jnp.pad(ffn_w1[l, which].astype(BF16), ((0, 0), (0, pad_f)))
        w3 = jnp.pad(ffn_w3[l, which].astype(BF16), ((0, 0), (0, pad_f)))
        w2 = jnp.pad(ffn_w2[l, which].astype(BF16), ((0, pad_f), (0, 0)))
        return _ffn_call(xf, norm_g[l, sub][None], mod[l, :, sub, 0], mod[l, :, sub, 1], mod[l, :, sub, 2],
                         w1, w3, w2, seq)

    xf = x.reshape(m_rows, d)
    for l in range(depth):
        j = l // 2
        xf = ffn(xf, l, 0, 0)
        even = l % 2 == 0
        qk_g = a_qk_g[j] if even else c_qk_g[j]
        q, k, v, u = _inproj_call(
            xf, norm_g[l, 1][None], mod[l, :, 1, 0], mod[l, :, 1, 1], w_in[l].astype(BF16),
            jnp.tile(qk_g[0], N_Q_HEADS)[None], jnp.tile(qk_g[1], N_KV_HEADS)[None], e_blk,
            None if even else _rope_tables(seq), seq)
        wo = w_out[l].astype(BF16)
        gate = mod[l, :, 1, 2]
        if even:
            bias = _bias_call(rel_table)
            xf = _mixer_ab_call(xf, q, k, v, u, bias, a_sink[j], b_conv_w[j], b_conv_b[j][None], wo, gate, seq)
        else:
            att = _dense_attn_call(q, k, v, bsz, seq)
            x0, z = _hyena_pre_call(u, d_conv_w[j], d_conv_b[j][None], seq)
            e1_full, e1_half, einv, f2f, f2i = _dft_tables(seq)
            zfeat = _filter_features(seq)
            fw1 = jnp.pad(d_f_w1[j], ((0, HYENA_EMB_PAD - HYENA_EMB), (0, 0)))
            h3 = _filter_trunk_call(zfeat, fw1, d_f_b1[j][None], d_f_w2[j], d_f_b2[j][None],
                                    d_f_w3[j], d_f_b3[j][None], d_f_freq[j][None])
            deltas = jnp.abs(jnp.linspace(HYENA_MIN_DECAY, HYENA_MAX_DECAY, CONV_WIDTH, dtype=F32))[None]
            kr, ki = _filter_spec_call(h3, d_f_w4[j].astype(BF16), deltas, e1_full, f2f, seq)
            y = _long_conv_call(z, x0, kr, ki, d_skip[j][None], e1_half, f2f, f2i, einv, bsz, seq)
            xf = _outproj_call(xf, att, y, wo, gate, seq)
        xf = ffn(xf, l, 1, 2)
    return xf.reshape(bsz, seq, d)
```

```python
import functools
import math

import jax
import jax.numpy as jnp
import numpy as np
from jax import lax
from jax.experimental import pallas as pl
from jax.experimental.pallas import tpu as pltpu

D_MODEL = 1024
HEAD_DIM = 64
N_Q_HEADS = 8
N_KV_HEADS = 2
GQA_GROUP = N_Q_HEADS // N_KV_HEADS
ATTN_WIDTH = N_Q_HEADS * HEAD_DIM
KV_WIDTH = N_KV_HEADS * HEAD_DIM
QKV_COLS = ATTN_WIDTH + 2 * KV_WIDTH
CONV_WIDTH = D_MODEL - ATTN_WIDTH
IN_COLS = QKV_COLS + 3 * CONV_WIDTH
D_FF = 2752
BLOCK = 128
WINDOW = 128
N_BUCKETS = 32
MAX_DISTANCE = 128
GRID_W = 64
ROPE_THETA = 10000.0
HYENA_EMB = 33
HYENA_FILTER_WIDTH = 64
HYENA_EMB_PAD = 64
HYENA_MIN_DECAY = math.log(1e-2) / 0.3
HYENA_MAX_DECAY = math.log(1e-2) / 1.5
EPS = 1e-6
NEG_INF = -1e30

LANES = 128
MXU_DIM = 256
FF_CHUNK = MXU_DIM
VMEM_LIMIT = 56 * 1024 * 1024
DFT_N2 = 128

BF16 = jnp.bfloat16
F32 = jnp.float32


def _cparams(sem):
    return pltpu.CompilerParams(dimension_semantics=sem, vmem_limit_bytes=VMEM_LIMIT)


def _resident(shape):
    nd = len(shape)
    return pl.BlockSpec(shape, lambda *_: (0,) * nd, pipeline_mode=pl.Buffered(1))


def _adaln(x, g, scale, shift):
    y = x * lax.rsqrt(jnp.mean(x * x, axis=-1, keepdims=True) + EPS) * g
    return y * (1.0 + scale) + shift


def _mod_kernel(c_ref, w_ref, b_ref, o_ref):
    w = w_ref[0]
    for b in range(c_ref.shape[0]):
        cc = c_ref[b]
        cond = cc * jax.nn.sigmoid(cc)
        o_ref[0, b:b + 1, :] = jnp.sum(w * cond, axis=0, keepdims=True) + b_ref[0]


def _mod_call(c, w_mod, b_mod):
    depth, d, n = w_mod.shape
    bsz = c.shape[0]
    tn = 512
    return pl.pallas_call(
        _mod_kernel,
        out_shape=jax.ShapeDtypeStruct((depth, bsz, n), F32),
        grid=(depth, n // tn),
        in_specs=[
            pl.BlockSpec((bsz, d, 1), lambda l, j: (0, 0, 0)),
            pl.BlockSpec((1, d, tn), lambda l, j: (l, 0, j)),
            pl.BlockSpec((1, 1, tn), lambda l, j: (l, 0, j)),
        ],
        out_specs=pl.BlockSpec((1, bsz, tn), lambda l, j: (l, 0, j)),
        compiler_params=_cparams(("arbitrary", "arbitrary")),
        name="adaln_modulation",
    )(c.reshape(bsz, d, 1), w_mod, b_mod.reshape(depth, 1, n))


def _ffn_kernel(x_ref, g_ref, shift_ref, scale_ref, gate_ref, w1_ref, w3_ref, w2_ref, o_ref):
    x = x_ref[...]
    h = _adaln(x, g_ref[...], scale_ref[0], shift_ref[0]).astype(BF16)
    acc = jnp.zeros(x.shape, F32)
    for c0 in range(0, D_FF, FF_CHUNK):
        cols = slice(c0, min(c0 + FF_CHUNK, D_FF))
        a = jnp.dot(h, w1_ref[:, cols], preferred_element_type=F32)
        b = jnp.dot(h, w3_ref[:, cols], preferred_element_type=F32)
        act = (a * jax.nn.sigmoid(a) * b).astype(BF16)
        acc = acc + jnp.dot(act, w2_ref[cols, :], preferred_element_type=F32)
    o_ref[...] = x + (0.5 * gate_ref[0]) * acc


def _ffn_call(x, g, shift, scale, gate, w1, w3, w2, rows_per_batch):
    m, d = x.shape
    tm = 512
    tpb = rows_per_batch // tm
    vec = pl.BlockSpec((1, 1, d), lambda i: (i // tpb, 0, 0))
    return pl.pallas_call(
        _ffn_kernel,
        out_shape=jax.ShapeDtypeStruct((m, d), F32),
        grid=(m // tm,),
        in_specs=[
            pl.BlockSpec((tm, d), lambda i: (i, 0)),
            _resident((1, d)),
            vec, vec, vec,
            _resident((d, D_FF)), _resident((d, D_FF)), _resident((D_FF, d)),
        ],
        out_specs=pl.BlockSpec((tm, d), lambda i: (i, 0)),
        compiler_params=_cparams(("arbitrary",)),
        name="adaln_swiglu_ffn",
    )(x, g, shift, scale, gate, w1, w3, w2)


def _group_norm_scale(v, e_ref):
    sq = v * v
    hi = sq.astype(BF16)
    lo = (sq - hi.astype(F32)).astype(BF16)
    e = e_ref[0:v.shape[1], 0:v.shape[1]]
    ss = jnp.dot(hi, e, preferred_element_type=F32) + jnp.dot(lo, e, preferred_element_type=F32)
    return lax.rsqrt(ss * (1.0 / HEAD_DIM) + EPS)


def _rope128(v, cos, sin_signed, even_lane):
    partner = jnp.where(even_lane, pltpu.roll(v, LANES - 1, 1), pltpu.roll(v, 1, 1))
    return v * cos + partner * sin_signed


def _inproj_kernel(rope, x_ref, g_ref, shift_ref, scale_ref, w_ref, qg_ref, kg_ref, e_ref, *rest):
    if rope:
        cos_ref, sin_ref, q_ref, k_ref, v_ref, u_ref = rest
    else:
        q_ref, k_ref, v_ref, u_ref = rest
    h = _adaln(x_ref[...], g_ref[...], scale_ref[0], shift_ref[0]).astype(BF16)
    q = jnp.dot(h, w_ref[:, 0:ATTN_WIDTH], preferred_element_type=F32)
    k = jnp.dot(h, w_ref[:, ATTN_WIDTH:ATTN_WIDTH + KV_WIDTH], preferred_element_type=F32)
    q = q * _group_norm_scale(q, e_ref) * qg_ref[...]
    k = k * _group_norm_scale(k, e_ref) * kg_ref[...]
    if rope:
        cos = cos_ref[...]
        sin = sin_ref[...]
        even = (lax.broadcasted_iota(jnp.int32, cos.shape, 1) % 2) == 0
        q = jnp.concatenate(
            [_rope128(q[:, j * LANES:(j + 1) * LANES], cos, sin, even) for j in range(ATTN_WIDTH // LANES)],
            axis=1)
        k = _rope128(k, cos, sin, even)
    q_ref[...] = (q * HEAD_DIM ** -0.5).astype(BF16)
    v = jnp.dot(h, w_ref[:, ATTN_WIDTH + KV_WIDTH:QKV_COLS], preferred_element_type=F32)
    k_ref[0] = k.T.astype(BF16)
    low = lax.broadcasted_iota(jnp.int32, v.shape, 1) < HEAD_DIM
    v_ref[...] = jnp.concatenate(
        [jnp.where(low, v, 1.0), jnp.where(low, pltpu.roll(v, HEAD_DIM, 1), 1.0)], axis=1).astype(BF16)
    u_ref[...] = jnp.dot(h, w_ref[:, QKV_COLS:IN_COLS], preferred_element_type=F32)


def _inproj_call(x, g, shift, scale, w_in, qg, kg, e, rope_tabs, rows_per_batch):
    m, d = x.shape
    tm = 512
    tpb = rows_per_batch // tm
    vec = pl.BlockSpec((1, 1, d), lambda i: (i // tpb, 0, 0))
    in_specs = [
        pl.BlockSpec((tm, d), lambda i: (i, 0)),
        _resident((1, d)), vec, vec,
        _resident((d, IN_COLS)),
        _resident((1, ATTN_WIDTH)), _resident((1, KV_WIDTH)),
        _resident((ATTN_WIDTH, ATTN_WIDTH)),
    ]
    args = [x, g, shift, scale, w_in, qg, kg, e]
    if rope_tabs is not None:
        tab = pl.BlockSpec((tm, LANES), lambda i: (i % tpb, 0))
        in_specs += [tab, tab]
        args += list(rope_tabs)
    row = lambda w: pl.BlockSpec((tm, w), lambda i: (i, 0))
    tk = _key_chunk(rows_per_batch)
    per = tk // tm
    k_shape, k_spec = (m // tk, KV_WIDTH, tk), pl.BlockSpec((1, KV_WIDTH, tm), lambda i: (i // per, 0, i % per))
    v_shape, v_spec = (m, 2 * KV_WIDTH), row(2 * KV_WIDTH)
    return pl.pallas_call(
        functools.partial(_inproj_kernel, rope_tabs is not None),
        out_shape=(
            jax.ShapeDtypeStruct((m, ATTN_WIDTH), BF16),
            jax.ShapeDtypeStruct(k_shape, BF16),
            jax.ShapeDtypeStruct(v_shape, BF16),
            jax.ShapeDtypeStruct((m, 3 * CONV_WIDTH), F32),
        ),
        grid=(m // tm,),
        in_specs=in_specs,
        out_specs=(row(ATTN_WIDTH), k_spec, v_spec, row(3 * CONV_WIDTH)),
        compiler_params=_cparams(("arbitrary",)),
        name="adaln_in_projection",
    )(*args)


_T5_STEPS = (12, 16, 23, 32, 46, 64, 91)


def _bias_kernel(tab_ref, sink_ref, o_ref, sink_rows_ref):
    for h in range(N_Q_HEADS):
        sink_rows_ref[h * BLOCK:(h + 1) * BLOCK, :] = jnp.full((BLOCK, LANES), sink_ref[h], F32)
    qi = lax.broadcasted_iota(jnp.int32, (BLOCK, 3 * BLOCK), 0)
    kj = lax.broadcasted_iota(jnp.int32, (BLOCK, 3 * BLOCK), 1)
    rel = kj - BLOCK - qi
    n = jnp.abs(rel)
    half = N_BUCKETS // 2
    max_exact = half // 2
    large = jnp.full(n.shape, max_exact, jnp.int32)
    for t in _T5_STEPS:
        large = large + (n >= t).astype(jnp.int32)
    bucket = jnp.where(rel > 0, half, 0) + jnp.where(n < max_exact, n, large)
    for h in range(N_Q_HEADS):
        bias = jnp.zeros(n.shape, F32)
        for b in range(N_BUCKETS):
            bias = jnp.where(bucket == b, tab_ref[b, h], bias)
        o_ref[h] = jnp.where(n <= WINDOW, bias, NEG_INF)


def _bias_call(rel_table, sink):
    return pl.pallas_call(
        _bias_kernel,
        out_shape=(jax.ShapeDtypeStruct((N_Q_HEADS, BLOCK, 3 * BLOCK), F32),
                   jax.ShapeDtypeStruct((N_Q_HEADS * BLOCK, LANES), F32)),
        in_specs=[pl.BlockSpec(memory_space=pltpu.SMEM), pl.BlockSpec(memory_space=pltpu.SMEM)],
        name="t5_bias_tile",
    )(rel_table, sink)


def _mixer_ab_kernel(tq, tpb, x_ref, q_ref, kc_ref, kp_ref, kn_ref, vc_ref, vp_ref, vn_ref,
                     uc_ref, up_ref, un_ref, bias_ref, sink_ref, cw_ref, cb_ref, wo_ref, gate_ref,
                     o_ref, kbuf, vbuf, qs, att):
    i = pl.program_id(0)
    first = (i % tpb) == 0
    last = (i % tpb) == tpb - 1
    nblk = tq // BLOCK
    grows = GQA_GROUP * BLOCK
    kbuf[:, 0:BLOCK] = kp_ref[0]
    kbuf[:, BLOCK:BLOCK + tq] = kc_ref[0]
    kbuf[:, BLOCK + tq:] = kn_ref[0]
    vbuf[0:BLOCK] = vp_ref[...]
    vbuf[BLOCK:BLOCK + tq] = vc_ref[...]
    vbuf[BLOCK + tq:] = vn_ref[...]
    for n in range(nblk):
        for h in range(N_Q_HEADS):
            g, j = divmod(h, GQA_GROUP)
            qs[g, n, j * BLOCK:(j + 1) * BLOCK, :] = q_ref[n * BLOCK:(n + 1) * BLOCK, h * HEAD_DIM:(h + 1) * HEAD_DIM]
    col = lax.broadcasted_iota(jnp.int32, (grows, 3 * BLOCK), 1)
    for n in range(nblk):
        keys = slice(n * BLOCK, (n + 3) * BLOCK)
        for g in range(N_KV_HEADS):
            s = jnp.dot(qs[g, n], kbuf[g * HEAD_DIM:(g + 1) * HEAD_DIM, keys], preferred_element_type=F32)
            s = s + bias_ref[g * GQA_GROUP:(g + 1) * GQA_GROUP].reshape(grows, 3 * BLOCK)
            if n == 0:
                s = jnp.where(jnp.logical_and(first, col < BLOCK), NEG_INF, s)
            if n == nblk - 1:
                s = jnp.where(jnp.logical_and(last, col >= 2 * BLOCK), NEG_INF, s)
            sk = sink_ref[g * grows:(g + 1) * grows, :]
            mx = jnp.maximum(jnp.broadcast_to(jnp.max(s, axis=-1, keepdims=True), sk.shape), sk)
            p = jnp.exp(s - jnp.tile(mx, (1, 3)))
            pv = jnp.dot(p.astype(BF16), vbuf[keys, g * LANES:(g + 1) * LANES], preferred_element_type=F32)
            o = pv / (pltpu.roll(pv, HEAD_DIM, 1) + jnp.exp(sk - mx))
            for j in range(GQA_GROUP):
                h = g * GQA_GROUP + j
                att[n * BLOCK:(n + 1) * BLOCK, h * HEAD_DIM:(h + 1) * HEAD_DIM] = o[j * BLOCK:(j + 1) * BLOCK, 0:HEAD_DIM]

    cw = cw_ref[...]
    gb = uc_ref[:, 0:CONV_WIDTH]
    p = uc_ref[:, CONV_WIDTH:2 * CONV_WIDTH] * uc_ref[:, 2 * CONV_WIDTH:]
    p_before = jnp.where(first, 0.0, up_ref[7:8, CONV_WIDTH:2 * CONV_WIDTH] * up_ref[7:8, 2 * CONV_WIDTH:])
    p_after = jnp.where(last, 0.0, un_ref[0:1, CONV_WIDTH:2 * CONV_WIDTH] * un_ref[0:1, 2 * CONV_WIDTH:])
    r = lax.broadcasted_iota(jnp.int32, p.shape, 0)
    pm1 = jnp.where(r == 0, p_before, pltpu.roll(p, 1, 0))
    pp1 = jnp.where(r == tq - 1, p_after, pltpu.roll(p, tq - 1, 0))
    conv = gb * (cw[0:1] * pm1 + cw[1:2] * p + cw[2:3] * pp1 + cb_ref[...])

    y = jnp.dot(att[...].astype(BF16), wo_ref[0:ATTN_WIDTH, :], preferred_element_type=F32)
    y = y + jnp.dot(conv.astype(BF16), wo_ref[ATTN_WIDTH:, :], preferred_element_type=F32)
    o_ref[...] = x_ref[...] + gate_ref[0] * y


def _mixer_ab_call(x, q, kt, v1, u, bias, sink_rows, conv_w, conv_b, w_out, gate, rows_per_batch):
    m, d = x.shape
    tq = 512
    tpb = rows_per_batch // tq
    r = tq // BLOCK
    nb = m // BLOCK
    n8 = m // 8
    cur = lambda w: pl.BlockSpec((tq, w), lambda i: (i, 0))
    prev_of = lambda i: jnp.maximum(i * r - 1, 0)
    next_of = lambda i: jnp.minimum((i + 1) * r, nb - 1)
    v_prev = pl.BlockSpec((BLOCK, 2 * KV_WIDTH), lambda i: (prev_of(i), 0))
    v_next = pl.BlockSpec((BLOCK, 2 * KV_WIDTH), lambda i: (next_of(i), 0))
    tk = kt.shape[2]
    k_cur = pl.BlockSpec((1, KV_WIDTH, tq), lambda i: (i // (tk // tq), 0, i % (tk // tq)))
    kpb = tk // BLOCK
    k_prev = pl.BlockSpec((1, KV_WIDTH, BLOCK), lambda i: (prev_of(i) // kpb, 0, prev_of(i) % kpb))
    k_next = pl.BlockSpec((1, KV_WIDTH, BLOCK), lambda i: (next_of(i) // kpb, 0, next_of(i) % kpb))
    uw = 3 * CONV_WIDTH
    return pl.pallas_call(
        functools.partial(_mixer_ab_kernel, tq, tpb),
        out_shape=jax.ShapeDtypeStruct((m, d), F32),
        grid=(m // tq,),
        in_specs=[
            cur(d), cur(ATTN_WIDTH),
            k_cur, k_prev, k_next,
            cur(2 * KV_WIDTH), v_prev, v_next,
            cur(uw),
            pl.BlockSpec((8, uw), lambda i: (jnp.maximum(i * (tq // 8) - 1, 0), 0)),
            pl.BlockSpec((8, uw), lambda i: (jnp.minimum((i + 1) * (tq // 8), n8 - 1), 0)),
            _resident((N_Q_HEADS, BLOCK, 3 * BLOCK)),
            _resident((N_Q_HEADS * BLOCK, LANES)),
            _resident((3, CONV_WIDTH)), _resident((1, CONV_WIDTH)),
            _resident((d, d)),
            pl.BlockSpec((1, 1, d), lambda i: (i // tpb, 0, 0)),
        ],
        out_specs=cur(d),
        scratch_shapes=[
            pltpu.VMEM((KV_WIDTH, tq + 2 * BLOCK), BF16),
            pltpu.VMEM((tq + 2 * BLOCK, 2 * KV_WIDTH), BF16),
            pltpu.VMEM((N_KV_HEADS, r, GQA_GROUP * BLOCK, HEAD_DIM), BF16),
            pltpu.VMEM((tq, ATTN_WIDTH), F32),
        ],
        compiler_params=_cparams(("arbitrary",)),
        name="windowed_attn_shortconv_outproj",
    )(x, q, kt, kt, kt, v1, v1, v1, u, u, u, bias, sink_rows, conv_w, conv_b, w_out, gate)


def _key_chunk(seq):
    return min(seq, 1024)


def _dense_attn_kernel(tq, tk, q_ref, k_ref, v_ref, o_ref, qs, s_bufs, p_bufs, rmax_bufs, alpha_bufs,
                       m_ref, acc_ref):
    nc = k_ref.shape[0]
    groups = range(N_KV_HEADS)
    for h in range(N_Q_HEADS):
        g, j = divmod(h, GQA_GROUP)
        qs[g, j * tq:(j + 1) * tq, :] = q_ref[:, h * HEAD_DIM:(h + 1) * HEAD_DIM]
    for g in groups:
        m_ref[g][...] = jnp.full(m_ref[g].shape, -jnp.inf, F32)
        acc_ref[g][...] = jnp.zeros(acc_ref[g].shape, F32)

    def scores(g, c, par):
        s = jnp.dot(qs[g], k_ref[c, g * HEAD_DIM:(g + 1) * HEAD_DIM, :], preferred_element_type=F32)
        s_bufs[g][par][...] = s
        rmax_bufs[g][par][...] = jnp.broadcast_to(jnp.max(s, axis=-1, keepdims=True), rmax_bufs[g][par].shape)

    def softmax(g, par):
        m_old = m_ref[g][...]
        m_new = jnp.maximum(m_old, rmax_bufs[g][par][...])
        alpha_bufs[g][par][...] = jnp.exp(m_old - m_new)
        m_ref[g][...] = m_new
        p = jnp.exp(s_bufs[g][par][...] - jnp.tile(m_new, (1, tk // LANES)))
        p_bufs[g][par][...] = p.astype(BF16)

    def weighted_values(g, c, par):
        start = c * tk if isinstance(c, int) else pl.multiple_of(c * tk, tk)
        vc = v_ref[pl.ds(start, tk), g * LANES:(g + 1) * LANES]
        pv = jnp.dot(p_bufs[g][par][...], vc, preferred_element_type=F32)
        acc_ref[g][...] = alpha_bufs[g][par][...] * acc_ref[g][...] + pv

    def step(t, par):
        static = isinstance(t, int)
        for g in groups:
            if not static or t + 1 < nc:
                softmax(g, 1 - par)
        for g in groups:
            weighted_values(g, t, par)
            if not static or t + 2 < nc:
                scores(g, t + 2, par)

    for g in groups:
        scores(g, 0, 0)
    for g in groups:
        if nc > 1:
            scores(g, 1, 1)
        softmax(g, 0)

    n_pairs = max(nc - 2, 0) // 2

    def steady(tt, carry):
        step(2 * tt, 0)
        step(2 * tt + 1, 1)
        return carry

    lax.fori_loop(0, n_pairs, steady, 0)
    for t in range(2 * n_pairs, nc):
        step(t, t % 2)

    for g in groups:
        acc = acc_ref[g][...]
        o = acc / pltpu.roll(acc, HEAD_DIM, 1)
        for j in range(GQA_GROUP):
            h = g * GQA_GROUP + j
            o_ref[:, h * HEAD_DIM:(h + 1) * HEAD_DIM] = o[j * tq:(j + 1) * tq, 0:HEAD_DIM].astype(o_ref.dtype)


def _dense_attn_call(q, kt, v1, bsz, seq):
    tq, tk = 128, _key_chunk(seq)
    nq = seq // tq
    nc = seq // tk
    rows = GQA_GROUP * tq
    per_group = lambda shape, dtype: [pltpu.VMEM(shape, dtype) for _ in range(N_KV_HEADS)]
    pair = lambda shape, dtype: [[pltpu.VMEM(shape, dtype), pltpu.VMEM(shape, dtype)]
                                 for _ in range(N_KV_HEADS)]
    return pl.pallas_call(
        functools.partial(_dense_attn_kernel, tq, tk),
        out_shape=jax.ShapeDtypeStruct((bsz * seq, ATTN_WIDTH), BF16),
        grid=(bsz, nq),
        in_specs=[
            pl.BlockSpec((tq, ATTN_WIDTH), lambda b, i: (b * nq + i, 0)),
            pl.BlockSpec((nc, KV_WIDTH, tk), lambda b, i: (b, 0, 0)),
            pl.BlockSpec((seq, 2 * KV_WIDTH), lambda b, i: (b, 0)),
        ],
        out_specs=pl.BlockSpec((tq, ATTN_WIDTH), lambda b, i: (b * nq + i, 0)),
        scratch_shapes=[
            pltpu.VMEM((N_KV_HEADS, rows, HEAD_DIM), BF16),
            pair((rows, tk), F32), pair((rows, tk), BF16), pair((rows, LANES), F32), pair((rows, LANES), F32),
            per_group((rows, LANES), F32), per_group((rows, LANES), F32),
        ],
        compiler_params=_cparams(("arbitrary", "arbitrary")),
        name="dense_gqa_attention",
    )(q, kt, v1)


def _hyena_pre_kernel(tm, tpb, uc_ref, up_ref, un_ref, cw_ref, cb_ref, x0_ref, z_ref):
    i = pl.program_id(0)
    first = (i % tpb) == 0
    last = (i % tpb) == tpb - 1
    u = uc_ref[...]
    before = jnp.where(first, 0.0, up_ref[7:8, :])
    after = jnp.where(last, 0.0, un_ref[0:1, :])
    r = lax.broadcasted_iota(jnp.int32, u.shape, 0)
    um1 = jnp.where(r == 0, before, pltpu.roll(u, 1, 0))
    up1 = jnp.where(r == tm - 1, after, pltpu.roll(u, tm - 1, 0))
    cw = cw_ref[...]
    t = cw[0:1] * um1 + cw[1:2] * u + cw[2:3] * up1 + cb_ref[...]
    x0_ref[...] = t[:, 0:CONV_WIDTH]
    z_ref[...] = t[:, CONV_WIDTH:2 * CONV_WIDTH] * t[:, 2 * CONV_WIDTH:]


def _hyena_pre_call(u, conv_w, conv_b, rows_per_batch):
    m, uw = u.shape
    tm = 512
    tpb = rows_per_batch // tm
    n8 = m // 8
    out = pl.BlockSpec((tm, CONV_WIDTH), lambda i: (i, 0))
    return pl.pallas_call(
        functools.partial(_hyena_pre_kernel, tm, tpb),
        out_shape=(jax.ShapeDtypeStruct((m, CONV_WIDTH), F32), jax.ShapeDtypeStruct((m, CONV_WIDTH), F32)),
        grid=(m // tm,),
        in_specs=[
            pl.BlockSpec((tm, uw), lambda i: (i, 0)),
            pl.BlockSpec((8, uw), lambda i: (jnp.maximum(i * (tm // 8) - 1, 0), 0)),
            pl.BlockSpec((8, uw), lambda i: (jnp.minimum((i + 1) * (tm // 8), n8 - 1), 0)),
            _resident((3, uw)), _resident((1, uw)),
        ],
        out_specs=(out, out),
        compiler_params=_cparams(("arbitrary",)),
        name="hyena_short_conv",
    )(u, u, u, conv_w, conv_b)


def _hp_dot(a, b):
    return jnp.dot(a, b, preferred_element_type=F32, precision=lax.Precision.HIGHEST)


def _filter_trunk_kernel(z_ref, w1_ref, b1_ref, w2_ref, b2_ref, w3_ref, b3_ref, fq_ref, o_ref):
    fq = fq_ref[...]
    h = jnp.sin(fq * (_hp_dot(w1_ref[...], z_ref[...]) + b1_ref[...]))
    h = jnp.sin(fq * (_hp_dot(w2_ref[...], h) + b2_ref[...]))
    h = jnp.sin(fq * (_hp_dot(w3_ref[...], h) + b3_ref[...]))
    o_ref[...] = h.T


def _filter_trunk_call(zfeat_t, w1, b1, w2, b2, w3, b3, freq):
    rows = zfeat_t.shape[1]
    tr = min(rows, 2048)
    fw = HYENA_FILTER_WIDTH
    col = lambda v: v.reshape(fw, 1)
    return pl.pallas_call(
        _filter_trunk_kernel,
        out_shape=jax.ShapeDtypeStruct((rows, fw), F32),
        grid=(rows // tr,),
        in_specs=[
            pl.BlockSpec((HYENA_EMB_PAD, tr), lambda i: (0, i)),
            _resident((fw, HYENA_EMB_PAD)), _resident((fw, 1)),
            _resident((fw, fw)), _resident((fw, 1)),
            _resident((fw, fw)), _resident((fw, 1)),
            _resident((fw, 1)),
        ],
        out_specs=pl.BlockSpec((tr, fw), lambda i: (i, 0)),
        compiler_params=_cparams(("arbitrary",)),
        name="hyena_filter_trunk",
    )(zfeat_t, w1.T, col(b1), w2.T, col(b2), w3.T, col(b3), col(freq))


def _dft_stage1(src_ref, tab_ref, re_ref, im_ref, n_rows, k1p):
    def body(n2, carry):
        xs = src_ref[pl.ds(n2, n_rows, stride=DFT_N2), :].astype(BF16)
        res = jnp.dot(tab_ref[n2], xs, preferred_element_type=F32)
        re_ref[pl.ds(n2, k1p, stride=DFT_N2), :] = res[0:k1p]
        im_ref[pl.ds(n2, k1p, stride=DFT_N2), :] = res[k1p:]
        return carry

    lax.fori_loop(0, DFT_N2, body, 0, unroll=8)


def _filter_spec_kernel(seq, k1p, h_ref, w4f_ref, w4b_ref, dl_ref, e1_ref, f2_ref,
                        kr_ref, ki_ref, kfull):
    n = 2 * seq
    chunk = min(seq, 1024)
    dl = dl_ref[...]

    def fill(c, ss):
        r0 = pl.multiple_of(c * chunk, chunk)
        hc = h_ref[pl.ds(r0, chunk), :].astype(BF16)
        fwd = jnp.dot(hc, w4f_ref[...], preferred_element_type=F32)
        bwd = jnp.dot(hc, w4b_ref[...], preferred_element_type=F32)
        rows = r0 + lax.broadcasted_iota(jnp.int32, (chunk, 1), 0)
        lag = jnp.where(rows < seq, rows, n - rows).astype(F32)
        t = lag * (1.0 / (seq - 1))
        val = jnp.where(rows < seq, fwd, bwd) * jnp.exp(-t * dl)
        val = jnp.where(rows == seq, 0.0, val)
        kfull[pl.ds(r0, chunk), :] = val
        return ss + jnp.sum(val * val, axis=0, keepdims=True)

    ss = lax.fori_loop(0, n // chunk, fill, jnp.zeros((1, dl.shape[1]), F32))
    norm = lax.rsqrt(ss + EPS)

    _dft_stage1(kfull, e1_ref, kr_ref, ki_ref, n // DFT_N2, k1p)

    f2 = f2_ref[...]

    def stage2(a, carry):
        rows = pl.ds(pl.multiple_of(a * DFT_N2, DFT_N2), DFT_N2)
        slab = jnp.concatenate([kr_ref[rows, :], ki_ref[rows, :]], axis=0).astype(BF16)
        xf = jnp.dot(f2, slab, preferred_element_type=F32)
        kr_ref[rows, :] = xf[0:DFT_N2] * norm
        ki_ref[rows, :] = xf[DFT_N2:] * norm
        return carry

    lax.fori_loop(0, seq // DFT_N2 + 1, stage2, 0, unroll=8)


def _filter_spec_call(h3, w4, deltas, e1f, f2f, seq):
    n = 2 * seq
    k1p = e1f.shape[1] // 2
    ct = LANES
    nct = CONV_WIDTH // ct
    fw = HYENA_FILTER_WIDTH
    spec_rows = k1p * DFT_N2
    out = pl.BlockSpec((spec_rows, ct), lambda j: (0, j))
    return pl.pallas_call(
        functools.partial(_filter_spec_kernel, seq, k1p),
        out_shape=(jax.ShapeDtypeStruct((spec_rows, CONV_WIDTH), F32),
                   jax.ShapeDtypeStruct((spec_rows, CONV_WIDTH), F32)),
        grid=(nct,),
        in_specs=[
            _resident((n, fw)),
            pl.BlockSpec((fw, ct), lambda j: (0, j)),
            pl.BlockSpec((fw, ct), lambda j: (0, nct + j)),
            pl.BlockSpec((1, ct), lambda j: (0, j)),
            _resident(e1f.shape),
            _resident((2 * DFT_N2, 2 * DFT_N2)),
        ],
        out_specs=(out, out),
        scratch_shapes=[pltpu.VMEM((n, ct), F32)],
        compiler_params=_cparams(("arbitrary",)),
        name="hyena_filter_spectrum",
    )(h3, w4, w4, deltas, e1f, f2f)


def _long_conv_kernel(seq, k1p, z_ref, x0_ref, kr_ref, ki_ref, skip_ref, e1_ref, f2f_ref, f2i_ref,
                      einv_ref, o_ref, br, bi):
    n1h = seq // DFT_N2
    _dft_stage1(z_ref, e1_ref, br, bi, n1h, k1p)

    f2f = f2f_ref[...]
    f2i = f2i_ref[...]

    def spectral(a, carry):
        r0 = pl.multiple_of(a * DFT_N2, DFT_N2)
        rows = pl.ds(r0, DFT_N2)
        slab = jnp.concatenate([br[rows, :], bi[rows, :]], axis=0).astype(BF16)
        xf = jnp.dot(f2f, slab, preferred_element_type=F32)
        xr, xi = xf[0:DFT_N2], xf[DFT_N2:]
        kr, ki = kr_ref[rows, :], ki_ref[rows, :]
        y = jnp.concatenate([xr * kr - xi * ki, xr * ki + xi * kr], axis=0).astype(BF16)
        d = jnp.dot(f2i, y, preferred_element_type=F32)
        br[rows, :] = d[0:DFT_N2]
        bi[rows, :] = d[DFT_N2:]
        return carry

    lax.fori_loop(0, n1h + 1, spectral, 0, unroll=8)

    skip = skip_ref[...]

    def synth(n2, carry):
        ds_spec = pl.ds(n2, k1p, stride=DFT_N2)
        d = jnp.concatenate([br[ds_spec, :], bi[ds_spec, :]], axis=0).astype(BF16)
        o_ref[pl.ds(n2, n1h, stride=DFT_N2), :] = jnp.dot(einv_ref[n2], d, preferred_element_type=F32)
        return carry

    lax.fori_loop(0, DFT_N2, synth, 0, unroll=8)

    chunk = min(seq, 512)

    def gate(c, carry):
        rows = pl.ds(pl.multiple_of(c * chunk, chunk), chunk)
        o_ref[rows, :] = x0_ref[rows, :] * (o_ref[rows, :] + skip * z_ref[rows, :])
        return carry

    lax.fori_loop(0, seq // chunk, gate, 0)


def _long_conv_call(z, x0, kr, ki, skip, e1, f2f, f2i, einv, bsz, seq):
    ct = LANES
    nct = CONV_WIDTH // ct
    k1p = e1.shape[1] // 2
    spec_rows = k1p * DFT_N2
    big = lambda: pl.BlockSpec((seq, ct), lambda j, b: (b, j), pipeline_mode=pl.Buffered(1))
    spec = lambda: pl.BlockSpec((spec_rows, ct), lambda j, b: (0, j), pipeline_mode=pl.Buffered(1))
    return pl.pallas_call(
        functools.partial(_long_conv_kernel, seq, k1p),
        out_shape=jax.ShapeDtypeStruct((bsz * seq, CONV_WIDTH), F32),
        grid=(nct, bsz),
        in_specs=[
            big(), big(), spec(), spec(),
            pl.BlockSpec((1, ct), lambda j, b: (0, j)),
            _resident(e1.shape),
            _resident((2 * DFT_N2, 2 * DFT_N2)), _resident((2 * DFT_N2, 2 * DFT_N2)),
            _resident(einv.shape),
        ],
        out_specs=pl.BlockSpec((seq, ct), lambda j, b: (b, j)),
        scratch_shapes=[pltpu.VMEM((spec_rows, ct), F32), pltpu.VMEM((spec_rows, ct), F32)],
        compiler_params=_cparams(("arbitrary", "arbitrary")),
        name="hyena_long_conv",
    )(z, x0, kr, ki, skip, e1, f2f, f2i, einv)


def _outproj_kernel(x_ref, att_ref, y_ref, wo_ref, gate_ref, o_ref):
    y = jnp.dot(att_ref[...], wo_ref[0:ATTN_WIDTH, :], preferred_element_type=F32)
    y = y + jnp.dot(y_ref[...].astype(BF16), wo_ref[ATTN_WIDTH:, :], preferred_element_type=F32)
    o_ref[...] = x_ref[...] + gate_ref[0] * y


def _outproj_call(x, att, y, w_out, gate, rows_per_batch):
    m, d = x.shape
    tm = 512
    tpb = rows_per_batch // tm
    row = lambda w: pl.BlockSpec((tm, w), lambda i: (i, 0))
    return pl.pallas_call(
        _outproj_kernel,
        out_shape=jax.ShapeDtypeStruct((m, d), F32),
        grid=(m // tm,),
        in_specs=[row(d), row(ATTN_WIDTH), row(CONV_WIDTH), _resident((d, d)),
                  pl.BlockSpec((1, 1, d), lambda i: (i // tpb, 0, 0))],
        out_specs=row(d),
        compiler_params=_cparams(("arbitrary",)),
        name="mixer_out_projection",
    )(x, att, y, w_out, gate)


def _rope_tables(seq):
    t = jnp.arange(seq)
    row = (t // GRID_W).astype(F32)
    col = (t % GRID_W).astype(F32)
    half = HEAD_DIM // 2
    inv = ROPE_THETA ** (-jnp.arange(0, half, 2, dtype=F32) / half)
    ang = jnp.concatenate([row[:, None] * inv, col[:, None] * inv], axis=-1)
    ang = jnp.repeat(ang, 2, axis=-1)
    sign = jnp.where(jnp.arange(HEAD_DIM) % 2 == 0, -1.0, 1.0).astype(F32)
    reps = LANES // HEAD_DIM
    return jnp.tile(jnp.cos(ang), (1, reps)), jnp.tile(jnp.sin(ang) * sign, (1, reps))


def _dft_tables(seq):
    n = 2 * seq
    n1 = n // DFT_N2
    k1 = n1 // 2 + 1
    k1p = -(-k1 // 8) * 8
    a = jnp.arange(k1p)
    live = (a < k1).astype(F32)
    ang1 = ((a[:, None] * jnp.arange(n1)[None, :]) % n1).astype(F32) * (2.0 * math.pi / n1)
    ang2 = (jnp.arange(DFT_N2)[:, None] * a[None, :]).astype(F32) * (2.0 * math.pi / n)
    c1, s1 = jnp.cos(ang1) * live[:, None], jnp.sin(ang1) * live[:, None]
    c2, s2 = jnp.cos(ang2), jnp.sin(ang2)
    cos_a = c1[None] * c2[:, :, None] - s1[None] * s2[:, :, None]
    sin_a = s1[None] * c2[:, :, None] + c1[None] * s2[:, :, None]
    e1_full = jnp.concatenate([cos_a, -sin_a], axis=1).astype(BF16)
    e1_half = e1_full[:, :, :n1 // 2]
    wgt = jnp.where((a == 0) | (a == n1 // 2), 1.0, 2.0) / n
    c1t, s1t = (c1 * wgt[:, None]).T[:n1 // 2], (s1 * wgt[:, None]).T[:n1 // 2]
    cos_s = c1t[None] * c2[:, None, :] - s1t[None] * s2[:, None, :]
    sin_s = s1t[None] * c2[:, None, :] + c1t[None] * s2[:, None, :]
    einv = jnp.concatenate([cos_s, -sin_s], axis=2).astype(BF16)
    kk = jnp.arange(DFT_N2)
    phi = ((kk[:, None] * kk[None, :]) % DFT_N2).astype(F32) * (2.0 * math.pi / DFT_N2)
    ci, si = jnp.cos(phi), jnp.sin(phi)
    f2f = jnp.block([[ci, si], [-si, ci]]).astype(BF16)
    f2i = jnp.block([[ci, -si], [si, ci]]).astype(BF16)
    return e1_full, e1_half, einv, f2f, f2i


def _filter_features(seq):
    t = jnp.linspace(0.0, 1.0, seq, dtype=F32)[:, None]
    bands = (HYENA_EMB - 1) // 2
    w = (2.0 * math.pi / seq) * jnp.arange(seq, dtype=F32)[:, None]
    fr = jnp.linspace(1e-4, bands - 1, bands, dtype=F32)[None, :]
    z = jnp.concatenate([t, jnp.cos(fr * w), -jnp.sin(fr * w)], axis=-1)
    back = jnp.concatenate([z[:1], z[:0:-1]], axis=0)
    feats = jnp.concatenate([z, back], axis=0)
    return jnp.pad(feats, ((0, 0), (0, HYENA_EMB_PAD - HYENA_EMB)))


def kernel(x, c, rel_table, norm_g, w_mod, b_mod, w_in, w_out, ffn_w1, ffn_w3, ffn_w2, a_qk_g, a_sink,
           b_conv_w, b_conv_b, c_qk_g, d_conv_w, d_conv_b, d_f_w1, d_f_b1, d_f_w2, d_f_b2, d_f_w3,
           d_f_b3, d_f_w4, d_f_freq, d_skip):
    bsz, seq, d = x.shape
    depth = w_mod.shape[0]
    m_rows = bsz * seq
    mod =_mod_call(c, w_mod, b_mod).reshape(depth, bsz, 3, 3, 1, d)
    e_blk = jnp.kron(jnp.eye(N_Q_HEADS, dtype=F32), jnp.ones((HEAD_DIM, HEAD_DIM), F32)).astype(BF16)

    def ffn(xf, l, which, sub):
        w1, w3, w2 = (w[l, which].astype(BF16) for w in (ffn_w1, ffn_w3, ffn_w2))
        return _ffn_call(xf, norm_g[l, sub][None], mod[l, :, sub, 0], mod[l, :, sub, 1], mod[l, :, sub, 2],
                         w1, w3, w2, seq)

    xf = x.reshape(m_rows, d)
    for l in range(depth):
        j = l // 2
        xf = ffn(xf, l, 0, 0)
        even = l % 2 == 0
        qk_g = a_qk_g[j] if even else c_qk_g[j]
        q, k, v, u = _inproj_call(
            xf, norm_g[l, 1][None], mod[l, :, 1, 0], mod[l, :, 1, 1], w_in[l].astype(BF16),
            jnp.tile(qk_g[0], N_Q_HEADS)[None], jnp.tile(qk_g[1], N_KV_HEADS)[None], e_blk,
            None if even else _rope_tables(seq), seq)
        wo = w_out[l].astype(BF16)
        gate = mod[l, :, 1, 2]
        if even:
            bias, sink_rows = _bias_call(rel_table, a_sink[j])
            xf = _mixer_ab_call(xf, q, k, v, u, bias, sink_rows, b_conv_w[j], b_conv_b[j][None], wo, gate, seq)
        else:
            att = _dense_attn_call(q, k, v, bsz, seq)
            x0, z = _hyena_pre_call(u, d_conv_w[j], d_conv_b[j][None], seq)
            e1_full, e1_half, einv, f2f, f2i = _dft_tables(seq)
            fw1 = jnp.pad(d_f_w1[j], ((0, HYENA_EMB_PAD - HYENA_EMB), (0, 0)))
            h3 = _filter_trunk_call(_filter_features(seq).T, fw1, d_f_b1[j], d_f_w2[j], d_f_b2[j],
                                    d_f_w3[j], d_f_b3[j], d_f_freq[j])
            deltas = jnp.abs(jnp.linspace(HYENA_MIN_DECAY, HYENA_MAX_DECAY, CONV_WIDTH, dtype=F32))[None]
            kr, ki = _filter_spec_call(h3, d_f_w4[j].astype(BF16), deltas, e1_full, f2f, seq)
            y = _long_conv_call(z, x0, kr, ki, d_skip[j][None], e1_half, f2f, f2i, einv, bsz, seq)
            xf = _outproj_call(xf, att, y, wo, gate, seq)
        xf = ffn(xf, l, 1, 2)
    return xf.reshape(bsz, seq, d)
```

```python
import functools
import math

import jax
import jax.numpy as jnp
import numpy as np
from jax import lax
from jax.experimental import pallas as pl
from jax.experimental.pallas import tpu as pltpu

D_MODEL = 1024
HEAD_DIM = 64
N_Q_HEADS = 8
N_KV_HEADS = 2
GQA_GROUP = N_Q_HEADS // N_KV_HEADS
ATTN_WIDTH = N_Q_HEADS * HEAD_DIM
KV_WIDTH = N_KV_HEADS * HEAD_DIM
QKV_COLS = ATTN_WIDTH + 2 * KV_WIDTH
CONV_WIDTH = D_MODEL - ATTN_WIDTH
IN_COLS = QKV_COLS + 3 * CONV_WIDTH
D_FF = 2752
BLOCK = 128
WINDOW = 128
N_BUCKETS = 32
MAX_DISTANCE = 128
GRID_W = 64
ROPE_THETA = 10000.0
HYENA_EMB = 33
HYENA_FILTER_WIDTH = 64
HYENA_EMB_PAD = 64
HYENA_MIN_DECAY = math.log(1e-2) / 0.3
HYENA_MAX_DECAY = math.log(1e-2) / 1.5
EPS = 1e-6
NEG_INF = -1e30

LANES = 128
HALO = 16
MXU_DIM = 256
FF_CHUNK = MXU_DIM
VMEM_LIMIT = 56 * 1024 * 1024
DFT_N2 = 128

BF16 = jnp.bfloat16
F32 = jnp.float32


def _cparams(sem):
    return pltpu.CompilerParams(dimension_semantics=sem, vmem_limit_bytes=VMEM_LIMIT)


def _resident(shape):
    nd = len(shape)
    return pl.BlockSpec(shape, lambda *_: (0,) * nd, pipeline_mode=pl.Buffered(1))


def _layer_slice(shape, layer):
    return pl.BlockSpec((None,) + shape, lambda *_: (layer, 0, 0), pipeline_mode=pl.Buffered(1))


def _adaln(x, g, scale, shift):
    y = x * lax.rsqrt(jnp.mean(x * x, axis=-1, keepdims=True) + EPS) * g
    return y * (1.0 + scale) + shift


def _mod_kernel(c_ref, w_ref, b_ref, o_ref):
    w = w_ref[0]
    for b in range(c_ref.shape[0]):
        cc = c_ref[b]
        cond = cc * jax.nn.sigmoid(cc)
        o_ref[0, b:b + 1, :] = jnp.sum(w * cond, axis=0, keepdims=True) + b_ref[0]


def _mod_call(c, w_mod, b_mod):
    depth, d, n = w_mod.shape
    bsz = c.shape[0]
    tn = 512
    return pl.pallas_call(
        _mod_kernel,
        out_shape=jax.ShapeDtypeStruct((depth, bsz, n), F32),
        grid=(depth, n // tn),
        in_specs=[
            pl.BlockSpec((bsz, d, 1), lambda l, j: (0, 0, 0)),
            pl.BlockSpec((1, d, tn), lambda l, j: (l, 0, j)),
            pl.BlockSpec((1, 1, tn), lambda l, j: (l, 0, j)),
        ],
        out_specs=pl.BlockSpec((1, bsz, tn), lambda l, j: (l, 0, j)),
        compiler_params=_cparams(("arbitrary", "arbitrary")),
        name="adaln_modulation",
    )(c.reshape(bsz, d, 1), w_mod, b_mod.reshape(depth, 1, n))


def _ffn_kernel(x_ref, g_ref, shift_ref, scale_ref, gate_ref, w1_ref, w3_ref, w2_ref, o_ref):
    x = x_ref[...]
    h = _adaln(x, g_ref[...], scale_ref[0], shift_ref[0]).astype(BF16)
    acc = jnp.zeros(x.shape, F32)
    for c0 in range(0, D_FF, FF_CHUNK):
        cols = slice(c0, min(c0 + FF_CHUNK, D_FF))
        a = jnp.dot(h, w1_ref[:, cols], preferred_element_type=F32)
        b = jnp.dot(h, w3_ref[:, cols], preferred_element_type=F32)
        act = (a * jax.nn.sigmoid(a) * b).astype(BF16)
        acc = acc + jnp.dot(act, w2_ref[cols, :], preferred_element_type=F32)
    o_ref[...] = x + (0.5 * gate_ref[0]) * acc


def _ffn_call(x, g, shift, scale, gate, w1, w3, w2, layer, which, rows_per_batch):
    m, d = x.shape
    tm = 512
    tpb = rows_per_batch // tm
    vec = pl.BlockSpec((1, 1, d), lambda i: (i // tpb, 0, 0))
    weight = lambda r, c: pl.BlockSpec((None, None, r, c), lambda i: (layer, which, 0, 0),
                                       pipeline_mode=pl.Buffered(1))
    return pl.pallas_call(
        _ffn_kernel,
        out_shape=jax.ShapeDtypeStruct((m, d), F32),
        grid=(m // tm,),
        in_specs=[
            pl.BlockSpec((tm, d), lambda i: (i, 0)),
            _resident((1, d)),
            vec, vec, vec,
            weight(d, D_FF), weight(d, D_FF), weight(D_FF, d),
        ],
        out_specs=pl.BlockSpec((tm, d), lambda i: (i, 0)),
        compiler_params=_cparams(("arbitrary",)),
        name="adaln_swiglu_ffn",
    )(x, g, shift, scale, gate, w1, w3, w2)


def _group_norm_scale(v, e_ref):
    sq = v * v
    hi = sq.astype(BF16)
    lo = (sq - hi.astype(F32)).astype(BF16)
    w = min(v.shape[1], MXU_DIM)
    e = e_ref[...]
    ss = jnp.concatenate(
        [jnp.dot(hi[:, c:c + w], e[0:w, 0:w], preferred_element_type=F32)
         + jnp.dot(lo[:, c:c + w], e[0:w, 0:w], preferred_element_type=F32) for c in range(0, v.shape[1], w)],
        axis=1)
    return lax.rsqrt(ss * (1.0 / HEAD_DIM) + EPS)


def _rope128(v, cos, sin_signed, even_lane):
    partner = jnp.where(even_lane, pltpu.roll(v, LANES - 1, 1), pltpu.roll(v, 1, 1))
    return v * cos + partner * sin_signed


def _short_conv_rows(u, before, after, cw, cb):
    rows = u.shape[0]
    r = lax.broadcasted_iota(jnp.int32, u.shape, 0)
    um1 = jnp.where(r == 0, before, pltpu.roll(u, 1, 0))
    up1 = jnp.where(r == rows - 1, after, pltpu.roll(u, rows - 1, 0))
    return cw[0:1] * um1 + cw[1:2] * u + cw[2:3] * up1 + cb


def _inproj_kernel(rope, tpb, x_ref, g_ref, shift_ref, scale_ref, w_ref, qg_ref, kg_ref, e_ref, *rest):
    if rope:
        cos_ref, sin_ref, xp_ref, xn_ref, cw_ref, cb_ref, q_ref, k_ref, v_ref, x0_ref, z_ref = rest
    else:
        q_ref, k_ref, v_ref, u_ref = rest
    h = _adaln(x_ref[...], g_ref[...], scale_ref[0], shift_ref[0]).astype(BF16)
    q = jnp.dot(h, w_ref[:, 0:ATTN_WIDTH], preferred_element_type=F32)
    k = jnp.dot(h, w_ref[:, ATTN_WIDTH:ATTN_WIDTH + KV_WIDTH], preferred_element_type=F32)
    q = q * _group_norm_scale(q, e_ref) * qg_ref[...]
    k = k * _group_norm_scale(k, e_ref) * kg_ref[...]
    if rope:
        cos = cos_ref[...]
        sin = sin_ref[...]
        even = (lax.broadcasted_iota(jnp.int32, cos.shape, 1) % 2) == 0
        q = jnp.concatenate(
            [_rope128(q[:, j * LANES:(j + 1) * LANES], cos, sin, even) for j in range(ATTN_WIDTH // LANES)],
            axis=1)
        k = _rope128(k, cos, sin, even)
    q_ref[...] = (q * HEAD_DIM ** -0.5).astype(BF16)
    v = jnp.dot(h, w_ref[:, ATTN_WIDTH + KV_WIDTH:QKV_COLS], preferred_element_type=F32)
    k_ref[0] = k.T.astype(BF16)
    low = lax.broadcasted_iota(jnp.int32, v.shape, 1) < HEAD_DIM
    v_ref[...] = jnp.concatenate(
        [jnp.where(low, v, 1.0), jnp.where(low, pltpu.roll(v, HEAD_DIM, 1), 1.0)], axis=1).astype(BF16)
    if not rope:
        u_ref[...] = jnp.dot(h, w_ref[:, QKV_COLS:IN_COLS], preferred_element_type=F32)
        return
    i = pl.program_id(0)
    h_prev = _adaln(xp_ref[...], g_ref[...], scale_ref[0], shift_ref[0]).astype(BF16)
    h_next = _adaln(xn_ref[...], g_ref[...], scale_ref[0], shift_ref[0]).astype(BF16)
    u_ext = jnp.dot(jnp.concatenate([h_prev, h, h_next], axis=0), w_ref[:, QKV_COLS:IN_COLS],
                    preferred_element_type=F32)
    tm = h.shape[0]
    before = jnp.where((i % tpb) == 0, 0.0, u_ext[HALO - 1:HALO])
    after = jnp.where((i % tpb) == tpb - 1, 0.0, u_ext[HALO + tm:HALO + tm + 1])
    t = _short_conv_rows(u_ext[HALO:HALO + tm], before, after, cw_ref[...], cb_ref[...])
    x0_ref[...] = t[:, 0:CONV_WIDTH]
    z_ref[...] = t[:, CONV_WIDTH:2 * CONV_WIDTH] * t[:, 2 * CONV_WIDTH:]


def _inproj_call(x, g, shift, scale, w_in, layer, qg, kg, e, rows_per_batch, hyena=None):
    m, d = x.shape
    tm = 512
    tpb = rows_per_batch // tm
    vec = pl.BlockSpec((1, 1, d), lambda i: (i // tpb, 0, 0))
    row = lambda w: pl.BlockSpec((tm, w), lambda i: (i, 0))
    in_specs = [
        row(d),
        _resident((1, d)), vec, vec,
        _layer_slice((d, IN_COLS), layer),
        _resident((1, ATTN_WIDTH)), _resident((1, KV_WIDTH)),
        _resident((MXU_DIM, MXU_DIM)),
    ]
    args = [x, g, shift, scale, w_in, qg, kg, e]
    uw = 3 * CONV_WIDTH
    if hyena is not None:
        cos, sin, conv_w, conv_b = hyena
        tab = pl.BlockSpec((tm, LANES), lambda i: (i % tpb, 0))
        hpt = tm // HALO
        last_halo = m // HALO - 1
        in_specs += [
            tab, tab,
            pl.BlockSpec((HALO, d), lambda i: (jnp.maximum(i * hpt - 1, 0), 0)),
            pl.BlockSpec((HALO, d), lambda i: (jnp.minimum((i + 1) * hpt, last_halo), 0)),
            _resident((3, uw)), _resident((1, uw)),
        ]
        args += [cos, sin, x, x, conv_w, conv_b]
        conv_shapes = [jax.ShapeDtypeStruct((m, CONV_WIDTH), F32)] * 2
        conv_specs = [row(CONV_WIDTH)] * 2
    else:
        conv_shapes = [jax.ShapeDtypeStruct((m, uw), F32)]
        conv_specs = [row(uw)]
    tk = _key_chunk(rows_per_batch)
    per = tk // tm
    return pl.pallas_call(
        functools.partial(_inproj_kernel, hyena is not None, tpb),
        out_shape=[
            jax.ShapeDtypeStruct((m, ATTN_WIDTH), BF16),
            jax.ShapeDtypeStruct((m // tk, KV_WIDTH, tk), BF16),
            jax.ShapeDtypeStruct((m, 2 * KV_WIDTH), BF16),
        ] + conv_shapes,
        grid=(m // tm,),
        in_specs=in_specs,
        out_specs=[row(ATTN_WIDTH), pl.BlockSpec((1, KV_WIDTH, tm), lambda i: (i // per, 0, i % per)),
                   row(2 * KV_WIDTH)] + conv_specs,
        compiler_params=_cparams(("arbitrary",)),
        name="adaln_in_projection",
    )(*args)


_T5_STEPS = (12, 16, 23, 32, 46, 64, 91)


def _bias_kernel(tab_ref, sink_ref, o_ref, sink_rows_ref):
    for h in range(N_Q_HEADS):
        sink_rows_ref[h * BLOCK:(h + 1) * BLOCK, :] = jnp.full((BLOCK, LANES), sink_ref[h], F32)
    qi = lax.broadcasted_iota(jnp.int32, (BLOCK, 3 * BLOCK), 0)
    kj = lax.broadcasted_iota(jnp.int32, (BLOCK, 3 * BLOCK), 1)
    rel = kj - BLOCK - qi
    n = jnp.abs(rel)
    half = N_BUCKETS // 2
    max_exact = half // 2
    large = jnp.full(n.shape, max_exact, jnp.int32)
    for t in _T5_STEPS:
        large = large + (n >= t).astype(jnp.int32)
    bucket = jnp.where(rel > 0, half, 0) + jnp.where(n < max_exact, n, large)
    for h in range(N_Q_HEADS):
        bias = jnp.zeros(n.shape, F32)
        for b in range(N_BUCKETS):
            bias = jnp.where(bucket == b, tab_ref[b, h], bias)
        o_ref[h] = jnp.where(n <= WINDOW, bias, NEG_INF)


def _bias_call(rel_table, sink):
    return pl.pallas_call(
        _bias_kernel,
        out_shape=(jax.ShapeDtypeStruct((N_Q_HEADS, BLOCK, 3 * BLOCK), F32),
                   jax.ShapeDtypeStruct((N_Q_HEADS * BLOCK, LANES), F32)),
        in_specs=[pl.BlockSpec(memory_space=pltpu.SMEM), pl.BlockSpec(memory_space=pltpu.SMEM)],
        name="t5_bias_tile",
    )(rel_table, sink)


def _mixer_ab_kernel(tq, tpb, x_ref, q_ref, kc_ref, kp_ref, kn_ref, vc_ref, vp_ref, vn_ref,
                     uc_ref, up_ref, un_ref, bias_ref, sink_ref, cw_ref, cb_ref, wo_ref, gate_ref,
                     o_ref, kbuf, vbuf, qs, att):
    i = pl.program_id(0)
    first = (i % tpb) == 0
    last = (i % tpb) == tpb - 1
    nblk = tq // BLOCK
    grows = GQA_GROUP * BLOCK
    kbuf[:, 0:BLOCK] = kp_ref[0]
    kbuf[:, BLOCK:BLOCK + tq] = kc_ref[0]
    kbuf[:, BLOCK + tq:] = kn_ref[0]
    vbuf[0:BLOCK] = vp_ref[...]
    vbuf[BLOCK:BLOCK + tq] = vc_ref[...]
    vbuf[BLOCK + tq:] = vn_ref[...]
    for n in range(nblk):
        for h in range(N_Q_HEADS):
            g, j = divmod(h, GQA_GROUP)
            qs[g, n, j * BLOCK:(j + 1) * BLOCK, :] = q_ref[n * BLOCK:(n + 1) * BLOCK, h * HEAD_DIM:(h + 1) * HEAD_DIM]
    col = lax.broadcasted_iota(jnp.int32, (grows, 3 * BLOCK), 1)
    for n in range(nblk):
        keys = slice(n * BLOCK, (n + 3) * BLOCK)
        for g in range(N_KV_HEADS):
            s = jnp.dot(qs[g, n], kbuf[g * HEAD_DIM:(g + 1) * HEAD_DIM, keys], preferred_element_type=F32)
            s = s + bias_ref[g * GQA_GROUP:(g + 1) * GQA_GROUP].reshape(grows, 3 * BLOCK)
            if n == 0:
                s = jnp.where(jnp.logical_and(first, col < BLOCK), NEG_INF, s)
            if n == nblk - 1:
                s = jnp.where(jnp.logical_and(last, col >= 2 * BLOCK), NEG_INF, s)
            sk = sink_ref[g * grows:(g + 1) * grows, :]
            mx = jnp.maximum(jnp.broadcast_to(jnp.max(s, axis=-1, keepdims=True), sk.shape), sk)
            p = jnp.exp(s - jnp.tile(mx, (1, 3)))
            pv = jnp.dot(p.astype(BF16), vbuf[keys, g * LANES:(g + 1) * LANES], preferred_element_type=F32)
            o = pv / (pltpu.roll(pv, HEAD_DIM, 1) + jnp.exp(sk - mx))
            for j in range(GQA_GROUP):
                h = g * GQA_GROUP + j
                att[n * BLOCK:(n + 1) * BLOCK, h * HEAD_DIM:(h + 1) * HEAD_DIM] = o[j * BLOCK:(j + 1) * BLOCK, 0:HEAD_DIM]

    gb = uc_ref[:, 0:CONV_WIDTH]
    p = uc_ref[:, CONV_WIDTH:2 * CONV_WIDTH] * uc_ref[:, 2 * CONV_WIDTH:]
    p_before = jnp.where(first, 0.0, up_ref[7:8, CONV_WIDTH:2 * CONV_WIDTH] * up_ref[7:8, 2 * CONV_WIDTH:])
    p_after = jnp.where(last, 0.0, un_ref[0:1, CONV_WIDTH:2 * CONV_WIDTH] * un_ref[0:1, 2 * CONV_WIDTH:])
    conv = gb * _short_conv_rows(p, p_before, p_after, cw_ref[...], cb_ref[...])

    y = jnp.dot(att[...].astype(BF16), wo_ref[0:ATTN_WIDTH, :], preferred_element_type=F32)
    y = y + jnp.dot(conv.astype(BF16), wo_ref[ATTN_WIDTH:, :], preferred_element_type=F32)
    o_ref[...] = x_ref[...] + gate_ref[0] * y


def _mixer_ab_call(x, q, kt, v1, u, bias, sink_rows, conv_w, conv_b, w_out, layer, gate, rows_per_batch):
    m, d = x.shape
    tq = 512
    tpb = rows_per_batch // tq
    r = tq // BLOCK
    nb = m // BLOCK
    n8 = m // 8
    cur = lambda w: pl.BlockSpec((tq, w), lambda i: (i, 0))
    prev_of = lambda i: jnp.maximum(i * r - 1, 0)
    next_of = lambda i: jnp.minimum((i + 1) * r, nb - 1)
    v_prev = pl.BlockSpec((BLOCK, 2 * KV_WIDTH), lambda i: (prev_of(i), 0))
    v_next = pl.BlockSpec((BLOCK, 2 * KV_WIDTH), lambda i: (next_of(i), 0))
    tk = kt.shape[2]
    k_cur = pl.BlockSpec((1, KV_WIDTH, tq), lambda i: (i // (tk // tq), 0, i % (tk // tq)))
    kpb = tk // BLOCK
    k_prev = pl.BlockSpec((1, KV_WIDTH, BLOCK), lambda i: (prev_of(i) // kpb, 0, prev_of(i) % kpb))
    k_next = pl.BlockSpec((1, KV_WIDTH, BLOCK), lambda i: (next_of(i) // kpb, 0, next_of(i) % kpb))
    uw = 3 * CONV_WIDTH
    return pl.pallas_call(
        functools.partial(_mixer_ab_kernel, tq, tpb),
        out_shape=jax.ShapeDtypeStruct((m, d), F32),
        grid=(m // tq,),
        in_specs=[
            cur(d), cur(ATTN_WIDTH),
            k_cur, k_prev, k_next,
            cur(2 * KV_WIDTH), v_prev, v_next,
            cur(uw),
            pl.BlockSpec((8, uw), lambda i: (jnp.maximum(i * (tq // 8) - 1, 0), 0)),
            pl.BlockSpec((8, uw), lambda i: (jnp.minimum((i + 1) * (tq // 8), n8 - 1), 0)),
            _resident((N_Q_HEADS, BLOCK, 3 * BLOCK)),
            _resident((N_Q_HEADS * BLOCK, LANES)),
            _resident((3, CONV_WIDTH)), _resident((1, CONV_WIDTH)),
            _layer_slice((d, d), layer),
            pl.BlockSpec((1, 1, d), lambda i: (i // tpb, 0, 0)),
        ],
        out_specs=cur(d),
        scratch_shapes=[
            pltpu.VMEM((KV_WIDTH, tq + 2 * BLOCK), BF16),
            pltpu.VMEM((tq + 2 * BLOCK, 2 * KV_WIDTH), BF16),
            pltpu.VMEM((N_KV_HEADS, r, GQA_GROUP * BLOCK, HEAD_DIM), BF16),
            pltpu.VMEM((tq, ATTN_WIDTH), F32),
        ],
        compiler_params=_cparams(("arbitrary",)),
        name="windowed_attn_shortconv_outproj",
    )(x, q, kt, kt, kt, v1, v1, v1, u, u, u, bias, sink_rows, conv_w, conv_b, w_out, gate)


def _key_chunk(seq):
    return min(seq, 1024)


def _dense_attn_kernel(tq, tk, q_ref, k_ref, v_ref, o_ref, qs, s_bufs, p_bufs, rmax_bufs, alpha_bufs,
                       m_ref, acc_ref):
    nc = k_ref.shape[0]
    groups = range(N_KV_HEADS)
    for h in range(N_Q_HEADS):
        g, j = divmod(h, GQA_GROUP)
        qs[g, j * tq:(j + 1) * tq, :] = q_ref[:, h * HEAD_DIM:(h + 1) * HEAD_DIM]
    for g in groups:
        m_ref[g][...] = jnp.full(m_ref[g].shape, -jnp.inf, F32)
        acc_ref[g][...] = jnp.zeros(acc_ref[g].shape, F32)

    def scores(g, c, par):
        s = jnp.dot(qs[g], k_ref[c, g * HEAD_DIM:(g + 1) * HEAD_DIM, :], preferred_element_type=F32)
        s_bufs[g][par][...] = s
        rmax_bufs[g][par][...] = jnp.broadcast_to(jnp.max(s, axis=-1, keepdims=True), rmax_bufs[g][par].shape)

    def softmax(g, par):
        m_old = m_ref[g][...]
        m_new = jnp.maximum(m_old, rmax_bufs[g][par][...])
        alpha_bufs[g][par][...] = jnp.exp(m_old - m_new)
        m_ref[g][...] = m_new
        p = jnp.exp(s_bufs[g][par][...] - jnp.tile(m_new, (1, tk // LANES)))
        p_bufs[g][par][...] = p.astype(BF16)

    def weighted_values(g, c, par):
        start = c * tk if isinstance(c, int) else pl.multiple_of(c * tk, tk)
        vc = v_ref[pl.ds(start, tk), g * LANES:(g + 1) * LANES]
        pv = jnp.dot(p_bufs[g][par][...], vc, preferred_element_type=F32)
        acc_ref[g][...] = alpha_bufs[g][par][...] * acc_ref[g][...] + pv

    def step(t, par):
        static = isinstance(t, int)
        for g in groups:
            if not static or t + 1 < nc:
                softmax(g, 1 - par)
        for g in groups:
            weighted_values(g, t, par)
            if not static or t + 2 < nc:
                scores(g, t + 2, par)

    for g in groups:
        scores(g, 0, 0)
    for g in groups:
        if nc > 1:
            scores(g, 1, 1)
        softmax(g, 0)

    n_pairs = max(nc - 2, 0) // 2

    def steady(tt, carry):
        step(2 * tt, 0)
        step(2 * tt + 1, 1)
        return carry

    lax.fori_loop(0, n_pairs, steady, 0)
    for t in range(2 * n_pairs, nc):
        step(t, t % 2)

    for g in groups:
        acc = acc_ref[g][...]
        o = acc / pltpu.roll(acc, HEAD_DIM, 1)
        for j in range(GQA_GROUP):
            h = g * GQA_GROUP + j
            o_ref[:, h * HEAD_DIM:(h + 1) * HEAD_DIM] = o[j * tq:(j + 1) * tq, 0:HEAD_DIM].astype(o_ref.dtype)


def _dense_attn_call(q, kt, v1, bsz, seq):
    tq, tk = 128, _key_chunk(seq)
    nq = seq // tq
    nc = seq // tk
    rows = GQA_GROUP * tq
    per_group = lambda shape, dtype: [pltpu.VMEM(shape, dtype) for _ in range(N_KV_HEADS)]
    pair = lambda shape, dtype: [[pltpu.VMEM(shape, dtype), pltpu.VMEM(shape, dtype)]
                                 for _ in range(N_KV_HEADS)]
    return pl.pallas_call(
        functools.partial(_dense_attn_kernel, tq, tk),
        out_shape=jax.ShapeDtypeStruct((bsz * seq, ATTN_WIDTH), BF16),
        grid=(bsz, nq),
        in_specs=[
            pl.BlockSpec((tq, ATTN_WIDTH), lambda b, i: (b * nq + i, 0)),
            pl.BlockSpec((nc, KV_WIDTH, tk), lambda b, i: (b, 0, 0)),
            pl.BlockSpec((seq, 2 * KV_WIDTH), lambda b, i: (b, 0)),
        ],
        out_specs=pl.BlockSpec((tq, ATTN_WIDTH), lambda b, i: (b * nq + i, 0)),
        scratch_shapes=[
            pltpu.VMEM((N_KV_HEADS, rows, HEAD_DIM), BF16),
            pair((rows, tk), F32), pair((rows, tk), BF16), pair((rows, LANES), F32), pair((rows, LANES), F32),
            per_group((rows, LANES), F32), per_group((rows, LANES), F32),
        ],
        compiler_params=_cparams(("arbitrary", "arbitrary")),
        name="dense_gqa_attention",
    )(q, kt, v1)


def _hp_dot(a, b):
    return jnp.dot(a, b, preferred_element_type=F32, precision=lax.Precision.HIGHEST)


def _filter_trunk_kernel(z_ref, w1_ref, b1_ref, w2_ref, b2_ref, w3_ref, b3_ref, fq_ref, o_ref):
    fq = fq_ref[...]
    h = jnp.sin(fq * (_hp_dot(w1_ref[...], z_ref[...]) + b1_ref[...]))
    h = jnp.sin(fq * (_hp_dot(w2_ref[...], h) + b2_ref[...]))
    h = jnp.sin(fq * (_hp_dot(w3_ref[...], h) + b3_ref[...]))
    o_ref[...] = h.T


def _filter_trunk_call(zfeat_t, w1, b1, w2, b2, w3, b3, freq):
    rows = zfeat_t.shape[1]
    tr = min(rows, 2048)
    fw = HYENA_FILTER_WIDTH
    col = lambda v: v.reshape(fw, 1)
    return pl.pallas_call(
        _filter_trunk_kernel,
        out_shape=jax.ShapeDtypeStruct((rows, fw), F32),
        grid=(rows // tr,),
        in_specs=[
            pl.BlockSpec((HYENA_EMB_PAD, tr), lambda i: (0, i)),
            _resident((fw, HYENA_EMB_PAD)), _resident((fw, 1)),
            _resident((fw, fw)), _resident((fw, 1)),
            _resident((fw, fw)), _resident((fw, 1)),
            _resident((fw, 1)),
        ],
        out_specs=pl.BlockSpec((tr, fw), lambda i: (i, 0)),
        compiler_params=_cparams(("arbitrary",)),
        name="hyena_filter_trunk",
    )(zfeat_t, w1.T, col(b1), w2.T, col(b2), w3.T, col(b3), col(freq))


def _dft_stage1(src_ref, tab_ref, re_ref, im_ref, n_rows, k1p):
    def body(n2, carry):
        xs = src_ref[pl.ds(n2, n_rows, stride=DFT_N2), :].astype(BF16)
        res = jnp.dot(tab_ref[n2], xs, preferred_element_type=F32)
        re_ref[pl.ds(n2, k1p, stride=DFT_N2), :] = res[0:k1p]
        im_ref[pl.ds(n2, k1p, stride=DFT_N2), :] = res[k1p:]
        return carry

    lax.fori_loop(0, DFT_N2, body, 0, unroll=8)


def _filter_spec_kernel(seq, k1p, h_ref, w4f_ref, w4b_ref, dl_ref, e1_ref, f2_ref,
                        kr_ref, ki_ref, kfull):
    n = 2 * seq
    chunk = min(seq, 1024)
    dl = dl_ref[...]

    def fill(c, ss):
        r0 = pl.multiple_of(c * chunk, chunk)
        hc = h_ref[pl.ds(r0, chunk), :].astype(BF16)
        fwd = jnp.dot(hc, w4f_ref[...], preferred_element_type=F32)
        bwd = jnp.dot(hc, w4b_ref[...], preferred_element_type=F32)
        rows = r0 + lax.broadcasted_iota(jnp.int32, (chunk, 1), 0)
        lag = jnp.where(rows < seq, rows, n - rows).astype(F32)
        t = lag * (1.0 / (seq - 1))
        val = jnp.where(rows < seq, fwd, bwd) * jnp.exp(-t * dl)
        val = jnp.where(rows == seq, 0.0, val)
        kfull[pl.ds(r0, chunk), :] = val
        return ss + jnp.sum(val * val, axis=0, keepdims=True)

    ss = lax.fori_loop(0, n // chunk, fill, jnp.zeros((1, dl.shape[1]), F32))
    norm = lax.rsqrt(ss + EPS)

    _dft_stage1(kfull, e1_ref, kr_ref, ki_ref, n // DFT_N2, k1p)

    f2 = f2_ref[...]

    def stage2(a, carry):
        rows = pl.ds(pl.multiple_of(a * DFT_N2, DFT_N2), DFT_N2)
        slab = jnp.concatenate([kr_ref[rows, :], ki_ref[rows, :]], axis=0).astype(BF16)
        xf = jnp.dot(f2, slab, preferred_element_type=F32)
        kr_ref[rows, :] = xf[0:DFT_N2] * norm
        ki_ref[rows, :] = xf[DFT_N2:] * norm
        return carry

    lax.fori_loop(0, seq // DFT_N2 + 1, stage2, 0, unroll=8)


def _filter_spec_call(h3, w4, deltas, e1f, f2f, seq):
    n = 2 * seq
    k1p = e1f.shape[1] // 2
    ct = LANES
    nct = CONV_WIDTH // ct
    fw = HYENA_FILTER_WIDTH
    spec_rows = k1p * DFT_N2
    out = pl.BlockSpec((spec_rows, ct), lambda j: (0, j))
    return pl.pallas_call(
        functools.partial(_filter_spec_kernel, seq, k1p),
        out_shape=(jax.ShapeDtypeStruct((spec_rows, CONV_WIDTH), F32),
                   jax.ShapeDtypeStruct((spec_rows, CONV_WIDTH), F32)),
        grid=(nct,),
        in_specs=[
            _resident((n, fw)),
            pl.BlockSpec((fw, ct), lambda j: (0, j)),
            pl.BlockSpec((fw, ct), lambda j: (0, nct + j)),
            pl.BlockSpec((1, ct), lambda j: (0, j)),
            _resident(e1f.shape),
            _resident((2 * DFT_N2, 2 * DFT_N2)),
        ],
        out_specs=(out, out),
        scratch_shapes=[pltpu.VMEM((n, ct), F32)],
        compiler_params=_cparams(("arbitrary",)),
        name="hyena_filter_spectrum",
    )(h3, w4, w4, deltas, e1f, f2f)


def _long_conv_kernel(seq, k1p, z_ref, x0_ref, kr_ref, ki_ref, skip_ref, e1_ref, f2f_ref, f2i_ref,
                      einv_ref, o_ref, br, bi):
    n1h = seq // DFT_N2
    _dft_stage1(z_ref, e1_ref, br, bi, n1h, k1p)

    f2f = f2f_ref[...]
    f2i = f2i_ref[...]

    def spectral(a, carry):
        r0 = pl.multiple_of(a * DFT_N2, DFT_N2)
        rows = pl.ds(r0, DFT_N2)
        slab = jnp.concatenate([br[rows, :], bi[rows, :]], axis=0).astype(BF16)
        xf = jnp.dot(f2f, slab, preferred_element_type=F32)
        xr, xi = xf[0:DFT_N2], xf[DFT_N2:]
        kr, ki = kr_ref[rows, :], ki_ref[rows, :]
        y = jnp.concatenate([xr * kr - xi * ki, xr * ki + xi * kr], axis=0).astype(BF16)
        d = jnp.dot(f2i, y, preferred_element_type=F32)
        br[rows, :] = d[0:DFT_N2]
        bi[rows, :] = d[DFT_N2:]
        return carry

    lax.fori_loop(0, n1h + 1, spectral, 0, unroll=8)

    skip = skip_ref[...]

    def synth(n2, carry):
        ds_spec = pl.ds(n2, k1p, stride=DFT_N2)
        d = jnp.concatenate([br[ds_spec, :], bi[ds_spec, :]], axis=0).astype(BF16)
        o_ref[pl.ds(n2, n1h, stride=DFT_N2), :] = jnp.dot(einv_ref[n2], d, preferred_element_type=F32)
        return carry

    lax.fori_loop(0, DFT_N2, synth, 0, unroll=8)

    chunk = min(seq, 512)

    def gate(c, carry):
        rows = pl.ds(pl.multiple_of(c * chunk, chunk), chunk)
        o_ref[rows, :] = x0_ref[rows, :] * (o_ref[rows, :] + skip * z_ref[rows, :])
        return carry

    lax.fori_loop(0, seq // chunk, gate, 0)


def _long_conv_call(z, x0, kr, ki, skip, e1, f2f, f2i, einv, bsz, seq):
    ct = LANES
    nct = CONV_WIDTH // ct
    k1p = e1.shape[1] // 2
    spec_rows = k1p * DFT_N2
    big = lambda: pl.BlockSpec((seq, ct), lambda j, b: (b, j), pipeline_mode=pl.Buffered(1))
    spec = lambda: pl.BlockSpec((spec_rows, ct), lambda j, b: (0, j), pipeline_mode=pl.Buffered(1))
    return pl.pallas_call(
        functools.partial(_long_conv_kernel, seq, k1p),
        out_shape=jax.ShapeDtypeStruct((bsz * seq, CONV_WIDTH), F32),
        grid=(nct, bsz),
        in_specs=[
            big(), big(), spec(), spec(),
            pl.BlockSpec((1, ct), lambda j, b: (0, j)),
            _resident(e1.shape),
            _resident((2 * DFT_N2, 2 * DFT_N2)), _resident((2 * DFT_N2, 2 * DFT_N2)),
            _resident(einv.shape),
        ],
        out_specs=pl.BlockSpec((seq, ct), lambda j, b: (b, j)),
        scratch_shapes=[pltpu.VMEM((spec_rows, ct), F32), pltpu.VMEM((spec_rows, ct), F32)],
        compiler_params=_cparams(("arbitrary", "arbitrary")),
        name="hyena_long_conv",
    )(z, x0, kr, ki, skip, e1, f2f, f2i, einv)


def _outproj_kernel(x_ref, att_ref, y_ref, wo_ref, gate_ref, o_ref):
    y = jnp.dot(att_ref[...], wo_ref[0:ATTN_WIDTH, :], preferred_element_type=F32)
    y = y + jnp.dot(y_ref[...].astype(BF16), wo_ref[ATTN_WIDTH:, :], preferred_element_type=F32)
    o_ref[...] = x_ref[...] + gate_ref[0] * y


def _outproj_call(x, att, y, w_out, layer, gate, rows_per_batch):
    m, d = x.shape
    tm = 512
    tpb = rows_per_batch // tm
    row = lambda w: pl.BlockSpec((tm, w), lambda i: (i, 0))
    return pl.pallas_call(
        _outproj_kernel,
        out_shape=jax.ShapeDtypeStruct((m, d), F32),
        grid=(m // tm,),
        in_specs=[row(d), row(ATTN_WIDTH), row(CONV_WIDTH), _layer_slice((d, d), layer),
                  pl.BlockSpec((1, 1, d), lambda i: (i // tpb, 0, 0))],
        out_specs=row(d),
        compiler_params=_cparams(("arbitrary",)),
        name="mixer_out_projection",
    )(x, att, y, w_out, gate)


def _rope_tables(seq):
    t = jnp.arange(seq)
    row = (t // GRID_W).astype(F32)
    col = (t % GRID_W).astype(F32)
    half = HEAD_DIM // 2
    inv = ROPE_THETA ** (-jnp.arange(0, half, 2, dtype=F32) / half)
    ang = jnp.concatenate([row[:, None] * inv, col[:, None] * inv], axis=-1)
    ang = jnp.repeat(ang, 2, axis=-1)
    sign = jnp.where(jnp.arange(HEAD_DIM) % 2 == 0, -1.0, 1.0).astype(F32)
    reps = LANES // HEAD_DIM
    return jnp.tile(jnp.cos(ang), (1, reps)), jnp.tile(jnp.sin(ang) * sign, (1, reps))


def _dft_tables(seq):
    n = 2 * seq
    n1 = n // DFT_N2
    k1 = n1 // 2 + 1
    k1p = -(-k1 // 8) * 8
    a = jnp.arange(k1p)
    live = (a < k1).astype(F32)
    ang1 = ((a[:, None] * jnp.arange(n1)[None, :]) % n1).astype(F32) * (2.0 * math.pi / n1)
    ang2 = (jnp.arange(DFT_N2)[:, None] * a[None, :]).astype(F32) * (2.0 * math.pi / n)
    c1, s1 = jnp.cos(ang1) * live[:, None], jnp.sin(ang1) * live[:, None]
    c2, s2 = jnp.cos(ang2), jnp.sin(ang2)
    cos_a = c1[None] * c2[:, :, None] - s1[None] * s2[:, :, None]
    sin_a = s1[None] * c2[:, :, None] + c1[None] * s2[:, :, None]
    e1_full = jnp.concatenate([cos_a, -sin_a], axis=1).astype(BF16)
    e1_half = e1_full[:, :, :n1 // 2]
    wgt = jnp.where((a == 0) | (a == n1 // 2), 1.0, 2.0) / n
    c1t, s1t = (c1 * wgt[:, None]).T[:n1 // 2], (s1 * wgt[:, None]).T[:n1 // 2]
    cos_s = c1t[None] * c2[:, None, :] - s1t[None] * s2[:, None, :]
    sin_s = s1t[None] * c2[:, None, :] + c1t[None] * s2[:, None, :]
    einv = jnp.concatenate([cos_s, -sin_s], axis=2).astype(BF16)
    kk = jnp.arange(DFT_N2)
    phi = ((kk[:, None] * kk[None, :]) % DFT_N2).astype(F32) * (2.0 * math.pi / DFT_N2)
    ci, si = jnp.cos(phi), jnp.sin(phi)
    f2f = jnp.block([[ci, si], [-si, ci]]).astype(BF16)
    f2i = jnp.block([[ci, -si], [si, ci]]).astype(BF16)
    return e1_full, e1_half, einv, f2f, f2i


def _filter_features_t(seq):
    j = jnp.arange(2 * seq)
    lag = jnp.where(j < seq, j, jnp.where(j == seq, 0, 2 * seq - j)).astype(F32)[None, :]
    t = lag / (seq - 1)
    bands = (HYENA_EMB - 1) // 2
    w = (2.0 * math.pi / seq) * lag
    fr = jnp.linspace(1e-4, bands - 1, bands, dtype=F32)[:, None]
    pad = jnp.zeros((HYENA_EMB_PAD - HYENA_EMB, 2 * seq), F32)
    return jnp.concatenate([t, jnp.cos(fr * w), -jnp.sin(fr * w), pad], axis=0)


def kernel(x, c, rel_table, norm_g, w_mod, b_mod, w_in, w_out, ffn_w1, ffn_w3, ffn_w2, a_qk_g, a_sink,
           b_conv_w, b_conv_b, c_qk_g, d_conv_w, d_conv_b, d_f_w1, d_f_b1, d_f_w2, d_f_b2, d_f_w3,
           d_f_b3, d_f_w4, d_f_freq, d_skip):
    bsz, seq, d = x.shape
    depth = w_mod.shape[0]
    m_rows = bsz * seq
    mod =_mod_call(c, w_mod, b_mod).reshape(depth, bsz, 3, 3, 1, d)
    e_blk = jnp.kron(jnp.eye(MXU_DIM // HEAD_DIM, dtype=F32), jnp.ones((HEAD_DIM, HEAD_DIM), F32)).astype(BF16)

    w1b, w3b, w2b, w_in_b, w_out_b = (w.astype(BF16) for w in (ffn_w1, ffn_w3, ffn_w2, w_in, w_out))

    def ffn(xf, l, which, sub):
        return _ffn_call(xf, norm_g[l, sub][None], mod[l, :, sub, 0], mod[l, :, sub, 1], mod[l, :, sub, 2],
                         w1b, w3b, w2b, l, which, seq)

    xf = x.reshape(m_rows, d)
    for l in range(depth):
        j = l // 2
        xf = ffn(xf, l, 0, 0)
        even = l % 2 == 0
        qk_g = a_qk_g[j] if even else c_qk_g[j]
        hyena = None if even else _rope_tables(seq) + (d_conv_w[j], d_conv_b[j][None])
        q, k, v, *conv_in = _inproj_call(
            xf, norm_g[l, 1][None], mod[l, :, 1, 0], mod[l, :, 1, 1], w_in_b, l,
            jnp.tile(qk_g[0], N_Q_HEADS)[None], jnp.tile(qk_g[1], N_KV_HEADS)[None], e_blk, seq, hyena)
        gate = mod[l, :, 1, 2]
        if even:
            (u,) = conv_in
            bias, sink_rows = _bias_call(rel_table, a_sink[j])
            xf = _mixer_ab_call(xf, q, k, v, u, bias, sink_rows, b_conv_w[j], b_conv_b[j][None], w_out_b, l,
                                gate, seq)
        else:
            x0, z = conv_in
            att = _dense_attn_call(q, k, v, bsz, seq)
            e1_full, e1_half, einv, f2f, f2i = _dft_tables(seq)
            fw1 = jnp.pad(d_f_w1[j], ((0, HYENA_EMB_PAD - HYENA_EMB), (0, 0)))
            h3 = _filter_trunk_call(_filter_features_t(seq), fw1, d_f_b1[j], d_f_w2[j], d_f_b2[j],
                                    d_f_w3[j], d_f_b3[j], d_f_freq[j])
            deltas = jnp.abs(jnp.linspace(HYENA_MIN_DECAY, HYENA_MAX_DECAY, CONV_WIDTH, dtype=F32))[None]
            kr, ki = _filter_spec_call(h3, d_f_w4[j].astype(BF16), deltas, e1_full, f2f, seq)
            y = _long_conv_call(z, x0, kr, ki, d_skip[j][None], e1_half, f2f, f2i, einv, bsz, seq)
            xf = _outproj_call(xf, att, y, w_out_b, l, gate, seq)
        xf = ffn(xf, l, 1, 2)
    return xf.reshape(bsz, seq, d)
```

```python
import functools
import math

import jax
import jax.numpy as jnp
import numpy as np
from jax import lax
from jax.experimental import pallas as pl
from jax.experimental.pallas import tpu as pltpu

D_MODEL = 1024
HEAD_DIM = 64
N_Q_HEADS = 8
N_KV_HEADS = 2
GQA_GROUP = N_Q_HEADS // N_KV_HEADS
ATTN_WIDTH = N_Q_HEADS * HEAD_DIM
KV_WIDTH = N_KV_HEADS * HEAD_DIM
QKV_COLS = ATTN_WIDTH + 2 * KV_WIDTH
CONV_WIDTH = D_MODEL - ATTN_WIDTH
IN_COLS = QKV_COLS + 3 * CONV_WIDTH
D_FF = 2752
BLOCK = 128
WINDOW = 128
N_BUCKETS = 32
MAX_DISTANCE = 128
GRID_W = 64
ROPE_THETA = 10000.0
HYENA_EMB = 33
HYENA_FILTER_WIDTH = 64
HYENA_EMB_PAD = 64
HYENA_MIN_DECAY = math.log(1e-2) / 0.3
HYENA_MAX_DECAY = math.log(1e-2) / 1.5
EPS = 1e-6
NEG_INF = -1e30
LOG2_E = 1.4426950408889634

LANES = 128
HALO = 16
MXU_DIM = 256
FF_CHUNK = MXU_DIM
VMEM_LIMIT = 56 * 1024 * 1024
DFT_N2 = 128

BF16 = jnp.bfloat16
F32 = jnp.float32


def _cparams(sem):
    return pltpu.CompilerParams(dimension_semantics=sem, vmem_limit_bytes=VMEM_LIMIT)


def _resident(shape):
    nd = len(shape)
    return pl.BlockSpec(shape, lambda *_: (0,) * nd, pipeline_mode=pl.Buffered(1))


def _layer_slice(shape, layer):
    return pl.BlockSpec((None,) + shape, lambda *_: (layer, 0, 0), pipeline_mode=pl.Buffered(1))


def _adaln(x, g, scale, shift):
    y = x * lax.rsqrt(jnp.mean(x * x, axis=-1, keepdims=True) + EPS) * g
    return y * (1.0 + scale) + shift


def _mod_kernel(c_ref, w_ref, b_ref, o_ref):
    w = w_ref[0]
    for b in range(c_ref.shape[0]):
        cc = c_ref[b]
        cond = cc * jax.nn.sigmoid(cc)
        o_ref[0, b:b + 1, :] = jnp.sum(w * cond, axis=0, keepdims=True) + b_ref[0]


def _mod_call(c, w_mod, b_mod):
    depth, d, n = w_mod.shape
    bsz = c.shape[0]
    tn = 512
    return pl.pallas_call(
        _mod_kernel,
        out_shape=jax.ShapeDtypeStruct((depth, bsz, n), F32),
        grid=(depth, n // tn),
        in_specs=[
            pl.BlockSpec((bsz, d, 1), lambda l, j: (0, 0, 0)),
            pl.BlockSpec((1, d, tn), lambda l, j: (l, 0, j)),
            pl.BlockSpec((1, 1, tn), lambda l, j: (l, 0, j)),
        ],
        out_specs=pl.BlockSpec((1, bsz, tn), lambda l, j: (l, 0, j)),
        compiler_params=_cparams(("arbitrary", "arbitrary")),
        name="adaln_modulation",
    )(c.reshape(bsz, d, 1), w_mod, b_mod.reshape(depth, 1, n))


def _ffn_kernel(x_ref, g_ref, shift_ref, scale_ref, gate_ref, w1_ref, w3_ref, w2_ref, o_ref):
    x = x_ref[...]
    h = _adaln(x, g_ref[...], scale_ref[0], shift_ref[0]).astype(BF16)
    acc = jnp.zeros(x.shape, F32)
    for c0 in range(0, D_FF, FF_CHUNK):
        cols = slice(c0, min(c0 + FF_CHUNK, D_FF))
        a = jnp.dot(h, w1_ref[:, cols], preferred_element_type=F32)
        b = jnp.dot(h, w3_ref[:, cols], preferred_element_type=F32)
        act = (a * jax.nn.sigmoid(a) * b).astype(BF16)
        acc = acc + jnp.dot(act, w2_ref[cols, :], preferred_element_type=F32)
    o_ref[...] = x + (0.5 * gate_ref[0]) * acc


def _ffn_call(x, g, shift, scale, gate, w1, w3, w2, layer, which, rows_per_batch):
    m, d = x.shape
    tm = 512
    tpb = rows_per_batch // tm
    vec = pl.BlockSpec((1, 1, d), lambda i: (i // tpb, 0, 0))
    weight = lambda r, c: pl.BlockSpec((None, None, r, c), lambda i: (layer, which, 0, 0),
                                       pipeline_mode=pl.Buffered(1))
    return pl.pallas_call(
        _ffn_kernel,
        out_shape=jax.ShapeDtypeStruct((m, d), F32),
        grid=(m // tm,),
        in_specs=[
            pl.BlockSpec((tm, d), lambda i: (i, 0)),
            _resident((1, d)),
            vec, vec, vec,
            weight(d, D_FF), weight(d, D_FF), weight(D_FF, d),
        ],
        out_specs=pl.BlockSpec((tm, d), lambda i: (i, 0)),
        compiler_params=_cparams(("arbitrary",)),
        name="adaln_swiglu_ffn",
    )(x, g, shift, scale, gate, w1, w3, w2)


def _group_norm_scale(v, e_ref):
    sq = v * v
    hi = sq.astype(BF16)
    lo = (sq - hi.astype(F32)).astype(BF16)
    w = min(v.shape[1], MXU_DIM)
    e = e_ref[...]
    ss = jnp.concatenate(
        [jnp.dot(hi[:, c:c + w], e[0:w, 0:w], preferred_element_type=F32)
         + jnp.dot(lo[:, c:c + w], e[0:w, 0:w], preferred_element_type=F32) for c in range(0, v.shape[1], w)],
        axis=1)
    return lax.rsqrt(ss * (1.0 / HEAD_DIM) + EPS)


def _rope128(v, cos, sin_signed, even_lane):
    partner = jnp.where(even_lane, pltpu.roll(v, LANES - 1, 1), pltpu.roll(v, 1, 1))
    return v * cos + partner * sin_signed


def _short_conv_rows(u, before, after, cw, cb):
    rows = u.shape[0]
    r = lax.broadcasted_iota(jnp.int32, u.shape, 0)
    um1 = jnp.where(r == 0, before, pltpu.roll(u, 1, 0))
    up1 = jnp.where(r == rows - 1, after, pltpu.roll(u, rows - 1, 0))
    return cw[0:1] * um1 + cw[1:2] * u + cw[2:3] * up1 + cb


def _inproj_kernel(rope, tpb, x_ref, g_ref, shift_ref, scale_ref, w_ref, qg_ref, kg_ref, e_ref, *rest):
    if rope:
        cos_ref, sin_ref, xp_ref, xn_ref, cw_ref, cb_ref, q_ref, k_ref, v_ref, x0_ref, z_ref = rest
    else:
        q_ref, k_ref, v_ref, u_ref = rest
    h = _adaln(x_ref[...], g_ref[...], scale_ref[0], shift_ref[0]).astype(BF16)
    q = jnp.dot(h, w_ref[:, 0:ATTN_WIDTH], preferred_element_type=F32)
    k = jnp.dot(h, w_ref[:, ATTN_WIDTH:ATTN_WIDTH + KV_WIDTH], preferred_element_type=F32)
    q = q * _group_norm_scale(q, e_ref) * qg_ref[...]
    k = k * _group_norm_scale(k, e_ref) * kg_ref[...]
    if rope:
        cos = cos_ref[...]
        sin = sin_ref[...]
        even = (lax.broadcasted_iota(jnp.int32, cos.shape, 1) % 2) == 0
        q = jnp.concatenate(
            [_rope128(q[:, j * LANES:(j + 1) * LANES], cos, sin, even) for j in range(ATTN_WIDTH // LANES)],
            axis=1)
        k = _rope128(k, cos, sin, even)
    q_ref[...] = (q * (HEAD_DIM ** -0.5 * (LOG2_E if rope else 1.0))).astype(BF16)
    v = jnp.dot(h, w_ref[:, ATTN_WIDTH + KV_WIDTH:QKV_COLS], preferred_element_type=F32)
    k_ref[0] = k.T.astype(BF16)
    low = lax.broadcasted_iota(jnp.int32, v.shape, 1) < HEAD_DIM
    v_ref[...] = jnp.concatenate(
        [jnp.where(low, v, 1.0), jnp.where(low, pltpu.roll(v, HEAD_DIM, 1), 1.0)], axis=1).astype(BF16)
    if not rope:
        u_ref[...] = jnp.dot(h, w_ref[:, QKV_COLS:IN_COLS], preferred_element_type=F32)
        return
    i = pl.program_id(0)
    h_prev = _adaln(xp_ref[...], g_ref[...], scale_ref[0], shift_ref[0]).astype(BF16)
    h_next = _adaln(xn_ref[...], g_ref[...], scale_ref[0], shift_ref[0]).astype(BF16)
    u_ext = jnp.dot(jnp.concatenate([h_prev, h, h_next], axis=0), w_ref[:, QKV_COLS:IN_COLS],
                    preferred_element_type=F32)
    tm = h.shape[0]
    before = jnp.where((i % tpb) == 0, 0.0, u_ext[HALO - 1:HALO])
    after = jnp.where((i % tpb) == tpb - 1, 0.0, u_ext[HALO + tm:HALO + tm + 1])
    t = _short_conv_rows(u_ext[HALO:HALO + tm], before, after, cw_ref[...], cb_ref[...])
    x0_ref[...] = t[:, 0:CONV_WIDTH]
    z_ref[...] = t[:, CONV_WIDTH:2 * CONV_WIDTH] * t[:, 2 * CONV_WIDTH:]


def _inproj_call(x, g, shift, scale, w_in, layer, qg, kg, e, rows_per_batch, hyena=None):
    m, d = x.shape
    tm = 512
    tpb = rows_per_batch // tm
    vec = pl.BlockSpec((1, 1, d), lambda i: (i // tpb, 0, 0))
    row = lambda w: pl.BlockSpec((tm, w), lambda i: (i, 0))
    in_specs = [
        row(d),
        _resident((1, d)), vec, vec,
        _layer_slice((d, IN_COLS), layer),
        _resident((1, ATTN_WIDTH)), _resident((1, KV_WIDTH)),
        _resident((MXU_DIM, MXU_DIM)),
    ]
    args = [x, g, shift, scale, w_in, qg, kg, e]
    uw = 3 * CONV_WIDTH
    if hyena is not None:
        cos, sin, conv_w, conv_b = hyena
        tab = pl.BlockSpec((tm, LANES), lambda i: (i % tpb, 0))
        hpt = tm // HALO
        last_halo = m // HALO - 1
        in_specs += [
            tab, tab,
            pl.BlockSpec((HALO, d), lambda i: (jnp.maximum(i * hpt - 1, 0), 0)),
            pl.BlockSpec((HALO, d), lambda i: (jnp.minimum((i + 1) * hpt, last_halo), 0)),
            _resident((3, uw)), _resident((1, uw)),
        ]
        args += [cos, sin, x, x, conv_w, conv_b]
        conv_shapes = [jax.ShapeDtypeStruct((m, CONV_WIDTH), F32)] * 2
        conv_specs = [row(CONV_WIDTH)] * 2
    else:
        conv_shapes = [jax.ShapeDtypeStruct((m, uw), F32)]
        conv_specs = [row(uw)]
    tk = _key_chunk(rows_per_batch)
    per = tk // tm
    return pl.pallas_call(
        functools.partial(_inproj_kernel, hyena is not None, tpb),
        out_shape=[
            jax.ShapeDtypeStruct((m, ATTN_WIDTH), BF16),
            jax.ShapeDtypeStruct((m // tk, KV_WIDTH, tk), BF16),
            jax.ShapeDtypeStruct((m, 2 * KV_WIDTH), BF16),
        ] + conv_shapes,
        grid=(m // tm,),
        in_specs=in_specs,
        out_specs=[row(ATTN_WIDTH), pl.BlockSpec((1, KV_WIDTH, tm), lambda i: (i // per, 0, i % per)),
                   row(2 * KV_WIDTH)] + conv_specs,
        compiler_params=_cparams(("arbitrary",)),
        name="adaln_in_projection",
    )(*args)


_T5_STEPS = (12, 16, 23, 32, 46, 64, 91)


def _bias_kernel(tab_ref, sink_ref, o_ref, sink_rows_ref):
    for h in range(N_Q_HEADS):
        sink_rows_ref[h * BLOCK:(h + 1) * BLOCK, :] = jnp.full((BLOCK, LANES), sink_ref[h], F32)
    qi = lax.broadcasted_iota(jnp.int32, (BLOCK, 3 * BLOCK), 0)
    kj = lax.broadcasted_iota(jnp.int32, (BLOCK, 3 * BLOCK), 1)
    rel = kj - BLOCK - qi
    n = jnp.abs(rel)
    half = N_BUCKETS // 2
    max_exact = half // 2
    large = jnp.full(n.shape, max_exact, jnp.int32)
    for t in _T5_STEPS:
        large = large + (n >= t).astype(jnp.int32)
    bucket = jnp.where(rel > 0, half, 0) + jnp.where(n < max_exact, n, large)
    for h in range(N_Q_HEADS):
        bias = jnp.zeros(n.shape, F32)
        for b in range(N_BUCKETS):
            bias = jnp.where(bucket == b, tab_ref[b, h], bias)
        o_ref[h] = jnp.where(n <= WINDOW, bias, NEG_INF)


def _bias_call(rel_table, sink):
    return pl.pallas_call(
        _bias_kernel,
        out_shape=(jax.ShapeDtypeStruct((N_Q_HEADS, BLOCK, 3 * BLOCK), F32),
                   jax.ShapeDtypeStruct((N_Q_HEADS * BLOCK, LANES), F32)),
        in_specs=[pl.BlockSpec(memory_space=pltpu.SMEM), pl.BlockSpec(memory_space=pltpu.SMEM)],
        name="t5_bias_tile",
    )(rel_table, sink)


def _mixer_ab_kernel(tq, tpb, x_ref, q_ref, kc_ref, kp_ref, kn_ref, vc_ref, vp_ref, vn_ref,
                     uc_ref, up_ref, un_ref, bias_ref, sink_ref, cw_ref, cb_ref, wo_ref, gate_ref,
                     o_ref, kbuf, vbuf, qs, att):
    i = pl.program_id(0)
    first = (i % tpb) == 0
    last = (i % tpb) == tpb - 1
    nblk = tq // BLOCK
    grows = GQA_GROUP * BLOCK
    kbuf[:, 0:BLOCK] = kp_ref[0]
    kbuf[:, BLOCK:BLOCK + tq] = kc_ref[0]
    kbuf[:, BLOCK + tq:] = kn_ref[0]
    vbuf[0:BLOCK] = vp_ref[...]
    vbuf[BLOCK:BLOCK + tq] = vc_ref[...]
    vbuf[BLOCK + tq:] = vn_ref[...]
    for n in range(nblk):
        for h in range(N_Q_HEADS):
            g, j = divmod(h, GQA_GROUP)
            qs[g, n, j * BLOCK:(j + 1) * BLOCK, :] = q_ref[n * BLOCK:(n + 1) * BLOCK, h * HEAD_DIM:(h + 1) * HEAD_DIM]
    col = lax.broadcasted_iota(jnp.int32, (grows, 3 * BLOCK), 1)
    for n in range(nblk):
        keys = slice(n * BLOCK, (n + 3) * BLOCK)
        for g in range(N_KV_HEADS):
            s = jnp.dot(qs[g, n], kbuf[g * HEAD_DIM:(g + 1) * HEAD_DIM, keys], preferred_element_type=F32)
            s = s + bias_ref[g * GQA_GROUP:(g + 1) * GQA_GROUP].reshape(grows, 3 * BLOCK)
            if n == 0:
                s = jnp.where(jnp.logical_and(first, col < BLOCK), NEG_INF, s)
            if n == nblk - 1:
                s = jnp.where(jnp.logical_and(last, col >= 2 * BLOCK), NEG_INF, s)
            sk = sink_ref[g * grows:(g + 1) * grows, :]
            mx = jnp.maximum(jnp.broadcast_to(jnp.max(s, axis=-1, keepdims=True), sk.shape), sk)
            p = jnp.exp(s - jnp.tile(mx, (1, 3)))
            pv = jnp.dot(p.astype(BF16), vbuf[keys, g * LANES:(g + 1) * LANES], preferred_element_type=F32)
            o = pv / (pltpu.roll(pv, HEAD_DIM, 1) + jnp.exp(sk - mx))
            for j in range(GQA_GROUP):
                h = g * GQA_GROUP + j
                att[n * BLOCK:(n + 1) * BLOCK, h * HEAD_DIM:(h + 1) * HEAD_DIM] = o[j * BLOCK:(j + 1) * BLOCK, 0:HEAD_DIM]

    gb = uc_ref[:, 0:CONV_WIDTH]
    p = uc_ref[:, CONV_WIDTH:2 * CONV_WIDTH] * uc_ref[:, 2 * CONV_WIDTH:]
    p_before = jnp.where(first, 0.0, up_ref[7:8, CONV_WIDTH:2 * CONV_WIDTH] * up_ref[7:8, 2 * CONV_WIDTH:])
    p_after = jnp.where(last, 0.0, un_ref[0:1, CONV_WIDTH:2 * CONV_WIDTH] * un_ref[0:1, 2 * CONV_WIDTH:])
    conv = gb * _short_conv_rows(p, p_before, p_after, cw_ref[...], cb_ref[...])

    y = jnp.dot(att[...].astype(BF16), wo_ref[0:ATTN_WIDTH, :], preferred_element_type=F32)
    y = y + jnp.dot(conv.astype(BF16), wo_ref[ATTN_WIDTH:, :], preferred_element_type=F32)
    o_ref[...] = x_ref[...] + gate_ref[0] * y


def _mixer_ab_call(x, q, kt, v1, u, bias, sink_rows, conv_w, conv_b, w_out, layer, gate, rows_per_batch):
    m, d = x.shape
    tq = 512
    tpb = rows_per_batch // tq
    r = tq // BLOCK
    nb = m // BLOCK
    n8 = m // 8
    cur = lambda w: pl.BlockSpec((tq, w), lambda i: (i, 0))
    prev_of = lambda i: jnp.maximum(i * r - 1, 0)
    next_of = lambda i: jnp.minimum((i + 1) * r, nb - 1)
    v_prev = pl.BlockSpec((BLOCK, 2 * KV_WIDTH), lambda i: (prev_of(i), 0))
    v_next = pl.BlockSpec((BLOCK, 2 * KV_WIDTH), lambda i: (next_of(i), 0))
    tk = kt.shape[2]
    k_cur = pl.BlockSpec((1, KV_WIDTH, tq), lambda i: (i // (tk // tq), 0, i % (tk // tq)))
    kpb = tk // BLOCK
    k_prev = pl.BlockSpec((1, KV_WIDTH, BLOCK), lambda i: (prev_of(i) // kpb, 0, prev_of(i) % kpb))
    k_next = pl.BlockSpec((1, KV_WIDTH, BLOCK), lambda i: (next_of(i) // kpb, 0, next_of(i) % kpb))
    uw = 3 * CONV_WIDTH
    return pl.pallas_call(
        functools.partial(_mixer_ab_kernel, tq, tpb),
        out_shape=jax.ShapeDtypeStruct((m, d), F32),
        grid=(m // tq,),
        in_specs=[
            cur(d), cur(ATTN_WIDTH),
            k_cur, k_prev, k_next,
            cur(2 * KV_WIDTH), v_prev, v_next,
            cur(uw),
            pl.BlockSpec((8, uw), lambda i: (jnp.maximum(i * (tq // 8) - 1, 0), 0)),
            pl.BlockSpec((8, uw), lambda i: (jnp.minimum((i + 1) * (tq // 8), n8 - 1), 0)),
            _resident((N_Q_HEADS, BLOCK, 3 * BLOCK)),
            _resident((N_Q_HEADS * BLOCK, LANES)),
            _resident((3, CONV_WIDTH)), _resident((1, CONV_WIDTH)),
            _layer_slice((d, d), layer),
            pl.BlockSpec((1, 1, d), lambda i: (i // tpb, 0, 0)),
        ],
        out_specs=cur(d),
        scratch_shapes=[
            pltpu.VMEM((KV_WIDTH, tq + 2 * BLOCK), BF16),
            pltpu.VMEM((tq + 2 * BLOCK, 2 * KV_WIDTH), BF16),
            pltpu.VMEM((N_KV_HEADS, r, GQA_GROUP * BLOCK, HEAD_DIM), BF16),
            pltpu.VMEM((tq, ATTN_WIDTH), F32),
        ],
        compiler_params=_cparams(("arbitrary",)),
        name="windowed_attn_shortconv_outproj",
    )(x, q, kt, kt, kt, v1, v1, v1, u, u, u, bias, sink_rows, conv_w, conv_b, w_out, gate)


def _key_chunk(seq):
    return min(seq, 1024)


def _dense_attn_kernel(tq, tk, q_ref, k_ref, v_ref, o_ref, qs, s_buf, p_buf, rmax_buf, alpha_buf,
                       m_ref, acc_ref):
    nc = k_ref.shape[0]
    groups = range(N_KV_HEADS)
    for h in range(N_Q_HEADS):
        g, j = divmod(h, GQA_GROUP)
        qs[g, j * tq:(j + 1) * tq, :] = q_ref[:, h * HEAD_DIM:(h + 1) * HEAD_DIM]
    for g in groups:
        m_ref[g][...] = jnp.full(m_ref[g].shape, -jnp.inf, F32)
        acc_ref[g][...] = jnp.zeros(acc_ref[g].shape, F32)

    def scores(g, c):
        s = jnp.dot(qs[g], k_ref[c, g * HEAD_DIM:(g + 1) * HEAD_DIM, :], preferred_element_type=F32)
        s_buf[g][...] = s
        rmax_buf[g][...] = jnp.broadcast_to(jnp.max(s, axis=-1, keepdims=True), rmax_buf[g].shape)

    def softmax(g):
        m_old = m_ref[g][...]
        m_new = jnp.maximum(m_old, rmax_buf[g][...])
        alpha_buf[g][...] = jnp.exp2(m_old - m_new)
        m_ref[g][...] = m_new
        p = jnp.exp2(s_buf[g][...] - jnp.tile(m_new, (1, tk // LANES)))
        p_buf[g][...] = p.astype(BF16)

    def weighted_values(g, c):
        start = c * tk if isinstance(c, int) else pl.multiple_of(c * tk, tk)
        vc = v_ref[pl.ds(start, tk), g * LANES:(g + 1) * LANES]
        pv = jnp.dot(p_buf[g][...], vc, preferred_element_type=F32)
        acc_ref[g][...] = alpha_buf[g][...] * acc_ref[g][...] + pv

    def step(t):
        static = isinstance(t, int)
        for g in groups:
            weighted_values(g, t)
            if not static or t + 1 < nc:
                softmax(g)
            if not static or t + 2 < nc:
                scores(g, t + 2)

    for g in groups:
        scores(g, 0)
        softmax(g)
        if nc > 1:
            scores(g, 1)

    def steady(t, carry):
        step(t)
        return carry

    lax.fori_loop(0, max(nc - 2, 0), steady, 0)
    for t in range(max(nc - 2, 0), nc):
        step(t)

    for g in groups:
        acc = acc_ref[g][...]
        o = acc / pltpu.roll(acc, HEAD_DIM, 1)
        for j in range(GQA_GROUP):
            h = g * GQA_GROUP + j
            o_ref[:, h * HEAD_DIM:(h + 1) * HEAD_DIM] = o[j * tq:(j + 1) * tq, 0:HEAD_DIM].astype(o_ref.dtype)


def _dense_attn_call(q, kt, v1, bsz, seq):
    tq, tk = min(seq, 512), _key_chunk(seq)
    nq = seq // tq
    nc = seq // tk
    rows = GQA_GROUP * tq
    per_group = lambda shape, dtype: [pltpu.VMEM(shape, dtype) for _ in range(N_KV_HEADS)]
    return pl.pallas_call(
        functools.partial(_dense_attn_kernel, tq, tk),
        out_shape=jax.ShapeDtypeStruct((bsz * seq, ATTN_WIDTH), BF16),
        grid=(bsz, nq),
        in_specs=[
            pl.BlockSpec((tq, ATTN_WIDTH), lambda b, i: (b * nq + i, 0)),
            pl.BlockSpec((nc, KV_WIDTH, tk), lambda b, i: (b, 0, 0), pipeline_mode=pl.Buffered(1)),
            pl.BlockSpec((seq, 2 * KV_WIDTH), lambda b, i: (b, 0), pipeline_mode=pl.Buffered(1)),
        ],
        out_specs=pl.BlockSpec((tq, ATTN_WIDTH), lambda b, i: (b * nq + i, 0)),
        scratch_shapes=[
            pltpu.VMEM((N_KV_HEADS, rows, HEAD_DIM), BF16),
            per_group((rows, tk), F32), per_group((rows, tk), BF16),
            per_group((rows, LANES), F32), per_group((rows, LANES), F32),
            per_group((rows, LANES), F32), per_group((rows, LANES), F32),
        ],
        compiler_params=_cparams(("arbitrary", "arbitrary")),
        name="dense_gqa_attention",
    )(q, kt, v1)


def _hp_dot(a, b):
    return jnp.dot(a, b, preferred_element_type=F32, precision=lax.Precision.HIGHEST)


def _filter_trunk_kernel(z_ref, w1_ref, b1_ref, w2_ref, b2_ref, w3_ref, b3_ref, fq_ref, o_ref):
    fq = fq_ref[...]
    h = jnp.sin(fq * (_hp_dot(w1_ref[...], z_ref[...]) + b1_ref[...]))
    h = jnp.sin(fq * (_hp_dot(w2_ref[...], h) + b2_ref[...]))
    h = jnp.sin(fq * (_hp_dot(w3_ref[...], h) + b3_ref[...]))
    o_ref[...] = h.T


def _filter_trunk_call(zfeat_t, w1, b1, w2, b2, w3, b3, freq):
    rows = zfeat_t.shape[1]
    tr = min(rows, 2048)
    fw = HYENA_FILTER_WIDTH
    col = lambda v: v.reshape(fw, 1)
    return pl.pallas_call(
        _filter_trunk_kernel,
        out_shape=jax.ShapeDtypeStruct((rows, fw), F32),
        grid=(rows // tr,),
        in_specs=[
            pl.BlockSpec((HYENA_EMB_PAD, tr), lambda i: (0, i)),
            _resident((fw, HYENA_EMB_PAD)), _resident((fw, 1)),
            _resident((fw, fw)), _resident((fw, 1)),
            _resident((fw, fw)), _resident((fw, 1)),
            _resident((fw, 1)),
        ],
        out_specs=pl.BlockSpec((tr, fw), lambda i: (i, 0)),
        compiler_params=_cparams(("arbitrary",)),
        name="hyena_filter_trunk",
    )(zfeat_t, w1.T, col(b1), w2.T, col(b2), w3.T, col(b3), col(freq))


def _dft_stage1(src_ref, tab_ref, re_ref, im_ref, n_rows, k1p):
    def body(n2, carry):
        xs = src_ref[pl.ds(n2, n_rows, stride=DFT_N2), :].astype(BF16)
        res = jnp.dot(tab_ref[n2], xs, preferred_element_type=F32)
        re_ref[pl.ds(n2, k1p, stride=DFT_N2), :] = res[0:k1p]
        im_ref[pl.ds(n2, k1p, stride=DFT_N2), :] = res[k1p:]
        return carry

    lax.fori_loop(0, DFT_N2, body, 0, unroll=8)


def _filter_spec_kernel(seq, k1p, h_ref, w4f_ref, w4b_ref, dl_ref, e1_ref, f2_ref,
                        kr_ref, ki_ref, kfull):
    n = 2 * seq
    chunk = min(seq, 1024)
    dl = dl_ref[...]

    def fill(c, ss):
        r0 = pl.multiple_of(c * chunk, chunk)
        hc = h_ref[pl.ds(r0, chunk), :].astype(BF16)
        fwd = jnp.dot(hc, w4f_ref[...], preferred_element_type=F32)
        bwd = jnp.dot(hc, w4b_ref[...], preferred_element_type=F32)
        rows = r0 + lax.broadcasted_iota(jnp.int32, (chunk, 1), 0)
        lag = jnp.where(rows < seq, rows, n - rows).astype(F32)
        t = lag * (1.0 / (seq - 1))
        val = jnp.where(rows < seq, fwd, bwd) * jnp.exp(-t * dl)
        val = jnp.where(rows == seq, 0.0, val)
        kfull[pl.ds(r0, chunk), :] = val
        return ss + jnp.sum(val * val, axis=0, keepdims=True)

    ss = lax.fori_loop(0, n // chunk, fill, jnp.zeros((1, dl.shape[1]), F32))
    norm = lax.rsqrt(ss + EPS)

    _dft_stage1(kfull, e1_ref, kr_ref, ki_ref, n // DFT_N2, k1p)

    f2 = f2_ref[...]

    def stage2(a, carry):
        rows = pl.ds(pl.multiple_of(a * DFT_N2, DFT_N2), DFT_N2)
        slab = jnp.concatenate([kr_ref[rows, :], ki_ref[rows, :]], axis=0).astype(BF16)
        xf = jnp.dot(f2, slab, preferred_element_type=F32)
        kr_ref[rows, :] = xf[0:DFT_N2] * norm
        ki_ref[rows, :] = xf[DFT_N2:] * norm
        return carry

    lax.fori_loop(0, seq // DFT_N2 + 1, stage2, 0, unroll=8)


def _filter_spec_call(h3, w4, deltas, e1f, f2f, seq):
    n = 2 * seq
    k1p = e1f.shape[1] // 2
    ct = LANES
    nct = CONV_WIDTH // ct
    fw = HYENA_FILTER_WIDTH
    spec_rows = k1p * DFT_N2
    out = pl.BlockSpec((spec_rows, ct), lambda j: (0, j))
    return pl.pallas_call(
        functools.partial(_filter_spec_kernel, seq, k1p),
        out_shape=(jax.ShapeDtypeStruct((spec_rows, CONV_WIDTH), F32),
                   jax.ShapeDtypeStruct((spec_rows, CONV_WIDTH), F32)),
        grid=(nct,),
        in_specs=[
            _resident((n, fw)),
            pl.BlockSpec((fw, ct), lambda j: (0, j)),
            pl.BlockSpec((fw, ct), lambda j: (0, nct + j)),
            pl.BlockSpec((1, ct), lambda j: (0, j)),
            _resident(e1f.shape),
            _resident((2 * DFT_N2, 2 * DFT_N2)),
        ],
        out_specs=(out, out),
        scratch_shapes=[pltpu.VMEM((n, ct), F32)],
        compiler_params=_cparams(("arbitrary",)),
        name="hyena_filter_spectrum",
    )(h3, w4, w4, deltas, e1f, f2f)


def _long_conv_kernel(seq, k1p, z_ref, x0_ref, kr_ref, ki_ref, skip_ref, e1_ref, f2f_ref, f2i_ref,
                      einv_ref, o_ref, br, bi):
    n1h = seq // DFT_N2
    _dft_stage1(z_ref, e1_ref, br, bi, n1h, k1p)

    f2f = f2f_ref[...]
    f2i = f2i_ref[...]

    def spectral(a, carry):
        r0 = pl.multiple_of(a * DFT_N2, DFT_N2)
        rows = pl.ds(r0, DFT_N2)
        slab = jnp.concatenate([br[rows, :], bi[rows, :]], axis=0).astype(BF16)
        xf = jnp.dot(f2f, slab, preferred_element_type=F32)
        xr, xi = xf[0:DFT_N2], xf[DFT_N2:]
        kr, ki = kr_ref[rows, :], ki_ref[rows, :]
        y = jnp.concatenate([xr * kr - xi * ki, xr * ki + xi * kr], axis=0).astype(BF16)
        d = jnp.dot(f2i, y, preferred_element_type=F32)
        br[rows, :] = d[0:DFT_N2]
        bi[rows, :] = d[DFT_N2:]
        return carry

    lax.fori_loop(0, n1h + 1, spectral, 0, unroll=8)

    skip = skip_ref[...]

    def synth(n2, carry):
        ds_spec = pl.ds(n2, k1p, stride=DFT_N2)
        d = jnp.concatenate([br[ds_spec, :], bi[ds_spec, :]], axis=0).astype(BF16)
        o_ref[pl.ds(n2, n1h, stride=DFT_N2), :] = jnp.dot(einv_ref[n2], d, preferred_element_type=F32)
        return carry

    lax.fori_loop(0, DFT_N2, synth, 0, unroll=8)

    chunk = min(seq, 512)

    def gate(c, carry):
        rows = pl.ds(pl.multiple_of(c * chunk, chunk), chunk)
        o_ref[rows, :] = x0_ref[rows, :] * (o_ref[rows, :] + skip * z_ref[rows, :])
        return carry

    lax.fori_loop(0, seq // chunk, gate, 0)


def _long_conv_call(z, x0, kr, ki, skip, e1, f2f, f2i, einv, bsz, seq):
    ct = LANES
    nct = CONV_WIDTH // ct
    k1p = e1.shape[1] // 2
    spec_rows = k1p * DFT_N2
    big = lambda: pl.BlockSpec((seq, ct), lambda j, b: (b, j), pipeline_mode=pl.Buffered(1))
    spec = lambda: pl.BlockSpec((spec_rows, ct), lambda j, b: (0, j), pipeline_mode=pl.Buffered(1))
    return pl.pallas_call(
        functools.partial(_long_conv_kernel, seq, k1p),
        out_shape=jax.ShapeDtypeStruct((bsz * seq, CONV_WIDTH), F32),
        grid=(nct, bsz),
        in_specs=[
            big(), big(), spec(), spec(),
            pl.BlockSpec((1, ct), lambda j, b: (0, j)),
            _resident(e1.shape),
            _resident((2 * DFT_N2, 2 * DFT_N2)), _resident((2 * DFT_N2, 2 * DFT_N2)),
            _resident(einv.shape),
        ],
        out_specs=pl.BlockSpec((seq, ct), lambda j, b: (b, j)),
        scratch_shapes=[pltpu.VMEM((spec_rows, ct), F32), pltpu.VMEM((spec_rows, ct), F32)],
        compiler_params=_cparams(("arbitrary", "arbitrary")),
        name="hyena_long_conv",
    )(z, x0, kr, ki, skip, e1, f2f, f2i, einv)


def _outproj_kernel(x_ref, att_ref, y_ref, wo_ref, gate_ref, o_ref):
    y = jnp.dot(att_ref[...], wo_ref[0:ATTN_WIDTH, :], preferred_element_type=F32)
    y = y + jnp.dot(y_ref[...].astype(BF16), wo_ref[ATTN_WIDTH:, :], preferred_element_type=F32)
    o_ref[...] = x_ref[...] + gate_ref[0] * y


def _outproj_call(x, att, y, w_out, layer, gate, rows_per_batch):
    m, d = x.shape
    tm = 512
    tpb = rows_per_batch // tm
    row = lambda w: pl.BlockSpec((tm, w), lambda i: (i, 0))
    return pl.pallas_call(
        _outproj_kernel,
        out_shape=jax.ShapeDtypeStruct((m, d), F32),
        grid=(m // tm,),
        in_specs=[row(d), row(ATTN_WIDTH), row(CONV_WIDTH), _layer_slice((d, d), layer),
                  pl.BlockSpec((1, 1, d), lambda i: (i // tpb, 0, 0))],
        out_specs=row(d),
        compiler_params=_cparams(("arbitrary",)),
        name="mixer_out_projection",
    )(x, att, y, w_out, gate)


def _rope_tables(seq):
    t = jnp.arange(seq)
    row = (t // GRID_W).astype(F32)
    col = (t % GRID_W).astype(F32)
    half = HEAD_DIM // 2
    inv = ROPE_THETA ** (-jnp.arange(0, half, 2, dtype=F32) / half)
    ang = jnp.concatenate([row[:, None] * inv, col[:, None] * inv], axis=-1)
    ang = jnp.repeat(ang, 2, axis=-1)
    sign = jnp.where(jnp.arange(HEAD_DIM) % 2 == 0, -1.0, 1.0).astype(F32)
    reps = LANES // HEAD_DIM
    return jnp.tile(jnp.cos(ang), (1, reps)), jnp.tile(jnp.sin(ang) * sign, (1, reps))


def _dft_tables(seq):
    n = 2 * seq
    n1 = n // DFT_N2
    k1 = n1 // 2 + 1
    k1p = -(-k1 // 8) * 8
    a = jnp.arange(k1p)
    live = (a < k1).astype(F32)
    ang1 = ((a[:, None] * jnp.arange(n1)[None, :]) % n1).astype(F32) * (2.0 * math.pi / n1)
    ang2 = (jnp.arange(DFT_N2)[:, None] * a[None, :]).astype(F32) * (2.0 * math.pi / n)
    c1, s1 = jnp.cos(ang1) * live[:, None], jnp.sin(ang1) * live[:, None]
    c2, s2 = jnp.cos(ang2), jnp.sin(ang2)
    cos_a = c1[None] * c2[:, :, None] - s1[None] * s2[:, :, None]
    sin_a = s1[None] * c2[:, :, None] + c1[None] * s2[:, :, None]
    e1_full = jnp.concatenate([cos_a, -sin_a], axis=1).astype(BF16)
    e1_half = e1_full[:, :, :n1 // 2]
    wgt = jnp.where((a == 0) | (a == n1 // 2), 1.0, 2.0) / n
    c1t, s1t = (c1 * wgt[:, None]).T[:n1 // 2], (s1 * wgt[:, None]).T[:n1 // 2]
    cos_s = c1t[None] * c2[:, None, :] - s1t[None] * s2[:, None, :]
    sin_s = s1t[None] * c2[:, None, :] + c1t[None] * s2[:, None, :]
    einv = jnp.concatenate([cos_s, -sin_s], axis=2).astype(BF16)
    kk = jnp.arange(DFT_N2)
    phi = ((kk[:, None] * kk[None, :]) % DFT_N2).astype(F32) * (2.0 * math.pi / DFT_N2)
    ci, si = jnp.cos(phi), jnp.sin(phi)
    f2f = jnp.block([[ci, si], [-si, ci]]).astype(BF16)
    f2i = jnp.block([[ci, -si], [si, ci]]).astype(BF16)
    return e1_full, e1_half, einv, f2f, f2i


def _filter_features_t(seq):
    j = jnp.arange(2 * seq)
    lag = jnp.where(j < seq, j, jnp.where(j == seq, 0, 2 * seq - j)).astype(F32)[None, :]
    t = lag / (seq - 1)
    bands = (HYENA_EMB - 1) // 2
    w = (2.0 * math.pi / seq) * lag
    fr = jnp.linspace(1e-4, bands - 1, bands, dtype=F32)[:, None]
    pad = jnp.zeros((HYENA_EMB_PAD - HYENA_EMB, 2 * seq), F32)
    return jnp.concatenate([t, jnp.cos(fr * w), -jnp.sin(fr * w), pad], axis=0)


def kernel(x, c, rel_table, norm_g, w_mod, b_mod, w_in, w_out, ffn_w1, ffn_w3, ffn_w2, a_qk_g, a_sink,
           b_conv_w, b_conv_b, c_qk_g, d_conv_w, d_conv_b, d_f_w1, d_f_b1, d_f_w2, d_f_b2, d_f_w3,
           d_f_b3, d_f_w4, d_f_freq, d_skip):
    bsz, seq, d = x.shape
    depth = w_mod.shape[0]
    m_rows = bsz * seq
    mod =_mod_call(c, w_mod, b_mod).reshape(depth, bsz, 3, 3, 1, d)
    e_blk = jnp.kron(jnp.eye(MXU_DIM // HEAD_DIM, dtype=F32), jnp.ones((HEAD_DIM, HEAD_DIM), F32)).astype(BF16)

    w1b, w3b, w2b, w_in_b, w_out_b = (w.astype(BF16) for w in (ffn_w1, ffn_w3, ffn_w2, w_in, w_out))

    def ffn(xf, l, which, sub):
        return _ffn_call(xf, norm_g[l, sub][None], mod[l, :, sub, 0], mod[l, :, sub, 1], mod[l, :, sub, 2],
                         w1b, w3b, w2b, l, which, seq)

    xf = x.reshape(m_rows, d)
    for l in range(depth):
        j = l // 2
        xf = ffn(xf, l, 0, 0)
        even = l % 2 == 0
        qk_g = a_qk_g[j] if even else c_qk_g[j]
        hyena = None if even else _rope_tables(seq) + (d_conv_w[j], d_conv_b[j][None])
        q, k, v, *conv_in = _inproj_call(
            xf, norm_g[l, 1][None], mod[l, :, 1, 0], mod[l, :, 1, 1], w_in_b, l,
            jnp.tile(qk_g[0], N_Q_HEADS)[None], jnp.tile(qk_g[1], N_KV_HEADS)[None], e_blk, seq, hyena)
        gate = mod[l, :, 1, 2]
        if even:
            (u,) = conv_in
            bias, sink_rows = _bias_call(rel_table, a_sink[j])
            xf = _mixer_ab_call(xf, q, k, v, u, bias, sink_rows, b_conv_w[j], b_conv_b[j][None], w_out_b, l,
                                gate, seq)
        else:
            x0, z = conv_in
            att = _dense_attn_call(q, k, v, bsz, seq)
            e1_full, e1_half, einv, f2f, f2i = _dft_tables(seq)
            fw1 = jnp.pad(d_f_w1[j], ((0, HYENA_EMB_PAD - HYENA_EMB), (0, 0)))
            h3 = _filter_trunk_call(_filter_features_t(seq), fw1, d_f_b1[j], d_f_w2[j], d_f_b2[j],
                                    d_f_w3[j], d_f_b3[j], d_f_freq[j])
            deltas = jnp.abs(jnp.linspace(HYENA_MIN_DECAY, HYENA_MAX_DECAY, CONV_WIDTH, dtype=F32))[None]
            kr, ki = _filter_spec_call(h3, d_f_w4[j].astype(BF16), deltas, e1_full, f2f, seq)
            y = _long_conv_call(z, x0, kr, ki, d_skip[j][None], e1_half, f2f, f2i, einv, bsz, seq)
            xf = _outproj_call(xf, att, y, w_out_b, l, gate, seq)
        xf = ffn(xf, l, 1, 2)
    return xf.reshape(bsz, seq, d)
```

```python
import functools
import math

import jax
import jax.numpy as jnp
import numpy as np
from jax import lax
from jax.experimental import pallas as pl
from jax.experimental.pallas import tpu as pltpu

D_MODEL = 1024
HEAD_DIM = 64
N_Q_HEADS = 8
N_KV_HEADS = 2
GQA_GROUP = N_Q_HEADS // N_KV_HEADS
ATTN_WIDTH = N_Q_HEADS * HEAD_DIM
KV_WIDTH = N_KV_HEADS * HEAD_DIM
QKV_COLS = ATTN_WIDTH + 2 * KV_WIDTH
CONV_WIDTH = D_MODEL - ATTN_WIDTH
IN_COLS = QKV_COLS + 3 * CONV_WIDTH
D_FF = 2752
BLOCK = 128
WINDOW = 128
N_BUCKETS = 32
MAX_DISTANCE = 128
GRID_W = 64
ROPE_THETA = 10000.0
HYENA_EMB = 33
HYENA_FILTER_WIDTH = 64
HYENA_EMB_PAD = 64
HYENA_MIN_DECAY = math.log(1e-2) / 0.3
HYENA_MAX_DECAY = math.log(1e-2) / 1.5
EPS = 1e-6
NEG_INF = -1e30
LOG2_E = 1.4426950408889634

LANES = 128
HALO = 16
MXU_DIM = 256
FF_CHUNK = MXU_DIM
VMEM_LIMIT = 56 * 1024 * 1024
DFT_N2 = 128

BF16 = jnp.bfloat16
F32 = jnp.float32


def _cparams(sem):
    return pltpu.CompilerParams(dimension_semantics=sem, vmem_limit_bytes=VMEM_LIMIT)


def _resident(shape):
    nd = len(shape)
    return pl.BlockSpec(shape, lambda *_: (0,) * nd, pipeline_mode=pl.Buffered(1))


def _layer_slice(shape, layer):
    return pl.BlockSpec((None,) + shape, lambda *_: (layer, 0, 0), pipeline_mode=pl.Buffered(1))


def _adaln(x, g, scale, shift):
    y = x * lax.rsqrt(jnp.mean(x * x, axis=-1, keepdims=True) + EPS) * g
    return y * (1.0 + scale) + shift


def _mod_kernel(c_ref, w_ref, b_ref, o_ref):
    w = w_ref[0]
    for b in range(c_ref.shape[0]):
        cc = c_ref[b]
        cond = cc * jax.nn.sigmoid(cc)
        o_ref[0, b:b + 1, :] = jnp.sum(w * cond, axis=0, keepdims=True) + b_ref[0]


def _mod_call(c, w_mod, b_mod):
    depth, d, n = w_mod.shape
    bsz = c.shape[0]
    tn = 1152 if n % 1152 == 0 else 512
    return pl.pallas_call(
        _mod_kernel,
        out_shape=jax.ShapeDtypeStruct((depth, bsz, n), F32),
        grid=(depth, n // tn),
        in_specs=[
            pl.BlockSpec((bsz, d, 1), lambda l, j: (0, 0, 0)),
            pl.BlockSpec((1, d, tn), lambda l, j: (l, 0, j)),
            pl.BlockSpec((1, 1, tn), lambda l, j: (l, 0, j)),
        ],
        out_specs=pl.BlockSpec((1, bsz, tn), lambda l, j: (l, 0, j)),
        compiler_params=_cparams(("arbitrary", "arbitrary")),
        name="adaln_modulation",
    )(c.reshape(bsz, d, 1), w_mod, b_mod.reshape(depth, 1, n))


def _cast_kernel(w_ref, o_ref):
    o_ref[...] = w_ref[...].astype(o_ref.dtype)


def _to_bf16(w):
    rows, cols = w.shape[-2:]
    w3d = w.reshape(-1, rows, cols)
    steps = next(s for s in (4, 2, 1) if rows % (16 * s) == 0)
    blk = pl.BlockSpec((1, rows // steps, cols), lambda n, r: (n, r, 0))
    out = pl.pallas_call(
        _cast_kernel,
        out_shape=jax.ShapeDtypeStruct(w3d.shape, BF16),
        grid=(w3d.shape[0], steps),
        in_specs=[blk], out_specs=blk,
        compiler_params=_cparams(("arbitrary", "arbitrary")),
        name="weights_to_bf16",
    )(w3d)
    return out.reshape(w.shape)


def _ffn_kernel(mixer_update, x_ref, g_ref, shift_ref, scale_ref, gate_ref, w1_ref, w3_ref, w2_ref, *rest):
    x = x_ref[...]
    if mixer_update:
        att_ref, y_ref, wo_ref, mgate_ref, o_ref = rest
        upd = jnp.dot(att_ref[...], wo_ref[0:ATTN_WIDTH, :], preferred_element_type=F32)
        upd = upd + jnp.dot(y_ref[...].astype(BF16), wo_ref[ATTN_WIDTH:, :], preferred_element_type=F32)
        x = x + mgate_ref[0] * upd
    else:
        (o_ref,) = rest
    h = _adaln(x, g_ref[...], scale_ref[0], shift_ref[0]).astype(BF16)
    acc = jnp.zeros(x.shape, F32)
    for c0 in range(0, D_FF, FF_CHUNK):
        cols = slice(c0, min(c0 + FF_CHUNK, D_FF))
        a = jnp.dot(h, w1_ref[:, cols], preferred_element_type=F32)
        b = jnp.dot(h, w3_ref[:, cols], preferred_element_type=F32)
        act = (a * jax.nn.sigmoid(a) * b).astype(BF16)
        acc = acc + jnp.dot(act, w2_ref[cols, :], preferred_element_type=F32)
    o_ref[...] = x + (0.5 * gate_ref[0]) * acc


def _ffn_call(x, g, shift, scale, gate, w1, w3, w2, layer, which, rows_per_batch, mixer=None):
    m, d = x.shape
    tm = 512
    tpb = rows_per_batch // tm
    row = lambda w: pl.BlockSpec((tm, w), lambda i: (i, 0))
    vec = pl.BlockSpec((1, 1, d), lambda i: (i // tpb, 0, 0))
    weight = lambda r, c: pl.BlockSpec((None, None, r, c), lambda i: (layer, which, 0, 0),
                                       pipeline_mode=pl.Buffered(1))
    in_specs = [row(d), _resident((1, d)), vec, vec, vec, weight(d, D_FF), weight(d, D_FF), weight(D_FF, d)]
    args = [x, g, shift, scale, gate, w1, w3, w2]
    if mixer is not None:
        in_specs += [row(ATTN_WIDTH), row(CONV_WIDTH), _layer_slice((d, d), layer), vec]
        args += list(mixer)
    return pl.pallas_call(
        functools.partial(_ffn_kernel, mixer is not None),
        out_shape=jax.ShapeDtypeStruct((m, d), F32),
        grid=(m // tm,),
        in_specs=in_specs,
        out_specs=row(d),
        compiler_params=_cparams(("arbitrary",)),
        name="adaln_swiglu_ffn",
    )(*args)


def _group_norm_scale(v, e_ref):
    sq = v * v
    hi = sq.astype(BF16)
    lo = (sq - hi.astype(F32)).astype(BF16)
    w = min(v.shape[1], MXU_DIM)
    e = e_ref[...]
    ss = jnp.concatenate(
        [jnp.dot(hi[:, c:c + w], e[0:w, 0:w], preferred_element_type=F32)
         + jnp.dot(lo[:, c:c + w], e[0:w, 0:w], preferred_element_type=F32) for c in range(0, v.shape[1], w)],
        axis=1)
    return lax.rsqrt(ss * (1.0 / HEAD_DIM) + EPS)


def _rope128(v, cos, sin_signed, even_lane):
    partner = jnp.where(even_lane, pltpu.roll(v, LANES - 1, 1), pltpu.roll(v, 1, 1))
    return v * cos + partner * sin_signed


def _short_conv_rows(u, before, after, cw, cb):
    rows = u.shape[0]
    r = lax.broadcasted_iota(jnp.int32, u.shape, 0)
    um1 = jnp.where(r == 0, before, pltpu.roll(u, 1, 0))
    up1 = jnp.where(r == rows - 1, after, pltpu.roll(u, rows - 1, 0))
    return cw[0:1] * um1 + cw[1:2] * u + cw[2:3] * up1 + cb


def _inproj_kernel(rope, tpb, x_ref, g_ref, shift_ref, scale_ref, w_ref, qg_ref, kg_ref, e_ref, *rest):
    if rope:
        cos_ref, sin_ref, xp_ref, xn_ref, cw_ref, cb_ref, q_ref, k_ref, v_ref, x0_ref, z_ref = rest
    else:
        q_ref, k_ref, v_ref, u_ref = rest
    h = _adaln(x_ref[...], g_ref[...], scale_ref[0], shift_ref[0]).astype(BF16)
    q = jnp.dot(h, w_ref[:, 0:ATTN_WIDTH], preferred_element_type=F32)
    k = jnp.dot(h, w_ref[:, ATTN_WIDTH:ATTN_WIDTH + KV_WIDTH], preferred_element_type=F32)
    q = q * _group_norm_scale(q, e_ref) * qg_ref[...]
    k = k * _group_norm_scale(k, e_ref) * kg_ref[...]
    if rope:
        cos = cos_ref[...]
        sin = sin_ref[...]
        even = (lax.broadcasted_iota(jnp.int32, cos.shape, 1) % 2) == 0
        q = jnp.concatenate(
            [_rope128(q[:, j * LANES:(j + 1) * LANES], cos, sin, even) for j in range(ATTN_WIDTH // LANES)],
            axis=1)
        k = _rope128(k, cos, sin, even)
    q_ref[...] = (q * (HEAD_DIM ** -0.5 * (LOG2_E if rope else 1.0))).astype(BF16)
    v = jnp.dot(h, w_ref[:, ATTN_WIDTH + KV_WIDTH:QKV_COLS], preferred_element_type=F32)
    k_ref[0] = k.T.astype(BF16)
    low = lax.broadcasted_iota(jnp.int32, v.shape, 1) < HEAD_DIM
    v_ref[...] = jnp.concatenate(
        [jnp.where(low, v, 1.0), jnp.where(low, pltpu.roll(v, HEAD_DIM, 1), 1.0)], axis=1).astype(BF16)
    if not rope:
        u_ref[...] = jnp.dot(h, w_ref[:, QKV_COLS:IN_COLS], preferred_element_type=F32)
        return
    i = pl.program_id(0)
    h_prev = _adaln(xp_ref[...], g_ref[...], scale_ref[0], shift_ref[0]).astype(BF16)
    h_next = _adaln(xn_ref[...], g_ref[...], scale_ref[0], shift_ref[0]).astype(BF16)
    u_ext = jnp.dot(jnp.concatenate([h_prev, h, h_next], axis=0), w_ref[:, QKV_COLS:IN_COLS],
                    preferred_element_type=F32)
    tm = h.shape[0]
    before = jnp.where((i % tpb) == 0, 0.0, u_ext[HALO - 1:HALO])
    after = jnp.where((i % tpb) == tpb - 1, 0.0, u_ext[HALO + tm:HALO + tm + 1])
    t = _short_conv_rows(u_ext[HALO:HALO + tm], before, after, cw_ref[...], cb_ref[...])
    x0_ref[...] = t[:, 0:CONV_WIDTH]
    z_ref[...] = t[:, CONV_WIDTH:2 * CONV_WIDTH] * t[:, 2 * CONV_WIDTH:]


def _inproj_call(x, g, shift, scale, w_in, layer, qg, kg, e, rows_per_batch, hyena=None):
    m, d = x.shape
    tm = 512
    tpb = rows_per_batch // tm
    vec = pl.BlockSpec((1, 1, d), lambda i: (i // tpb, 0, 0))
    row = lambda w: pl.BlockSpec((tm, w), lambda i: (i, 0))
    in_specs = [
        row(d),
        _resident((1, d)), vec, vec,
        _layer_slice((d, IN_COLS), layer),
        _resident((1, ATTN_WIDTH)), _resident((1, KV_WIDTH)),
        _resident((MXU_DIM, MXU_DIM)),
    ]
    args = [x, g, shift, scale, w_in, qg, kg, e]
    uw = 3 * CONV_WIDTH
    if hyena is not None:
        cos, sin, conv_w, conv_b = hyena
        tab = pl.BlockSpec((tm, LANES), lambda i: (i % tpb, 0))
        hpt = tm // HALO
        last_halo = m // HALO - 1
        in_specs += [
            tab, tab,
            pl.BlockSpec((HALO, d), lambda i: (jnp.maximum(i * hpt - 1, 0), 0)),
            pl.BlockSpec((HALO, d), lambda i: (jnp.minimum((i + 1) * hpt, last_halo), 0)),
            _resident((3, uw)), _resident((1, uw)),
        ]
        args += [cos, sin, x, x, conv_w, conv_b]
        conv_shapes = [jax.ShapeDtypeStruct((m, CONV_WIDTH), F32)] * 2
        conv_specs = [row(CONV_WIDTH)] * 2
    else:
        conv_shapes = [jax.ShapeDtypeStruct((m, uw), F32)]
        conv_specs = [row(uw)]
    tk = _key_chunk(rows_per_batch)
    per = tk // tm
    return pl.pallas_call(
        functools.partial(_inproj_kernel, hyena is not None, tpb),
        out_shape=[
            jax.ShapeDtypeStruct((m, ATTN_WIDTH), BF16),
            jax.ShapeDtypeStruct((m // tk, KV_WIDTH, tk), BF16),
            jax.ShapeDtypeStruct((m, 2 * KV_WIDTH), BF16),
        ] + conv_shapes,
        grid=(m // tm,),
        in_specs=in_specs,
        out_specs=[row(ATTN_WIDTH), pl.BlockSpec((1, KV_WIDTH, tm), lambda i: (i // per, 0, i % per)),
                   row(2 * KV_WIDTH)] + conv_specs,
        compiler_params=_cparams(("arbitrary",)),
        name="adaln_in_projection",
    )(*args)


_T5_STEPS = (12, 16, 23, 32, 46, 64, 91)


def _bias_kernel(tab_ref, sink_ref, o_ref, sink_rows_ref):
    for h in range(N_Q_HEADS):
        sink_rows_ref[h * BLOCK:(h + 1) * BLOCK, :] = jnp.full((BLOCK, LANES), sink_ref[h], F32)
    qi = lax.broadcasted_iota(jnp.int32, (BLOCK, 3 * BLOCK), 0)
    kj = lax.broadcasted_iota(jnp.int32, (BLOCK, 3 * BLOCK), 1)
    rel = kj - BLOCK - qi
    n = jnp.abs(rel)
    half = N_BUCKETS // 2
    max_exact = half // 2
    large = jnp.full(n.shape, max_exact, jnp.int32)
    for t in _T5_STEPS:
        large = large + (n >= t).astype(jnp.int32)
    bucket = jnp.where(rel > 0, half, 0) + jnp.where(n < max_exact, n, large)
    for h in range(N_Q_HEADS):
        bias = jnp.zeros(n.shape, F32)
        for b in range(N_BUCKETS):
            bias = jnp.where(bucket == b, tab_ref[b, h], bias)
        o_ref[h] = jnp.where(n <= WINDOW, bias, NEG_INF)


def _bias_call(rel_table, sink):
    return pl.pallas_call(
        _bias_kernel,
        out_shape=(jax.ShapeDtypeStruct((N_Q_HEADS, BLOCK, 3 * BLOCK), F32),
                   jax.ShapeDtypeStruct((N_Q_HEADS * BLOCK, LANES), F32)),
        in_specs=[pl.BlockSpec(memory_space=pltpu.SMEM), pl.BlockSpec(memory_space=pltpu.SMEM)],
        name="t5_bias_tile",
    )(rel_table, sink)


def _mixer_ab_kernel(tq, tpb, x_ref, q_ref, kc_ref, kp_ref, kn_ref, vc_ref, vp_ref, vn_ref,
                     uc_ref, up_ref, un_ref, bias_ref, sink_ref, cw_ref, cb_ref, wo_ref, gate_ref,
                     o_ref, kbuf, vbuf, qs, att):
    i = pl.program_id(0)
    first = (i % tpb) == 0
    last = (i % tpb) == tpb - 1
    nblk = tq // BLOCK
    grows = GQA_GROUP * BLOCK
    kbuf[:, 0:BLOCK] = kp_ref[0]
    kbuf[:, BLOCK:BLOCK + tq] = kc_ref[0]
    kbuf[:, BLOCK + tq:] = kn_ref[0]
    vbuf[0:BLOCK] = vp_ref[...]
    vbuf[BLOCK:BLOCK + tq] = vc_ref[...]
    vbuf[BLOCK + tq:] = vn_ref[...]
    for n in range(nblk):
        for h in range(N_Q_HEADS):
            g, j = divmod(h, GQA_GROUP)
            qs[g, n, j * BLOCK:(j + 1) * BLOCK, :] = q_ref[n * BLOCK:(n + 1) * BLOCK, h * HEAD_DIM:(h + 1) * HEAD_DIM]
    col = lax.broadcasted_iota(jnp.int32, (grows, 3 * BLOCK), 1)
    for n in range(nblk):
        keys = slice(n * BLOCK, (n + 3) * BLOCK)
        for g in range(N_KV_HEADS):
            s = jnp.dot(qs[g, n], kbuf[g * HEAD_DIM:(g + 1) * HEAD_DIM, keys], preferred_element_type=F32)
            s = s + bias_ref[g * GQA_GROUP:(g + 1) * GQA_GROUP].reshape(grows, 3 * BLOCK)
            if n == 0:
                s = jnp.where(jnp.logical_and(first, col < BLOCK), NEG_INF, s)
            if n == nblk - 1:
                s = jnp.where(jnp.logical_and(last, col >= 2 * BLOCK), NEG_INF, s)
            sk = sink_ref[g * grows:(g + 1) * grows, :]
            mx = jnp.maximum(jnp.broadcast_to(jnp.max(s, axis=-1, keepdims=True), sk.shape), sk)
            p = jnp.exp(s - jnp.tile(mx, (1, 3)))
            pv = jnp.dot(p.astype(BF16), vbuf[keys, g * LANES:(g + 1) * LANES], preferred_element_type=F32)
            o = pv / (pltpu.roll(pv, HEAD_DIM, 1) + jnp.exp(sk - mx))
            for j in range(GQA_GROUP):
                h = g * GQA_GROUP + j
                att[n * BLOCK:(n + 1) * BLOCK, h * HEAD_DIM:(h + 1) * HEAD_DIM] = o[j * BLOCK:(j + 1) * BLOCK, 0:HEAD_DIM]

    gb = uc_ref[:, 0:CONV_WIDTH]
    p = uc_ref[:, CONV_WIDTH:2 * CONV_WIDTH] * uc_ref[:, 2 * CONV_WIDTH:]
    p_before = jnp.where(first, 0.0, up_ref[7:8, CONV_WIDTH:2 * CONV_WIDTH] * up_ref[7:8, 2 * CONV_WIDTH:])
    p_after = jnp.where(last, 0.0, un_ref[0:1, CONV_WIDTH:2 * CONV_WIDTH] * un_ref[0:1, 2 * CONV_WIDTH:])
    conv = gb * _short_conv_rows(p, p_before, p_after, cw_ref[...], cb_ref[...])

    y = jnp.dot(att[...].astype(BF16), wo_ref[0:ATTN_WIDTH, :], preferred_element_type=F32)
    y = y + jnp.dot(conv.astype(BF16), wo_ref[ATTN_WIDTH:, :], preferred_element_type=F32)
    o_ref[...] = x_ref[...] + gate_ref[0] * y


def _mixer_ab_call(x, q, kt, v1, u, bias, sink_rows, conv_w, conv_b, w_out, layer, gate, rows_per_batch):
    m, d = x.shape
    tq = 512
    tpb = rows_per_batch // tq
    r = tq // BLOCK
    nb = m // BLOCK
    n8 = m // 8
    cur = lambda w: pl.BlockSpec((tq, w), lambda i: (i, 0))
    prev_of = lambda i: jnp.maximum(i * r - 1, 0)
    next_of = lambda i: jnp.minimum((i + 1) * r, nb - 1)
    v_prev = pl.BlockSpec((BLOCK, 2 * KV_WIDTH), lambda i: (prev_of(i), 0))
    v_next = pl.BlockSpec((BLOCK, 2 * KV_WIDTH), lambda i: (next_of(i), 0))
    tk = kt.shape[2]
    k_cur = pl.BlockSpec((1, KV_WIDTH, tq), lambda i: (i // (tk // tq), 0, i % (tk // tq)))
    kpb = tk // BLOCK
    k_prev = pl.BlockSpec((1, KV_WIDTH, BLOCK), lambda i: (prev_of(i) // kpb, 0, prev_of(i) % kpb))
    k_next = pl.BlockSpec((1, KV_WIDTH, BLOCK), lambda i: (next_of(i) // kpb, 0, next_of(i) % kpb))
    uw = 3 * CONV_WIDTH
    return pl.pallas_call(
        functools.partial(_mixer_ab_kernel, tq, tpb),
        out_shape=jax.ShapeDtypeStruct((m, d), F32),
        grid=(m // tq,),
        in_specs=[
            cur(d), cur(ATTN_WIDTH),
            k_cur, k_prev, k_next,
            cur(2 * KV_WIDTH), v_prev, v_next,
            cur(uw),
            pl.BlockSpec((8, uw), lambda i: (jnp.maximum(i * (tq // 8) - 1, 0), 0)),
            pl.BlockSpec((8, uw), lambda i: (jnp.minimum((i + 1) * (tq // 8), n8 - 1), 0)),
            _resident((N_Q_HEADS, BLOCK, 3 * BLOCK)),
            _resident((N_Q_HEADS * BLOCK, LANES)),
            _resident((3, CONV_WIDTH)), _resident((1, CONV_WIDTH)),
            _layer_slice((d, d), layer),
            pl.BlockSpec((1, 1, d), lambda i: (i // tpb, 0, 0)),
        ],
        out_specs=cur(d),
        scratch_shapes=[
            pltpu.VMEM((KV_WIDTH, tq + 2 * BLOCK), BF16),
            pltpu.VMEM((tq + 2 * BLOCK, 2 * KV_WIDTH), BF16),
            pltpu.VMEM((N_KV_HEADS, r, GQA_GROUP * BLOCK, HEAD_DIM), BF16),
            pltpu.VMEM((tq, ATTN_WIDTH), F32),
        ],
        compiler_params=_cparams(("arbitrary",)),
        name="windowed_attn_shortconv_outproj",
    )(x, q, kt, kt, kt, v1, v1, v1, u, u, u, bias, sink_rows, conv_w, conv_b, w_out, gate)


def _key_chunk(seq):
    return min(seq, 1024)


def _dense_attn_kernel(tq, tk, q_ref, k_ref, v_ref, o_ref, qs, s_buf, p_buf, rmax_buf, alpha_buf,
                       m_ref, acc_ref):
    nc = k_ref.shape[0]
    groups = range(N_KV_HEADS)
    for h in range(N_Q_HEADS):
        g, j = divmod(h, GQA_GROUP)
        qs[g, j * tq:(j + 1) * tq, :] = q_ref[:, h * HEAD_DIM:(h + 1) * HEAD_DIM]
    for g in groups:
        m_ref[g][...] = jnp.full(m_ref[g].shape, -jnp.inf, F32)
        acc_ref[g][...] = jnp.zeros(acc_ref[g].shape, F32)

    def scores(g, c):
        s = jnp.dot(qs[g], k_ref[c, g * HEAD_DIM:(g + 1) * HEAD_DIM, :], preferred_element_type=F32)
        s_buf[g][...] = s
        rmax_buf[g][...] = jnp.broadcast_to(jnp.max(s, axis=-1, keepdims=True), rmax_buf[g].shape)

    def softmax(g):
        m_old = m_ref[g][...]
        m_new = jnp.maximum(m_old, rmax_buf[g][...])
        alpha_buf[g][...] = jnp.exp2(m_old - m_new)
        m_ref[g][...] = m_new
        p = jnp.exp2(s_buf[g][...] - jnp.tile(m_new, (1, tk // LANES)))
        p_buf[g][...] = p.astype(BF16)

    def weighted_values(g, c):
        start = c * tk if isinstance(c, int) else pl.multiple_of(c * tk, tk)
        vc = v_ref[pl.ds(start, tk), g * LANES:(g + 1) * LANES]
        pv = jnp.dot(p_buf[g][...], vc, preferred_element_type=F32)
        acc_ref[g][...] = alpha_buf[g][...] * acc_ref[g][...] + pv

    def step(t):
        static = isinstance(t, int)
        for g in groups:
            weighted_values(g, t)
            if not static or t + 1 < nc:
                softmax(g)
            if not static or t + 2 < nc:
                scores(g, t + 2)

    for g in groups:
        scores(g, 0)
        softmax(g)
        if nc > 1:
            scores(g, 1)

    def steady(t, carry):
        step(t)
        return carry

    lax.fori_loop(0, max(nc - 2, 0), steady, 0)
    for t in range(max(nc - 2, 0), nc):
        step(t)

    for g in groups:
        acc = acc_ref[g][...]
        o = acc / pltpu.roll(acc, HEAD_DIM, 1)
        for j in range(GQA_GROUP):
            h = g * GQA_GROUP + j
            o_ref[:, h * HEAD_DIM:(h + 1) * HEAD_DIM] = o[j * tq:(j + 1) * tq, 0:HEAD_DIM].astype(o_ref.dtype)


def _dense_attn_call(q, kt, v1, bsz, seq):
    tq, tk = min(seq, 512), _key_chunk(seq)
    nq = seq // tq
    nc = seq // tk
    rows = GQA_GROUP * tq
    per_group = lambda shape, dtype: [pltpu.VMEM(shape, dtype) for _ in range(N_KV_HEADS)]
    return pl.pallas_call(
        functools.partial(_dense_attn_kernel, tq, tk),
        out_shape=jax.ShapeDtypeStruct((bsz * seq, ATTN_WIDTH), BF16),
        grid=(bsz, nq),
        in_specs=[
            pl.BlockSpec((tq, ATTN_WIDTH), lambda b, i: (b * nq + i, 0)),
            pl.BlockSpec((nc, KV_WIDTH, tk), lambda b, i: (b, 0, 0), pipeline_mode=pl.Buffered(1)),
            pl.BlockSpec((seq, 2 * KV_WIDTH), lambda b, i: (b, 0), pipeline_mode=pl.Buffered(1)),
        ],
        out_specs=pl.BlockSpec((tq, ATTN_WIDTH), lambda b, i: (b * nq + i, 0)),
        scratch_shapes=[
            pltpu.VMEM((N_KV_HEADS, rows, HEAD_DIM), BF16),
            per_group((rows, tk), F32), per_group((rows, tk), BF16),
            per_group((rows, LANES), F32), per_group((rows, LANES), F32),
            per_group((rows, LANES), F32), per_group((rows, LANES), F32),
        ],
        compiler_params=_cparams(("arbitrary", "arbitrary")),
        name="dense_gqa_attention",
    )(q, kt, v1)


def _hp_dot(a, b):
    return jnp.dot(a, b, preferred_element_type=F32, precision=lax.Precision.HIGHEST)


def _filter_trunk_kernel(z_ref, w1_ref, b1_ref, w2_ref, b2_ref, w3_ref, b3_ref, fq_ref, o_ref):
    fq = fq_ref[...]
    h = jnp.sin(fq * (_hp_dot(w1_ref[...], z_ref[...]) + b1_ref[...]))
    h = jnp.sin(fq * (_hp_dot(w2_ref[...], h) + b2_ref[...]))
    h = jnp.sin(fq * (_hp_dot(w3_ref[...], h) + b3_ref[...]))
    o_ref[...] = h.T


def _filter_trunk_call(zfeat_t, w1, b1, w2, b2, w3, b3, freq):
    rows = zfeat_t.shape[1]
    tr = min(rows, 2048)
    fw = HYENA_FILTER_WIDTH
    col = lambda v: v.reshape(fw, 1)
    return pl.pallas_call(
        _filter_trunk_kernel,
        out_shape=jax.ShapeDtypeStruct((rows, fw), F32),
        grid=(rows // tr,),
        in_specs=[
            pl.BlockSpec((HYENA_EMB_PAD, tr), lambda i: (0, i)),
            _resident((fw, HYENA_EMB_PAD)), _resident((fw, 1)),
            _resident((fw, fw)), _resident((fw, 1)),
            _resident((fw, fw)), _resident((fw, 1)),
            _resident((fw, 1)),
        ],
        out_specs=pl.BlockSpec((tr, fw), lambda i: (i, 0)),
        compiler_params=_cparams(("arbitrary",)),
        name="hyena_filter_trunk",
    )(zfeat_t, w1.T, col(b1), w2.T, col(b2), w3.T, col(b3), col(freq))


def _dft_stage1(src_ref, tab_ref, re_ref, im_ref, n_rows, k1p):
    def body(n2, carry):
        xs = src_ref[pl.ds(n2, n_rows, stride=DFT_N2), :].astype(BF16)
        res = jnp.dot(tab_ref[n2], xs, preferred_element_type=F32)
        re_ref[pl.ds(n2, k1p, stride=DFT_N2), :] = res[0:k1p]
        im_ref[pl.ds(n2, k1p, stride=DFT_N2), :] = res[k1p:]
        return carry

    lax.fori_loop(0, DFT_N2, body, 0, unroll=8)


def _filter_spec_kernel(seq, k1p, h_ref, w4f_ref, w4b_ref, dl_ref, e1_ref, f2_ref,
                        kr_ref, ki_ref, kfull):
    n = 2 * seq
    chunk = min(seq, 1024)
    dl = dl_ref[...]

    def fill(c, ss):
        r0 = pl.multiple_of(c * chunk, chunk)
        hc = h_ref[pl.ds(r0, chunk), :].astype(BF16)
        fwd = jnp.dot(hc, w4f_ref[...], preferred_element_type=F32)
        bwd = jnp.dot(hc, w4b_ref[...], preferred_element_type=F32)
        rows = r0 + lax.broadcasted_iota(jnp.int32, (chunk, 1), 0)
        lag = jnp.where(rows < seq, rows, n - rows).astype(F32)
        t = lag * (1.0 / (seq - 1))
        val = jnp.where(rows < seq, fwd, bwd) * jnp.exp(-t * dl)
        val = jnp.where(rows == seq, 0.0, val)
        kfull[pl.ds(r0, chunk), :] = val
        return ss + jnp.sum(val * val, axis=0, keepdims=True)

    ss = lax.fori_loop(0, n // chunk, fill, jnp.zeros((1, dl.shape[1]), F32))
    norm = lax.rsqrt(ss + EPS)

    _dft_stage1(kfull, e1_ref, kr_ref, ki_ref, n // DFT_N2, k1p)

    f2 = f2_ref[...]

    def stage2(a, carry):
        rows = pl.ds(pl.multiple_of(a * DFT_N2, DFT_N2), DFT_N2)
        slab = jnp.concatenate([kr_ref[rows, :], ki_ref[rows, :]], axis=0).astype(BF16)
        xf = jnp.dot(f2, slab, preferred_element_type=F32)
        kr_ref[rows, :] = xf[0:DFT_N2] * norm
        ki_ref[rows, :] = xf[DFT_N2:] * norm
        return carry

    lax.fori_loop(0, seq // DFT_N2 + 1, stage2, 0, unroll=8)


def _filter_spec_call(h3, w4, deltas, e1f, f2f, seq):
    n = 2 * seq
    k1p = e1f.shape[1] // 2
    ct = LANES
    nct = CONV_WIDTH // ct
    fw = HYENA_FILTER_WIDTH
    spec_rows = k1p * DFT_N2
    out = pl.BlockSpec((spec_rows, ct), lambda j: (0, j))
    return pl.pallas_call(
        functools.partial(_filter_spec_kernel, seq, k1p),
        out_shape=(jax.ShapeDtypeStruct((spec_rows, CONV_WIDTH), F32),
                   jax.ShapeDtypeStruct((spec_rows, CONV_WIDTH), F32)),
        grid=(nct,),
        in_specs=[
            _resident((n, fw)),
            pl.BlockSpec((fw, ct), lambda j: (0, j)),
            pl.BlockSpec((fw, ct), lambda j: (0, nct + j)),
            pl.BlockSpec((1, ct), lambda j: (0, j)),
            _resident(e1f.shape),
            _resident((2 * DFT_N2, 2 * DFT_N2)),
        ],
        out_specs=(out, out),
        scratch_shapes=[pltpu.VMEM((n, ct), F32)],
        compiler_params=_cparams(("arbitrary",)),
        name="hyena_filter_spectrum",
    )(h3, w4, w4, deltas, e1f, f2f)


def _long_conv_kernel(seq, k1p, z_ref, x0_ref, kr_ref, ki_ref, skip_ref, e1_ref, f2f_ref, f2i_ref,
                      einv_ref, o_ref, br, bi):
    n1h = seq // DFT_N2
    _dft_stage1(z_ref, e1_ref, br, bi, n1h, k1p)

    f2f = f2f_ref[...]
    f2i = f2i_ref[...]

    def spectral(a, carry):
        r0 = pl.multiple_of(a * DFT_N2, DFT_N2)
        rows = pl.ds(r0, DFT_N2)
        slab = jnp.concatenate([br[rows, :], bi[rows, :]], axis=0).astype(BF16)
        xf = jnp.dot(f2f, slab, preferred_element_type=F32)
        xr, xi = xf[0:DFT_N2], xf[DFT_N2:]
        kr, ki = kr_ref[rows, :], ki_ref[rows, :]
        y = jnp.concatenate([xr * kr - xi * ki, xr * ki + xi * kr], axis=0).astype(BF16)
        d = jnp.dot(f2i, y, preferred_element_type=F32)
        br[rows, :] = d[0:DFT_N2]
        bi[rows, :] = d[DFT_N2:]
        return carry

    lax.fori_loop(0, n1h + 1, spectral, 0, unroll=8)

    skip = skip_ref[...]

    def synth(n2, carry):
        ds_spec = pl.ds(n2, k1p, stride=DFT_N2)
        d = jnp.concatenate([br[ds_spec, :], bi[ds_spec, :]], axis=0).astype(BF16)
        o_ref[pl.ds(n2, n1h, stride=DFT_N2), :] = jnp.dot(einv_ref[n2], d, preferred_element_type=F32)
        return carry

    lax.fori_loop(0, DFT_N2, synth, 0, unroll=8)

    chunk = min(seq, 512)

    def gate(c, carry):
        rows = pl.ds(pl.multiple_of(c * chunk, chunk), chunk)
        o_ref[rows, :] = x0_ref[rows, :] * (o_ref[rows, :] + skip * z_ref[rows, :])
        return carry

    lax.fori_loop(0, seq // chunk, gate, 0)


def _long_conv_call(z, x0, kr, ki, skip, e1, f2f, f2i, einv, bsz, seq):
    ct = LANES
    nct = CONV_WIDTH // ct
    k1p = e1.shape[1] // 2
    spec_rows = k1p * DFT_N2
    big = lambda: pl.BlockSpec((seq, ct), lambda j, b: (b, j), pipeline_mode=pl.Buffered(1))
    spec = lambda: pl.BlockSpec((spec_rows, ct), lambda j, b: (0, j), pipeline_mode=pl.Buffered(1))
    return pl.pallas_call(
        functools.partial(_long_conv_kernel, seq, k1p),
        out_shape=jax.ShapeDtypeStruct((bsz * seq, CONV_WIDTH), F32),
        grid=(nct, bsz),
        in_specs=[
            big(), big(), spec(), spec(),
            pl.BlockSpec((1, ct), lambda j, b: (0, j)),
            _resident(e1.shape),
            _resident((2 * DFT_N2, 2 * DFT_N2)), _resident((2 * DFT_N2, 2 * DFT_N2)),
            _resident(einv.shape),
        ],
        out_specs=pl.BlockSpec((seq, ct), lambda j, b: (b, j)),
        scratch_shapes=[pltpu.VMEM((spec_rows, ct), F32), pltpu.VMEM((spec_rows, ct), F32)],
        compiler_params=_cparams(("arbitrary", "arbitrary")),
        name="hyena_long_conv",
    )(z, x0, kr, ki, skip, e1, f2f, f2i, einv)


def _rope_tables(seq):
    t = jnp.arange(seq)
    row = (t // GRID_W).astype(F32)
    col = (t % GRID_W).astype(F32)
    half = HEAD_DIM // 2
    inv = ROPE_THETA ** (-jnp.arange(0, half, 2, dtype=F32) / half)
    ang = jnp.concatenate([row[:, None] * inv, col[:, None] * inv], axis=-1)
    ang = jnp.repeat(ang, 2, axis=-1)
    sign = jnp.where(jnp.arange(HEAD_DIM) % 2 == 0, -1.0, 1.0).astype(F32)
    reps = LANES // HEAD_DIM
    return jnp.tile(jnp.cos(ang), (1, reps)), jnp.tile(jnp.sin(ang) * sign, (1, reps))


def _dft_tables(seq):
    n = 2 * seq
    n1 = n // DFT_N2
    k1 = n1 // 2 + 1
    k1p = -(-k1 // 8) * 8
    a = jnp.arange(k1p)
    live = (a < k1).astype(F32)
    ang1 = ((a[:, None] * jnp.arange(n1)[None, :]) % n1).astype(F32) * (2.0 * math.pi / n1)
    ang2 = (jnp.arange(DFT_N2)[:, None] * a[None, :]).astype(F32) * (2.0 * math.pi / n)
    c1, s1 = jnp.cos(ang1) * live[:, None], jnp.sin(ang1) * live[:, None]
    c2, s2 = jnp.cos(ang2), jnp.sin(ang2)
    cos_a = c1[None] * c2[:, :, None] - s1[None] * s2[:, :, None]
    sin_a = s1[None] * c2[:, :, None] + c1[None] * s2[:, :, None]
    e1_full = jnp.concatenate([cos_a, -sin_a], axis=1).astype(BF16)
    e1_half = e1_full[:, :, :n1 // 2]
    wgt = jnp.where((a == 0) | (a == n1 // 2), 1.0, 2.0) / n
    c1t, s1t = (c1 * wgt[:, None]).T[:n1 // 2], (s1 * wgt[:, None]).T[:n1 // 2]
    cos_s = c1t[None] * c2[:, None, :] - s1t[None] * s2[:, None, :]
    sin_s = s1t[None] * c2[:, None, :] + c1t[None] * s2[:, None, :]
    einv = jnp.concatenate([cos_s, -sin_s], axis=2).astype(BF16)
    kk = jnp.arange(DFT_N2)
    phi = ((kk[:, None] * kk[None, :]) % DFT_N2).astype(F32) * (2.0 * math.pi / DFT_N2)
    ci, si = jnp.cos(phi), jnp.sin(phi)
    f2f = jnp.block([[ci, si], [-si, ci]]).astype(BF16)
    f2i = jnp.block([[ci, -si], [si, ci]]).astype(BF16)
    return e1_full, e1_half, einv, f2f, f2i


def _filter_features_t(seq):
    j = jnp.arange(2 * seq)
    lag = jnp.where(j < seq, j, jnp.where(j == seq, 0, 2 * seq - j)).astype(F32)[None, :]
    t = lag / (seq - 1)
    bands = (HYENA_EMB - 1) // 2
    w = (2.0 * math.pi / seq) * lag
    fr = jnp.linspace(1e-4, bands - 1, bands, dtype=F32)[:, None]
    pad = jnp.zeros((HYENA_EMB_PAD - HYENA_EMB, 2 * seq), F32)
    return jnp.concatenate([t, jnp.cos(fr * w), -jnp.sin(fr * w), pad], axis=0)


def kernel(x, c, rel_table, norm_g, w_mod, b_mod, w_in, w_out, ffn_w1, ffn_w3, ffn_w2, a_qk_g, a_sink,
           b_conv_w, b_conv_b, c_qk_g, d_conv_w, d_conv_b, d_f_w1, d_f_b1, d_f_w2, d_f_b2, d_f_w3,
           d_f_b3, d_f_w4, d_f_freq, d_skip):
    bsz, seq, d = x.shape
    depth = w_mod.shape[0]
    m_rows = bsz * seq
    mod =_mod_call(c, w_mod, b_mod).reshape(depth, bsz, 3, 3, 1, d)
    e_blk = jnp.kron(jnp.eye(MXU_DIM // HEAD_DIM, dtype=F32), jnp.ones((HEAD_DIM, HEAD_DIM), F32)).astype(BF16)

    w1b, w3b, w2b, w_in_b, w_out_b = (_to_bf16(w) for w in (ffn_w1, ffn_w3, ffn_w2, w_in, w_out))

    def ffn(xf, l, which, sub, mixer=None):
        return _ffn_call(xf, norm_g[l, sub][None], mod[l, :, sub, 0], mod[l, :, sub, 1], mod[l, :, sub, 2],
                         w1b, w3b, w2b, l, which, seq, mixer)

    xf = x.reshape(m_rows, d)
    for l in range(depth):
        j = l // 2
        xf = ffn(xf, l, 0, 0)
        even = l % 2 == 0
        qk_g = a_qk_g[j] if even else c_qk_g[j]
        hyena = None if even else _rope_tables(seq) + (d_conv_w[j], d_conv_b[j][None])
        q, k, v, *conv_in = _inproj_call(
            xf, norm_g[l, 1][None], mod[l, :, 1, 0], mod[l, :, 1, 1], w_in_b, l,
            jnp.tile(qk_g[0], N_Q_HEADS)[None], jnp.tile(qk_g[1], N_KV_HEADS)[None], e_blk, seq, hyena)
        gate = mod[l, :, 1, 2]
        if even:
            (u,) = conv_in
            bias, sink_rows = _bias_call(rel_table, a_sink[j])
            xf = _mixer_ab_call(xf, q, k, v, u, bias, sink_rows, b_conv_w[j], b_conv_b[j][None], w_out_b, l,
                                gate, seq)
            pending = None
        else:
            x0, z = conv_in
            att = _dense_attn_call(q, k, v, bsz, seq)
            e1_full, e1_half, einv, f2f, f2i = _dft_tables(seq)
            fw1 = jnp.pad(d_f_w1[j], ((0, HYENA_EMB_PAD - HYENA_EMB), (0, 0)))
            h3 = _filter_trunk_call(_filter_features_t(seq), fw1, d_f_b1[j], d_f_w2[j], d_f_b2[j],
                                    d_f_w3[j], d_f_b3[j], d_f_freq[j])
            deltas = jnp.abs(jnp.linspace(HYENA_MIN_DECAY, HYENA_MAX_DECAY, CONV_WIDTH, dtype=F32))[None]
            kr, ki = _filter_spec_call(h3, d_f_w4[j].astype(BF16), deltas, e1_full, f2f, seq)
            y = _long_conv_call(z, x0, kr, ki, d_skip[j][None], e1_half, f2f, f2i, einv, bsz, seq)
            pending = (att, y, w_out_b, gate)
        xf = ffn(xf, l, 1, 2, pending)
    return xf.reshape(bsz, seq, d)
```

```python
import functools
import math

import jax
import jax.numpy as jnp
import numpy as np
from jax import lax
from jax.experimental import pallas as pl
from jax.experimental.pallas import tpu as pltpu

D_MODEL = 1024
HEAD_DIM = 64
N_Q_HEADS = 8
N_KV_HEADS = 2
GQA_GROUP = N_Q_HEADS // N_KV_HEADS
ATTN_WIDTH = N_Q_HEADS * HEAD_DIM
KV_WIDTH = N_KV_HEADS * HEAD_DIM
QKV_COLS = ATTN_WIDTH + 2 * KV_WIDTH
CONV_WIDTH = D_MODEL - ATTN_WIDTH
IN_COLS = QKV_COLS + 3 * CONV_WIDTH
D_FF = 2752
BLOCK = 128
WINDOW = 128
N_BUCKETS = 32
MAX_DISTANCE = 128
GRID_W = 64
ROPE_THETA = 10000.0
HYENA_EMB = 33
HYENA_FILTER_WIDTH = 64
HYENA_EMB_PAD = 64
HYENA_MIN_DECAY = math.log(1e-2) / 0.3
HYENA_MAX_DECAY = math.log(1e-2) / 1.5
EPS = 1e-6
NEG_INF = -1e30
LOG2_E = 1.4426950408889634

LANES = 128
HALO = 16
MXU_DIM = 256
FF_CHUNK = MXU_DIM
VMEM_LIMIT = 56 * 1024 * 1024
DFT_N2 = 128

BF16 = jnp.bfloat16
F32 = jnp.float32


def _cparams(sem):
    return pltpu.CompilerParams(dimension_semantics=sem, vmem_limit_bytes=VMEM_LIMIT)


def _resident(shape):
    nd = len(shape)
    return pl.BlockSpec(shape, lambda *_: (0,) * nd, pipeline_mode=pl.Buffered(1))


def _layer_slice(shape, layer):
    return pl.BlockSpec((None,) + shape, lambda *_: (layer, 0, 0), pipeline_mode=pl.Buffered(1))


def _adaln(x, g, scale, shift):
    y = x * lax.rsqrt(jnp.mean(x * x, axis=-1, keepdims=True) + EPS) * g
    return y * (1.0 + scale) + shift


def _mod_kernel(c_ref, w_ref, b_ref, o_ref):
    w = w_ref[0]
    for b in range(c_ref.shape[0]):
        cc = c_ref[b]
        cond = cc * jax.nn.sigmoid(cc)
        o_ref[0, b:b + 1, :] = jnp.sum(w * cond, axis=0, keepdims=True) + b_ref[0]


def _mod_call(c, w_mod, b_mod):
    depth, d, n = w_mod.shape
    bsz = c.shape[0]
    tn = 1152 if n % 1152 == 0 else 512
    return pl.pallas_call(
        _mod_kernel,
        out_shape=jax.ShapeDtypeStruct((depth, bsz, n), F32),
        grid=(depth, n // tn),
        in_specs=[
            pl.BlockSpec((bsz, d, 1), lambda l, j: (0, 0, 0)),
            pl.BlockSpec((1, d, tn), lambda l, j: (l, 0, j)),
            pl.BlockSpec((1, 1, tn), lambda l, j: (l, 0, j)),
        ],
        out_specs=pl.BlockSpec((1, bsz, tn), lambda l, j: (l, 0, j)),
        compiler_params=_cparams(("arbitrary", "arbitrary")),
        name="adaln_modulation",
    )(c.reshape(bsz, d, 1), w_mod, b_mod.reshape(depth, 1, n))


def _cast_transpose_kernel(wt_ref, o_ref):
    cols = wt_ref.shape[0]
    starts = list(range(0, cols - MXU_DIM, MXU_DIM)) + [cols - MXU_DIM]
    for c0 in starts:
        o_ref[:, c0:c0 + MXU_DIM] = wt_ref[c0:c0 + MXU_DIM, :].T.astype(o_ref.dtype)


def _to_bf16_from_transposed(w):
    lead, (rows, cols) = w.shape[:-2], w.shape[-2:]
    none = (None,) * len(lead)
    return pl.pallas_call(
        _cast_transpose_kernel,
        out_shape=jax.ShapeDtypeStruct(w.shape, BF16),
        grid=lead,
        in_specs=[pl.BlockSpec(none + (cols, rows), lambda *idx: idx + (0, 0))],
        out_specs=pl.BlockSpec(none + (rows, cols), lambda *idx: idx + (0, 0)),
        compiler_params=_cparams(("arbitrary",) * len(lead)),
        name="weights_to_bf16_transposing",
    )(jnp.swapaxes(w, -1, -2))


def _ffn_kernel(mixer_update, x_ref, g_ref, shift_ref, scale_ref, gate_ref, w1_ref, w3_ref, w2_ref, *rest):
    x = x_ref[...]
    if mixer_update:
        att_ref, y_ref, wo_ref, mgate_ref, o_ref = rest
        upd = jnp.dot(att_ref[...], wo_ref[0:ATTN_WIDTH, :], preferred_element_type=F32)
        upd = upd + jnp.dot(y_ref[...].astype(BF16), wo_ref[ATTN_WIDTH:, :], preferred_element_type=F32)
        x = x + mgate_ref[0] * upd
    else:
        (o_ref,) = rest
    h = _adaln(x, g_ref[...], scale_ref[0], shift_ref[0]).astype(BF16)
    acc = jnp.zeros(x.shape, F32)
    for c0 in range(0, D_FF, FF_CHUNK):
        cols = slice(c0, min(c0 + FF_CHUNK, D_FF))
        a = jnp.dot(h, w1_ref[:, cols], preferred_element_type=F32)
        b = jnp.dot(h, w3_ref[:, cols], preferred_element_type=F32)
        act = (a * jax.nn.sigmoid(a) * b).astype(BF16)
        acc = acc + jnp.dot(act, w2_ref[cols, :], preferred_element_type=F32)
    o_ref[...] = x + (0.5 * gate_ref[0]) * acc


def _ffn_call(x, g, shift, scale, gate, w1, w3, w2, layer, which, rows_per_batch, mixer=None):
    m, d = x.shape
    tm = 512
    tpb = rows_per_batch // tm
    row = lambda w: pl.BlockSpec((tm, w), lambda i: (i, 0))
    vec = pl.BlockSpec((1, 1, d), lambda i: (i // tpb, 0, 0))
    weight = lambda r, c: pl.BlockSpec((None, None, r, c), lambda i: (layer, which, 0, 0),
                                       pipeline_mode=pl.Buffered(1))
    in_specs = [row(d), _resident((1, d)), vec, vec, vec, weight(d, D_FF), weight(d, D_FF), weight(D_FF, d)]
    args = [x, g, shift, scale, gate, w1, w3, w2]
    if mixer is not None:
        in_specs += [row(ATTN_WIDTH), row(CONV_WIDTH), _layer_slice((d, d), layer), vec]
        args += list(mixer)
    return pl.pallas_call(
        functools.partial(_ffn_kernel, mixer is not None),
        out_shape=jax.ShapeDtypeStruct((m, d), F32),
        grid=(m // tm,),
        in_specs=in_specs,
        out_specs=row(d),
        compiler_params=_cparams(("arbitrary",)),
        name="adaln_swiglu_ffn",
    )(*args)


def _group_norm_scale(v, e_ref):
    sq = v * v
    hi = sq.astype(BF16)
    lo = (sq - hi.astype(F32)).astype(BF16)
    w = min(v.shape[1], MXU_DIM)
    e = e_ref[...]
    ss = jnp.concatenate(
        [jnp.dot(hi[:, c:c + w], e[0:w, 0:w], preferred_element_type=F32)
         + jnp.dot(lo[:, c:c + w], e[0:w, 0:w], preferred_element_type=F32) for c in range(0, v.shape[1], w)],
        axis=1)
    return lax.rsqrt(ss * (1.0 / HEAD_DIM) + EPS)


def _rope128(v, cos, sin_signed, even_lane):
    partner = jnp.where(even_lane, pltpu.roll(v, LANES - 1, 1), pltpu.roll(v, 1, 1))
    return v * cos + partner * sin_signed


def _short_conv_rows(u, before, after, cw, cb):
    rows = u.shape[0]
    r = lax.broadcasted_iota(jnp.int32, u.shape, 0)
    um1 = jnp.where(r == 0, before, pltpu.roll(u, 1, 0))
    up1 = jnp.where(r == rows - 1, after, pltpu.roll(u, rows - 1, 0))
    return cw[0:1] * um1 + cw[1:2] * u + cw[2:3] * up1 + cb


def _inproj_kernel(rope, tpb, x_ref, g_ref, shift_ref, scale_ref, w_ref, qg_ref, kg_ref, e_ref, *rest):
    if rope:
        cos_ref, sin_ref, xp_ref, xn_ref, cw_ref, cb_ref, q_ref, k_ref, v_ref, x0_ref, z_ref = rest
    else:
        q_ref, k_ref, v_ref, u_ref = rest
    h = _adaln(x_ref[...], g_ref[...], scale_ref[0], shift_ref[0]).astype(BF16)
    q = jnp.dot(h, w_ref[:, 0:ATTN_WIDTH], preferred_element_type=F32)
    k = jnp.dot(h, w_ref[:, ATTN_WIDTH:ATTN_WIDTH + KV_WIDTH], preferred_element_type=F32)
    q = q * _group_norm_scale(q, e_ref) * qg_ref[...]
    k = k * _group_norm_scale(k, e_ref) * kg_ref[...]
    if rope:
        cos = cos_ref[...]
        sin = sin_ref[...]
        even = (lax.broadcasted_iota(jnp.int32, cos.shape, 1) % 2) == 0
        q = jnp.concatenate(
            [_rope128(q[:, j * LANES:(j + 1) * LANES], cos, sin, even) for j in range(ATTN_WIDTH // LANES)],
            axis=1)
        k = _rope128(k, cos, sin, even)
    q_ref[...] = (q * (HEAD_DIM ** -0.5 * (LOG2_E if rope else 1.0))).astype(BF16)
    v = jnp.dot(h, w_ref[:, ATTN_WIDTH + KV_WIDTH:QKV_COLS], preferred_element_type=F32)
    k_ref[0] = k.T.astype(BF16)
    low = lax.broadcasted_iota(jnp.int32, v.shape, 1) < HEAD_DIM
    v_ref[...] = jnp.concatenate(
        [jnp.where(low, v, 1.0), jnp.where(low, pltpu.roll(v, HEAD_DIM, 1), 1.0)], axis=1).astype(BF16)
    if not rope:
        u_ref[...] = jnp.dot(h, w_ref[:, QKV_COLS:IN_COLS], preferred_element_type=F32)
        return
    i = pl.program_id(0)
    h_prev = _adaln(xp_ref[...], g_ref[...], scale_ref[0], shift_ref[0]).astype(BF16)
    h_next = _adaln(xn_ref[...], g_ref[...], scale_ref[0], shift_ref[0]).astype(BF16)
    u_ext = jnp.dot(jnp.concatenate([h_prev, h, h_next], axis=0), w_ref[:, QKV_COLS:IN_COLS],
                    preferred_element_type=F32)
    tm = h.shape[0]
    before = jnp.where((i % tpb) == 0, 0.0, u_ext[HALO - 1:HALO])
    after = jnp.where((i % tpb) == tpb - 1, 0.0, u_ext[HALO + tm:HALO + tm + 1])
    t = _short_conv_rows(u_ext[HALO:HALO + tm], before, after, cw_ref[...], cb_ref[...])
    x0_ref[...] = t[:, 0:CONV_WIDTH]
    z_ref[...] = t[:, CONV_WIDTH:2 * CONV_WIDTH] * t[:, 2 * CONV_WIDTH:]


def _inproj_call(x, g, shift, scale, w_in, layer, qg, kg, e, rows_per_batch, hyena=None):
    m, d = x.shape
    tm = 512
    tpb = rows_per_batch // tm
    vec = pl.BlockSpec((1, 1, d), lambda i: (i // tpb, 0, 0))
    row = lambda w: pl.BlockSpec((tm, w), lambda i: (i, 0))
    in_specs = [
        row(d),
        _resident((1, d)), vec, vec,
        _layer_slice((d, IN_COLS), layer),
        _resident((1, ATTN_WIDTH)), _resident((1, KV_WIDTH)),
        _resident((MXU_DIM, MXU_DIM)),
    ]
    args = [x, g, shift, scale, w_in, qg, kg, e]
    uw = 3 * CONV_WIDTH
    if hyena is not None:
        cos, sin, conv_w, conv_b = hyena
        tab = pl.BlockSpec((tm, LANES), lambda i: (i % tpb, 0))
        hpt = tm // HALO
        last_halo = m // HALO - 1
        in_specs += [
            tab, tab,
            pl.BlockSpec((HALO, d), lambda i: (jnp.maximum(i * hpt - 1, 0), 0)),
            pl.BlockSpec((HALO, d), lambda i: (jnp.minimum((i + 1) * hpt, last_halo), 0)),
            _resident((3, uw)), _resident((1, uw)),
        ]
        args += [cos, sin, x, x, conv_w, conv_b]
        conv_shapes = [jax.ShapeDtypeStruct((m, CONV_WIDTH), F32)] * 2
        conv_specs = [row(CONV_WIDTH)] * 2
    else:
        conv_shapes = [jax.ShapeDtypeStruct((m, uw), F32)]
        conv_specs = [row(uw)]
    tk = _key_chunk(rows_per_batch)
    per = tk // tm
    return pl.pallas_call(
        functools.partial(_inproj_kernel, hyena is not None, tpb),
        out_shape=[
            jax.ShapeDtypeStruct((m, ATTN_WIDTH), BF16),
            jax.ShapeDtypeStruct((m // tk, KV_WIDTH, tk), BF16),
            jax.ShapeDtypeStruct((m, 2 * KV_WIDTH), BF16),
        ] + conv_shapes,
        grid=(m // tm,),
        in_specs=in_specs,
        out_specs=[row(ATTN_WIDTH), pl.BlockSpec((1, KV_WIDTH, tm), lambda i: (i // per, 0, i % per)),
                   row(2 * KV_WIDTH)] + conv_specs,
        compiler_params=_cparams(("arbitrary",)),
        name="adaln_in_projection",
    )(*args)


_T5_STEPS = (12, 16, 23, 32, 46, 64, 91)


def _bias_kernel(tab_ref, sink_ref, o_ref, sink_rows_ref):
    for h in range(N_Q_HEADS):
        sink_rows_ref[h * BLOCK:(h + 1) * BLOCK, :] = jnp.full((BLOCK, LANES), sink_ref[h], F32)
    qi = lax.broadcasted_iota(jnp.int32, (BLOCK, 3 * BLOCK), 0)
    kj = lax.broadcasted_iota(jnp.int32, (BLOCK, 3 * BLOCK), 1)
    rel = kj - BLOCK - qi
    n = jnp.abs(rel)
    half = N_BUCKETS // 2
    max_exact = half // 2
    large = jnp.full(n.shape, max_exact, jnp.int32)
    for t in _T5_STEPS:
        large = large + (n >= t).astype(jnp.int32)
    bucket = jnp.where(rel > 0, half, 0) + jnp.where(n < max_exact, n, large)
    for h in range(N_Q_HEADS):
        bias = jnp.zeros(n.shape, F32)
        for b in range(N_BUCKETS):
            bias = jnp.where(bucket == b, tab_ref[b, h], bias)
        o_ref[h] = jnp.where(n <= WINDOW, bias, NEG_INF)


def _bias_call(rel_table, sink):
    return pl.pallas_call(
        _bias_kernel,
        out_shape=(jax.ShapeDtypeStruct((N_Q_HEADS, BLOCK, 3 * BLOCK), F32),
                   jax.ShapeDtypeStruct((N_Q_HEADS * BLOCK, LANES), F32)),
        in_specs=[pl.BlockSpec(memory_space=pltpu.SMEM), pl.BlockSpec(memory_space=pltpu.SMEM)],
        name="t5_bias_tile",
    )(rel_table, sink)


def _mixer_ab_kernel(tq, tpb, x_ref, q_ref, kc_ref, kp_ref, kn_ref, vc_ref, vp_ref, vn_ref,
                     uc_ref, up_ref, un_ref, bias_ref, sink_ref, cw_ref, cb_ref, wo_ref, gate_ref,
                     o_ref, kbuf, vbuf, qs, att):
    i = pl.program_id(0)
    first = (i % tpb) == 0
    last = (i % tpb) == tpb - 1
    nblk = tq // BLOCK
    grows = GQA_GROUP * BLOCK
    kbuf[:, 0:BLOCK] = kp_ref[0]
    kbuf[:, BLOCK:BLOCK + tq] = kc_ref[0]
    kbuf[:, BLOCK + tq:] = kn_ref[0]
    vbuf[0:BLOCK] = vp_ref[...]
    vbuf[BLOCK:BLOCK + tq] = vc_ref[...]
    vbuf[BLOCK + tq:] = vn_ref[...]
    for n in range(nblk):
        for h in range(N_Q_HEADS):
            g, j = divmod(h, GQA_GROUP)
            qs[g, n, j * BLOCK:(j + 1) * BLOCK, :] = q_ref[n * BLOCK:(n + 1) * BLOCK, h * HEAD_DIM:(h + 1) * HEAD_DIM]
    col = lax.broadcasted_iota(jnp.int32, (grows, 3 * BLOCK), 1)
    for n in range(nblk):
        keys = slice(n * BLOCK, (n + 3) * BLOCK)
        for g in range(N_KV_HEADS):
            s = jnp.dot(qs[g, n], kbuf[g * HEAD_DIM:(g + 1) * HEAD_DIM, keys], preferred_element_type=F32)
            s = s + bias_ref[g * GQA_GROUP:(g + 1) * GQA_GROUP].reshape(grows, 3 * BLOCK)
            if n == 0:
                s = jnp.where(jnp.logical_and(first, col < BLOCK), NEG_INF, s)
            if n == nblk - 1:
                s = jnp.where(jnp.logical_and(last, col >= 2 * BLOCK), NEG_INF, s)
            sk = sink_ref[g * grows:(g + 1) * grows, :]
            mx = jnp.maximum(jnp.broadcast_to(jnp.max(s, axis=-1, keepdims=True), sk.shape), sk)
            p = jnp.exp(s - jnp.tile(mx, (1, 3)))
            pv = jnp.dot(p.astype(BF16), vbuf[keys, g * LANES:(g + 1) * LANES], preferred_element_type=F32)
            o = pv / (pltpu.roll(pv, HEAD_DIM, 1) + jnp.exp(sk - mx))
            for j in range(GQA_GROUP):
                h = g * GQA_GROUP + j
                att[n * BLOCK:(n + 1) * BLOCK, h * HEAD_DIM:(h + 1) * HEAD_DIM] = o[j * BLOCK:(j + 1) * BLOCK, 0:HEAD_DIM]

    gb = uc_ref[:, 0:CONV_WIDTH]
    p = uc_ref[:, CONV_WIDTH:2 * CONV_WIDTH] * uc_ref[:, 2 * CONV_WIDTH:]
    p_before = jnp.where(first, 0.0, up_ref[7:8, CONV_WIDTH:2 * CONV_WIDTH] * up_ref[7:8, 2 * CONV_WIDTH:])
    p_after = jnp.where(last, 0.0, un_ref[0:1, CONV_WIDTH:2 * CONV_WIDTH] * un_ref[0:1, 2 * CONV_WIDTH:])
    conv = gb * _short_conv_rows(p, p_before, p_after, cw_ref[...], cb_ref[...])

    y = jnp.dot(att[...].astype(BF16), wo_ref[0:ATTN_WIDTH, :], preferred_element_type=F32)
    y = y + jnp.dot(conv.astype(BF16), wo_ref[ATTN_WIDTH:, :], preferred_element_type=F32)
    o_ref[...] = x_ref[...] + gate_ref[0] * y


def _mixer_ab_call(x, q, kt, v1, u, bias, sink_rows, conv_w, conv_b, w_out, layer, gate, rows_per_batch):
    m, d = x.shape
    tq = 512
    tpb = rows_per_batch // tq
    r = tq // BLOCK
    nb = m // BLOCK
    n8 = m // 8
    cur = lambda w: pl.BlockSpec((tq, w), lambda i: (i, 0))
    prev_of = lambda i: jnp.maximum(i * r - 1, 0)
    next_of = lambda i: jnp.minimum((i + 1) * r, nb - 1)
    v_prev = pl.BlockSpec((BLOCK, 2 * KV_WIDTH), lambda i: (prev_of(i), 0))
    v_next = pl.BlockSpec((BLOCK, 2 * KV_WIDTH), lambda i: (next_of(i), 0))
    tk = kt.shape[2]
    k_cur = pl.BlockSpec((1, KV_WIDTH, tq), lambda i: (i // (tk // tq), 0, i % (tk // tq)))
    kpb = tk // BLOCK
    k_prev = pl.BlockSpec((1, KV_WIDTH, BLOCK), lambda i: (prev_of(i) // kpb, 0, prev_of(i) % kpb))
    k_next = pl.BlockSpec((1, KV_WIDTH, BLOCK), lambda i: (next_of(i) // kpb, 0, next_of(i) % kpb))
    uw = 3 * CONV_WIDTH
    return pl.pallas_call(
        functools.partial(_mixer_ab_kernel, tq, tpb),
        out_shape=jax.ShapeDtypeStruct((m, d), F32),
        grid=(m // tq,),
        in_specs=[
            cur(d), cur(ATTN_WIDTH),
            k_cur, k_prev, k_next,
            cur(2 * KV_WIDTH), v_prev, v_next,
            cur(uw),
            pl.BlockSpec((8, uw), lambda i: (jnp.maximum(i * (tq // 8) - 1, 0), 0)),
            pl.BlockSpec((8, uw), lambda i: (jnp.minimum((i + 1) * (tq // 8), n8 - 1), 0)),
            _resident((N_Q_HEADS, BLOCK, 3 * BLOCK)),
            _resident((N_Q_HEADS * BLOCK, LANES)),
            _resident((3, CONV_WIDTH)), _resident((1, CONV_WIDTH)),
            _layer_slice((d, d), layer),
            pl.BlockSpec((1, 1, d), lambda i: (i // tpb, 0, 0)),
        ],
        out_specs=cur(d),
        scratch_shapes=[
            pltpu.VMEM((KV_WIDTH, tq + 2 * BLOCK), BF16),
            pltpu.VMEM((tq + 2 * BLOCK, 2 * KV_WIDTH), BF16),
            pltpu.VMEM((N_KV_HEADS, r, GQA_GROUP * BLOCK, HEAD_DIM), BF16),
            pltpu.VMEM((tq, ATTN_WIDTH), F32),
        ],
        compiler_params=_cparams(("arbitrary",)),
        name="windowed_attn_shortconv_outproj",
    )(x, q, kt, kt, kt, v1, v1, v1, u, u, u, bias, sink_rows, conv_w, conv_b, w_out, gate)


def _key_chunk(seq):
    return min(seq, 1024)


def _dense_attn_kernel(tq, tk, q_ref, k_ref, v_ref, o_ref, qs, s_buf, p_buf, rmax_buf, alpha_buf,
                       m_ref, acc_ref):
    nc = k_ref.shape[0]
    groups = range(N_KV_HEADS)
    for h in range(N_Q_HEADS):
        g, j = divmod(h, GQA_GROUP)
        qs[g, j * tq:(j + 1) * tq, :] = q_ref[:, h * HEAD_DIM:(h + 1) * HEAD_DIM]
    for g in groups:
        m_ref[g][...] = jnp.full(m_ref[g].shape, -jnp.inf, F32)
        acc_ref[g][...] = jnp.zeros(acc_ref[g].shape, F32)

    def scores(g, c):
        s = jnp.dot(qs[g], k_ref[c, g * HEAD_DIM:(g + 1) * HEAD_DIM, :], preferred_element_type=F32)
        s_buf[g][...] = s
        rmax_buf[g][...] = jnp.broadcast_to(jnp.max(s, axis=-1, keepdims=True), rmax_buf[g].shape)

    def softmax(g):
        m_old = m_ref[g][...]
        m_new = jnp.maximum(m_old, rmax_buf[g][...])
        alpha_buf[g][...] = jnp.exp2(m_old - m_new)
        m_ref[g][...] = m_new
        p = jnp.exp2(s_buf[g][...] - jnp.tile(m_new, (1, tk // LANES)))
        p_buf[g][...] = p.astype(BF16)

    def weighted_values(g, c):
        start = c * tk if isinstance(c, int) else pl.multiple_of(c * tk, tk)
        vc = v_ref[pl.ds(start, tk), g * LANES:(g + 1) * LANES]
        pv = jnp.dot(p_buf[g][...], vc, preferred_element_type=F32)
        acc_ref[g][...] = alpha_buf[g][...] * acc_ref[g][...] + pv

    def step(t):
        static = isinstance(t, int)
        for g in groups:
            weighted_values(g, t)
            if not static or t + 1 < nc:
                softmax(g)
            if not static or t + 2 < nc:
                scores(g, t + 2)

    for g in groups:
        scores(g, 0)
        softmax(g)
        if nc > 1:
            scores(g, 1)

    def steady(t, carry):
        step(t)
        return carry

    lax.fori_loop(0, max(nc - 2, 0), steady, 0)
    for t in range(max(nc - 2, 0), nc):
        step(t)

    for g in groups:
        acc = acc_ref[g][...]
        o = acc / pltpu.roll(acc, HEAD_DIM, 1)
        for j in range(GQA_GROUP):
            h = g * GQA_GROUP + j
            o_ref[:, h * HEAD_DIM:(h + 1) * HEAD_DIM] = o[j * tq:(j + 1) * tq, 0:HEAD_DIM].astype(o_ref.dtype)


def _dense_attn_call(q, kt, v1, bsz, seq):
    tq, tk = min(seq, 512), _key_chunk(seq)
    nq = seq // tq
    nc = seq // tk
    rows = GQA_GROUP * tq
    per_group = lambda shape, dtype: [pltpu.VMEM(shape, dtype) for _ in range(N_KV_HEADS)]
    return pl.pallas_call(
        functools.partial(_dense_attn_kernel, tq, tk),
        out_shape=jax.ShapeDtypeStruct((bsz * seq, ATTN_WIDTH), BF16),
        grid=(bsz, nq),
        in_specs=[
            pl.BlockSpec((tq, ATTN_WIDTH), lambda b, i: (b * nq + i, 0)),
            pl.BlockSpec((nc, KV_WIDTH, tk), lambda b, i: (b, 0, 0), pipeline_mode=pl.Buffered(1)),
            pl.BlockSpec((seq, 2 * KV_WIDTH), lambda b, i: (b, 0), pipeline_mode=pl.Buffered(1)),
        ],
        out_specs=pl.BlockSpec((tq, ATTN_WIDTH), lambda b, i: (b * nq + i, 0)),
        scratch_shapes=[
            pltpu.VMEM((N_KV_HEADS, rows, HEAD_DIM), BF16),
            per_group((rows, tk), F32), per_group((rows, tk), BF16),
            per_group((rows, LANES), F32), per_group((rows, LANES), F32),
            per_group((rows, LANES), F32), per_group((rows, LANES), F32),
        ],
        compiler_params=_cparams(("arbitrary", "arbitrary")),
        name="dense_gqa_attention",
    )(q, kt, v1)


def _hp_dot(a, b):
    return jnp.dot(a, b, preferred_element_type=F32, precision=lax.Precision.HIGHEST)


def _filter_trunk_kernel(z_ref, w1_ref, b1_ref, w2_ref, b2_ref, w3_ref, b3_ref, fq_ref, o_ref):
    fq = fq_ref[...]
    h = jnp.sin(fq * (_hp_dot(w1_ref[...], z_ref[...]) + b1_ref[...]))
    h = jnp.sin(fq * (_hp_dot(w2_ref[...], h) + b2_ref[...]))
    h = jnp.sin(fq * (_hp_dot(w3_ref[...], h) + b3_ref[...]))
    o_ref[...] = h.T


def _filter_trunk_call(zfeat_t, w1, b1, w2, b2, w3, b3, freq):
    rows = zfeat_t.shape[1]
    tr = min(rows, 2048)
    fw = HYENA_FILTER_WIDTH
    col = lambda v: v.reshape(fw, 1)
    return pl.pallas_call(
        _filter_trunk_kernel,
        out_shape=jax.ShapeDtypeStruct((rows, fw), F32),
        grid=(rows // tr,),
        in_specs=[
            pl.BlockSpec((HYENA_EMB_PAD, tr), lambda i: (0, i)),
            _resident((fw, HYENA_EMB_PAD)), _resident((fw, 1)),
            _resident((fw, fw)), _resident((fw, 1)),
            _resident((fw, fw)), _resident((fw, 1)),
            _resident((fw, 1)),
        ],
        out_specs=pl.BlockSpec((tr, fw), lambda i: (i, 0)),
        compiler_params=_cparams(("arbitrary",)),
        name="hyena_filter_trunk",
    )(zfeat_t, w1.T, col(b1), w2.T, col(b2), w3.T, col(b3), col(freq))


def _dft_stage1(src_ref, tab_ref, re_ref, im_ref, n_rows, k1p):
    def body(n2, carry):
        xs = src_ref[pl.ds(n2, n_rows, stride=DFT_N2), :].astype(BF16)
        res = jnp.dot(tab_ref[n2], xs, preferred_element_type=F32)
        re_ref[pl.ds(n2, k1p, stride=DFT_N2), :] = res[0:k1p]
        im_ref[pl.ds(n2, k1p, stride=DFT_N2), :] = res[k1p:]
        return carry

    lax.fori_loop(0, DFT_N2, body, 0, unroll=8)


def _filter_spec_kernel(seq, k1p, h_ref, w4f_ref, w4b_ref, dl_ref, e1_ref, f2_ref,
                        kr_ref, ki_ref, kfull):
    n = 2 * seq
    chunk = min(seq, 1024)
    dl = dl_ref[...]

    def fill(c, ss):
        r0 = pl.multiple_of(c * chunk, chunk)
        hc = h_ref[pl.ds(r0, chunk), :].astype(BF16)
        fwd = jnp.dot(hc, w4f_ref[...], preferred_element_type=F32)
        bwd = jnp.dot(hc, w4b_ref[...], preferred_element_type=F32)
        rows = r0 + lax.broadcasted_iota(jnp.int32, (chunk, 1), 0)
        lag = jnp.where(rows < seq, rows, n - rows).astype(F32)
        t = lag * (1.0 / (seq - 1))
        val = jnp.where(rows < seq, fwd, bwd) * jnp.exp(-t * dl)
        val = jnp.where(rows == seq, 0.0, val)
        kfull[pl.ds(r0, chunk), :] = val
        return ss + jnp.sum(val * val, axis=0, keepdims=True)

    ss = lax.fori_loop(0, n // chunk, fill, jnp.zeros((1, dl.shape[1]), F32))
    norm = lax.rsqrt(ss + EPS)

    _dft_stage1(kfull, e1_ref, kr_ref, ki_ref, n // DFT_N2, k1p)

    f2 = f2_ref[...]

    def stage2(a, carry):
        rows = pl.ds(pl.multiple_of(a * DFT_N2, DFT_N2), DFT_N2)
        slab = jnp.concatenate([kr_ref[rows, :], ki_ref[rows, :]], axis=0).astype(BF16)
        xf = jnp.dot(f2, slab, preferred_element_type=F32)
        kr_ref[rows, :] = xf[0:DFT_N2] * norm
        ki_ref[rows, :] = xf[DFT_N2:] * norm
        return carry

    lax.fori_loop(0, seq // DFT_N2 + 1, stage2, 0, unroll=8)


def _filter_spec_call(h3, w4, deltas, e1f, f2f, seq):
    n = 2 * seq
    k1p = e1f.shape[1] // 2
    ct = LANES
    nct = CONV_WIDTH // ct
    fw = HYENA_FILTER_WIDTH
    spec_rows = k1p * DFT_N2
    out = pl.BlockSpec((spec_rows, ct), lambda j: (0, j))
    return pl.pallas_call(
        functools.partial(_filter_spec_kernel, seq, k1p),
        out_shape=(jax.ShapeDtypeStruct((spec_rows, CONV_WIDTH), F32),
                   jax.ShapeDtypeStruct((spec_rows, CONV_WIDTH), F32)),
        grid=(nct,),
        in_specs=[
            _resident((n, fw)),
            pl.BlockSpec((fw, ct), lambda j: (0, j)),
            pl.BlockSpec((fw, ct), lambda j: (0, nct + j)),
            pl.BlockSpec((1, ct), lambda j: (0, j)),
            _resident(e1f.shape),
            _resident((2 * DFT_N2, 2 * DFT_N2)),
        ],
        out_specs=(out, out),
        scratch_shapes=[pltpu.VMEM((n, ct), F32)],
        compiler_params=_cparams(("arbitrary",)),
        name="hyena_filter_spectrum",
    )(h3, w4, w4, deltas, e1f, f2f)


def _long_conv_kernel(seq, k1p, z_ref, x0_ref, kr_ref, ki_ref, skip_ref, e1_ref, f2f_ref, f2i_ref,
                      einv_ref, o_ref, br, bi):
    n1h = seq // DFT_N2
    _dft_stage1(z_ref, e1_ref, br, bi, n1h, k1p)

    f2f = f2f_ref[...]
    f2i = f2i_ref[...]

    def spectral(a, carry):
        r0 = pl.multiple_of(a * DFT_N2, DFT_N2)
        rows = pl.ds(r0, DFT_N2)
        slab = jnp.concatenate([br[rows, :], bi[rows, :]], axis=0).astype(BF16)
        xf = jnp.dot(f2f, slab, preferred_element_type=F32)
        xr, xi = xf[0:DFT_N2], xf[DFT_N2:]
        kr, ki = kr_ref[rows, :], ki_ref[rows, :]
        y = jnp.concatenate([xr * kr - xi * ki, xr * ki + xi * kr], axis=0).astype(BF16)
        d = jnp.dot(f2i, y, preferred_element_type=F32)
        br[rows, :] = d[0:DFT_N2]
        bi[rows, :] = d[DFT_N2:]
        return carry

    lax.fori_loop(0, n1h + 1, spectral, 0, unroll=8)

    skip = skip_ref[...]

    def synth(n2, carry):
        ds_spec = pl.ds(n2, k1p, stride=DFT_N2)
        d = jnp.concatenate([br[ds_spec, :], bi[ds_spec, :]], axis=0).astype(BF16)
        o_ref[pl.ds(n2, n1h, stride=DFT_N2), :] = jnp.dot(einv_ref[n2], d, preferred_element_type=F32)
        return carry

    lax.fori_loop(0, DFT_N2, synth, 0, unroll=8)

    chunk = min(seq, 512)

    def gate(c, carry):
        rows = pl.ds(pl.multiple_of(c * chunk, chunk), chunk)
        o_ref[rows, :] = x0_ref[rows, :] * (o_ref[rows, :] + skip * z_ref[rows, :])
        return carry

    lax.fori_loop(0, seq // chunk, gate, 0)


def _long_conv_call(z, x0, kr, ki, skip, e1, f2f, f2i, einv, bsz, seq):
    ct = LANES
    nct = CONV_WIDTH // ct
    k1p = e1.shape[1] // 2
    spec_rows = k1p * DFT_N2
    big = lambda: pl.BlockSpec((seq, ct), lambda j, b: (b, j), pipeline_mode=pl.Buffered(1))
    spec = lambda: pl.BlockSpec((spec_rows, ct), lambda j, b: (0, j), pipeline_mode=pl.Buffered(1))
    return pl.pallas_call(
        functools.partial(_long_conv_kernel, seq, k1p),
        out_shape=jax.ShapeDtypeStruct((bsz * seq, CONV_WIDTH), F32),
        grid=(nct, bsz),
        in_specs=[
            big(), big(), spec(), spec(),
            pl.BlockSpec((1, ct), lambda j, b: (0, j)),
            _resident(e1.shape),
            _resident((2 * DFT_N2, 2 * DFT_N2)), _resident((2 * DFT_N2, 2 * DFT_N2)),
            _resident(einv.shape),
        ],
        out_specs=pl.BlockSpec((seq, ct), lambda j, b: (b, j)),
        scratch_shapes=[pltpu.VMEM((spec_rows, ct), F32), pltpu.VMEM((spec_rows, ct), F32)],
        compiler_params=_cparams(("arbitrary", "arbitrary")),
        name="hyena_long_conv",
    )(z, x0, kr, ki, skip, e1, f2f, f2i, einv)


def _rope_tables(seq):
    t = jnp.arange(seq)
    row = (t // GRID_W).astype(F32)
    col = (t % GRID_W).astype(F32)
    half = HEAD_DIM // 2
    inv = ROPE_THETA ** (-jnp.arange(0, half, 2, dtype=F32) / half)
    ang = jnp.concatenate([row[:, None] * inv, col[:, None] * inv], axis=-1)
    ang = jnp.repeat(ang, 2, axis=-1)
    sign = jnp.where(jnp.arange(HEAD_DIM) % 2 == 0, -1.0, 1.0).astype(F32)
    reps = LANES // HEAD_DIM
    return jnp.tile(jnp.cos(ang), (1, reps)), jnp.tile(jnp.sin(ang) * sign, (1, reps))


def _dft_tables(seq):
    n = 2 * seq
    n1 = n // DFT_N2
    k1 = n1 // 2 + 1
    k1p = -(-k1 // 8) * 8
    a = jnp.arange(k1p)
    live = (a < k1).astype(F32)
    ang1 = ((a[:, None] * jnp.arange(n1)[None, :]) % n1).astype(F32) * (2.0 * math.pi / n1)
    ang2 = (jnp.arange(DFT_N2)[:, None] * a[None, :]).astype(F32) * (2.0 * math.pi / n)
    c1, s1 = jnp.cos(ang1) * live[:, None], jnp.sin(ang1) * live[:, None]
    c2, s2 = jnp.cos(ang2), jnp.sin(ang2)
    cos_a = c1[None] * c2[:, :, None] - s1[None] * s2[:, :, None]
    sin_a = s1[None] * c2[:, :, None] + c1[None] * s2[:, :, None]
    e1_full = jnp.concatenate([cos_a, -sin_a], axis=1).astype(BF16)
    e1_half = e1_full[:, :, :n1 // 2]
    wgt = jnp.where((a == 0) | (a == n1 // 2), 1.0, 2.0) / n
    c1t, s1t = (c1 * wgt[:, None]).T[:n1 // 2], (s1 * wgt[:, None]).T[:n1 // 2]
    cos_s = c1t[None] * c2[:, None, :] - s1t[None] * s2[:, None, :]
    sin_s = s1t[None] * c2[:, None, :] + c1t[None] * s2[:, None, :]
    einv = jnp.concatenate([cos_s, -sin_s], axis=2).astype(BF16)
    kk = jnp.arange(DFT_N2)
    phi = ((kk[:, None] * kk[None, :]) % DFT_N2).astype(F32) * (2.0 * math.pi / DFT_N2)
    ci, si = jnp.cos(phi), jnp.sin(phi)
    f2f = jnp.block([[ci, si], [-si, ci]]).astype(BF16)
    f2i = jnp.block([[ci, -si], [si, ci]]).astype(BF16)
    return e1_full, e1_half, einv, f2f, f2i


def _filter_features_t(seq):
    j = jnp.arange(2 * seq)
    lag = jnp.where(j < seq, j, jnp.where(j == seq, 0, 2 * seq - j)).astype(F32)[None, :]
    t = lag / (seq - 1)
    bands = (HYENA_EMB - 1) // 2
    w = (2.0 * math.pi / seq) * lag
    fr = jnp.linspace(1e-4, bands - 1, bands, dtype=F32)[:, None]
    pad = jnp.zeros((HYENA_EMB_PAD - HYENA_EMB, 2 * seq), F32)
    return jnp.concatenate([t, jnp.cos(fr * w), -jnp.sin(fr * w), pad], axis=0)


def kernel(x, c, rel_table, norm_g, w_mod, b_mod, w_in, w_out, ffn_w1, ffn_w3, ffn_w2, a_qk_g, a_sink,
           b_conv_w, b_conv_b, c_qk_g, d_conv_w, d_conv_b, d_f_w1, d_f_b1, d_f_w2, d_f_b2, d_f_w3,
           d_f_b3, d_f_w4, d_f_freq, d_skip):
    bsz, seq, d = x.shape
    depth = w_mod.shape[0]
    m_rows = bsz * seq
    mod =_mod_call(c, w_mod, b_mod).reshape(depth, bsz, 3, 3, 1, d)
    e_blk = jnp.kron(jnp.eye(MXU_DIM // HEAD_DIM, dtype=F32), jnp.ones((HEAD_DIM, HEAD_DIM), F32)).astype(BF16)

    w1b, w3b = _to_bf16_from_transposed(ffn_w1), _to_bf16_from_transposed(ffn_w3)
    w2b, w_in_b, w_out_b = (w.astype(BF16) for w in (ffn_w2, w_in, w_out))

    def ffn(xf, l, which, sub, mixer=None):
        return _ffn_call(xf, norm_g[l, sub][None], mod[l, :, sub, 0], mod[l, :, sub, 1], mod[l, :, sub, 2],
                         w1b, w3b, w2b, l, which, seq, mixer)

    xf = x.reshape(m_rows, d)
    for l in range(depth):
        j = l // 2
        xf = ffn(xf, l, 0, 0)
        even = l % 2 == 0
        qk_g = a_qk_g[j] if even else c_qk_g[j]
        hyena = None if even else _rope_tables(seq) + (d_conv_w[j], d_conv_b[j][None])
        q, k, v, *conv_in = _inproj_call(
            xf, norm_g[l, 1][None], mod[l, :, 1, 0], mod[l, :, 1, 1], w_in_b, l,
            jnp.tile(qk_g[0], N_Q_HEADS)[None], jnp.tile(qk_g[1], N_KV_HEADS)[None], e_blk, seq, hyena)
        gate = mod[l, :, 1, 2]
        if even:
            (u,) = conv_in
            bias, sink_rows = _bias_call(rel_table, a_sink[j])
            xf = _mixer_ab_call(xf, q, k, v, u, bias, sink_rows, b_conv_w[j], b_conv_b[j][None], w_out_b, l,
                                gate, seq)
            pending = None
        else:
            x0, z = conv_in
            att = _dense_attn_call(q, k, v, bsz, seq)
            e1_full, e1_half, einv, f2f, f2i = _dft_tables(seq)
            fw1 = jnp.pad(d_f_w1[j], ((0, HYENA_EMB_PAD - HYENA_EMB), (0, 0)))
            h3 = _filter_trunk_call(_filter_features_t(seq), fw1, d_f_b1[j], d_f_w2[j], d_f_b2[j],
                                    d_f_w3[j], d_f_b3[j], d_f_freq[j])
            deltas = jnp.abs(jnp.linspace(HYENA_MIN_DECAY, HYENA_MAX_DECAY, CONV_WIDTH, dtype=F32))[None]
            kr, ki = _filter_spec_call(h3, d_f_w4[j].astype(BF16), deltas, e1_full, f2f, seq)
            y = _long_conv_call(z, x0, kr, ki, d_skip[j][None], e1_half, f2f, f2i, einv, bsz, seq)
            pending = (att, y, w_out_b, gate)
        xf = ffn(xf, l, 1, 2, pending)
    return xf.reshape(bsz, seq, d)
```

```python
import functools
import math

import jax
import jax.numpy as jnp
import numpy as np
from jax import lax
from jax.experimental import pallas as pl
from jax.experimental.pallas import tpu as pltpu

D_MODEL = 1024
HEAD_DIM = 64
N_Q_HEADS = 8
N_KV_HEADS = 2
GQA_GROUP = N_Q_HEADS // N_KV_HEADS
ATTN_WIDTH = N_Q_HEADS * HEAD_DIM
KV_WIDTH = N_KV_HEADS * HEAD_DIM
QKV_COLS = ATTN_WIDTH + 2 * KV_WIDTH
CONV_WIDTH = D_MODEL - ATTN_WIDTH
IN_COLS = QKV_COLS + 3 * CONV_WIDTH
D_FF = 2752
BLOCK = 128
WINDOW = 128
N_BUCKETS = 32
MAX_DISTANCE = 128
GRID_W = 64
ROPE_THETA = 10000.0
HYENA_EMB = 33
HYENA_FILTER_WIDTH = 64
HYENA_EMB_PAD = 64
HYENA_MIN_DECAY = math.log(1e-2) / 0.3
HYENA_MAX_DECAY = math.log(1e-2) / 1.5
EPS = 1e-6
NEG_INF = -1e30
LOG2_E = 1.4426950408889634

LANES = 128
HALO = 16
MXU_DIM = 256
FF_CHUNK = MXU_DIM
VMEM_LIMIT = 56 * 1024 * 1024
DFT_N2 = 128

BF16 = jnp.bfloat16
F32 = jnp.float32


def _cparams(sem):
    return pltpu.CompilerParams(dimension_semantics=sem, vmem_limit_bytes=VMEM_LIMIT)


def _resident(shape):
    nd = len(shape)
    return pl.BlockSpec(shape, lambda *_: (0,) * nd, pipeline_mode=pl.Buffered(1))


def _layer_slice(shape, layer):
    return pl.BlockSpec((None,) + shape, lambda *_: (layer, 0, 0), pipeline_mode=pl.Buffered(1))


def _adaln(x, g, scale, shift):
    y = x * lax.rsqrt(jnp.mean(x * x, axis=-1, keepdims=True) + EPS) * g
    return y * (1.0 + scale) + shift


def _mod_kernel(c_ref, w_ref, b_ref, o_ref):
    w = w_ref[0]
    for b in range(c_ref.shape[0]):
        cc = c_ref[b]
        cond = cc * jax.nn.sigmoid(cc)
        o_ref[0, b:b + 1, :] = jnp.sum(w * cond, axis=0, keepdims=True) + b_ref[0]


def _mod_call(c, w_mod, b_mod):
    depth, d, n = w_mod.shape
    bsz = c.shape[0]
    tn = 1152 if n % 1152 == 0 else 512
    return pl.pallas_call(
        _mod_kernel,
        out_shape=jax.ShapeDtypeStruct((depth, bsz, n), F32),
        grid=(depth, n // tn),
        in_specs=[
            pl.BlockSpec((bsz, d, 1), lambda l, j: (0, 0, 0)),
            pl.BlockSpec((1, d, tn), lambda l, j: (l, 0, j)),
            pl.BlockSpec((1, 1, tn), lambda l, j: (l, 0, j)),
        ],
        out_specs=pl.BlockSpec((1, bsz, tn), lambda l, j: (l, 0, j)),
        compiler_params=_cparams(("arbitrary", "arbitrary")),
        name="adaln_modulation",
    )(c.reshape(bsz, d, 1), w_mod, b_mod.reshape(depth, 1, n))


def _cast_transpose_kernel(wt_ref, o_ref):
    cols = wt_ref.shape[0]
    starts = list(range(0, cols - MXU_DIM, MXU_DIM)) + [cols - MXU_DIM]
    for c0 in starts:
        o_ref[:, c0:c0 + MXU_DIM] = wt_ref[c0:c0 + MXU_DIM, :].T.astype(o_ref.dtype)


def _to_bf16_from_transposed(w):
    lead, (rows, cols) = w.shape[:-2], w.shape[-2:]
    none = (None,) * len(lead)
    return pl.pallas_call(
        _cast_transpose_kernel,
        out_shape=jax.ShapeDtypeStruct(w.shape, BF16),
        grid=lead,
        in_specs=[pl.BlockSpec(none + (cols, rows), lambda *idx: idx + (0, 0))],
        out_specs=pl.BlockSpec(none + (rows, cols), lambda *idx: idx + (0, 0)),
        compiler_params=_cparams(("arbitrary",) * len(lead)),
        name="weights_to_bf16_transposing",
    )(jnp.swapaxes(w, -1, -2))


def _ffn_kernel(mixer_update, x_ref, g_ref, shift_ref, scale_ref, gate_ref, w1_ref, w3_ref, w2_ref, *rest):
    x = x_ref[...]
    if mixer_update:
        att_ref, y_ref, wo_ref, mgate_ref, o_ref = rest
        upd = jnp.dot(att_ref[...], wo_ref[0:ATTN_WIDTH, :], preferred_element_type=F32)
        upd = upd + jnp.dot(y_ref[...].astype(BF16), wo_ref[ATTN_WIDTH:, :], preferred_element_type=F32)
        x = x + mgate_ref[0] * upd
    else:
        (o_ref,) = rest
    h = _adaln(x, g_ref[...], scale_ref[0], shift_ref[0]).astype(BF16)
    acc = jnp.zeros(x.shape, F32)
    for c0 in range(0, D_FF, FF_CHUNK):
        cols = slice(c0, min(c0 + FF_CHUNK, D_FF))
        a = jnp.dot(h, w1_ref[:, cols], preferred_element_type=F32)
        b = jnp.dot(h, w3_ref[:, cols], preferred_element_type=F32)
        act = (a * jax.nn.sigmoid(a) * b).astype(BF16)
        acc = acc + jnp.dot(act, w2_ref[cols, :], preferred_element_type=F32)
    o_ref[...] = x + (0.5 * gate_ref[0]) * acc


def _ffn_call(x, g, shift, scale, gate, w1, w3, w2, layer, which, rows_per_batch, mixer=None):
    m, d = x.shape
    tm = 512
    tpb = rows_per_batch // tm
    row = lambda w: pl.BlockSpec((tm, w), lambda i: (i, 0))
    vec = pl.BlockSpec((1, 1, d), lambda i: (i // tpb, 0, 0))
    weight = lambda r, c: pl.BlockSpec((None, None, r, c), lambda i: (layer, which, 0, 0),
                                       pipeline_mode=pl.Buffered(1))
    in_specs = [row(d), _resident((1, d)), vec, vec, vec, weight(d, D_FF), weight(d, D_FF), weight(D_FF, d)]
    args = [x, g, shift, scale, gate, w1, w3, w2]
    if mixer is not None:
        in_specs += [row(ATTN_WIDTH), row(CONV_WIDTH), _layer_slice((d, d), layer), vec]
        args += list(mixer)
    return pl.pallas_call(
        functools.partial(_ffn_kernel, mixer is not None),
        out_shape=jax.ShapeDtypeStruct((m, d), F32),
        grid=(m // tm,),
        in_specs=in_specs,
        out_specs=row(d),
        compiler_params=_cparams(("arbitrary",)),
        name="adaln_swiglu_ffn",
    )(*args)


def _group_norm_scale(v, e_ref):
    sq = v * v
    hi = sq.astype(BF16)
    lo = (sq - hi.astype(F32)).astype(BF16)
    w = min(v.shape[1], MXU_DIM)
    e = e_ref[...]
    ss = jnp.concatenate(
        [jnp.dot(hi[:, c:c + w], e[0:w, 0:w], preferred_element_type=F32)
         + jnp.dot(lo[:, c:c + w], e[0:w, 0:w], preferred_element_type=F32) for c in range(0, v.shape[1], w)],
        axis=1)
    return lax.rsqrt(ss * (1.0 / HEAD_DIM) + EPS)


def _rope128(v, cos, sin_signed, even_lane):
    partner = jnp.where(even_lane, pltpu.roll(v, LANES - 1, 1), pltpu.roll(v, 1, 1))
    return v * cos + partner * sin_signed


def _short_conv_rows(u, before, after, cw, cb):
    rows = u.shape[0]
    r = lax.broadcasted_iota(jnp.int32, u.shape, 0)
    um1 = jnp.where(r == 0, before, pltpu.roll(u, 1, 0))
    up1 = jnp.where(r == rows - 1, after, pltpu.roll(u, rows - 1, 0))
    return cw[0:1] * um1 + cw[1:2] * u + cw[2:3] * up1 + cb


def _inproj_kernel(rope, tpb, x_ref, g_ref, shift_ref, scale_ref, w_ref, qg_ref, kg_ref, e_ref, *rest):
    if rope:
        cos_ref, sin_ref, xp_ref, xn_ref, cw_ref, cb_ref, q_ref, k_ref, v_ref, x0_ref, z_ref = rest
    else:
        q_ref, k_ref, v_ref, u_ref = rest
    h = _adaln(x_ref[...], g_ref[...], scale_ref[0], shift_ref[0]).astype(BF16)
    q = jnp.dot(h, w_ref[:, 0:ATTN_WIDTH], preferred_element_type=F32)
    k = jnp.dot(h, w_ref[:, ATTN_WIDTH:ATTN_WIDTH + KV_WIDTH], preferred_element_type=F32)
    q = q * _group_norm_scale(q, e_ref) * qg_ref[...]
    k = k * _group_norm_scale(k, e_ref) * kg_ref[...]
    if rope:
        cos = cos_ref[...]
        sin = sin_ref[...]
        even = (lax.broadcasted_iota(jnp.int32, cos.shape, 1) % 2) == 0
        q = jnp.concatenate(
            [_rope128(q[:, j * LANES:(j + 1) * LANES], cos, sin, even) for j in range(ATTN_WIDTH // LANES)],
            axis=1)
        k = _rope128(k, cos, sin, even)
    q_ref[...] = (q * (HEAD_DIM ** -0.5 * (LOG2_E if rope else 1.0))).astype(BF16)
    v = jnp.dot(h, w_ref[:, ATTN_WIDTH + KV_WIDTH:QKV_COLS], preferred_element_type=F32)
    k_ref[0] = k.T.astype(BF16)
    low = lax.broadcasted_iota(jnp.int32, v.shape, 1) < HEAD_DIM
    v_ref[...] = jnp.concatenate(
        [jnp.where(low, v, 1.0), jnp.where(low, pltpu.roll(v, HEAD_DIM, 1), 1.0)], axis=1).astype(BF16)
    if not rope:
        u_ref[...] = jnp.dot(h, w_ref[:, QKV_COLS:IN_COLS], preferred_element_type=F32)
        return
    i = pl.program_id(0)
    h_prev = _adaln(xp_ref[...], g_ref[...], scale_ref[0], shift_ref[0]).astype(BF16)
    h_next = _adaln(xn_ref[...], g_ref[...], scale_ref[0], shift_ref[0]).astype(BF16)
    u_ext = jnp.dot(jnp.concatenate([h_prev, h, h_next], axis=0), w_ref[:, QKV_COLS:IN_COLS],
                    preferred_element_type=F32)
    tm = h.shape[0]
    before = jnp.where((i % tpb) == 0, 0.0, u_ext[HALO - 1:HALO])
    after = jnp.where((i % tpb) == tpb - 1, 0.0, u_ext[HALO + tm:HALO + tm + 1])
    t = _short_conv_rows(u_ext[HALO:HALO + tm], before, after, cw_ref[...], cb_ref[...])
    x0_ref[...] = t[:, 0:CONV_WIDTH]
    z_ref[...] = t[:, CONV_WIDTH:2 * CONV_WIDTH] * t[:, 2 * CONV_WIDTH:]


def _inproj_call(x, g, shift, scale, w_in, layer, qg, kg, e, rows_per_batch, hyena=None):
    m, d = x.shape
    tm = 512
    tpb = rows_per_batch // tm
    vec = pl.BlockSpec((1, 1, d), lambda i: (i // tpb, 0, 0))
    row = lambda w: pl.BlockSpec((tm, w), lambda i: (i, 0))
    in_specs = [
        row(d),
        _resident((1, d)), vec, vec,
        _layer_slice((d, IN_COLS), layer),
        _resident((1, ATTN_WIDTH)), _resident((1, KV_WIDTH)),
        _resident((MXU_DIM, MXU_DIM)),
    ]
    args = [x, g, shift, scale, w_in, qg, kg, e]
    uw = 3 * CONV_WIDTH
    if hyena is not None:
        cos, sin, conv_w, conv_b = hyena
        tab = pl.BlockSpec((tm, LANES), lambda i: (i % tpb, 0))
        hpt = tm // HALO
        last_halo = m // HALO - 1
        in_specs += [
            tab, tab,
            pl.BlockSpec((HALO, d), lambda i: (jnp.maximum(i * hpt - 1, 0), 0)),
            pl.BlockSpec((HALO, d), lambda i: (jnp.minimum((i + 1) * hpt, last_halo), 0)),
            _resident((3, uw)), _resident((1, uw)),
        ]
        args += [cos, sin, x, x, conv_w, conv_b]
        conv_shapes = [jax.ShapeDtypeStruct((m, CONV_WIDTH), F32)] * 2
        conv_specs = [row(CONV_WIDTH)] * 2
    else:
        conv_shapes = [jax.ShapeDtypeStruct((m, uw), F32)]
        conv_specs = [row(uw)]
    tk = _key_chunk(rows_per_batch)
    per = tk // tm
    return pl.pallas_call(
        functools.partial(_inproj_kernel, hyena is not None, tpb),
        out_shape=[
            jax.ShapeDtypeStruct((m, ATTN_WIDTH), BF16),
            jax.ShapeDtypeStruct((m // tk, KV_WIDTH, tk), BF16),
            jax.ShapeDtypeStruct((m, 2 * KV_WIDTH), BF16),
        ] + conv_shapes,
        grid=(m // tm,),
        in_specs=in_specs,
        out_specs=[row(ATTN_WIDTH), pl.BlockSpec((1, KV_WIDTH, tm), lambda i: (i // per, 0, i % per)),
                   row(2 * KV_WIDTH)] + conv_specs,
        compiler_params=_cparams(("arbitrary",)),
        name="adaln_in_projection",
    )(*args)


_T5_STEPS = (12, 16, 23, 32, 46, 64, 91)


def _bias_kernel(tab_ref, sink_ref, o_ref, sink_rows_ref):
    for h in range(N_Q_HEADS):
        sink_rows_ref[h * BLOCK:(h + 1) * BLOCK, :] = jnp.full((BLOCK, LANES), sink_ref[h], F32)
    qi = lax.broadcasted_iota(jnp.int32, (BLOCK, 3 * BLOCK), 0)
    kj = lax.broadcasted_iota(jnp.int32, (BLOCK, 3 * BLOCK), 1)
    rel = kj - BLOCK - qi
    n = jnp.abs(rel)
    half = N_BUCKETS // 2
    max_exact = half // 2
    large = jnp.full(n.shape, max_exact, jnp.int32)
    for t in _T5_STEPS:
        large = large + (n >= t).astype(jnp.int32)
    bucket = jnp.where(rel > 0, half, 0) + jnp.where(n < max_exact, n, large)
    for h in range(N_Q_HEADS):
        bias = jnp.zeros(n.shape, F32)
        for b in range(N_BUCKETS):
            bias = jnp.where(bucket == b, tab_ref[b, h], bias)
        o_ref[h] = jnp.where(n <= WINDOW, bias, NEG_INF)


def _bias_call(rel_table, sink):
    return pl.pallas_call(
        _bias_kernel,
        out_shape=(jax.ShapeDtypeStruct((N_Q_HEADS, BLOCK, 3 * BLOCK), F32),
                   jax.ShapeDtypeStruct((N_Q_HEADS * BLOCK, LANES), F32)),
        in_specs=[pl.BlockSpec(memory_space=pltpu.SMEM), pl.BlockSpec(memory_space=pltpu.SMEM)],
        name="t5_bias_tile",
    )(rel_table, sink)


def _mixer_ab_kernel(tq, tpb, x_ref, q_ref, kc_ref, kp_ref, kn_ref, vc_ref, vp_ref, vn_ref,
                     uc_ref, up_ref, un_ref, bias_ref, sink_ref, cw_ref, cb_ref, wo_ref, gate_ref,
                     o_ref, kbuf, vbuf, qs, att):
    i = pl.program_id(0)
    first = (i % tpb) == 0
    last = (i % tpb) == tpb - 1
    nblk = tq // BLOCK
    grows = GQA_GROUP * BLOCK
    kbuf[:, 0:BLOCK] = kp_ref[0]
    kbuf[:, BLOCK:BLOCK + tq] = kc_ref[0]
    kbuf[:, BLOCK + tq:] = kn_ref[0]
    vbuf[0:BLOCK] = vp_ref[...]
    vbuf[BLOCK:BLOCK + tq] = vc_ref[...]
    vbuf[BLOCK + tq:] = vn_ref[...]
    for n in range(nblk):
        for h in range(N_Q_HEADS):
            g, j = divmod(h, GQA_GROUP)
            qs[g, n, j * BLOCK:(j + 1) * BLOCK, :] = q_ref[n * BLOCK:(n + 1) * BLOCK, h * HEAD_DIM:(h + 1) * HEAD_DIM]
    col = lax.broadcasted_iota(jnp.int32, (grows, 3 * BLOCK), 1)
    for n in range(nblk):
        keys = slice(n * BLOCK, (n + 3) * BLOCK)
        for g in range(N_KV_HEADS):
            s = jnp.dot(qs[g, n], kbuf[g * HEAD_DIM:(g + 1) * HEAD_DIM, keys], preferred_element_type=F32)
            s = s + bias_ref[g * GQA_GROUP:(g + 1) * GQA_GROUP].reshape(grows, 3 * BLOCK)
            if n == 0:
                s = jnp.where(jnp.logical_and(first, col < BLOCK), NEG_INF, s)
            if n == nblk - 1:
                s = jnp.where(jnp.logical_and(last, col >= 2 * BLOCK), NEG_INF, s)
            sk = sink_ref[g * grows:(g + 1) * grows, :]
            mx = jnp.maximum(jnp.broadcast_to(jnp.max(s, axis=-1, keepdims=True), sk.shape), sk)
            p = jnp.exp(s - jnp.tile(mx, (1, 3)))
            pv = jnp.dot(p.astype(BF16), vbuf[keys, g * LANES:(g + 1) * LANES], preferred_element_type=F32)
            o = pv / (pltpu.roll(pv, HEAD_DIM, 1) + jnp.exp(sk - mx))
            for j in range(GQA_GROUP):
                h = g * GQA_GROUP + j
                att[n * BLOCK:(n + 1) * BLOCK, h * HEAD_DIM:(h + 1) * HEAD_DIM] = o[j * BLOCK:(j + 1) * BLOCK, 0:HEAD_DIM]

    gb = uc_ref[:, 0:CONV_WIDTH]
    p = uc_ref[:, CONV_WIDTH:2 * CONV_WIDTH] * uc_ref[:, 2 * CONV_WIDTH:]
    p_before = jnp.where(first, 0.0, up_ref[7:8, CONV_WIDTH:2 * CONV_WIDTH] * up_ref[7:8, 2 * CONV_WIDTH:])
    p_after = jnp.where(last, 0.0, un_ref[0:1, CONV_WIDTH:2 * CONV_WIDTH] * un_ref[0:1, 2 * CONV_WIDTH:])
    conv = gb * _short_conv_rows(p, p_before, p_after, cw_ref[...], cb_ref[...])

    y = jnp.dot(att[...].astype(BF16), wo_ref[0:ATTN_WIDTH, :], preferred_element_type=F32)
    y = y + jnp.dot(conv.astype(BF16), wo_ref[ATTN_WIDTH:, :], preferred_element_type=F32)
    o_ref[...] = x_ref[...] + gate_ref[0] * y


def _mixer_ab_call(x, q, kt, v1, u, bias, sink_rows, conv_w, conv_b, w_out, layer, gate, rows_per_batch):
    m, d = x.shape
    tq = 512
    tpb = rows_per_batch // tq
    r = tq // BLOCK
    nb = m // BLOCK
    n8 = m // 8
    cur = lambda w: pl.BlockSpec((tq, w), lambda i: (i, 0))
    prev_of = lambda i: jnp.maximum(i * r - 1, 0)
    next_of = lambda i: jnp.minimum((i + 1) * r, nb - 1)
    v_prev = pl.BlockSpec((BLOCK, 2 * KV_WIDTH), lambda i: (prev_of(i), 0))
    v_next = pl.BlockSpec((BLOCK, 2 * KV_WIDTH), lambda i: (next_of(i), 0))
    tk = kt.shape[2]
    k_cur = pl.BlockSpec((1, KV_WIDTH, tq), lambda i: (i // (tk // tq), 0, i % (tk // tq)))
    kpb = tk // BLOCK
    k_prev = pl.BlockSpec((1, KV_WIDTH, BLOCK), lambda i: (prev_of(i) // kpb, 0, prev_of(i) % kpb))
    k_next = pl.BlockSpec((1, KV_WIDTH, BLOCK), lambda i: (next_of(i) // kpb, 0, next_of(i) % kpb))
    uw = 3 * CONV_WIDTH
    return pl.pallas_call(
        functools.partial(_mixer_ab_kernel, tq, tpb),
        out_shape=jax.ShapeDtypeStruct((m, d), F32),
        grid=(m // tq,),
        in_specs=[
            cur(d), cur(ATTN_WIDTH),
            k_cur, k_prev, k_next,
            cur(2 * KV_WIDTH), v_prev, v_next,
            cur(uw),
            pl.BlockSpec((8, uw), lambda i: (jnp.maximum(i * (tq // 8) - 1, 0), 0)),
            pl.BlockSpec((8, uw), lambda i: (jnp.minimum((i + 1) * (tq // 8), n8 - 1), 0)),
            _resident((N_Q_HEADS, BLOCK, 3 * BLOCK)),
            _resident((N_Q_HEADS * BLOCK, LANES)),
            _resident((3, CONV_WIDTH)), _resident((1, CONV_WIDTH)),
            _layer_slice((d, d), layer),
            pl.BlockSpec((1, 1, d), lambda i: (i // tpb, 0, 0)),
        ],
        out_specs=cur(d),
        scratch_shapes=[
            pltpu.VMEM((KV_WIDTH, tq + 2 * BLOCK), BF16),
            pltpu.VMEM((tq + 2 * BLOCK, 2 * KV_WIDTH), BF16),
            pltpu.VMEM((N_KV_HEADS, r, GQA_GROUP * BLOCK, HEAD_DIM), BF16),
            pltpu.VMEM((tq, ATTN_WIDTH), F32),
        ],
        compiler_params=_cparams(("arbitrary",)),
        name="windowed_attn_shortconv_outproj",
    )(x, q, kt, kt, kt, v1, v1, v1, u, u, u, bias, sink_rows, conv_w, conv_b, w_out, gate)


def _key_chunk(seq):
    return min(seq, 1024)


def _dense_attn_kernel(tq, tk, q_ref, k_ref, v_ref, o_ref, qs, s_buf, p_buf, rmax_buf, alpha_buf,
                       m_ref, acc_ref):
    nc = k_ref.shape[0]
    for j in range(GQA_GROUP):
        qs[j * tq:(j + 1) * tq, :] = q_ref[:, j * HEAD_DIM:(j + 1) * HEAD_DIM]
    m_ref[...] = jnp.full(m_ref.shape, -jnp.inf, F32)
    acc_ref[...] = jnp.zeros(acc_ref.shape, F32)

    def scores(c):
        s = jnp.dot(qs[...], k_ref[c], preferred_element_type=F32)
        s_buf[...] = s
        rmax_buf[...] = jnp.broadcast_to(jnp.max(s, axis=-1, keepdims=True), rmax_buf.shape)

    def softmax():
        m_old = m_ref[...]
        m_new = jnp.maximum(m_old, rmax_buf[...])
        alpha_buf[...] = jnp.exp2(m_old - m_new)
        m_ref[...] = m_new
        p = jnp.exp2(s_buf[...] - jnp.tile(m_new, (1, tk // LANES)))
        p_buf[...] = p.astype(BF16)

    def weighted_values(c):
        start = c * tk if isinstance(c, int) else pl.multiple_of(c * tk, tk)
        pv = jnp.dot(p_buf[...], v_ref[pl.ds(start, tk), :], preferred_element_type=F32)
        acc_ref[...] = alpha_buf[...] * acc_ref[...] + pv

    def step(t):
        static = isinstance(t, int)
        weighted_values(t)
        if not static or t + 1 < nc:
            softmax()
        if not static or t + 2 < nc:
            scores(t + 2)

    scores(0)
    softmax()
    if nc > 1:
        scores(1)

    def steady(t, carry):
        step(t)
        return carry

    lax.fori_loop(0, max(nc - 2, 0), steady, 0)
    for t in range(max(nc - 2, 0), nc):
        step(t)

    acc = acc_ref[...]
    o = acc / pltpu.roll(acc, HEAD_DIM, 1)
    for j in range(GQA_GROUP):
        o_ref[:, j * HEAD_DIM:(j + 1) * HEAD_DIM] = o[j * tq:(j + 1) * tq, 0:HEAD_DIM].astype(o_ref.dtype)


def _dense_attn_bounded_kernel(tq, tk, bound_ref, q_ref, k_ref, v_ref, o_ref, qs, p_buf, acc_ref):
    nc = k_ref.shape[0]
    for j in range(GQA_GROUP):
        qs[j * tq:(j + 1) * tq, :] = q_ref[:, j * HEAD_DIM:(j + 1) * HEAD_DIM]
    shift = bound_ref[0]

    def probabilities(c):
        s = jnp.dot(qs[...], k_ref[c], preferred_element_type=F32)
        p_buf[...] = jnp.exp2(s - shift).astype(BF16)

    def weighted_values(c):
        start = c * tk if isinstance(c, int) else pl.multiple_of(c * tk, tk)
        return jnp.dot(p_buf[...], v_ref[pl.ds(start, tk), :], preferred_element_type=F32)

    probabilities(0)
    acc_ref[...] = weighted_values(0)
    if nc > 1:
        probabilities(1)

    def steady(t, carry):
        acc_ref[...] += weighted_values(t)
        probabilities(t + 1)
        return carry

    lax.fori_loop(1, nc - 1, steady, 0)
    if nc > 1:
        acc_ref[...] += weighted_values(nc - 1)

    acc = acc_ref[...]
    o = acc / pltpu.roll(acc, HEAD_DIM, 1)
    for j in range(GQA_GROUP):
        o_ref[:, j * HEAD_DIM:(j + 1) * HEAD_DIM] = o[j * tq:(j + 1) * tq, 0:HEAD_DIM].astype(o_ref.dtype)


MAX_FIXED_SHIFT = 50.0


def _dense_attn_call(q, kt, v1, logit_bound, bsz, seq):
    tq, tk = min(seq, 1024), _key_chunk(seq)
    nq = seq // tq
    nc = seq // tk
    rows = GQA_GROUP * tq
    gw = GQA_GROUP * HEAD_DIM
    stat = pltpu.VMEM((rows, LANES), F32)
    q_spec = pl.BlockSpec((tq, gw), lambda b, g, i: (b * nq + i, g))
    k_spec = pl.BlockSpec((nc, HEAD_DIM, tk), lambda b, g, i: (b, g, 0), pipeline_mode=pl.Buffered(1))
    v_spec = pl.BlockSpec((seq, LANES), lambda b, g, i: (b, g), pipeline_mode=pl.Buffered(1))
    common = dict(
        out_shape=jax.ShapeDtypeStruct((bsz * seq, ATTN_WIDTH), BF16),
        grid=(bsz, N_KV_HEADS, nq),
        out_specs=q_spec,
        compiler_params=_cparams(("arbitrary", "arbitrary", "arbitrary")),
    )

    def running_max():
        return pl.pallas_call(
            functools.partial(_dense_attn_kernel, tq, tk),
            in_specs=[q_spec, k_spec, v_spec],
            scratch_shapes=[
                pltpu.VMEM((rows, HEAD_DIM), BF16),
                pltpu.VMEM((rows, tk), F32), pltpu.VMEM((rows, tk), BF16),
                stat, stat, stat, stat,
            ],
            name="dense_gqa_attention", **common,
        )(q, kt, v1)

    def fixed_shift():
        return pl.pallas_call(
            functools.partial(_dense_attn_bounded_kernel, tq, tk),
            in_specs=[pl.BlockSpec(memory_space=pltpu.SMEM), q_spec, k_spec, v_spec],
            scratch_shapes=[pltpu.VMEM((rows, HEAD_DIM), BF16), pltpu.VMEM((rows, tk), BF16), stat],
            name="dense_gqa_attention_fixed_shift", **common,
        )(logit_bound.reshape(1), q, kt, v1)

    return lax.cond(logit_bound <= MAX_FIXED_SHIFT, fixed_shift, running_max)


def _hp_dot(a, b):
    return jnp.dot(a, b, preferred_element_type=F32, precision=lax.Precision.HIGHEST)


def _filter_trunk_kernel(z_ref, w1_ref, b1_ref, w2_ref, b2_ref, w3_ref, b3_ref, fq_ref, o_ref):
    fq = fq_ref[...]
    h = jnp.sin(fq * (_hp_dot(w1_ref[...], z_ref[...]) + b1_ref[...]))
    h = jnp.sin(fq * (_hp_dot(w2_ref[...], h) + b2_ref[...]))
    h = jnp.sin(fq * (_hp_dot(w3_ref[...], h) + b3_ref[...]))
    o_ref[...] = h.T


def _filter_trunk_call(zfeat_t, w1, b1, w2, b2, w3, b3, freq):
    rows = zfeat_t.shape[1]
    tr = min(rows, 2048)
    fw = HYENA_FILTER_WIDTH
    col = lambda v: v.reshape(fw, 1)
    return pl.pallas_call(
        _filter_trunk_kernel,
        out_shape=jax.ShapeDtypeStruct((rows, fw), F32),
        grid=(rows // tr,),
        in_specs=[
            pl.BlockSpec((HYENA_EMB_PAD, tr), lambda i: (0, i)),
            _resident((fw, HYENA_EMB_PAD)), _resident((fw, 1)),
            _resident((fw, fw)), _resident((fw, 1)),
            _resident((fw, fw)), _resident((fw, 1)),
            _resident((fw, 1)),
        ],
        out_specs=pl.BlockSpec((tr, fw), lambda i: (i, 0)),
        compiler_params=_cparams(("arbitrary",)),
        name="hyena_filter_trunk",
    )(zfeat_t, w1.T, col(b1), w2.T, col(b2), w3.T, col(b3), col(freq))


def _dft_stage1(src_ref, tab_ref, re_ref, im_ref, n_rows, k1p):
    def body(n2, carry):
        xs = src_ref[pl.ds(n2, n_rows, stride=DFT_N2), :].astype(BF16)
        res = jnp.dot(tab_ref[n2], xs, preferred_element_type=F32)
        re_ref[pl.ds(n2, k1p, stride=DFT_N2), :] = res[0:k1p]
        im_ref[pl.ds(n2, k1p, stride=DFT_N2), :] = res[k1p:]
        return carry

    lax.fori_loop(0, DFT_N2, body, 0, unroll=8)


def _filter_spec_kernel(seq, k1p, h_ref, w4f_ref, w4b_ref, dl_ref, e1_ref, f2_ref,
                        kr_ref, ki_ref, kfull):
    n = 2 * seq
    chunk = min(seq, 1024)
    dl = dl_ref[...]

    def fill(c, ss):
        r0 = pl.multiple_of(c * chunk, chunk)
        hc = h_ref[pl.ds(r0, chunk), :].astype(BF16)
        fwd = jnp.dot(hc, w4f_ref[...], preferred_element_type=F32)
        bwd = jnp.dot(hc, w4b_ref[...], preferred_element_type=F32)
        rows = r0 + lax.broadcasted_iota(jnp.int32, (chunk, 1), 0)
        lag = jnp.where(rows < seq, rows, n - rows).astype(F32)
        t = lag * (1.0 / (seq - 1))
        val = jnp.where(rows < seq, fwd, bwd) * jnp.exp(-t * dl)
        val = jnp.where(rows == seq, 0.0, val)
        kfull[pl.ds(r0, chunk), :] = val
        return ss + jnp.sum(val * val, axis=0, keepdims=True)

    ss = lax.fori_loop(0, n // chunk, fill, jnp.zeros((1, dl.shape[1]), F32))
    norm = lax.rsqrt(ss + EPS)

    _dft_stage1(kfull, e1_ref, kr_ref, ki_ref, n // DFT_N2, k1p)

    f2 = f2_ref[...]

    def stage2(a, carry):
        rows = pl.ds(pl.multiple_of(a * DFT_N2, DFT_N2), DFT_N2)
        slab = jnp.concatenate([kr_ref[rows, :], ki_ref[rows, :]], axis=0).astype(BF16)
        xf = jnp.dot(f2, slab, preferred_element_type=F32)
        kr_ref[rows, :] = xf[0:DFT_N2] * norm
        ki_ref[rows, :] = xf[DFT_N2:] * norm
        return carry

    lax.fori_loop(0, seq // DFT_N2 + 1, stage2, 0, unroll=8)


def _filter_spec_call(h3, w4, deltas, e1f, f2f, seq):
    n = 2 * seq
    k1p = e1f.shape[1] // 2
    ct = LANES
    nct = CONV_WIDTH // ct
    fw = HYENA_FILTER_WIDTH
    spec_rows = k1p * DFT_N2
    out = pl.BlockSpec((spec_rows, ct), lambda j: (0, j))
    return pl.pallas_call(
        functools.partial(_filter_spec_kernel, seq, k1p),
        out_shape=(jax.ShapeDtypeStruct((spec_rows, CONV_WIDTH), F32),
                   jax.ShapeDtypeStruct((spec_rows, CONV_WIDTH), F32)),
        grid=(nct,),
        in_specs=[
            _resident((n, fw)),
            pl.BlockSpec((fw, ct), lambda j: (0, j)),
            pl.BlockSpec((fw, ct), lambda j: (0, nct + j)),
            pl.BlockSpec((1, ct), lambda j: (0, j)),
            _resident(e1f.shape),
            _resident((2 * DFT_N2, 2 * DFT_N2)),
        ],
        out_specs=(out, out),
        scratch_shapes=[pltpu.VMEM((n, ct), F32)],
        compiler_params=_cparams(("arbitrary",)),
        name="hyena_filter_spectrum",
    )(h3, w4, w4, deltas, e1f, f2f)


def _long_conv_kernel(seq, k1p, z_ref, x0_ref, kr_ref, ki_ref, skip_ref, e1_ref, f2f_ref, f2i_ref,
                      einv_ref, o_ref, br, bi):
    n1h = seq // DFT_N2
    _dft_stage1(z_ref, e1_ref, br, bi, n1h, k1p)

    f2f = f2f_ref[...]
    f2i = f2i_ref[...]

    def spectral(a, carry):
        r0 = pl.multiple_of(a * DFT_N2, DFT_N2)
        rows = pl.ds(r0, DFT_N2)
        slab = jnp.concatenate([br[rows, :], bi[rows, :]], axis=0).astype(BF16)
        xf = jnp.dot(f2f, slab, preferred_element_type=F32)
        xr, xi = xf[0:DFT_N2], xf[DFT_N2:]
        kr, ki = kr_ref[rows, :], ki_ref[rows, :]
        y = jnp.concatenate([xr * kr - xi * ki, xr * ki + xi * kr], axis=0).astype(BF16)
        d = jnp.dot(f2i, y, preferred_element_type=F32)
        br[rows, :] = d[0:DFT_N2]
        bi[rows, :] = d[DFT_N2:]
        return carry

    lax.fori_loop(0, n1h + 1, spectral, 0, unroll=8)

    skip = skip_ref[...]

    def synth(n2, carry):
        ds_spec = pl.ds(n2, k1p, stride=DFT_N2)
        d = jnp.concatenate([br[ds_spec, :], bi[ds_spec, :]], axis=0).astype(BF16)
        o_ref[pl.ds(n2, n1h, stride=DFT_N2), :] = jnp.dot(einv_ref[n2], d, preferred_element_type=F32)
        return carry

    lax.fori_loop(0, DFT_N2, synth, 0, unroll=8)

    chunk = min(seq, 512)

    def gate(c, carry):
        rows = pl.ds(pl.multiple_of(c * chunk, chunk), chunk)
        o_ref[rows, :] = x0_ref[rows, :] * (o_ref[rows, :] + skip * z_ref[rows, :])
        return carry

    lax.fori_loop(0, seq // chunk, gate, 0)


def _long_conv_call(z, x0, kr, ki, skip, e1, f2f, f2i, einv, bsz, seq):
    ct = LANES
    nct = CONV_WIDTH // ct
    k1p = e1.shape[1] // 2
    spec_rows = k1p * DFT_N2
    big = lambda: pl.BlockSpec((seq, ct), lambda j, b: (b, j), pipeline_mode=pl.Buffered(1))
    spec = lambda: pl.BlockSpec((spec_rows, ct), lambda j, b: (0, j), pipeline_mode=pl.Buffered(1))
    return pl.pallas_call(
        functools.partial(_long_conv_kernel, seq, k1p),
        out_shape=jax.ShapeDtypeStruct((bsz * seq, CONV_WIDTH), F32),
        grid=(nct, bsz),
        in_specs=[
            big(), big(), spec(), spec(),
            pl.BlockSpec((1, ct), lambda j, b: (0, j)),
            _resident(e1.shape),
            _resident((2 * DFT_N2, 2 * DFT_N2)), _resident((2 * DFT_N2, 2 * DFT_N2)),
            _resident(einv.shape),
        ],
        out_specs=pl.BlockSpec((seq, ct), lambda j, b: (b, j)),
        scratch_shapes=[pltpu.VMEM((spec_rows, ct), F32), pltpu.VMEM((spec_rows, ct), F32)],
        compiler_params=_cparams(("arbitrary", "arbitrary")),
        name="hyena_long_conv",
    )(z, x0, kr, ki, skip, e1, f2f, f2i, einv)


def _rope_tables(seq):
    t = jnp.arange(seq)
    row = (t // GRID_W).astype(F32)
    col = (t % GRID_W).astype(F32)
    half = HEAD_DIM // 2
    inv = ROPE_THETA ** (-jnp.arange(0, half, 2, dtype=F32) / half)
    ang = jnp.concatenate([row[:, None] * inv, col[:, None] * inv], axis=-1)
    ang = jnp.repeat(ang, 2, axis=-1)
    sign = jnp.where(jnp.arange(HEAD_DIM) % 2 == 0, -1.0, 1.0).astype(F32)
    reps = LANES // HEAD_DIM
    return jnp.tile(jnp.cos(ang), (1, reps)), jnp.tile(jnp.sin(ang) * sign, (1, reps))


def _dft_tables(seq):
    n = 2 * seq
    n1 = n // DFT_N2
    k1 = n1 // 2 + 1
    k1p = -(-k1 // 8) * 8
    a = jnp.arange(k1p)
    live = (a < k1).astype(F32)
    ang1 = ((a[:, None] * jnp.arange(n1)[None, :]) % n1).astype(F32) * (2.0 * math.pi / n1)
    ang2 = (jnp.arange(DFT_N2)[:, None] * a[None, :]).astype(F32) * (2.0 * math.pi / n)
    c1, s1 = jnp.cos(ang1) * live[:, None], jnp.sin(ang1) * live[:, None]
    c2, s2 = jnp.cos(ang2), jnp.sin(ang2)
    cos_a = c1[None] * c2[:, :, None] - s1[None] * s2[:, :, None]
    sin_a = s1[None] * c2[:, :, None] + c1[None] * s2[:, :, None]
    e1_full = jnp.concatenate([cos_a, -sin_a], axis=1).astype(BF16)
    e1_half = e1_full[:, :, :n1 // 2]
    wgt = jnp.where((a == 0) | (a == n1 // 2), 1.0, 2.0) / n
    c1t, s1t = (c1 * wgt[:, None]).T[:n1 // 2], (s1 * wgt[:, None]).T[:n1 // 2]
    cos_s = c1t[None] * c2[:, None, :] - s1t[None] * s2[:, None, :]
    sin_s = s1t[None] * c2[:, None, :] + c1t[None] * s2[:, None, :]
    einv = jnp.concatenate([cos_s, -sin_s], axis=2).astype(BF16)
    kk = jnp.arange(DFT_N2)
    phi = ((kk[:, None] * kk[None, :]) % DFT_N2).astype(F32) * (2.0 * math.pi / DFT_N2)
    ci, si = jnp.cos(phi), jnp.sin(phi)
    f2f = jnp.block([[ci, si], [-si, ci]]).astype(BF16)
    f2i = jnp.block([[ci, -si], [si, ci]]).astype(BF16)
    return e1_full, e1_half, einv, f2f, f2i


def _filter_features_t(seq):
    j = jnp.arange(2 * seq)
    lag = jnp.where(j < seq, j, jnp.where(j == seq, 0, 2 * seq - j)).astype(F32)[None, :]
    t = lag / (seq - 1)
    bands = (HYENA_EMB - 1) // 2
    w = (2.0 * math.pi / seq) * lag
    fr = jnp.linspace(1e-4, bands - 1, bands, dtype=F32)[:, None]
    pad = jnp.zeros((HYENA_EMB_PAD - HYENA_EMB, 2 * seq), F32)
    return jnp.concatenate([t, jnp.cos(fr * w), -jnp.sin(fr * w), pad], axis=0)


def kernel(x, c, rel_table, norm_g, w_mod, b_mod, w_in, w_out, ffn_w1, ffn_w3, ffn_w2, a_qk_g, a_sink,
           b_conv_w, b_conv_b, c_qk_g, d_conv_w, d_conv_b, d_f_w1, d_f_b1, d_f_w2, d_f_b2, d_f_w3,
           d_f_b3, d_f_w4, d_f_freq, d_skip):
    bsz, seq, d = x.shape
    depth = w_mod.shape[0]
    m_rows = bsz * seq
    mod =_mod_call(c, w_mod, b_mod).reshape(depth, bsz, 3, 3, 1, d)
    e_blk = jnp.kron(jnp.eye(MXU_DIM // HEAD_DIM, dtype=F32), jnp.ones((HEAD_DIM, HEAD_DIM), F32)).astype(BF16)

    w1b, w3b = _to_bf16_from_transposed(ffn_w1), _to_bf16_from_transposed(ffn_w3)
    w2b, w_in_b, w_out_b = (w.astype(BF16) for w in (ffn_w2, w_in, w_out))

    def ffn(xf, l, which, sub, mixer=None):
        return _ffn_call(xf, norm_g[l, sub][None], mod[l, :, sub, 0], mod[l, :, sub, 1], mod[l, :, sub, 2],
                         w1b, w3b, w2b, l, which, seq, mixer)

    xf = x.reshape(m_rows, d)
    for l in range(depth):
        j = l // 2
        xf = ffn(xf, l, 0, 0)
        even = l % 2 == 0
        qk_g = a_qk_g[j] if even else c_qk_g[j]
        hyena = None if even else _rope_tables(seq) + (d_conv_w[j], d_conv_b[j][None])
        q, k, v, *conv_in = _inproj_call(
            xf, norm_g[l, 1][None], mod[l, :, 1, 0], mod[l, :, 1, 1], w_in_b, l,
            jnp.tile(qk_g[0], N_Q_HEADS)[None], jnp.tile(qk_g[1], N_KV_HEADS)[None], e_blk, seq, hyena)
        gate = mod[l, :, 1, 2]
        if even:
            (u,) = conv_in
            bias, sink_rows = _bias_call(rel_table, a_sink[j])
            xf = _mixer_ab_call(xf, q, k, v, u, bias, sink_rows, b_conv_w[j], b_conv_b[j][None], w_out_b, l,
                                gate, seq)
            pending = None
        else:
            x0, z = conv_in
            logit_bound = (1.01 * LOG2_E * HEAD_DIM ** 0.5) * jnp.max(jnp.abs(qk_g[0])) * jnp.max(jnp.abs(qk_g[1]))
            att = _dense_attn_call(q, k, v, logit_bound, bsz, seq)
            e1_full, e1_half, einv, f2f, f2i = _dft_tables(seq)
            fw1 = jnp.pad(d_f_w1[j], ((0, HYENA_EMB_PAD - HYENA_EMB), (0, 0)))
            h3 = _filter_trunk_call(_filter_features_t(seq), fw1, d_f_b1[j], d_f_w2[j], d_f_b2[j],
                                    d_f_w3[j], d_f_b3[j], d_f_freq[j])
            deltas = jnp.abs(jnp.linspace(HYENA_MIN_DECAY, HYENA_MAX_DECAY, CONV_WIDTH, dtype=F32))[None]
            kr, ki = _filter_spec_call(h3, d_f_w4[j].astype(BF16), deltas, e1_full, f2f, seq)
            y = _long_conv_call(z, x0, kr, ki, d_skip[j][None], e1_half, f2f, f2i, einv, bsz, seq)
            pending = (att, y, w_out_b, gate)
        xf = ffn(xf, l, 1, 2, pending)
    return xf.reshape(bsz, seq, d)
```

```python
import functools
import math

import jax
import jax.numpy as jnp
import numpy as np
from jax import lax
from jax.experimental import pallas as pl
from jax.experimental.pallas import tpu as pltpu

D_MODEL = 1024
HEAD_DIM = 64
N_Q_HEADS = 8
N_KV_HEADS = 2
GQA_GROUP = N_Q_HEADS // N_KV_HEADS
ATTN_WIDTH = N_Q_HEADS * HEAD_DIM
KV_WIDTH = N_KV_HEADS * HEAD_DIM
QKV_COLS = ATTN_WIDTH + 2 * KV_WIDTH
CONV_WIDTH = D_MODEL - ATTN_WIDTH
IN_COLS = QKV_COLS + 3 * CONV_WIDTH
D_FF = 2752
BLOCK = 128
WINDOW = 128
N_BUCKETS = 32
MAX_DISTANCE = 128
GRID_W = 64
ROPE_THETA = 10000.0
HYENA_EMB = 33
HYENA_FILTER_WIDTH = 64
HYENA_EMB_PAD = 64
HYENA_MIN_DECAY = math.log(1e-2) / 0.3
HYENA_MAX_DECAY = math.log(1e-2) / 1.5
EPS = 1e-6
NEG_INF = -1e30
LOG2_E = 1.4426950408889634
MAX_FIXED_SHIFT_GAP = 80.0

LANES = 128
HALO = 16
MXU_DIM = 256
FF_CHUNK = MXU_DIM
VMEM_LIMIT = 56 * 1024 * 1024
DFT_N2 = 128

BF16 = jnp.bfloat16
F32 = jnp.float32


def _cparams(sem):
    return pltpu.CompilerParams(dimension_semantics=sem, vmem_limit_bytes=VMEM_LIMIT)


def _resident(shape):
    nd = len(shape)
    return pl.BlockSpec(shape, lambda *_: (0,) * nd, pipeline_mode=pl.Buffered(1))


def _layer_slice(shape, layer):
    return pl.BlockSpec((None,) + shape, lambda *_: (layer, 0, 0), pipeline_mode=pl.Buffered(1))


def _adaln(x, g, scale, shift):
    y = x * lax.rsqrt(jnp.mean(x * x, axis=-1, keepdims=True) + EPS) * g
    return y * (1.0 + scale) + shift


def _mod_kernel(c_ref, w_ref, b_ref, o_ref):
    w = w_ref[0]
    for b in range(c_ref.shape[0]):
        cc = c_ref[b]
        cond = cc * jax.nn.sigmoid(cc)
        o_ref[0, b:b + 1, :] = jnp.sum(w * cond, axis=0, keepdims=True) + b_ref[0]


def _mod_call(c, w_mod, b_mod):
    depth, d, n = w_mod.shape
    bsz = c.shape[0]
    tn = 1152 if n % 1152 == 0 else 512
    return pl.pallas_call(
        _mod_kernel,
        out_shape=jax.ShapeDtypeStruct((depth, bsz, n), F32),
        grid=(depth, n // tn),
        in_specs=[
            pl.BlockSpec((bsz, d, 1), lambda l, j: (0, 0, 0)),
            pl.BlockSpec((1, d, tn), lambda l, j: (l, 0, j)),
            pl.BlockSpec((1, 1, tn), lambda l, j: (l, 0, j)),
        ],
        out_specs=pl.BlockSpec((1, bsz, tn), lambda l, j: (l, 0, j)),
        compiler_params=_cparams(("arbitrary", "arbitrary")),
        name="adaln_modulation",
    )(c.reshape(bsz, d, 1), w_mod, b_mod.reshape(depth, 1, n))


def _cast_transpose_kernel(wt_ref, o_ref):
    cols = wt_ref.shape[0]
    starts = list(range(0, cols - MXU_DIM, MXU_DIM)) + [cols - MXU_DIM]
    for c0 in starts:
        o_ref[:, c0:c0 + MXU_DIM] = wt_ref[c0:c0 + MXU_DIM, :].T.astype(o_ref.dtype)


def _to_bf16_from_transposed(w):
    lead, (rows, cols) = w.shape[:-2], w.shape[-2:]
    none = (None,) * len(lead)
    return pl.pallas_call(
        _cast_transpose_kernel,
        out_shape=jax.ShapeDtypeStruct(w.shape, BF16),
        grid=lead,
        in_specs=[pl.BlockSpec(none + (cols, rows), lambda *idx: idx + (0, 0))],
        out_specs=pl.BlockSpec(none + (rows, cols), lambda *idx: idx + (0, 0)),
        compiler_params=_cparams(("arbitrary",) * len(lead)),
        name="weights_to_bf16_transposing",
    )(jnp.swapaxes(w, -1, -2))


def _ffn_kernel(mixer_update, x_ref, g_ref, shift_ref, scale_ref, gate_ref, w1_ref, w3_ref, w2_ref, *rest):
    x = x_ref[...]
    if mixer_update:
        att_ref, y_ref, wo_ref, mgate_ref, o_ref = rest
        upd = jnp.dot(att_ref[...], wo_ref[0:ATTN_WIDTH, :], preferred_element_type=F32)
        upd = upd + jnp.dot(y_ref[...].astype(BF16), wo_ref[ATTN_WIDTH:, :], preferred_element_type=F32)
        x = x + mgate_ref[0] * upd
    else:
        (o_ref,) = rest
    h = _adaln(x, g_ref[...], scale_ref[0], shift_ref[0]).astype(BF16)
    acc = jnp.zeros(x.shape, F32)
    for c0 in range(0, D_FF, FF_CHUNK):
        cols = slice(c0, min(c0 + FF_CHUNK, D_FF))
        a = jnp.dot(h, w1_ref[:, cols], preferred_element_type=F32)
        b = jnp.dot(h, w3_ref[:, cols], preferred_element_type=F32)
        act = (a * jax.nn.sigmoid(a) * b).astype(BF16)
        acc = acc + jnp.dot(act, w2_ref[cols, :], preferred_element_type=F32)
    o_ref[...] = x + (0.5 * gate_ref[0]) * acc


def _ffn_call(x, g, shift, scale, gate, w1, w3, w2, layer, which, rows_per_batch, mixer=None):
    m, d = x.shape
    tm = 512
    tpb = rows_per_batch // tm
    row = lambda w: pl.BlockSpec((tm, w), lambda i: (i, 0))
    vec = pl.BlockSpec((1, 1, d), lambda i: (i // tpb, 0, 0))
    weight = lambda r, c: pl.BlockSpec((None, None, r, c), lambda i: (layer, which, 0, 0),
                                       pipeline_mode=pl.Buffered(1))
    in_specs = [row(d), _resident((1, d)), vec, vec, vec, weight(d, D_FF), weight(d, D_FF), weight(D_FF, d)]
    args = [x, g, shift, scale, gate, w1, w3, w2]
    if mixer is not None:
        in_specs += [row(ATTN_WIDTH), row(CONV_WIDTH), _layer_slice((d, d), layer), vec]
        args += list(mixer)
    return pl.pallas_call(
        functools.partial(_ffn_kernel, mixer is not None),
        out_shape=jax.ShapeDtypeStruct((m, d), F32),
        grid=(m // tm,),
        in_specs=in_specs,
        out_specs=row(d),
        compiler_params=_cparams(("arbitrary",)),
        name="adaln_swiglu_ffn",
    )(*args)


def _group_norm_scale(v, e_ref):
    sq = v * v
    hi = sq.astype(BF16)
    lo = (sq - hi.astype(F32)).astype(BF16)
    w = min(v.shape[1], MXU_DIM)
    e = e_ref[...]
    ss = jnp.concatenate(
        [jnp.dot(hi[:, c:c + w], e[0:w, 0:w], preferred_element_type=F32)
         + jnp.dot(lo[:, c:c + w], e[0:w, 0:w], preferred_element_type=F32) for c in range(0, v.shape[1], w)],
        axis=1)
    return lax.rsqrt(ss * (1.0 / HEAD_DIM) + EPS)


def _rope128(v, cos, sin_signed, even_lane):
    partner = jnp.where(even_lane, pltpu.roll(v, LANES - 1, 1), pltpu.roll(v, 1, 1))
    return v * cos + partner * sin_signed


def _short_conv_rows(u, before, after, cw, cb):
    rows = u.shape[0]
    r = lax.broadcasted_iota(jnp.int32, u.shape, 0)
    um1 = jnp.where(r == 0, before, pltpu.roll(u, 1, 0))
    up1 = jnp.where(r == rows - 1, after, pltpu.roll(u, rows - 1, 0))
    return cw[0:1] * um1 + cw[1:2] * u + cw[2:3] * up1 + cb


def _inproj_kernel(rope, tpb, x_ref, g_ref, shift_ref, scale_ref, w_ref, qg_ref, kg_ref, e_ref, *rest):
    if rope:
        cos_ref, sin_ref, xp_ref, xn_ref, cw_ref, cb_ref, q_ref, k_ref, v_ref, x0_ref, z_ref = rest
    else:
        q_ref, k_ref, v_ref, u_ref = rest
    h = _adaln(x_ref[...], g_ref[...], scale_ref[0], shift_ref[0]).astype(BF16)
    q = jnp.dot(h, w_ref[:, 0:ATTN_WIDTH], preferred_element_type=F32)
    k = jnp.dot(h, w_ref[:, ATTN_WIDTH:ATTN_WIDTH + KV_WIDTH], preferred_element_type=F32)
    q = q * _group_norm_scale(q, e_ref) * qg_ref[...]
    k = k * _group_norm_scale(k, e_ref) * kg_ref[...]
    if rope:
        cos = cos_ref[...]
        sin = sin_ref[...]
        even = (lax.broadcasted_iota(jnp.int32, cos.shape, 1) % 2) == 0
        q = jnp.concatenate(
            [_rope128(q[:, j * LANES:(j + 1) * LANES], cos, sin, even) for j in range(ATTN_WIDTH // LANES)],
            axis=1)
        k = _rope128(k, cos, sin, even)
    q_ref[...] = (q * (HEAD_DIM ** -0.5 * (LOG2_E if rope else 1.0))).astype(BF16)
    v = jnp.dot(h, w_ref[:, ATTN_WIDTH + KV_WIDTH:QKV_COLS], preferred_element_type=F32)
    k_ref[0] = k.T.astype(BF16)
    low = lax.broadcasted_iota(jnp.int32, v.shape, 1) < HEAD_DIM
    v_ref[...] = jnp.concatenate(
        [jnp.where(low, v, 1.0), jnp.where(low, pltpu.roll(v, HEAD_DIM, 1), 1.0)], axis=1).astype(BF16)
    if not rope:
        u_ref[...] = jnp.dot(h, w_ref[:, QKV_COLS:IN_COLS], preferred_element_type=F32)
        return
    i = pl.program_id(0)
    h_prev = _adaln(xp_ref[...], g_ref[...], scale_ref[0], shift_ref[0]).astype(BF16)
    h_next = _adaln(xn_ref[...], g_ref[...], scale_ref[0], shift_ref[0]).astype(BF16)
    u_ext = jnp.dot(jnp.concatenate([h_prev, h, h_next], axis=0), w_ref[:, QKV_COLS:IN_COLS],
                    preferred_element_type=F32)
    tm = h.shape[0]
    before = jnp.where((i % tpb) == 0, 0.0, u_ext[HALO - 1:HALO])
    after = jnp.where((i % tpb) == tpb - 1, 0.0, u_ext[HALO + tm:HALO + tm + 1])
    t = _short_conv_rows(u_ext[HALO:HALO + tm], before, after, cw_ref[...], cb_ref[...])
    x0_ref[...] = t[:, 0:CONV_WIDTH]
    z_ref[...] = t[:, CONV_WIDTH:2 * CONV_WIDTH] * t[:, 2 * CONV_WIDTH:]


def _inproj_call(x, g, shift, scale, w_in, layer, qg, kg, e, rows_per_batch, hyena=None):
    m, d = x.shape
    tm = 512
    tpb = rows_per_batch // tm
    vec = pl.BlockSpec((1, 1, d), lambda i: (i // tpb, 0, 0))
    row = lambda w: pl.BlockSpec((tm, w), lambda i: (i, 0))
    in_specs = [
        row(d),
        _resident((1, d)), vec, vec,
        _layer_slice((d, IN_COLS), layer),
        _resident((1, ATTN_WIDTH)), _resident((1, KV_WIDTH)),
        _resident((MXU_DIM, MXU_DIM)),
    ]
    args = [x, g, shift, scale, w_in, qg, kg, e]
    uw = 3 * CONV_WIDTH
    if hyena is not None:
        cos, sin, conv_w, conv_b = hyena
        tab = pl.BlockSpec((tm, LANES), lambda i: (i % tpb, 0))
        hpt = tm // HALO
        last_halo = m // HALO - 1
        in_specs += [
            tab, tab,
            pl.BlockSpec((HALO, d), lambda i: (jnp.maximum(i * hpt - 1, 0), 0)),
            pl.BlockSpec((HALO, d), lambda i: (jnp.minimum((i + 1) * hpt, last_halo), 0)),
            _resident((3, uw)), _resident((1, uw)),
        ]
        args += [cos, sin, x, x, conv_w, conv_b]
        conv_shapes = [jax.ShapeDtypeStruct((m, CONV_WIDTH), F32)] * 2
        conv_specs = [row(CONV_WIDTH)] * 2
    else:
        conv_shapes = [jax.ShapeDtypeStruct((m, uw), F32)]
        conv_specs = [row(uw)]
    tk = _key_chunk(rows_per_batch)
    per = tk // tm
    return pl.pallas_call(
        functools.partial(_inproj_kernel, hyena is not None, tpb),
        out_shape=[
            jax.ShapeDtypeStruct((m, ATTN_WIDTH), BF16),
            jax.ShapeDtypeStruct((m // tk, KV_WIDTH, tk), BF16),
            jax.ShapeDtypeStruct((m, 2 * KV_WIDTH), BF16),
        ] + conv_shapes,
        grid=(m // tm,),
        in_specs=in_specs,
        out_specs=[row(ATTN_WIDTH), pl.BlockSpec((1, KV_WIDTH, tm), lambda i: (i // per, 0, i % per)),
                   row(2 * KV_WIDTH)] + conv_specs,
        compiler_params=_cparams(("arbitrary",)),
        name="adaln_in_projection",
    )(*args)


_T5_STEPS = (12, 16, 23, 32, 46, 64, 91)


def _bias_kernel(tab_ref, sink_ref, shift_ref, o_ref, sink_rows_ref, shift_rows_ref, sink_term_rows_ref):
    for h in range(N_Q_HEADS):
        rows = slice(h * BLOCK, (h + 1) * BLOCK)
        sink_rows_ref[rows, :] = jnp.full((BLOCK, LANES), sink_ref[h], F32)
        shift_rows_ref[rows, :] = jnp.full((BLOCK, LANES), shift_ref[h], F32)
        sink_term_rows_ref[rows, :] = jnp.exp(jnp.full((BLOCK, LANES), sink_ref[h] - shift_ref[h], F32))
    qi = lax.broadcasted_iota(jnp.int32, (BLOCK, 3 * BLOCK), 0)
    kj = lax.broadcasted_iota(jnp.int32, (BLOCK, 3 * BLOCK), 1)
    rel = kj - BLOCK - qi
    n = jnp.abs(rel)
    half = N_BUCKETS // 2
    max_exact = half // 2
    large = jnp.full(n.shape, max_exact, jnp.int32)
    for t in _T5_STEPS:
        large = large + (n >= t).astype(jnp.int32)
    bucket = jnp.where(rel > 0, half, 0) + jnp.where(n < max_exact, n, large)
    for h in range(N_Q_HEADS):
        bias = jnp.zeros(n.shape, F32)
        for b in range(N_BUCKETS):
            bias = jnp.where(bucket == b, tab_ref[b, h], bias)
        o_ref[h] = jnp.where(n <= WINDOW, bias, NEG_INF)


def _bias_call(rel_table, sink, shift):
    rows = jax.ShapeDtypeStruct((N_Q_HEADS * BLOCK, LANES), F32)
    smem = pl.BlockSpec(memory_space=pltpu.SMEM)
    return pl.pallas_call(
        _bias_kernel,
        out_shape=(jax.ShapeDtypeStruct((N_Q_HEADS, BLOCK, 3 * BLOCK), F32), rows, rows, rows),
        in_specs=[smem, smem, smem],
        name="t5_bias_tile",
    )(rel_table, sink, shift)


def _mixer_ab_kernel(tq, tpb, fixed_shift, x_ref, q_ref, kc_ref, kp_ref, kn_ref, vc_ref, vp_ref, vn_ref,
                     uc_ref, up_ref, un_ref, bias_ref, sink_ref, shift_ref, sink_term_ref, cw_ref, cb_ref,
                     wo_ref, gate_ref, o_ref, kbuf, vbuf, qs, att):
    i = pl.program_id(0)
    first = (i % tpb) == 0
    last = (i % tpb) == tpb - 1
    nblk = tq // BLOCK
    grows = GQA_GROUP * BLOCK
    kbuf[:, 0:BLOCK] = kp_ref[0]
    kbuf[:, BLOCK:BLOCK + tq] = kc_ref[0]
    kbuf[:, BLOCK + tq:] = kn_ref[0]
    vbuf[0:BLOCK] = vp_ref[...]
    vbuf[BLOCK:BLOCK + tq] = vc_ref[...]
    vbuf[BLOCK + tq:] = vn_ref[...]
    for n in range(nblk):
        for h in range(N_Q_HEADS):
            g, j = divmod(h, GQA_GROUP)
            qs[g, n, j * BLOCK:(j + 1) * BLOCK, :] = q_ref[n * BLOCK:(n + 1) * BLOCK, h * HEAD_DIM:(h + 1) * HEAD_DIM]
    col = lax.broadcasted_iota(jnp.int32, (grows, 3 * BLOCK), 1)
    for n in range(nblk):
        keys = slice(n * BLOCK, (n + 3) * BLOCK)
        for g in range(N_KV_HEADS):
            s = jnp.dot(qs[g, n], kbuf[g * HEAD_DIM:(g + 1) * HEAD_DIM, keys], preferred_element_type=F32)
            s = s + bias_ref[g * GQA_GROUP:(g + 1) * GQA_GROUP].reshape(grows, 3 * BLOCK)
            if n == 0:
                s = jnp.where(jnp.logical_and(first, col < BLOCK), NEG_INF, s)
            if n == nblk - 1:
                s = jnp.where(jnp.logical_and(last, col >= 2 * BLOCK), NEG_INF, s)
            grp = slice(g * grows, (g + 1) * grows)
            if fixed_shift:
                mx = shift_ref[grp, :]
                sink_term = sink_term_ref[grp, :]
            else:
                sk = sink_ref[grp, :]
                mx = jnp.maximum(jnp.broadcast_to(jnp.max(s, axis=-1, keepdims=True), sk.shape), sk)
                sink_term = jnp.exp(sk - mx)
            p = jnp.exp(s - jnp.tile(mx, (1, 3)))
            pv = jnp.dot(p.astype(BF16), vbuf[keys, g * LANES:(g + 1) * LANES], preferred_element_type=F32)
            o = pv / (pltpu.roll(pv, HEAD_DIM, 1) + sink_term)
            for j in range(GQA_GROUP):
                h = g * GQA_GROUP + j
                att[n * BLOCK:(n + 1) * BLOCK, h * HEAD_DIM:(h + 1) * HEAD_DIM] = o[j * BLOCK:(j + 1) * BLOCK, 0:HEAD_DIM]

    gb = uc_ref[:, 0:CONV_WIDTH]
    p = uc_ref[:, CONV_WIDTH:2 * CONV_WIDTH] * uc_ref[:, 2 * CONV_WIDTH:]
    p_before = jnp.where(first, 0.0, up_ref[7:8, CONV_WIDTH:2 * CONV_WIDTH] * up_ref[7:8, 2 * CONV_WIDTH:])
    p_after = jnp.where(last, 0.0, un_ref[0:1, CONV_WIDTH:2 * CONV_WIDTH] * un_ref[0:1, 2 * CONV_WIDTH:])
    conv = gb * _short_conv_rows(p, p_before, p_after, cw_ref[...], cb_ref[...])

    y = jnp.dot(att[...].astype(BF16), wo_ref[0:ATTN_WIDTH, :], preferred_element_type=F32)
    y = y + jnp.dot(conv.astype(BF16), wo_ref[ATTN_WIDTH:, :], preferred_element_type=F32)
    o_ref[...] = x_ref[...] + gate_ref[0] * y


def _mixer_ab_call(x, q, kt, v1, u, bias, row_tables, fixed_shift, conv_w, conv_b, w_out, layer, gate,
                   rows_per_batch):
    m, d = x.shape
    tq = 512
    tpb = rows_per_batch // tq
    r = tq // BLOCK
    nb = m // BLOCK
    n8 = m // 8
    cur = lambda w: pl.BlockSpec((tq, w), lambda i: (i, 0))
    prev_of = lambda i: jnp.maximum(i * r - 1, 0)
    next_of = lambda i: jnp.minimum((i + 1) * r, nb - 1)
    v_prev = pl.BlockSpec((BLOCK, 2 * KV_WIDTH), lambda i: (prev_of(i), 0))
    v_next = pl.BlockSpec((BLOCK, 2 * KV_WIDTH), lambda i: (next_of(i), 0))
    tk = kt.shape[2]
    k_cur = pl.BlockSpec((1, KV_WIDTH, tq), lambda i: (i // (tk // tq), 0, i % (tk // tq)))
    kpb = tk // BLOCK
    k_prev = pl.BlockSpec((1, KV_WIDTH, BLOCK), lambda i: (prev_of(i) // kpb, 0, prev_of(i) % kpb))
    k_next = pl.BlockSpec((1, KV_WIDTH, BLOCK), lambda i: (next_of(i) // kpb, 0, next_of(i) % kpb))
    uw = 3 * CONV_WIDTH
    row_table = _resident((N_Q_HEADS * BLOCK, LANES))
    return pl.pallas_call(
        functools.partial(_mixer_ab_kernel, tq, tpb, fixed_shift),
        out_shape=jax.ShapeDtypeStruct((m, d), F32),
        grid=(m // tq,),
        in_specs=[
            cur(d), cur(ATTN_WIDTH),
            k_cur, k_prev, k_next,
            cur(2 * KV_WIDTH), v_prev, v_next,
            cur(uw),
            pl.BlockSpec((8, uw), lambda i: (jnp.maximum(i * (tq // 8) - 1, 0), 0)),
            pl.BlockSpec((8, uw), lambda i: (jnp.minimum((i + 1) * (tq // 8), n8 - 1), 0)),
            _resident((N_Q_HEADS, BLOCK, 3 * BLOCK)),
            row_table, row_table, row_table,
            _resident((3, CONV_WIDTH)), _resident((1, CONV_WIDTH)),
            _layer_slice((d, d), layer),
            pl.BlockSpec((1, 1, d), lambda i: (i // tpb, 0, 0)),
        ],
        out_specs=cur(d),
        scratch_shapes=[
            pltpu.VMEM((KV_WIDTH, tq + 2 * BLOCK), BF16),
            pltpu.VMEM((tq + 2 * BLOCK, 2 * KV_WIDTH), BF16),
            pltpu.VMEM((N_KV_HEADS, r, GQA_GROUP * BLOCK, HEAD_DIM), BF16),
            pltpu.VMEM((tq, ATTN_WIDTH), F32),
        ],
        compiler_params=_cparams(("arbitrary",)),
        name="windowed_attn_shortconv_outproj",
    )(x, q, kt, kt, kt, v1, v1, v1, u, u, u, bias, *row_tables, conv_w, conv_b, w_out, gate)


def _key_chunk(seq):
    return min(seq, 1024)


def _dense_attn_kernel(tq, tk, q_ref, k_ref, v_ref, o_ref, qs, s_buf, p_buf, rmax_buf, alpha_buf,
                       m_ref, acc_ref):
    nc = k_ref.shape[0]
    for j in range(GQA_GROUP):
        qs[j * tq:(j + 1) * tq, :] = q_ref[:, j * HEAD_DIM:(j + 1) * HEAD_DIM]
    m_ref[...] = jnp.full(m_ref.shape, -jnp.inf, F32)
    acc_ref[...] = jnp.zeros(acc_ref.shape, F32)

    def scores(c):
        s = jnp.dot(qs[...], k_ref[c], preferred_element_type=F32)
        s_buf[...] = s
        rmax_buf[...] = jnp.broadcast_to(jnp.max(s, axis=-1, keepdims=True), rmax_buf.shape)

    def softmax():
        m_old = m_ref[...]
        m_new = jnp.maximum(m_old, rmax_buf[...])
        alpha_buf[...] = jnp.exp2(m_old - m_new)
        m_ref[...] = m_new
        p = jnp.exp2(s_buf[...] - jnp.tile(m_new, (1, tk // LANES)))
        p_buf[...] = p.astype(BF16)

    def weighted_values(c):
        start = c * tk if isinstance(c, int) else pl.multiple_of(c * tk, tk)
        pv = jnp.dot(p_buf[...], v_ref[pl.ds(start, tk), :], preferred_element_type=F32)
        acc_ref[...] = alpha_buf[...] * acc_ref[...] + pv

    def step(t):
        static = isinstance(t, int)
        weighted_values(t)
        if not static or t + 1 < nc:
            softmax()
        if not static or t + 2 < nc:
            scores(t + 2)

    scores(0)
    softmax()
    if nc > 1:
        scores(1)

    def steady(t, carry):
        step(t)
        return carry

    lax.fori_loop(0, max(nc - 2, 0), steady, 0)
    for t in range(max(nc - 2, 0), nc):
        step(t)

    acc = acc_ref[...]
    o = acc / pltpu.roll(acc, HEAD_DIM, 1)
    for j in range(GQA_GROUP):
        o_ref[:, j * HEAD_DIM:(j + 1) * HEAD_DIM] = o[j * tq:(j + 1) * tq, 0:HEAD_DIM].astype(o_ref.dtype)


def _dense_attn_bounded_kernel(tq, tk, bound_ref, q_ref, k_ref, v_ref, o_ref, qs, p_buf, acc_ref):
    nc = k_ref.shape[0]
    for j in range(GQA_GROUP):
        qs[j * tq:(j + 1) * tq, :] = q_ref[:, j * HEAD_DIM:(j + 1) * HEAD_DIM]
    shift = bound_ref[0]

    def probabilities(c):
        s = jnp.dot(qs[...], k_ref[c], preferred_element_type=F32)
        p_buf[...] = jnp.exp2(s - shift).astype(BF16)

    def weighted_values(c):
        start = c * tk if isinstance(c, int) else pl.multiple_of(c * tk, tk)
        return jnp.dot(p_buf[...], v_ref[pl.ds(start, tk), :], preferred_element_type=F32)

    probabilities(0)
    acc_ref[...] = weighted_values(0)
    if nc > 1:
        probabilities(1)

    def steady(t, carry):
        acc_ref[...] += weighted_values(t)
        probabilities(t + 1)
        return carry

    lax.fori_loop(1, nc - 1, steady, 0)
    if nc > 1:
        acc_ref[...] += weighted_values(nc - 1)

    acc = acc_ref[...]
    o = acc / pltpu.roll(acc, HEAD_DIM, 1)
    for j in range(GQA_GROUP):
        o_ref[:, j * HEAD_DIM:(j + 1) * HEAD_DIM] = o[j * tq:(j + 1) * tq, 0:HEAD_DIM].astype(o_ref.dtype)


MAX_FIXED_SHIFT = 50.0


def _dense_attn_call(q, kt, v1, logit_bound, bsz, seq):
    tq, tk = min(seq, 1024), _key_chunk(seq)
    nq = seq // tq
    nc = seq // tk
    rows = GQA_GROUP * tq
    gw = GQA_GROUP * HEAD_DIM
    stat = pltpu.VMEM((rows, LANES), F32)
    q_spec = pl.BlockSpec((tq, gw), lambda b, g, i: (b * nq + i, g))
    k_spec = pl.BlockSpec((nc, HEAD_DIM, tk), lambda b, g, i: (b, g, 0), pipeline_mode=pl.Buffered(1))
    v_spec = pl.BlockSpec((seq, LANES), lambda b, g, i: (b, g), pipeline_mode=pl.Buffered(1))
    common = dict(
        out_shape=jax.ShapeDtypeStruct((bsz * seq, ATTN_WIDTH), BF16),
        grid=(bsz, N_KV_HEADS, nq),
        out_specs=q_spec,
        compiler_params=_cparams(("arbitrary", "arbitrary", "arbitrary")),
    )

    def running_max():
        return pl.pallas_call(
            functools.partial(_dense_attn_kernel, tq, tk),
            in_specs=[q_spec, k_spec, v_spec],
            scratch_shapes=[
                pltpu.VMEM((rows, HEAD_DIM), BF16),
                pltpu.VMEM((rows, tk), F32), pltpu.VMEM((rows, tk), BF16),
                stat, stat, stat, stat,
            ],
            name="dense_gqa_attention", **common,
        )(q, kt, v1)

    def fixed_shift():
        return pl.pallas_call(
            functools.partial(_dense_attn_bounded_kernel, tq, tk),
            in_specs=[pl.BlockSpec(memory_space=pltpu.SMEM), q_spec, k_spec, v_spec],
            scratch_shapes=[pltpu.VMEM((rows, HEAD_DIM), BF16), pltpu.VMEM((rows, tk), BF16), stat],
            name="dense_gqa_attention_fixed_shift", **common,
        )(logit_bound.reshape(1), q, kt, v1)

    return lax.cond(logit_bound <= MAX_FIXED_SHIFT, fixed_shift, running_max)


def _hp_dot(a, b):
    return jnp.dot(a, b, preferred_element_type=F32, precision=lax.Precision.HIGHEST)


def _filter_trunk_kernel(z_ref, w1_ref, b1_ref, w2_ref, b2_ref, w3_ref, b3_ref, fq_ref, o_ref):
    fq = fq_ref[...]
    h = jnp.sin(fq * (_hp_dot(w1_ref[...], z_ref[...]) + b1_ref[...]))
    h = jnp.sin(fq * (_hp_dot(w2_ref[...], h) + b2_ref[...]))
    h = jnp.sin(fq * (_hp_dot(w3_ref[...], h) + b3_ref[...]))
    o_ref[...] = h.T


def _filter_trunk_call(zfeat_t, w1, b1, w2, b2, w3, b3, freq):
    rows = zfeat_t.shape[1]
    tr = min(rows, 2048)
    fw = HYENA_FILTER_WIDTH
    col = lambda v: v.reshape(fw, 1)
    return pl.pallas_call(
        _filter_trunk_kernel,
        out_shape=jax.ShapeDtypeStruct((rows, fw), F32),
        grid=(rows // tr,),
        in_specs=[
            pl.BlockSpec((HYENA_EMB_PAD, tr), lambda i: (0, i)),
            _resident((fw, HYENA_EMB_PAD)), _resident((fw, 1)),
            _resident((fw, fw)), _resident((fw, 1)),
            _resident((fw, fw)), _resident((fw, 1)),
            _resident((fw, 1)),
        ],
        out_specs=pl.BlockSpec((tr, fw), lambda i: (i, 0)),
        compiler_params=_cparams(("arbitrary",)),
        name="hyena_filter_trunk",
    )(zfeat_t, w1.T, col(b1), w2.T, col(b2), w3.T, col(b3), col(freq))


def _dft_stage1(src_ref, tab_ref, re_ref, im_ref, n_rows, k1p):
    def body(n2, carry):
        xs = src_ref[pl.ds(n2, n_rows, stride=DFT_N2), :].astype(BF16)
        res = jnp.dot(tab_ref[n2], xs, preferred_element_type=F32)
        re_ref[pl.ds(n2, k1p, stride=DFT_N2), :] = res[0:k1p]
        im_ref[pl.ds(n2, k1p, stride=DFT_N2), :] = res[k1p:]
        return carry

    lax.fori_loop(0, DFT_N2, body, 0, unroll=8)


def _filter_spec_kernel(seq, k1p, h_ref, w4f_ref, w4b_ref, dl_ref, e1_ref, f2_ref,
                        kr_ref, ki_ref, kfull):
    n = 2 * seq
    chunk = min(seq, 1024)
    dl = dl_ref[...]

    def fill(c, ss):
        r0 = pl.multiple_of(c * chunk, chunk)
        hc = h_ref[pl.ds(r0, chunk), :].astype(BF16)
        fwd = jnp.dot(hc, w4f_ref[...], preferred_element_type=F32)
        bwd = jnp.dot(hc, w4b_ref[...], preferred_element_type=F32)
        rows = r0 + lax.broadcasted_iota(jnp.int32, (chunk, 1), 0)
        lag = jnp.where(rows < seq, rows, n - rows).astype(F32)
        t = lag * (1.0 / (seq - 1))
        val = jnp.where(rows < seq, fwd, bwd) * jnp.exp(-t * dl)
        val = jnp.where(rows == seq, 0.0, val)
        kfull[pl.ds(r0, chunk), :] = val
        return ss + jnp.sum(val * val, axis=0, keepdims=True)

    ss = lax.fori_loop(0, n // chunk, fill, jnp.zeros((1, dl.shape[1]), F32))
    norm = lax.rsqrt(ss + EPS)

    _dft_stage1(kfull, e1_ref, kr_ref, ki_ref, n // DFT_N2, k1p)

    f2 = f2_ref[...]

    def stage2(a, carry):
        rows = pl.ds(pl.multiple_of(a * DFT_N2, DFT_N2), DFT_N2)
        slab = jnp.concatenate([kr_ref[rows, :], ki_ref[rows, :]], axis=0).astype(BF16)
        xf = jnp.dot(f2, slab, preferred_element_type=F32)
        kr_ref[rows, :] = xf[0:DFT_N2] * norm
        ki_ref[rows, :] = xf[DFT_N2:] * norm
        return carry

    lax.fori_loop(0, seq // DFT_N2 + 1, stage2, 0, unroll=8)


def _filter_spec_call(h3, w4, deltas, e1f, f2f, seq):
    n = 2 * seq
    k1p = e1f.shape[1] // 2
    ct = LANES
    nct = CONV_WIDTH // ct
    fw = HYENA_FILTER_WIDTH
    spec_rows = k1p * DFT_N2
    out = pl.BlockSpec((spec_rows, ct), lambda j: (0, j))
    return pl.pallas_call(
        functools.partial(_filter_spec_kernel, seq, k1p),
        out_shape=(jax.ShapeDtypeStruct((spec_rows, CONV_WIDTH), F32),
                   jax.ShapeDtypeStruct((spec_rows, CONV_WIDTH), F32)),
        grid=(nct,),
        in_specs=[
            _resident((n, fw)),
            pl.BlockSpec((fw, ct), lambda j: (0, j)),
            pl.BlockSpec((fw, ct), lambda j: (0, nct + j)),
            pl.BlockSpec((1, ct), lambda j: (0, j)),
            _resident(e1f.shape),
            _resident((2 * DFT_N2, 2 * DFT_N2)),
        ],
        out_specs=(out, out),
        scratch_shapes=[pltpu.VMEM((n, ct), F32)],
        compiler_params=_cparams(("arbitrary",)),
        name="hyena_filter_spectrum",
    )(h3, w4, w4, deltas, e1f, f2f)


def _long_conv_kernel(seq, k1p, z_ref, x0_ref, kr_ref, ki_ref, skip_ref, e1_ref, f2f_ref, f2i_ref,
                      einv_ref, o_ref, br, bi):
    n1h = seq // DFT_N2
    _dft_stage1(z_ref, e1_ref, br, bi, n1h, k1p)

    f2f = f2f_ref[...]
    f2i = f2i_ref[...]

    def spectral(a, carry):
        r0 = pl.multiple_of(a * DFT_N2, DFT_N2)
        rows = pl.ds(r0, DFT_N2)
        slab = jnp.concatenate([br[rows, :], bi[rows, :]], axis=0).astype(BF16)
        xf = jnp.dot(f2f, slab, preferred_element_type=F32)
        xr, xi = xf[0:DFT_N2], xf[DFT_N2:]
        kr, ki = kr_ref[rows, :], ki_ref[rows, :]
        y = jnp.concatenate([xr * kr - xi * ki, xr * ki + xi * kr], axis=0).astype(BF16)
        d = jnp.dot(f2i, y, preferred_element_type=F32)
        br[rows, :] = d[0:DFT_N2]
        bi[rows, :] = d[DFT_N2:]
        return carry

    lax.fori_loop(0, n1h + 1, spectral, 0, unroll=8)

    skip = skip_ref[...]

    def synth(n2, carry):
        ds_spec = pl.ds(n2, k1p, stride=DFT_N2)
        d = jnp.concatenate([br[ds_spec, :], bi[ds_spec, :]], axis=0).astype(BF16)
        o_ref[pl.ds(n2, n1h, stride=DFT_N2), :] = jnp.dot(einv_ref[n2], d, preferred_element_type=F32)
        return carry

    lax.fori_loop(0, DFT_N2, synth, 0, unroll=8)

    chunk = min(seq, 512)

    def gate(c, carry):
        rows = pl.ds(pl.multiple_of(c * chunk, chunk), chunk)
        o_ref[rows, :] = x0_ref[rows, :] * (o_ref[rows, :] + skip * z_ref[rows, :])
        return carry

    lax.fori_loop(0, seq // chunk, gate, 0)


def _long_conv_call(z, x0, kr, ki, skip, e1, f2f, f2i, einv, bsz, seq):
    ct = LANES
    nct = CONV_WIDTH // ct
    k1p = e1.shape[1] // 2
    spec_rows = k1p * DFT_N2
    big = lambda: pl.BlockSpec((seq, ct), lambda j, b: (b, j), pipeline_mode=pl.Buffered(1))
    spec = lambda: pl.BlockSpec((spec_rows, ct), lambda j, b: (0, j), pipeline_mode=pl.Buffered(1))
    return pl.pallas_call(
        functools.partial(_long_conv_kernel, seq, k1p),
        out_shape=jax.ShapeDtypeStruct((bsz * seq, CONV_WIDTH), F32),
        grid=(nct, bsz),
        in_specs=[
            big(), big(), spec(), spec(),
            pl.BlockSpec((1, ct), lambda j, b: (0, j)),
            _resident(e1.shape),
            _resident((2 * DFT_N2, 2 * DFT_N2)), _resident((2 * DFT_N2, 2 * DFT_N2)),
            _resident(einv.shape),
        ],
        out_specs=pl.BlockSpec((seq, ct), lambda j, b: (b, j)),
        scratch_shapes=[pltpu.VMEM((spec_rows, ct), F32), pltpu.VMEM((spec_rows, ct), F32)],
        compiler_params=_cparams(("arbitrary", "arbitrary")),
        name="hyena_long_conv",
    )(z, x0, kr, ki, skip, e1, f2f, f2i, einv)


def _rope_tables(seq):
    t = jnp.arange(seq)
    row = (t // GRID_W).astype(F32)
    col = (t % GRID_W).astype(F32)
    half = HEAD_DIM // 2
    inv = ROPE_THETA ** (-jnp.arange(0, half, 2, dtype=F32) / half)
    ang = jnp.concatenate([row[:, None] * inv, col[:, None] * inv], axis=-1)
    ang = jnp.repeat(ang, 2, axis=-1)
    sign = jnp.where(jnp.arange(HEAD_DIM) % 2 == 0, -1.0, 1.0).astype(F32)
    reps = LANES // HEAD_DIM
    return jnp.tile(jnp.cos(ang), (1, reps)), jnp.tile(jnp.sin(ang) * sign, (1, reps))


def _dft_tables(seq):
    n = 2 * seq
    n1 = n // DFT_N2
    k1 = n1 // 2 + 1
    k1p = -(-k1 // 8) * 8
    a = jnp.arange(k1p)
    live = (a < k1).astype(F32)
    ang1 = ((a[:, None] * jnp.arange(n1)[None, :]) % n1).astype(F32) * (2.0 * math.pi / n1)
    ang2 = (jnp.arange(DFT_N2)[:, None] * a[None, :]).astype(F32) * (2.0 * math.pi / n)
    c1, s1 = jnp.cos(ang1) * live[:, None], jnp.sin(ang1) * live[:, None]
    c2, s2 = jnp.cos(ang2), jnp.sin(ang2)
    cos_a = c1[None] * c2[:, :, None] - s1[None] * s2[:, :, None]
    sin_a = s1[None] * c2[:, :, None] + c1[None] * s2[:, :, None]
    e1_full = jnp.concatenate([cos_a, -sin_a], axis=1).astype(BF16)
    e1_half = e1_full[:, :, :n1 // 2]
    wgt = jnp.where((a == 0) | (a == n1 // 2), 1.0, 2.0) / n
    c1t, s1t = (c1 * wgt[:, None]).T[:n1 // 2], (s1 * wgt[:, None]).T[:n1 // 2]
    cos_s = c1t[None] * c2[:, None, :] - s1t[None] * s2[:, None, :]
    sin_s = s1t[None] * c2[:, None, :] + c1t[None] * s2[:, None, :]
    einv = jnp.concatenate([cos_s, -sin_s], axis=2).astype(BF16)
    kk = jnp.arange(DFT_N2)
    phi = ((kk[:, None] * kk[None, :]) % DFT_N2).astype(F32) * (2.0 * math.pi / DFT_N2)
    ci, si = jnp.cos(phi), jnp.sin(phi)
    f2f = jnp.block([[ci, si], [-si, ci]]).astype(BF16)
    f2i = jnp.block([[ci, -si], [si, ci]]).astype(BF16)
    return e1_full, e1_half, einv, f2f, f2i


def _filter_features_t(seq):
    j = jnp.arange(2 * seq)
    lag = jnp.where(j < seq, j, jnp.where(j == seq, 0, 2 * seq - j)).astype(F32)[None, :]
    t = lag / (seq - 1)
    bands = (HYENA_EMB - 1) // 2
    w = (2.0 * math.pi / seq) * lag
    fr = jnp.linspace(1e-4, bands - 1, bands, dtype=F32)[:, None]
    pad = jnp.zeros((HYENA_EMB_PAD - HYENA_EMB, 2 * seq), F32)
    return jnp.concatenate([t, jnp.cos(fr * w), -jnp.sin(fr * w), pad], axis=0)


def kernel(x, c, rel_table, norm_g, w_mod, b_mod, w_in, w_out, ffn_w1, ffn_w3, ffn_w2, a_qk_g, a_sink,
           b_conv_w, b_conv_b, c_qk_g, d_conv_w, d_conv_b, d_f_w1, d_f_b1, d_f_w2, d_f_b2, d_f_w3,
           d_f_b3, d_f_w4, d_f_freq, d_skip):
    bsz, seq, d = x.shape
    depth = w_mod.shape[0]
    m_rows = bsz * seq
    mod =_mod_call(c, w_mod, b_mod).reshape(depth, bsz, 3, 3, 1, d)
    e_blk = jnp.kron(jnp.eye(MXU_DIM // HEAD_DIM, dtype=F32), jnp.ones((HEAD_DIM, HEAD_DIM), F32)).astype(BF16)

    w1b, w3b = _to_bf16_from_transposed(ffn_w1), _to_bf16_from_transposed(ffn_w3)
    w2b, w_in_b, w_out_b = (w.astype(BF16) for w in (ffn_w2, w_in, w_out))

    def ffn(xf, l, which, sub, mixer=None):
        return _ffn_call(xf, norm_g[l, sub][None], mod[l, :, sub, 0], mod[l, :, sub, 1], mod[l, :, sub, 2],
                         w1b, w3b, w2b, l, which, seq, mixer)

    xf = x.reshape(m_rows, d)
    for l in range(depth):
        j = l // 2
        xf = ffn(xf, l, 0, 0)
        even = l % 2 == 0
        qk_g = a_qk_g[j] if even else c_qk_g[j]
        hyena = None if even else _rope_tables(seq) + (d_conv_w[j], d_conv_b[j][None])
        q, k, v, *conv_in = _inproj_call(
            xf, norm_g[l, 1][None], mod[l, :, 1, 0], mod[l, :, 1, 1], w_in_b, l,
            jnp.tile(qk_g[0], N_Q_HEADS)[None], jnp.tile(qk_g[1], N_KV_HEADS)[None], e_blk, seq, hyena)
        gate = mod[l, :, 1, 2]
        if even:
            (u,) = conv_in
            qk_bound = (1.01 * HEAD_DIM ** 0.5) * jnp.max(jnp.abs(qk_g[0])) * jnp.max(jnp.abs(qk_g[1]))
            shift = jnp.maximum(qk_bound + jnp.max(rel_table, axis=0), a_sink[j])
            worst_gap = jnp.max(shift + qk_bound - rel_table[0])
            bias, *row_tables = _bias_call(rel_table, a_sink[j], shift)
            mixer_ab = lambda fixed: _mixer_ab_call(xf, q, k, v, u, bias, row_tables, fixed, b_conv_w[j],
                                                    b_conv_b[j][None], w_out_b, l, gate, seq)
            xf = lax.cond(worst_gap <= MAX_FIXED_SHIFT_GAP, lambda: mixer_ab(True), lambda: mixer_ab(False))
            pending = None
        else:
            x0, z = conv_in
            logit_bound = (1.01 * LOG2_E * HEAD_DIM ** 0.5) * jnp.max(jnp.abs(qk_g[0])) * jnp.max(jnp.abs(qk_g[1]))
            att = _dense_attn_call(q, k, v, logit_bound, bsz, seq)
            e1_full, e1_half, einv, f2f, f2i = _dft_tables(seq)
            fw1 = jnp.pad(d_f_w1[j], ((0, HYENA_EMB_PAD - HYENA_EMB), (0, 0)))
            h3 = _filter_trunk_call(_filter_features_t(seq), fw1, d_f_b1[j], d_f_w2[j], d_f_b2[j],
                                    d_f_w3[j], d_f_b3[j], d_f_freq[j])
            deltas = jnp.abs(jnp.linspace(HYENA_MIN_DECAY, HYENA_MAX_DECAY, CONV_WIDTH, dtype=F32))[None]
            kr, ki = _filter_spec_call(h3, d_f_w4[j].astype(BF16), deltas, e1_full, f2f, seq)
            y = _long_conv_call(z, x0, kr, ki, d_skip[j][None], e1_half, f2f, f2i, einv, bsz, seq)
            pending = (att, y, w_out_b, gate)
        xf = ffn(xf, l, 1, 2, pending)
    return xf.reshape(bsz, seq, d)
```

```python
import functools
import math

import jax
import jax.numpy as jnp
import numpy as np
from jax import lax
from jax.experimental import pallas as pl
from jax.experimental.pallas import tpu as pltpu

D_MODEL = 1024
HEAD_DIM = 64
N_Q_HEADS = 8
N_KV_HEADS = 2
GQA_GROUP = N_Q_HEADS // N_KV_HEADS
ATTN_WIDTH = N_Q_HEADS * HEAD_DIM
KV_WIDTH = N_KV_HEADS * HEAD_DIM
QKV_COLS = ATTN_WIDTH + 2 * KV_WIDTH
CONV_WIDTH = D_MODEL - ATTN_WIDTH
IN_COLS = QKV_COLS + 3 * CONV_WIDTH
D_FF = 2752
BLOCK = 128
WINDOW = 128
N_BUCKETS = 32
MAX_DISTANCE = 128
GRID_W = 64
ROPE_THETA = 10000.0
HYENA_EMB = 33
HYENA_FILTER_WIDTH = 64
HYENA_EMB_PAD = 64
HYENA_MIN_DECAY = math.log(1e-2) / 0.3
HYENA_MAX_DECAY = math.log(1e-2) / 1.5
EPS = 1e-6
NEG_INF = -1e30
LOG2_E = 1.4426950408889634
MAX_FIXED_SHIFT_GAP = 80.0

LANES = 128
HALO = 16
MXU_DIM = 256
FF_CHUNK = MXU_DIM
VMEM_LIMIT = 56 * 1024 * 1024
DFT_N2 = 128

BF16 = jnp.bfloat16
F32 = jnp.float32


def _cparams(sem):
    return pltpu.CompilerParams(dimension_semantics=sem, vmem_limit_bytes=VMEM_LIMIT)


def _resident(shape):
    nd = len(shape)
    return pl.BlockSpec(shape, lambda *_: (0,) * nd, pipeline_mode=pl.Buffered(1))


def _layer_slice(shape, layer):
    return pl.BlockSpec((None,) + shape, lambda *_: (layer, 0, 0), pipeline_mode=pl.Buffered(1))


def _adaln(x, g, scale, shift):
    y = x * lax.rsqrt(jnp.mean(x * x, axis=-1, keepdims=True) + EPS) * g
    return y * (1.0 + scale) + shift


def _mod_kernel(c_ref, w_ref, b_ref, o_ref):
    w = w_ref[0]
    for b in range(c_ref.shape[0]):
        cc = c_ref[b]
        cond = cc * jax.nn.sigmoid(cc)
        o_ref[0, b:b + 1, :] = jnp.sum(w * cond, axis=0, keepdims=True) + b_ref[0]


def _mod_call(c, w_mod, b_mod):
    depth, d, n = w_mod.shape
    bsz = c.shape[0]
    tn = 1152 if n % 1152 == 0 else 512
    return pl.pallas_call(
        _mod_kernel,
        out_shape=jax.ShapeDtypeStruct((depth, bsz, n), F32),
        grid=(depth, n // tn),
        in_specs=[
            pl.BlockSpec((bsz, d, 1), lambda l, j: (0, 0, 0)),
            pl.BlockSpec((1, d, tn), lambda l, j: (l, 0, j)),
            pl.BlockSpec((1, 1, tn), lambda l, j: (l, 0, j)),
        ],
        out_specs=pl.BlockSpec((1, bsz, tn), lambda l, j: (l, 0, j)),
        compiler_params=_cparams(("arbitrary", "arbitrary")),
        name="adaln_modulation",
    )(c.reshape(bsz, d, 1), w_mod, b_mod.reshape(depth, 1, n))


def _cast_transpose_kernel(wt_ref, o_ref):
    cols = wt_ref.shape[0]
    starts = list(range(0, cols - MXU_DIM, MXU_DIM)) + [cols - MXU_DIM]
    for c0 in starts:
        o_ref[:, c0:c0 + MXU_DIM] = wt_ref[c0:c0 + MXU_DIM, :].T.astype(o_ref.dtype)


def _to_bf16_from_transposed(w):
    lead, (rows, cols) = w.shape[:-2], w.shape[-2:]
    none = (None,) * len(lead)
    return pl.pallas_call(
        _cast_transpose_kernel,
        out_shape=jax.ShapeDtypeStruct(w.shape, BF16),
        grid=lead,
        in_specs=[pl.BlockSpec(none + (cols, rows), lambda *idx: idx + (0, 0))],
        out_specs=pl.BlockSpec(none + (rows, cols), lambda *idx: idx + (0, 0)),
        compiler_params=_cparams(("arbitrary",) * len(lead)),
        name="weights_to_bf16_transposing",
    )(jnp.swapaxes(w, -1, -2))


def _ffn_kernel(mixer_update, x_ref, g_ref, shift_ref, scale_ref, gate_ref, w1_ref, w3_ref, w2_ref, *rest):
    x = x_ref[...]
    if mixer_update:
        att_ref, y_ref, wo_ref, mgate_ref, o_ref = rest
        upd = jnp.dot(att_ref[...], wo_ref[0:ATTN_WIDTH, :], preferred_element_type=F32)
        upd = upd + jnp.dot(y_ref[...].astype(BF16), wo_ref[ATTN_WIDTH:, :], preferred_element_type=F32)
        x = x + mgate_ref[0] * upd
    else:
        (o_ref,) = rest
    h = _adaln(x, g_ref[...], scale_ref[0], shift_ref[0]).astype(BF16)
    acc = jnp.zeros(x.shape, F32)
    for c0 in range(0, D_FF, FF_CHUNK):
        cols = slice(c0, min(c0 + FF_CHUNK, D_FF))
        a = jnp.dot(h, w1_ref[:, cols], preferred_element_type=F32)
        b = jnp.dot(h, w3_ref[:, cols], preferred_element_type=F32)
        act = (a * jax.nn.sigmoid(a) * b).astype(BF16)
        acc = acc + jnp.dot(act, w2_ref[cols, :], preferred_element_type=F32)
    o_ref[...] = x + (0.5 * gate_ref[0]) * acc


def _ffn_call(x, g, shift, scale, gate, w1, w3, w2, layer, which, rows_per_batch, mixer=None):
    m, d = x.shape
    tm = min(rows_per_batch, 1024)
    tpb = rows_per_batch // tm
    row = lambda w: pl.BlockSpec((tm, w), lambda i: (i, 0))
    vec = pl.BlockSpec((1, 1, d), lambda i: (i // tpb, 0, 0))
    weight = lambda r, c: pl.BlockSpec((None, None, r, c), lambda i: (layer, which, 0, 0),
                                       pipeline_mode=pl.Buffered(1))
    in_specs = [row(d), _resident((1, d)), vec, vec, vec, weight(d, D_FF), weight(d, D_FF), weight(D_FF, d)]
    args = [x, g, shift, scale, gate, w1, w3, w2]
    if mixer is not None:
        in_specs += [row(ATTN_WIDTH), row(CONV_WIDTH), _layer_slice((d, d), layer), vec]
        args += list(mixer)
    return pl.pallas_call(
        functools.partial(_ffn_kernel, mixer is not None),
        out_shape=jax.ShapeDtypeStruct((m, d), F32),
        grid=(m // tm,),
        in_specs=in_specs,
        out_specs=row(d),
        compiler_params=_cparams(("arbitrary",)),
        name="adaln_swiglu_ffn",
    )(*args)


def _group_norm_scale(v, e_ref):
    sq = v * v
    hi = sq.astype(BF16)
    lo = (sq - hi.astype(F32)).astype(BF16)
    w = min(v.shape[1], MXU_DIM)
    e = e_ref[...]
    ss = jnp.concatenate(
        [jnp.dot(hi[:, c:c + w], e[0:w, 0:w], preferred_element_type=F32)
         + jnp.dot(lo[:, c:c + w], e[0:w, 0:w], preferred_element_type=F32) for c in range(0, v.shape[1], w)],
        axis=1)
    return lax.rsqrt(ss * (1.0 / HEAD_DIM) + EPS)


def _rope128(v, cos, sin_signed, even_lane):
    partner = jnp.where(even_lane, pltpu.roll(v, LANES - 1, 1), pltpu.roll(v, 1, 1))
    return v * cos + partner * sin_signed


def _short_conv_rows(u, before, after, cw, cb):
    rows = u.shape[0]
    r = lax.broadcasted_iota(jnp.int32, u.shape, 0)
    um1 = jnp.where(r == 0, before, pltpu.roll(u, 1, 0))
    up1 = jnp.where(r == rows - 1, after, pltpu.roll(u, rows - 1, 0))
    return cw[0:1] * um1 + cw[1:2] * u + cw[2:3] * up1 + cb


def _inproj_kernel(rope, tpb, x_ref, g_ref, shift_ref, scale_ref, w_ref, qg_ref, kg_ref, e_ref, *rest):
    if rope:
        cos_ref, sin_ref, xp_ref, xn_ref, cw_ref, cb_ref, q_ref, k_ref, v_ref, x0_ref, z_ref = rest
    else:
        q_ref, k_ref, v_ref, u_ref = rest
    h = _adaln(x_ref[...], g_ref[...], scale_ref[0], shift_ref[0]).astype(BF16)
    q = jnp.dot(h, w_ref[:, 0:ATTN_WIDTH], preferred_element_type=F32)
    k = jnp.dot(h, w_ref[:, ATTN_WIDTH:ATTN_WIDTH + KV_WIDTH], preferred_element_type=F32)
    q = q * _group_norm_scale(q, e_ref) * qg_ref[...]
    k = k * _group_norm_scale(k, e_ref) * kg_ref[...]
    if rope:
        cos = cos_ref[...]
        sin = sin_ref[...]
        even = (lax.broadcasted_iota(jnp.int32, cos.shape, 1) % 2) == 0
        q = jnp.concatenate(
            [_rope128(q[:, j * LANES:(j + 1) * LANES], cos, sin, even) for j in range(ATTN_WIDTH // LANES)],
            axis=1)
        k = _rope128(k, cos, sin, even)
    q_ref[...] = (q * (HEAD_DIM ** -0.5 * (LOG2_E if rope else 1.0))).astype(BF16)
    v = jnp.dot(h, w_ref[:, ATTN_WIDTH + KV_WIDTH:QKV_COLS], preferred_element_type=F32)
    k_ref[0] = k.T.astype(BF16)
    low = lax.broadcasted_iota(jnp.int32, v.shape, 1) < HEAD_DIM
    v_ref[...] = jnp.concatenate(
        [jnp.where(low, v, 1.0), jnp.where(low, pltpu.roll(v, HEAD_DIM, 1), 1.0)], axis=1).astype(BF16)
    if not rope:
        u_ref[...] = jnp.dot(h, w_ref[:, QKV_COLS:IN_COLS], preferred_element_type=F32)
        return
    i = pl.program_id(0)
    h_prev = _adaln(xp_ref[...], g_ref[...], scale_ref[0], shift_ref[0]).astype(BF16)
    h_next = _adaln(xn_ref[...], g_ref[...], scale_ref[0], shift_ref[0]).astype(BF16)
    u_ext = jnp.dot(jnp.concatenate([h_prev, h, h_next], axis=0), w_ref[:, QKV_COLS:IN_COLS],
                    preferred_element_type=F32)
    tm = h.shape[0]
    before = jnp.where((i % tpb) == 0, 0.0, u_ext[HALO - 1:HALO])
    after = jnp.where((i % tpb) == tpb - 1, 0.0, u_ext[HALO + tm:HALO + tm + 1])
    t = _short_conv_rows(u_ext[HALO:HALO + tm], before, after, cw_ref[...], cb_ref[...])
    x0_ref[...] = t[:, 0:CONV_WIDTH]
    z_ref[...] = t[:, CONV_WIDTH:2 * CONV_WIDTH] * t[:, 2 * CONV_WIDTH:]


def _inproj_call(x, g, shift, scale, w_in, layer, qg, kg, e, rows_per_batch, hyena=None):
    m, d = x.shape
    tm = 512
    tpb = rows_per_batch // tm
    vec = pl.BlockSpec((1, 1, d), lambda i: (i // tpb, 0, 0))
    row = lambda w: pl.BlockSpec((tm, w), lambda i: (i, 0))
    in_specs = [
        row(d),
        _resident((1, d)), vec, vec,
        _layer_slice((d, IN_COLS), layer),
        _resident((1, ATTN_WIDTH)), _resident((1, KV_WIDTH)),
        _resident((MXU_DIM, MXU_DIM)),
    ]
    args = [x, g, shift, scale, w_in, qg, kg, e]
    uw = 3 * CONV_WIDTH
    if hyena is not None:
        cos, sin, conv_w, conv_b = hyena
        tab = pl.BlockSpec((tm, LANES), lambda i: (i % tpb, 0))
        hpt = tm // HALO
        last_halo = m // HALO - 1
        in_specs += [
            tab, tab,
            pl.BlockSpec((HALO, d), lambda i: (jnp.maximum(i * hpt - 1, 0), 0)),
            pl.BlockSpec((HALO, d), lambda i: (jnp.minimum((i + 1) * hpt, last_halo), 0)),
            _resident((3, uw)), _resident((1, uw)),
        ]
        args += [cos, sin, x, x, conv_w, conv_b]
        conv_shapes = [jax.ShapeDtypeStruct((m, CONV_WIDTH), F32)] * 2
        conv_specs = [row(CONV_WIDTH)] * 2
    else:
        conv_shapes = [jax.ShapeDtypeStruct((m, uw), F32)]
        conv_specs = [row(uw)]
    tk = _key_chunk(rows_per_batch)
    per = tk // tm
    return pl.pallas_call(
        functools.partial(_inproj_kernel, hyena is not None, tpb),
        out_shape=[
            jax.ShapeDtypeStruct((m, ATTN_WIDTH), BF16),
            jax.ShapeDtypeStruct((m // tk, KV_WIDTH, tk), BF16),
            jax.ShapeDtypeStruct((m, 2 * KV_WIDTH), BF16),
        ] + conv_shapes,
        grid=(m // tm,),
        in_specs=in_specs,
        out_specs=[row(ATTN_WIDTH), pl.BlockSpec((1, KV_WIDTH, tm), lambda i: (i // per, 0, i % per)),
                   row(2 * KV_WIDTH)] + conv_specs,
        compiler_params=_cparams(("arbitrary",)),
        name="adaln_in_projection",
    )(*args)


_T5_STEPS = (12, 16, 23, 32, 46, 64, 91)


def _bias_kernel(tab_ref, sink_ref, shift_ref, o_ref, sink_rows_ref, shift_rows_ref, sink_term_rows_ref):
    for h in range(N_Q_HEADS):
        rows = slice(h * BLOCK, (h + 1) * BLOCK)
        sink_rows_ref[rows, :] = jnp.full((BLOCK, LANES), sink_ref[h], F32)
        shift_rows_ref[rows, :] = jnp.full((BLOCK, LANES), shift_ref[h], F32)
        sink_term_rows_ref[rows, :] = jnp.exp(jnp.full((BLOCK, LANES), sink_ref[h] - shift_ref[h], F32))
    qi = lax.broadcasted_iota(jnp.int32, (BLOCK, 3 * BLOCK), 0)
    kj = lax.broadcasted_iota(jnp.int32, (BLOCK, 3 * BLOCK), 1)
    rel = kj - BLOCK - qi
    n = jnp.abs(rel)
    half = N_BUCKETS // 2
    max_exact = half // 2
    large = jnp.full(n.shape, max_exact, jnp.int32)
    for t in _T5_STEPS:
        large = large + (n >= t).astype(jnp.int32)
    bucket = jnp.where(rel > 0, half, 0) + jnp.where(n < max_exact, n, large)
    for h in range(N_Q_HEADS):
        bias = jnp.zeros(n.shape, F32)
        for b in range(N_BUCKETS):
            bias = jnp.where(bucket == b, tab_ref[b, h], bias)
        o_ref[h] = jnp.where(n <= WINDOW, bias, NEG_INF)


def _bias_call(rel_table, sink, shift):
    rows = jax.ShapeDtypeStruct((N_Q_HEADS * BLOCK, LANES), F32)
    smem = pl.BlockSpec(memory_space=pltpu.SMEM)
    return pl.pallas_call(
        _bias_kernel,
        out_shape=(jax.ShapeDtypeStruct((N_Q_HEADS, BLOCK, 3 * BLOCK), F32), rows, rows, rows),
        in_specs=[smem, smem, smem],
        name="t5_bias_tile",
    )(rel_table, sink, shift)


def _mixer_ab_kernel(tq, tpb, fixed_shift, x_ref, q_ref, kc_ref, kp_ref, kn_ref, vc_ref, vp_ref, vn_ref,
                     uc_ref, up_ref, un_ref, bias_ref, sink_ref, shift_ref, sink_term_ref, cw_ref, cb_ref,
                     wo_ref, gate_ref, o_ref, kbuf, vbuf, qs, att):
    i = pl.program_id(0)
    first = (i % tpb) == 0
    last = (i % tpb) == tpb - 1
    nblk = tq // BLOCK
    grows = GQA_GROUP * BLOCK
    kbuf[:, 0:BLOCK] = kp_ref[0]
    kbuf[:, BLOCK:BLOCK + tq] = kc_ref[0]
    kbuf[:, BLOCK + tq:] = kn_ref[0]
    vbuf[0:BLOCK] = vp_ref[...]
    vbuf[BLOCK:BLOCK + tq] = vc_ref[...]
    vbuf[BLOCK + tq:] = vn_ref[...]
    for n in range(nblk):
        for h in range(N_Q_HEADS):
            g, j = divmod(h, GQA_GROUP)
            qs[g, n, j * BLOCK:(j + 1) * BLOCK, :] = q_ref[n * BLOCK:(n + 1) * BLOCK, h * HEAD_DIM:(h + 1) * HEAD_DIM]
    col = lax.broadcasted_iota(jnp.int32, (grows, 3 * BLOCK), 1)
    for n in range(nblk):
        keys = slice(n * BLOCK, (n + 3) * BLOCK)
        for g in range(N_KV_HEADS):
            s = jnp.dot(qs[g, n], kbuf[g * HEAD_DIM:(g + 1) * HEAD_DIM, keys], preferred_element_type=F32)
            s = s + bias_ref[g * GQA_GROUP:(g + 1) * GQA_GROUP].reshape(grows, 3 * BLOCK)
            if n == 0:
                s = jnp.where(jnp.logical_and(first, col < BLOCK), NEG_INF, s)
            if n == nblk - 1:
                s = jnp.where(jnp.logical_and(last, col >= 2 * BLOCK), NEG_INF, s)
            grp = slice(g * grows, (g + 1) * grows)
            if fixed_shift:
                mx = shift_ref[grp, :]
                sink_term = sink_term_ref[grp, :]
            else:
                sk = sink_ref[grp, :]
                mx = jnp.maximum(jnp.broadcast_to(jnp.max(s, axis=-1, keepdims=True), sk.shape), sk)
                sink_term = jnp.exp(sk - mx)
            p = jnp.exp(s - jnp.tile(mx, (1, 3)))
            pv = jnp.dot(p.astype(BF16), vbuf[keys, g * LANES:(g + 1) * LANES], preferred_element_type=F32)
            o = pv / (pltpu.roll(pv, HEAD_DIM, 1) + sink_term)
            for j in range(GQA_GROUP):
                h = g * GQA_GROUP + j
                att[n * BLOCK:(n + 1) * BLOCK, h * HEAD_DIM:(h + 1) * HEAD_DIM] = o[j * BLOCK:(j + 1) * BLOCK, 0:HEAD_DIM]

    gb = uc_ref[:, 0:CONV_WIDTH]
    p = uc_ref[:, CONV_WIDTH:2 * CONV_WIDTH] * uc_ref[:, 2 * CONV_WIDTH:]
    p_before = jnp.where(first, 0.0, up_ref[7:8, CONV_WIDTH:2 * CONV_WIDTH] * up_ref[7:8, 2 * CONV_WIDTH:])
    p_after = jnp.where(last, 0.0, un_ref[0:1, CONV_WIDTH:2 * CONV_WIDTH] * un_ref[0:1, 2 * CONV_WIDTH:])
    conv = gb * _short_conv_rows(p, p_before, p_after, cw_ref[...], cb_ref[...])

    y = jnp.dot(att[...].astype(BF16), wo_ref[0:ATTN_WIDTH, :], preferred_element_type=F32)
    y = y + jnp.dot(conv.astype(BF16), wo_ref[ATTN_WIDTH:, :], preferred_element_type=F32)
    o_ref[...] = x_ref[...] + gate_ref[0] * y


def _mixer_ab_call(x, q, kt, v1, u, bias, row_tables, fixed_shift, conv_w, conv_b, w_out, layer, gate,
                   rows_per_batch):
    m, d = x.shape
    tq = 512
    tpb = rows_per_batch // tq
    r = tq // BLOCK
    nb = m // BLOCK
    n8 = m // 8
    cur = lambda w: pl.BlockSpec((tq, w), lambda i: (i, 0))
    prev_of = lambda i: jnp.maximum(i * r - 1, 0)
    next_of = lambda i: jnp.minimum((i + 1) * r, nb - 1)
    v_prev = pl.BlockSpec((BLOCK, 2 * KV_WIDTH), lambda i: (prev_of(i), 0))
    v_next = pl.BlockSpec((BLOCK, 2 * KV_WIDTH), lambda i: (next_of(i), 0))
    tk = kt.shape[2]
    k_cur = pl.BlockSpec((1, KV_WIDTH, tq), lambda i: (i // (tk // tq), 0, i % (tk // tq)))
    kpb = tk // BLOCK
    k_prev = pl.BlockSpec((1, KV_WIDTH, BLOCK), lambda i: (prev_of(i) // kpb, 0, prev_of(i) % kpb))
    k_next = pl.BlockSpec((1, KV_WIDTH, BLOCK), lambda i: (next_of(i) // kpb, 0, next_of(i) % kpb))
    uw = 3 * CONV_WIDTH
    row_table = _resident((N_Q_HEADS * BLOCK, LANES))
    return pl.pallas_call(
        functools.partial(_mixer_ab_kernel, tq, tpb, fixed_shift),
        out_shape=jax.ShapeDtypeStruct((m, d), F32),
        grid=(m // tq,),
        in_specs=[
            cur(d), cur(ATTN_WIDTH),
            k_cur, k_prev, k_next,
            cur(2 * KV_WIDTH), v_prev, v_next,
            cur(uw),
            pl.BlockSpec((8, uw), lambda i: (jnp.maximum(i * (tq // 8) - 1, 0), 0)),
            pl.BlockSpec((8, uw), lambda i: (jnp.minimum((i + 1) * (tq // 8), n8 - 1), 0)),
            _resident((N_Q_HEADS, BLOCK, 3 * BLOCK)),
            row_table, row_table, row_table,
            _resident((3, CONV_WIDTH)), _resident((1, CONV_WIDTH)),
            _layer_slice((d, d), layer),
            pl.BlockSpec((1, 1, d), lambda i: (i // tpb, 0, 0)),
        ],
        out_specs=cur(d),
        scratch_shapes=[
            pltpu.VMEM((KV_WIDTH, tq + 2 * BLOCK), BF16),
            pltpu.VMEM((tq + 2 * BLOCK, 2 * KV_WIDTH), BF16),
            pltpu.VMEM((N_KV_HEADS, r, GQA_GROUP * BLOCK, HEAD_DIM), BF16),
            pltpu.VMEM((tq, ATTN_WIDTH), F32),
        ],
        compiler_params=_cparams(("arbitrary",)),
        name="windowed_attn_shortconv_outproj",
    )(x, q, kt, kt, kt, v1, v1, v1, u, u, u, bias, *row_tables, conv_w, conv_b, w_out, gate)


def _key_chunk(seq):
    return min(seq, 1024)


def _dense_attn_kernel(tq, tk, q_ref, k_ref, v_ref, o_ref, qs, s_buf, p_buf, rmax_buf, alpha_buf,
                       m_ref, acc_ref):
    nc = k_ref.shape[0]
    for j in range(GQA_GROUP):
        qs[j * tq:(j + 1) * tq, :] = q_ref[:, j * HEAD_DIM:(j + 1) * HEAD_DIM]
    m_ref[...] = jnp.full(m_ref.shape, -jnp.inf, F32)
    acc_ref[...] = jnp.zeros(acc_ref.shape, F32)

    def scores(c):
        s = jnp.dot(qs[...], k_ref[c], preferred_element_type=F32)
        s_buf[...] = s
        rmax_buf[...] = jnp.broadcast_to(jnp.max(s, axis=-1, keepdims=True), rmax_buf.shape)

    def softmax():
        m_old = m_ref[...]
        m_new = jnp.maximum(m_old, rmax_buf[...])
        alpha_buf[...] = jnp.exp2(m_old - m_new)
        m_ref[...] = m_new
        p = jnp.exp2(s_buf[...] - jnp.tile(m_new, (1, tk // LANES)))
        p_buf[...] = p.astype(BF16)

    def weighted_values(c):
        start = c * tk if isinstance(c, int) else pl.multiple_of(c * tk, tk)
        pv = jnp.dot(p_buf[...], v_ref[pl.ds(start, tk), :], preferred_element_type=F32)
        acc_ref[...] = alpha_buf[...] * acc_ref[...] + pv

    def step(t):
        static = isinstance(t, int)
        weighted_values(t)
        if not static or t + 1 < nc:
            softmax()
        if not static or t + 2 < nc:
            scores(t + 2)

    scores(0)
    softmax()
    if nc > 1:
        scores(1)

    def steady(t, carry):
        step(t)
        return carry

    lax.fori_loop(0, max(nc - 2, 0), steady, 0)
    for t in range(max(nc - 2, 0), nc):
        step(t)

    acc = acc_ref[...]
    o = acc / pltpu.roll(acc, HEAD_DIM, 1)
    for j in range(GQA_GROUP):
        o_ref[:, j * HEAD_DIM:(j + 1) * HEAD_DIM] = o[j * tq:(j + 1) * tq, 0:HEAD_DIM].astype(o_ref.dtype)


def _dense_attn_bounded_kernel(tq, tk, bound_ref, q_ref, k_ref, v_ref, o_ref, qs, p_buf, acc_ref):
    nc = k_ref.shape[0]
    for j in range(GQA_GROUP):
        qs[j * tq:(j + 1) * tq, :] = q_ref[:, j * HEAD_DIM:(j + 1) * HEAD_DIM]
    shift = bound_ref[0]

    def probabilities(c):
        s = jnp.dot(qs[...], k_ref[c], preferred_element_type=F32)
        p_buf[...] = jnp.exp2(s - shift).astype(BF16)

    def weighted_values(c):
        start = c * tk if isinstance(c, int) else pl.multiple_of(c * tk, tk)
        return jnp.dot(p_buf[...], v_ref[pl.ds(start, tk), :], preferred_element_type=F32)

    probabilities(0)
    acc_ref[...] = weighted_values(0)
    if nc > 1:
        probabilities(1)

    def steady(t, carry):
        acc_ref[...] += weighted_values(t)
        probabilities(t + 1)
        return carry

    lax.fori_loop(1, nc - 1, steady, 0)
    if nc > 1:
        acc_ref[...] += weighted_values(nc - 1)

    acc = acc_ref[...]
    o = acc / pltpu.roll(acc, HEAD_DIM, 1)
    for j in range(GQA_GROUP):
        o_ref[:, j * HEAD_DIM:(j + 1) * HEAD_DIM] = o[j * tq:(j + 1) * tq, 0:HEAD_DIM].astype(o_ref.dtype)


MAX_FIXED_SHIFT = 50.0


def _dense_attn_call(q, kt, v1, logit_bound, bsz, seq):
    tq, tk = min(seq, 1024), _key_chunk(seq)
    nq = seq // tq
    nc = seq // tk
    rows = GQA_GROUP * tq
    gw = GQA_GROUP * HEAD_DIM
    stat = pltpu.VMEM((rows, LANES), F32)
    q_spec = pl.BlockSpec((tq, gw), lambda b, g, i: (b * nq + i, g))
    k_spec = pl.BlockSpec((nc, HEAD_DIM, tk), lambda b, g, i: (b, g, 0), pipeline_mode=pl.Buffered(1))
    v_spec = pl.BlockSpec((seq, LANES), lambda b, g, i: (b, g), pipeline_mode=pl.Buffered(1))
    common = dict(
        out_shape=jax.ShapeDtypeStruct((bsz * seq, ATTN_WIDTH), BF16),
        grid=(bsz, N_KV_HEADS, nq),
        out_specs=q_spec,
        compiler_params=_cparams(("arbitrary", "arbitrary", "arbitrary")),
    )

    def running_max():
        return pl.pallas_call(
            functools.partial(_dense_attn_kernel, tq, tk),
            in_specs=[q_spec, k_spec, v_spec],
            scratch_shapes=[
                pltpu.VMEM((rows, HEAD_DIM), BF16),
                pltpu.VMEM((rows, tk), F32), pltpu.VMEM((rows, tk), BF16),
                stat, stat, stat, stat,
            ],
            name="dense_gqa_attention", **common,
        )(q, kt, v1)

    def fixed_shift():
        return pl.pallas_call(
            functools.partial(_dense_attn_bounded_kernel, tq, tk),
            in_specs=[pl.BlockSpec(memory_space=pltpu.SMEM), q_spec, k_spec, v_spec],
            scratch_shapes=[pltpu.VMEM((rows, HEAD_DIM), BF16), pltpu.VMEM((rows, tk), BF16), stat],
            name="dense_gqa_attention_fixed_shift", **common,
        )(logit_bound.reshape(1), q, kt, v1)

    return lax.cond(logit_bound <= MAX_FIXED_SHIFT, fixed_shift, running_max)


def _hp_dot(a, b):
    return jnp.dot(a, b, preferred_element_type=F32, precision=lax.Precision.HIGHEST)


def _filter_trunk_kernel(z_ref, w1_ref, b1_ref, w2_ref, b2_ref, w3_ref, b3_ref, fq_ref, o_ref):
    fq = fq_ref[...]
    h = jnp.sin(fq * (_hp_dot(w1_ref[...], z_ref[...]) + b1_ref[...]))
    h = jnp.sin(fq * (_hp_dot(w2_ref[...], h) + b2_ref[...]))
    h = jnp.sin(fq * (_hp_dot(w3_ref[...], h) + b3_ref[...]))
    o_ref[...] = h.T


def _filter_trunk_call(zfeat_t, w1, b1, w2, b2, w3, b3, freq):
    rows = zfeat_t.shape[1]
    tr = min(rows, 2048)
    fw = HYENA_FILTER_WIDTH
    col = lambda v: v.reshape(fw, 1)
    return pl.pallas_call(
        _filter_trunk_kernel,
        out_shape=jax.ShapeDtypeStruct((rows, fw), F32),
        grid=(rows // tr,),
        in_specs=[
            pl.BlockSpec((HYENA_EMB_PAD, tr), lambda i: (0, i)),
            _resident((fw, HYENA_EMB_PAD)), _resident((fw, 1)),
            _resident((fw, fw)), _resident((fw, 1)),
            _resident((fw, fw)), _resident((fw, 1)),
            _resident((fw, 1)),
        ],
        out_specs=pl.BlockSpec((tr, fw), lambda i: (i, 0)),
        compiler_params=_cparams(("arbitrary",)),
        name="hyena_filter_trunk",
    )(zfeat_t, w1.T, col(b1), w2.T, col(b2), w3.T, col(b3), col(freq))


def _dft_stage1(src_ref, tab_ref, re_ref, im_ref, n_rows, k1p, n2_major=False):
    def body(n2, carry):
        xs = src_ref[pl.ds(n2, n_rows, stride=DFT_N2), :].astype(BF16)
        res = jnp.dot(tab_ref[n2], xs, preferred_element_type=F32)
        dst = pl.ds(pl.multiple_of(n2 * k1p, 8), k1p) if n2_major else pl.ds(n2, k1p, stride=DFT_N2)
        re_ref[dst, :] = res[0:k1p]
        im_ref[dst, :] = res[k1p:]
        return carry

    lax.fori_loop(0, DFT_N2, body, 0, unroll=8)


def _filter_spec_kernel(seq, k1p, h_ref, w4f_ref, w4b_ref, dl_ref, e1_ref, f2_ref,
                        kr_ref, ki_ref, kfull, bre, bim):
    n = 2 * seq
    chunk = min(seq, 1024)
    dl = dl_ref[...]
    live = seq // DFT_N2 + 1

    def fill_half(w_ref, backward):
        def fill(c, ss):
            r0 = pl.multiple_of(c * chunk, chunk)
            val = jnp.dot(h_ref[pl.ds(r0, chunk), :].astype(BF16), w_ref[...], preferred_element_type=F32)
            rows = r0 + lax.broadcasted_iota(jnp.int32, (chunk, 1), 0)
            lag = ((n - rows) if backward else rows).astype(F32)
            val = val * jnp.exp(-(lag * (1.0 / (seq - 1))) * dl)
            if backward:
                val = jnp.where(rows == seq, 0.0, val)
            kfull[pl.ds(r0, chunk), :] = val
            return ss + jnp.sum(val * val, axis=0, keepdims=True)
        return fill

    ss = lax.fori_loop(0, seq // chunk, fill_half(w4f_ref, False), jnp.zeros((1, dl.shape[1]), F32))
    ss = lax.fori_loop(seq // chunk, n // chunk, fill_half(w4b_ref, True), ss)
    norm = lax.rsqrt(ss + EPS)

    _dft_stage1(kfull, e1_ref, bre, bim, n // DFT_N2, k1p, n2_major=True)

    kr_ref[live * DFT_N2:, :] = jnp.zeros(((k1p - live) * DFT_N2, kr_ref.shape[1]), F32)
    ki_ref[live * DFT_N2:, :] = jnp.zeros(((k1p - live) * DFT_N2, ki_ref.shape[1]), F32)
    f2 = f2_ref[...]

    def stage2(a, carry):
        src = pl.ds(a, DFT_N2, stride=k1p)
        slab = jnp.concatenate([bre[src, :], bim[src, :]], axis=0).astype(BF16)
        xf = jnp.dot(f2, slab, preferred_element_type=F32)
        rows = pl.ds(pl.multiple_of(a * DFT_N2, DFT_N2), DFT_N2)
        kr_ref[rows, :] = xf[0:DFT_N2] * norm
        ki_ref[rows, :] = xf[DFT_N2:] * norm
        return carry

    lax.fori_loop(0, live, stage2, 0, unroll=8)


def _filter_spec_call(h3, w4, deltas, e1f, f2f, seq):
    n = 2 * seq
    k1p = e1f.shape[1] // 2
    ct = LANES
    nct = CONV_WIDTH // ct
    fw = HYENA_FILTER_WIDTH
    spec_rows = k1p * DFT_N2
    out = pl.BlockSpec((spec_rows, ct), lambda j: (0, j))
    return pl.pallas_call(
        functools.partial(_filter_spec_kernel, seq, k1p),
        out_shape=(jax.ShapeDtypeStruct((spec_rows, CONV_WIDTH), F32),
                   jax.ShapeDtypeStruct((spec_rows, CONV_WIDTH), F32)),
        grid=(nct,),
        in_specs=[
            _resident((n, fw)),
            pl.BlockSpec((fw, ct), lambda j: (0, j)),
            pl.BlockSpec((fw, ct), lambda j: (0, nct + j)),
            pl.BlockSpec((1, ct), lambda j: (0, j)),
            _resident(e1f.shape),
            _resident((2 * DFT_N2, 2 * DFT_N2)),
        ],
        out_specs=(out, out),
        scratch_shapes=[pltpu.VMEM((n, ct), F32), pltpu.VMEM((spec_rows, ct), F32),
                        pltpu.VMEM((spec_rows, ct), F32)],
        compiler_params=_cparams(("arbitrary",)),
        name="hyena_filter_spectrum",
    )(h3, w4, w4, deltas, e1f, f2f)


def _long_conv_kernel(seq, k1p, z_ref, x0_ref, kr_ref, ki_ref, skip_ref, e1_ref, f2f_ref, f2i_ref,
                      einv_ref, o_ref, br, bi):
    n1h = seq // DFT_N2
    _dft_stage1(z_ref, e1_ref, br, bi, n1h, k1p)

    f2f = f2f_ref[...]
    f2i = f2i_ref[...]

    def spectral(a, carry):
        r0 = pl.multiple_of(a * DFT_N2, DFT_N2)
        rows = pl.ds(r0, DFT_N2)
        slab = jnp.concatenate([br[rows, :], bi[rows, :]], axis=0).astype(BF16)
        xf = jnp.dot(f2f, slab, preferred_element_type=F32)
        xr, xi = xf[0:DFT_N2], xf[DFT_N2:]
        kr, ki = kr_ref[rows, :], ki_ref[rows, :]
        y = jnp.concatenate([xr * kr - xi * ki, xr * ki + xi * kr], axis=0).astype(BF16)
        d = jnp.dot(f2i, y, preferred_element_type=F32)
        br[rows, :] = d[0:DFT_N2]
        bi[rows, :] = d[DFT_N2:]
        return carry

    lax.fori_loop(0, n1h + 1, spectral, 0, unroll=8)

    skip = skip_ref[...]

    def synth(n2, carry):
        ds_spec = pl.ds(n2, k1p, stride=DFT_N2)
        d = jnp.concatenate([br[ds_spec, :], bi[ds_spec, :]], axis=0).astype(BF16)
        o_ref[pl.ds(n2, n1h, stride=DFT_N2), :] = jnp.dot(einv_ref[n2], d, preferred_element_type=F32)
        return carry

    lax.fori_loop(0, DFT_N2, synth, 0, unroll=8)

    chunk = min(seq, 512)

    def gate(c, carry):
        rows = pl.ds(pl.multiple_of(c * chunk, chunk), chunk)
        o_ref[rows, :] = x0_ref[rows, :] * (o_ref[rows, :] + skip * z_ref[rows, :])
        return carry

    lax.fori_loop(0, seq // chunk, gate, 0)


def _long_conv_call(z, x0, kr, ki, skip, e1, f2f, f2i, einv, bsz, seq):
    ct = LANES
    nct = CONV_WIDTH // ct
    k1p = e1.shape[1] // 2
    spec_rows = k1p * DFT_N2
    big = lambda: pl.BlockSpec((seq, ct), lambda j, b: (b, j), pipeline_mode=pl.Buffered(1))
    spec = lambda: pl.BlockSpec((spec_rows, ct), lambda j, b: (0, j), pipeline_mode=pl.Buffered(1))
    return pl.pallas_call(
        functools.partial(_long_conv_kernel, seq, k1p),
        out_shape=jax.ShapeDtypeStruct((bsz * seq, CONV_WIDTH), F32),
        grid=(nct, bsz),
        in_specs=[
            pl.BlockSpec((seq, ct), lambda j, b: (b, j)), big(), spec(), spec(),
            pl.BlockSpec((1, ct), lambda j, b: (0, j)),
            _resident(e1.shape),
            _resident((2 * DFT_N2, 2 * DFT_N2)), _resident((2 * DFT_N2, 2 * DFT_N2)),
            _resident(einv.shape),
        ],
        out_specs=pl.BlockSpec((seq, ct), lambda j, b: (b, j)),
        scratch_shapes=[pltpu.VMEM((spec_rows, ct), F32), pltpu.VMEM((spec_rows, ct), F32)],
        compiler_params=_cparams(("arbitrary", "arbitrary")),
        name="hyena_long_conv",
    )(z, x0, kr, ki, skip, e1, f2f, f2i, einv)


def _rope_tables(seq):
    t = jnp.arange(seq)
    row = (t // GRID_W).astype(F32)
    col = (t % GRID_W).astype(F32)
    half = HEAD_DIM // 2
    inv = ROPE_THETA ** (-jnp.arange(0, half, 2, dtype=F32) / half)
    ang = jnp.concatenate([row[:, None] * inv, col[:, None] * inv], axis=-1)
    ang = jnp.repeat(ang, 2, axis=-1)
    sign = jnp.where(jnp.arange(HEAD_DIM) % 2 == 0, -1.0, 1.0).astype(F32)
    reps = LANES // HEAD_DIM
    return jnp.tile(jnp.cos(ang), (1, reps)), jnp.tile(jnp.sin(ang) * sign, (1, reps))


def _dft_tables(seq):
    n = 2 * seq
    n1 = n // DFT_N2
    k1 = n1 // 2 + 1
    k1p = -(-k1 // 8) * 8
    a = jnp.arange(k1p)
    live = (a < k1).astype(F32)
    ang1 = ((a[:, None] * jnp.arange(n1)[None, :]) % n1).astype(F32) * (2.0 * math.pi / n1)
    ang2 = (jnp.arange(DFT_N2)[:, None] * a[None, :]).astype(F32) * (2.0 * math.pi / n)
    c1, s1 = jnp.cos(ang1) * live[:, None], jnp.sin(ang1) * live[:, None]
    c2, s2 = jnp.cos(ang2), jnp.sin(ang2)
    cos_a = c1[None] * c2[:, :, None] - s1[None] * s2[:, :, None]
    sin_a = s1[None] * c2[:, :, None] + c1[None] * s2[:, :, None]
    e1_full = jnp.concatenate([cos_a, -sin_a], axis=1).astype(BF16)
    e1_half = e1_full[:, :, :n1 // 2]
    wgt = jnp.where((a == 0) | (a == n1 // 2), 1.0, 2.0) / n
    c1t, s1t = (c1 * wgt[:, None]).T[:n1 // 2], (s1 * wgt[:, None]).T[:n1 // 2]
    cos_s = c1t[None] * c2[:, None, :] - s1t[None] * s2[:, None, :]
    sin_s = s1t[None] * c2[:, None, :] + c1t[None] * s2[:, None, :]
    einv = jnp.concatenate([cos_s, -sin_s], axis=2).astype(BF16)
    kk = jnp.arange(DFT_N2)
    phi = ((kk[:, None] * kk[None, :]) % DFT_N2).astype(F32) * (2.0 * math.pi / DFT_N2)
    ci, si = jnp.cos(phi), jnp.sin(phi)
    f2f = jnp.block([[ci, si], [-si, ci]]).astype(BF16)
    f2i = jnp.block([[ci, -si], [si, ci]]).astype(BF16)
    return e1_full, e1_half, einv, f2f, f2i


def _filter_features_t(seq):
    j = jnp.arange(2 * seq)
    lag = jnp.where(j < seq, j, jnp.where(j == seq, 0, 2 * seq - j)).astype(F32)[None, :]
    t = lag / (seq - 1)
    bands = (HYENA_EMB - 1) // 2
    w = (2.0 * math.pi / seq) * lag
    fr = jnp.linspace(1e-4, bands - 1, bands, dtype=F32)[:, None]
    pad = jnp.zeros((HYENA_EMB_PAD - HYENA_EMB, 2 * seq), F32)
    return jnp.concatenate([t, jnp.cos(fr * w), -jnp.sin(fr * w), pad], axis=0)


def kernel(x, c, rel_table, norm_g, w_mod, b_mod, w_in, w_out, ffn_w1, ffn_w3, ffn_w2, a_qk_g, a_sink,
           b_conv_w, b_conv_b, c_qk_g, d_conv_w, d_conv_b, d_f_w1, d_f_b1, d_f_w2, d_f_b2, d_f_w3,
           d_f_b3, d_f_w4, d_f_freq, d_skip):
    bsz, seq, d = x.shape
    depth = w_mod.shape[0]
    m_rows = bsz * seq
    mod =_mod_call(c, w_mod, b_mod).reshape(depth, bsz, 3, 3, 1, d)
    e_blk = jnp.kron(jnp.eye(MXU_DIM // HEAD_DIM, dtype=F32), jnp.ones((HEAD_DIM, HEAD_DIM), F32)).astype(BF16)

    w1b, w3b = _to_bf16_from_transposed(ffn_w1), _to_bf16_from_transposed(ffn_w3)
    w2b, w_in_b, w_out_b = (w.astype(BF16) for w in (ffn_w2, w_in, w_out))

    def ffn(xf, l, which, sub, mixer=None):
        return _ffn_call(xf, norm_g[l, sub][None], mod[l, :, sub, 0], mod[l, :, sub, 1], mod[l, :, sub, 2],
                         w1b, w3b, w2b, l, which, seq, mixer)

    xf = x.reshape(m_rows, d)
    for l in range(depth):
        j = l // 2
        xf = ffn(xf, l, 0, 0)
        even = l % 2 == 0
        qk_g = a_qk_g[j] if even else c_qk_g[j]
        hyena = None if even else _rope_tables(seq) + (d_conv_w[j], d_conv_b[j][None])
        q, k, v, *conv_in = _inproj_call(
            xf, norm_g[l, 1][None], mod[l, :, 1, 0], mod[l, :, 1, 1], w_in_b, l,
            jnp.tile(qk_g[0], N_Q_HEADS)[None], jnp.tile(qk_g[1], N_KV_HEADS)[None], e_blk, seq, hyena)
        gate = mod[l, :, 1, 2]
        if even:
            (u,) = conv_in
            qk_bound = (1.01 * HEAD_DIM ** 0.5) * jnp.max(jnp.abs(qk_g[0])) * jnp.max(jnp.abs(qk_g[1]))
            shift = jnp.maximum(qk_bound + jnp.max(rel_table, axis=0), a_sink[j])
            worst_gap = jnp.max(shift + qk_bound - rel_table[0])
            bias, *row_tables = _bias_call(rel_table, a_sink[j], shift)
            mixer_ab = lambda fixed: _mixer_ab_call(xf, q, k, v, u, bias, row_tables, fixed, b_conv_w[j],
                                                    b_conv_b[j][None], w_out_b, l, gate, seq)
            xf = lax.cond(worst_gap <= MAX_FIXED_SHIFT_GAP, lambda: mixer_ab(True), lambda: mixer_ab(False))
            pending = None
        else:
            x0, z = conv_in
            logit_bound = (1.01 * LOG2_E * HEAD_DIM ** 0.5) * jnp.max(jnp.abs(qk_g[0])) * jnp.max(jnp.abs(qk_g[1]))
            att = _dense_attn_call(q, k, v, logit_bound, bsz, seq)
            e1_full, e1_half, einv, f2f, f2i = _dft_tables(seq)
            fw1 = jnp.pad(d_f_w1[j], ((0, HYENA_EMB_PAD - HYENA_EMB), (0, 0)))
            h3 = _filter_trunk_call(_filter_features_t(seq), fw1, d_f_b1[j], d_f_w2[j], d_f_b2[j],
                                    d_f_w3[j], d_f_b3[j], d_f_freq[j])
            deltas = jnp.abs(jnp.linspace(HYENA_MIN_DECAY, HYENA_MAX_DECAY, CONV_WIDTH, dtype=F32))[None]
            kr, ki = _filter_spec_call(h3, d_f_w4[j].astype(BF16), deltas, e1_full, f2f, seq)
            y = _long_conv_call(z, x0, kr, ki, d_skip[j][None], e1_half, f2f, f2i, einv, bsz, seq)
            pending = (att, y, w_out_b, gate)
        xf = ffn(xf, l, 1, 2, pending)
    return xf.reshape(bsz, seq, d)
```

```python
import functools
import math

import jax
import jax.numpy as jnp
import numpy as np
from jax import lax
from jax.experimental import pallas as pl
from jax.experimental.pallas import tpu as pltpu

D_MODEL = 1024
HEAD_DIM = 64
N_Q_HEADS = 8
N_KV_HEADS = 2
GQA_GROUP = N_Q_HEADS // N_KV_HEADS
ATTN_WIDTH = N_Q_HEADS * HEAD_DIM
KV_WIDTH = N_KV_HEADS * HEAD_DIM
QKV_COLS = ATTN_WIDTH + 2 * KV_WIDTH
CONV_WIDTH = D_MODEL - ATTN_WIDTH
IN_COLS = QKV_COLS + 3 * CONV_WIDTH
D_FF = 2752
BLOCK = 128
WINDOW = 128
N_BUCKETS = 32
MAX_DISTANCE = 128
GRID_W = 64
ROPE_THETA = 10000.0
HYENA_EMB = 33
HYENA_FILTER_WIDTH = 64
HYENA_EMB_PAD = 64
HYENA_MIN_DECAY = math.log(1e-2) / 0.3
HYENA_MAX_DECAY = math.log(1e-2) / 1.5
EPS = 1e-6
NEG_INF = -1e30
LOG2_E = 1.4426950408889634
MAX_FIXED_SHIFT_GAP = 80.0

LANES = 128
HALO = 16
MXU_DIM = 256
FF_CHUNK = MXU_DIM
VMEM_LIMIT = 56 * 1024 * 1024
DFT_N2 = 128

BF16 = jnp.bfloat16
F32 = jnp.float32


def _cparams(sem):
    return pltpu.CompilerParams(dimension_semantics=sem, vmem_limit_bytes=VMEM_LIMIT)


def _resident(shape):
    nd = len(shape)
    return pl.BlockSpec(shape, lambda *_: (0,) * nd, pipeline_mode=pl.Buffered(1))


def _layer_slice(shape, layer):
    return pl.BlockSpec((None,) + shape, lambda *_: (layer, 0, 0), pipeline_mode=pl.Buffered(1))


def _adaln(x, g, scale, shift):
    y = x * lax.rsqrt(jnp.mean(x * x, axis=-1, keepdims=True) + EPS) * g
    return y * (1.0 + scale) + shift


def _mod_kernel(c_ref, w_ref, b_ref, o_ref):
    w = w_ref[0]
    for b in range(c_ref.shape[0]):
        cc = c_ref[b]
        cond = cc * jax.nn.sigmoid(cc)
        o_ref[0, b:b + 1, :] = jnp.sum(w * cond, axis=0, keepdims=True) + b_ref[0]


def _mod_call(c, w_mod, b_mod):
    depth, d, n = w_mod.shape
    bsz = c.shape[0]
    tn = 1152 if n % 1152 == 0 else 512
    return pl.pallas_call(
        _mod_kernel,
        out_shape=jax.ShapeDtypeStruct((depth, bsz, n), F32),
        grid=(depth, n // tn),
        in_specs=[
            pl.BlockSpec((bsz, d, 1), lambda l, j: (0, 0, 0)),
            pl.BlockSpec((1, d, tn), lambda l, j: (l, 0, j)),
            pl.BlockSpec((1, 1, tn), lambda l, j: (l, 0, j)),
        ],
        out_specs=pl.BlockSpec((1, bsz, tn), lambda l, j: (l, 0, j)),
        compiler_params=_cparams(("arbitrary", "arbitrary")),
        name="adaln_modulation",
    )(c.reshape(bsz, d, 1), w_mod, b_mod.reshape(depth, 1, n))


def _cast_transpose_kernel(wt_ref, o_ref):
    cols = wt_ref.shape[0]
    starts = list(range(0, cols - MXU_DIM, MXU_DIM)) + [cols - MXU_DIM]
    for c0 in starts:
        o_ref[:, c0:c0 + MXU_DIM] = wt_ref[c0:c0 + MXU_DIM, :].T.astype(o_ref.dtype)


def _to_bf16_from_transposed(w):
    lead, (rows, cols) = w.shape[:-2], w.shape[-2:]
    none = (None,) * len(lead)
    return pl.pallas_call(
        _cast_transpose_kernel,
        out_shape=jax.ShapeDtypeStruct(w.shape, BF16),
        grid=lead,
        in_specs=[pl.BlockSpec(none + (cols, rows), lambda *idx: idx + (0, 0))],
        out_specs=pl.BlockSpec(none + (rows, cols), lambda *idx: idx + (0, 0)),
        compiler_params=_cparams(("arbitrary",) * len(lead)),
        name="weights_to_bf16_transposing",
    )(jnp.swapaxes(w, -1, -2))


def _ffn_kernel(mixer_update, x_ref, g_ref, shift_ref, scale_ref, gate_ref, w1_ref, w3_ref, w2_ref, *rest):
    x = x_ref[...]
    if mixer_update:
        att_ref, y_ref, wo_ref, mgate_ref, o_ref = rest
        upd = jnp.dot(att_ref[...], wo_ref[0:ATTN_WIDTH, :], preferred_element_type=F32)
        upd = upd + jnp.dot(y_ref[...].astype(BF16), wo_ref[ATTN_WIDTH:, :], preferred_element_type=F32)
        x = x + mgate_ref[0] * upd
    else:
        (o_ref,) = rest
    h = _adaln(x, g_ref[...], scale_ref[0], shift_ref[0]).astype(BF16)
    acc = jnp.zeros(x.shape, F32)
    for c0 in range(0, D_FF, FF_CHUNK):
        cols = slice(c0, min(c0 + FF_CHUNK, D_FF))
        a = jnp.dot(h, w1_ref[:, cols], preferred_element_type=F32)
        b = jnp.dot(h, w3_ref[:, cols], preferred_element_type=F32)
        act = (a * jax.nn.sigmoid(a) * b).astype(BF16)
        acc = acc + jnp.dot(act, w2_ref[cols, :], preferred_element_type=F32)
    o_ref[...] = x + (0.5 * gate_ref[0]) * acc


def _ffn_call(x, g, shift, scale, gate, w1, w3, w2, layer, which, rows_per_batch, mixer=None):
    m, d = x.shape
    tm = min(rows_per_batch, 1024)
    tpb = rows_per_batch // tm
    row = lambda w: pl.BlockSpec((tm, w), lambda i: (i, 0))
    vec = pl.BlockSpec((1, 1, d), lambda i: (i // tpb, 0, 0))
    weight = lambda r, c: pl.BlockSpec((None, None, r, c), lambda i: (layer, which, 0, 0),
                                       pipeline_mode=pl.Buffered(1))
    in_specs = [row(d), _resident((1, d)), vec, vec, vec, weight(d, D_FF), weight(d, D_FF), weight(D_FF, d)]
    args = [x, g, shift, scale, gate, w1, w3, w2]
    if mixer is not None:
        in_specs += [row(ATTN_WIDTH), row(CONV_WIDTH), _layer_slice((d, d), layer), vec]
        args += list(mixer)
    return pl.pallas_call(
        functools.partial(_ffn_kernel, mixer is not None),
        out_shape=jax.ShapeDtypeStruct((m, d), F32),
        grid=(m // tm,),
        in_specs=in_specs,
        out_specs=row(d),
        compiler_params=_cparams(("arbitrary",)),
        name="adaln_swiglu_ffn",
    )(*args)


def _group_norm_scale(v, e_ref):
    sq = v * v
    hi = sq.astype(BF16)
    lo = (sq - hi.astype(F32)).astype(BF16)
    w = min(v.shape[1], MXU_DIM)
    e = e_ref[...]
    ss = jnp.concatenate(
        [jnp.dot(hi[:, c:c + w], e[0:w, 0:w], preferred_element_type=F32)
         + jnp.dot(lo[:, c:c + w], e[0:w, 0:w], preferred_element_type=F32) for c in range(0, v.shape[1], w)],
        axis=1)
    return lax.rsqrt(ss * (1.0 / HEAD_DIM) + EPS)


def _rope128(v, cos, sin_signed, even_lane):
    partner = jnp.where(even_lane, pltpu.roll(v, LANES - 1, 1), pltpu.roll(v, 1, 1))
    return v * cos + partner * sin_signed


def _short_conv_rows(u, before, after, cw, cb):
    rows = u.shape[0]
    r = lax.broadcasted_iota(jnp.int32, u.shape, 0)
    um1 = jnp.where(r == 0, before, pltpu.roll(u, 1, 0))
    up1 = jnp.where(r == rows - 1, after, pltpu.roll(u, rows - 1, 0))
    return cw[0:1] * um1 + cw[1:2] * u + cw[2:3] * up1 + cb


def _inproj_kernel(rope, tpb, x_ref, g_ref, shift_ref, scale_ref, w_ref, qg_ref, kg_ref, e_ref, *rest):
    if rope:
        cos_ref, sin_ref, xp_ref, xn_ref, cw_ref, cb_ref, q_ref, k_ref, v_ref, x0_ref, z_ref = rest
    else:
        q_ref, k_ref, v_ref, u_ref = rest
    h = _adaln(x_ref[...], g_ref[...], scale_ref[0], shift_ref[0]).astype(BF16)
    q = jnp.dot(h, w_ref[:, 0:ATTN_WIDTH], preferred_element_type=F32)
    k = jnp.dot(h, w_ref[:, ATTN_WIDTH:ATTN_WIDTH + KV_WIDTH], preferred_element_type=F32)
    q = q * _group_norm_scale(q, e_ref) * qg_ref[...]
    k = k * _group_norm_scale(k, e_ref) * kg_ref[...]
    if rope:
        cos = cos_ref[...]
        sin = sin_ref[...]
        even = (lax.broadcasted_iota(jnp.int32, cos.shape, 1) % 2) == 0
        q = jnp.concatenate(
            [_rope128(q[:, j * LANES:(j + 1) * LANES], cos, sin, even) for j in range(ATTN_WIDTH // LANES)],
            axis=1)
        k = _rope128(k, cos, sin, even)
    q_ref[...] = (q * (HEAD_DIM ** -0.5 * (LOG2_E if rope else 1.0))).astype(BF16)
    v = jnp.dot(h, w_ref[:, ATTN_WIDTH + KV_WIDTH:QKV_COLS], preferred_element_type=F32)
    k_ref[0] = k.T.astype(BF16)
    low = lax.broadcasted_iota(jnp.int32, v.shape, 1) < HEAD_DIM
    v_ref[...] = jnp.concatenate(
        [jnp.where(low, v, 1.0), jnp.where(low, pltpu.roll(v, HEAD_DIM, 1), 1.0)], axis=1).astype(BF16)
    if not rope:
        u_ref[...] = jnp.dot(h, w_ref[:, QKV_COLS:IN_COLS], preferred_element_type=F32)
        return
    i = pl.program_id(0)
    h_prev = _adaln(xp_ref[...], g_ref[...], scale_ref[0], shift_ref[0]).astype(BF16)
    h_next = _adaln(xn_ref[...], g_ref[...], scale_ref[0], shift_ref[0]).astype(BF16)
    u_ext = jnp.dot(jnp.concatenate([h_prev, h, h_next], axis=0), w_ref[:, QKV_COLS:IN_COLS],
                    preferred_element_type=F32)
    tm = h.shape[0]
    before = jnp.where((i % tpb) == 0, 0.0, u_ext[HALO - 1:HALO])
    after = jnp.where((i % tpb) == tpb - 1, 0.0, u_ext[HALO + tm:HALO + tm + 1])
    t = _short_conv_rows(u_ext[HALO:HALO + tm], before, after, cw_ref[...], cb_ref[...])
    x0_ref[...] = t[:, 0:CONV_WIDTH]
    z_ref[...] = t[:, CONV_WIDTH:2 * CONV_WIDTH] * t[:, 2 * CONV_WIDTH:]


def _inproj_call(x, g, shift, scale, w_in, layer, qg, kg, e, rows_per_batch, hyena=None):
    m, d = x.shape
    tm = min(rows_per_batch, 1024)
    tpb = rows_per_batch // tm
    vec = pl.BlockSpec((1, 1, d), lambda i: (i // tpb, 0, 0))
    row = lambda w: pl.BlockSpec((tm, w), lambda i: (i, 0))
    in_specs = [
        row(d),
        _resident((1, d)), vec, vec,
        _layer_slice((d, IN_COLS), layer),
        _resident((1, ATTN_WIDTH)), _resident((1, KV_WIDTH)),
        _resident((MXU_DIM, MXU_DIM)),
    ]
    args = [x, g, shift, scale, w_in, qg, kg, e]
    uw = 3 * CONV_WIDTH
    if hyena is not None:
        cos, sin, conv_w, conv_b = hyena
        tab = pl.BlockSpec((tm, LANES), lambda i: (i % tpb, 0))
        hpt = tm // HALO
        last_halo = m // HALO - 1
        in_specs += [
            tab, tab,
            pl.BlockSpec((HALO, d), lambda i: (jnp.maximum(i * hpt - 1, 0), 0)),
            pl.BlockSpec((HALO, d), lambda i: (jnp.minimum((i + 1) * hpt, last_halo), 0)),
            _resident((3, uw)), _resident((1, uw)),
        ]
        args += [cos, sin, x, x, conv_w, conv_b]
        conv_shapes = [jax.ShapeDtypeStruct((m, CONV_WIDTH), F32)] * 2
        conv_specs = [row(CONV_WIDTH)] * 2
    else:
        conv_shapes = [jax.ShapeDtypeStruct((m, uw), F32)]
        conv_specs = [row(uw)]
    tk = _key_chunk(rows_per_batch)
    per = tk // tm
    return pl.pallas_call(
        functools.partial(_inproj_kernel, hyena is not None, tpb),
        out_shape=[
            jax.ShapeDtypeStruct((m, ATTN_WIDTH), BF16),
            jax.ShapeDtypeStruct((m // tk, KV_WIDTH, tk), BF16),
            jax.ShapeDtypeStruct((m, 2 * KV_WIDTH), BF16),
        ] + conv_shapes,
        grid=(m // tm,),
        in_specs=in_specs,
        out_specs=[row(ATTN_WIDTH), pl.BlockSpec((1, KV_WIDTH, tm), lambda i: (i // per, 0, i % per)),
                   row(2 * KV_WIDTH)] + conv_specs,
        compiler_params=_cparams(("arbitrary",)),
        name="adaln_in_projection",
    )(*args)


_T5_STEPS = (12, 16, 23, 32, 46, 64, 91)


def _bias_kernel(tab_ref, sink_ref, shift_ref, o_ref, sink_rows_ref, shift_rows_ref, sink_term_rows_ref):
    for h in range(N_Q_HEADS):
        rows = slice(h * BLOCK, (h + 1) * BLOCK)
        sink_rows_ref[rows, :] = jnp.full((BLOCK, LANES), sink_ref[h], F32)
        shift_rows_ref[rows, :] = jnp.full((BLOCK, LANES), shift_ref[h], F32)
        sink_term_rows_ref[rows, :] = jnp.exp(jnp.full((BLOCK, LANES), sink_ref[h] - shift_ref[h], F32))
    qi = lax.broadcasted_iota(jnp.int32, (BLOCK, 3 * BLOCK), 0)
    kj = lax.broadcasted_iota(jnp.int32, (BLOCK, 3 * BLOCK), 1)
    rel = kj - BLOCK - qi
    n = jnp.abs(rel)
    half = N_BUCKETS // 2
    max_exact = half // 2
    large = jnp.full(n.shape, max_exact, jnp.int32)
    for t in _T5_STEPS:
        large = large + (n >= t).astype(jnp.int32)
    bucket = jnp.where(rel > 0, half, 0) + jnp.where(n < max_exact, n, large)
    for h in range(N_Q_HEADS):
        bias = jnp.zeros(n.shape, F32)
        for b in range(N_BUCKETS):
            bias = jnp.where(bucket == b, tab_ref[b, h], bias)
        o_ref[h] = jnp.where(n <= WINDOW, bias, NEG_INF)


def _bias_call(rel_table, sink, shift):
    rows = jax.ShapeDtypeStruct((N_Q_HEADS * BLOCK, LANES), F32)
    smem = pl.BlockSpec(memory_space=pltpu.SMEM)
    return pl.pallas_call(
        _bias_kernel,
        out_shape=(jax.ShapeDtypeStruct((N_Q_HEADS, BLOCK, 3 * BLOCK), F32), rows, rows, rows),
        in_specs=[smem, smem, smem],
        name="t5_bias_tile",
    )(rel_table, sink, shift)


def _mixer_ab_kernel(tq, tpb, fixed_shift, x_ref, q_ref, kc_ref, kp_ref, kn_ref, vc_ref, vp_ref, vn_ref,
                     uc_ref, up_ref, un_ref, bias_ref, sink_ref, shift_ref, sink_term_ref, cw_ref, cb_ref,
                     wo_ref, gate_ref, o_ref, kbuf, vbuf, qs, att):
    i = pl.program_id(0)
    first = (i % tpb) == 0
    last = (i % tpb) == tpb - 1
    nblk = tq // BLOCK
    grows = GQA_GROUP * BLOCK
    kbuf[:, 0:BLOCK] = kp_ref[0]
    kbuf[:, BLOCK:BLOCK + tq] = kc_ref[0]
    kbuf[:, BLOCK + tq:] = kn_ref[0]
    vbuf[0:BLOCK] = vp_ref[...]
    vbuf[BLOCK:BLOCK + tq] = vc_ref[...]
    vbuf[BLOCK + tq:] = vn_ref[...]
    for n in range(nblk):
        for h in range(N_Q_HEADS):
            g, j = divmod(h, GQA_GROUP)
            qs[g, n, j * BLOCK:(j + 1) * BLOCK, :] = q_ref[n * BLOCK:(n + 1) * BLOCK, h * HEAD_DIM:(h + 1) * HEAD_DIM]
    col = lax.broadcasted_iota(jnp.int32, (grows, 3 * BLOCK), 1)
    for n in range(nblk):
        keys = slice(n * BLOCK, (n + 3) * BLOCK)
        for g in range(N_KV_HEADS):
            s = jnp.dot(qs[g, n], kbuf[g * HEAD_DIM:(g + 1) * HEAD_DIM, keys], preferred_element_type=F32)
            s = s + bias_ref[g * GQA_GROUP:(g + 1) * GQA_GROUP].reshape(grows, 3 * BLOCK)
            if n == 0:
                s = jnp.where(jnp.logical_and(first, col < BLOCK), NEG_INF, s)
            if n == nblk - 1:
                s = jnp.where(jnp.logical_and(last, col >= 2 * BLOCK), NEG_INF, s)
            grp = slice(g * grows, (g + 1) * grows)
            if fixed_shift:
                mx = shift_ref[grp, :]
                sink_term = sink_term_ref[grp, :]
            else:
                sk = sink_ref[grp, :]
                mx = jnp.maximum(jnp.broadcast_to(jnp.max(s, axis=-1, keepdims=True), sk.shape), sk)
                sink_term = jnp.exp(sk - mx)
            p = jnp.exp(s - jnp.tile(mx, (1, 3)))
            pv = jnp.dot(p.astype(BF16), vbuf[keys, g * LANES:(g + 1) * LANES], preferred_element_type=F32)
            o = pv / (pltpu.roll(pv, HEAD_DIM, 1) + sink_term)
            for j in range(GQA_GROUP):
                h = g * GQA_GROUP + j
                att[n * BLOCK:(n + 1) * BLOCK, h * HEAD_DIM:(h + 1) * HEAD_DIM] = o[j * BLOCK:(j + 1) * BLOCK, 0:HEAD_DIM]

    gb = uc_ref[:, 0:CONV_WIDTH]
    p = uc_ref[:, CONV_WIDTH:2 * CONV_WIDTH] * uc_ref[:, 2 * CONV_WIDTH:]
    p_before = jnp.where(first, 0.0, up_ref[7:8, CONV_WIDTH:2 * CONV_WIDTH] * up_ref[7:8, 2 * CONV_WIDTH:])
    p_after = jnp.where(last, 0.0, un_ref[0:1, CONV_WIDTH:2 * CONV_WIDTH] * un_ref[0:1, 2 * CONV_WIDTH:])
    conv = gb * _short_conv_rows(p, p_before, p_after, cw_ref[...], cb_ref[...])

    y = jnp.dot(att[...].astype(BF16), wo_ref[0:ATTN_WIDTH, :], preferred_element_type=F32)
    y = y + jnp.dot(conv.astype(BF16), wo_ref[ATTN_WIDTH:, :], preferred_element_type=F32)
    o_ref[...] = x_ref[...] + gate_ref[0] * y


def _mixer_ab_call(x, q, kt, v1, u, bias, row_tables, fixed_shift, conv_w, conv_b, w_out, layer, gate,
                   rows_per_batch):
    m, d = x.shape
    tq = min(rows_per_batch, 1024)
    tpb = rows_per_batch // tq
    r = tq // BLOCK
    nb = m // BLOCK
    n8 = m // 8
    cur = lambda w: pl.BlockSpec((tq, w), lambda i: (i, 0))
    prev_of = lambda i: jnp.maximum(i * r - 1, 0)
    next_of = lambda i: jnp.minimum((i + 1) * r, nb - 1)
    v_prev = pl.BlockSpec((BLOCK, 2 * KV_WIDTH), lambda i: (prev_of(i), 0))
    v_next = pl.BlockSpec((BLOCK, 2 * KV_WIDTH), lambda i: (next_of(i), 0))
    tk = kt.shape[2]
    k_cur = pl.BlockSpec((1, KV_WIDTH, tq), lambda i: (i // (tk // tq), 0, i % (tk // tq)))
    kpb = tk // BLOCK
    k_prev = pl.BlockSpec((1, KV_WIDTH, BLOCK), lambda i: (prev_of(i) // kpb, 0, prev_of(i) % kpb))
    k_next = pl.BlockSpec((1, KV_WIDTH, BLOCK), lambda i: (next_of(i) // kpb, 0, next_of(i) % kpb))
    uw = 3 * CONV_WIDTH
    row_table = _resident((N_Q_HEADS * BLOCK, LANES))
    return pl.pallas_call(
        functools.partial(_mixer_ab_kernel, tq, tpb, fixed_shift),
        out_shape=jax.ShapeDtypeStruct((m, d), F32),
        grid=(m // tq,),
        in_specs=[
            cur(d), cur(ATTN_WIDTH),
            k_cur, k_prev, k_next,
            cur(2 * KV_WIDTH), v_prev, v_next,
            cur(uw),
            pl.BlockSpec((8, uw), lambda i: (jnp.maximum(i * (tq // 8) - 1, 0), 0)),
            pl.BlockSpec((8, uw), lambda i: (jnp.minimum((i + 1) * (tq // 8), n8 - 1), 0)),
            _resident((N_Q_HEADS, BLOCK, 3 * BLOCK)),
            row_table, row_table, row_table,
            _resident((3, CONV_WIDTH)), _resident((1, CONV_WIDTH)),
            _layer_slice((d, d), layer),
            pl.BlockSpec((1, 1, d), lambda i: (i // tpb, 0, 0)),
        ],
        out_specs=cur(d),
        scratch_shapes=[
            pltpu.VMEM((KV_WIDTH, tq + 2 * BLOCK), BF16),
            pltpu.VMEM((tq + 2 * BLOCK, 2 * KV_WIDTH), BF16),
            pltpu.VMEM((N_KV_HEADS, r, GQA_GROUP * BLOCK, HEAD_DIM), BF16),
            pltpu.VMEM((tq, ATTN_WIDTH), F32),
        ],
        compiler_params=_cparams(("arbitrary",)),
        name="windowed_attn_shortconv_outproj",
    )(x, q, kt, kt, kt, v1, v1, v1, u, u, u, bias, *row_tables, conv_w, conv_b, w_out, gate)


def _key_chunk(seq):
    return min(seq, 1024)


def _dense_attn_kernel(tq, tk, q_ref, k_ref, v_ref, o_ref, qs, s_buf, p_buf, rmax_buf, alpha_buf,
                       m_ref, acc_ref):
    nc = k_ref.shape[0]
    for j in range(GQA_GROUP):
        qs[j * tq:(j + 1) * tq, :] = q_ref[:, j * HEAD_DIM:(j + 1) * HEAD_DIM]
    m_ref[...] = jnp.full(m_ref.shape, -jnp.inf, F32)
    acc_ref[...] = jnp.zeros(acc_ref.shape, F32)

    def scores(c):
        s = jnp.dot(qs[...], k_ref[c], preferred_element_type=F32)
        s_buf[...] = s
        rmax_buf[...] = jnp.broadcast_to(jnp.max(s, axis=-1, keepdims=True), rmax_buf.shape)

    def softmax():
        m_old = m_ref[...]
        m_new = jnp.maximum(m_old, rmax_buf[...])
        alpha_buf[...] = jnp.exp2(m_old - m_new)
        m_ref[...] = m_new
        p = jnp.exp2(s_buf[...] - jnp.tile(m_new, (1, tk // LANES)))
        p_buf[...] = p.astype(BF16)

    def weighted_values(c):
        start = c * tk if isinstance(c, int) else pl.multiple_of(c * tk, tk)
        pv = jnp.dot(p_buf[...], v_ref[pl.ds(start, tk), :], preferred_element_type=F32)
        acc_ref[...] = alpha_buf[...] * acc_ref[...] + pv

    def step(t):
        static = isinstance(t, int)
        weighted_values(t)
        if not static or t + 1 < nc:
            softmax()
        if not static or t + 2 < nc:
            scores(t + 2)

    scores(0)
    softmax()
    if nc > 1:
        scores(1)

    def steady(t, carry):
        step(t)
        return carry

    lax.fori_loop(0, max(nc - 2, 0), steady, 0)
    for t in range(max(nc - 2, 0), nc):
        step(t)

    acc = acc_ref[...]
    o = acc / pltpu.roll(acc, HEAD_DIM, 1)
    for j in range(GQA_GROUP):
        o_ref[:, j * HEAD_DIM:(j + 1) * HEAD_DIM] = o[j * tq:(j + 1) * tq, 0:HEAD_DIM].astype(o_ref.dtype)


def _dense_attn_bounded_kernel(tq, tk, bound_ref, q_ref, k_ref, v_ref, o_ref, qs, p_buf, acc_ref):
    nc = k_ref.shape[0]
    for j in range(GQA_GROUP):
        qs[j * tq:(j + 1) * tq, :] = q_ref[:, j * HEAD_DIM:(j + 1) * HEAD_DIM]
    shift = bound_ref[0]

    def probabilities(c):
        s = jnp.dot(qs[...], k_ref[c], preferred_element_type=F32)
        p_buf[...] = jnp.exp2(s - shift).astype(BF16)

    def weighted_values(c):
        start = c * tk if isinstance(c, int) else pl.multiple_of(c * tk, tk)
        return jnp.dot(p_buf[...], v_ref[pl.ds(start, tk), :], preferred_element_type=F32)

    probabilities(0)
    acc_ref[...] = weighted_values(0)
    if nc > 1:
        probabilities(1)

    def steady(t, carry):
        acc_ref[...] += weighted_values(t)
        probabilities(t + 1)
        return carry

    lax.fori_loop(1, nc - 1, steady, 0)
    if nc > 1:
        acc_ref[...] += weighted_values(nc - 1)

    acc = acc_ref[...]
    o = acc / pltpu.roll(acc, HEAD_DIM, 1)
    for j in range(GQA_GROUP):
        o_ref[:, j * HEAD_DIM:(j + 1) * HEAD_DIM] = o[j * tq:(j + 1) * tq, 0:HEAD_DIM].astype(o_ref.dtype)


MAX_FIXED_SHIFT = 50.0


def _dense_attn_call(q, kt, v1, logit_bound, bsz, seq):
    tq, tk = min(seq, 1024), _key_chunk(seq)
    nq = seq // tq
    nc = seq // tk
    rows = GQA_GROUP * tq
    gw = GQA_GROUP * HEAD_DIM
    stat = pltpu.VMEM((rows, LANES), F32)
    q_spec = pl.BlockSpec((tq, gw), lambda b, g, i: (b * nq + i, g))
    k_spec = pl.BlockSpec((nc, HEAD_DIM, tk), lambda b, g, i: (b, g, 0), pipeline_mode=pl.Buffered(1))
    v_spec = pl.BlockSpec((seq, LANES), lambda b, g, i: (b, g), pipeline_mode=pl.Buffered(1))
    common = dict(
        out_shape=jax.ShapeDtypeStruct((bsz * seq, ATTN_WIDTH), BF16),
        grid=(bsz, N_KV_HEADS, nq),
        out_specs=q_spec,
        compiler_params=_cparams(("arbitrary", "arbitrary", "arbitrary")),
    )

    def running_max():
        return pl.pallas_call(
            functools.partial(_dense_attn_kernel, tq, tk),
            in_specs=[q_spec, k_spec, v_spec],
            scratch_shapes=[
                pltpu.VMEM((rows, HEAD_DIM), BF16),
                pltpu.VMEM((rows, tk), F32), pltpu.VMEM((rows, tk), BF16),
                stat, stat, stat, stat,
            ],
            name="dense_gqa_attention", **common,
        )(q, kt, v1)

    def fixed_shift():
        return pl.pallas_call(
            functools.partial(_dense_attn_bounded_kernel, tq, tk),
            in_specs=[pl.BlockSpec(memory_space=pltpu.SMEM), q_spec, k_spec, v_spec],
            scratch_shapes=[pltpu.VMEM((rows, HEAD_DIM), BF16), pltpu.VMEM((rows, tk), BF16), stat],
            name="dense_gqa_attention_fixed_shift", **common,
        )(logit_bound.reshape(1), q, kt, v1)

    return lax.cond(logit_bound <= MAX_FIXED_SHIFT, fixed_shift, running_max)


def _hp_dot(a, b):
    return jnp.dot(a, b, preferred_element_type=F32, precision=lax.Precision.HIGHEST)


def _filter_trunk_kernel(z_ref, w1_ref, b1_ref, w2_ref, b2_ref, w3_ref, b3_ref, fq_ref, o_ref):
    fq = fq_ref[...]
    h = jnp.sin(fq * (_hp_dot(w1_ref[...], z_ref[...]) + b1_ref[...]))
    h = jnp.sin(fq * (_hp_dot(w2_ref[...], h) + b2_ref[...]))
    h = jnp.sin(fq * (_hp_dot(w3_ref[...], h) + b3_ref[...]))
    o_ref[...] = h.T


def _filter_trunk_call(zfeat_t, w1, b1, w2, b2, w3, b3, freq):
    rows = zfeat_t.shape[1]
    tr = min(rows, 2048)
    fw = HYENA_FILTER_WIDTH
    col = lambda v: v.reshape(fw, 1)
    return pl.pallas_call(
        _filter_trunk_kernel,
        out_shape=jax.ShapeDtypeStruct((rows, fw), F32),
        grid=(rows // tr,),
        in_specs=[
            pl.BlockSpec((HYENA_EMB_PAD, tr), lambda i: (0, i)),
            _resident((fw, HYENA_EMB_PAD)), _resident((fw, 1)),
            _resident((fw, fw)), _resident((fw, 1)),
            _resident((fw, fw)), _resident((fw, 1)),
            _resident((fw, 1)),
        ],
        out_specs=pl.BlockSpec((tr, fw), lambda i: (i, 0)),
        compiler_params=_cparams(("arbitrary",)),
        name="hyena_filter_trunk",
    )(zfeat_t, w1.T, col(b1), w2.T, col(b2), w3.T, col(b3), col(freq))


def _dft_stage1(src_ref, tab_ref, re_ref, im_ref, n_rows, k1p, n2_major=False):
    def body(n2, carry):
        xs = src_ref[pl.ds(n2, n_rows, stride=DFT_N2), :].astype(BF16)
        res = jnp.dot(tab_ref[n2], xs, preferred_element_type=F32)
        dst = pl.ds(pl.multiple_of(n2 * k1p, 8), k1p) if n2_major else pl.ds(n2, k1p, stride=DFT_N2)
        re_ref[dst, :] = res[0:k1p]
        im_ref[dst, :] = res[k1p:]
        return carry

    lax.fori_loop(0, DFT_N2, body, 0, unroll=8)


def _filter_spec_kernel(seq, k1p, h_ref, w4f_ref, w4b_ref, dl_ref, e1_ref, f2_ref,
                        kr_ref, ki_ref, kfull, bre, bim):
    n = 2 * seq
    chunk = min(seq, 1024)
    dl = dl_ref[...]
    live = seq // DFT_N2 + 1

    def fill_half(w_ref, backward):
        def fill(c, ss):
            r0 = pl.multiple_of(c * chunk, chunk)
            val = jnp.dot(h_ref[pl.ds(r0, chunk), :].astype(BF16), w_ref[...], preferred_element_type=F32)
            rows = r0 + lax.broadcasted_iota(jnp.int32, (chunk, 1), 0)
            lag = ((n - rows) if backward else rows).astype(F32)
            val = val * jnp.exp(-(lag * (1.0 / (seq - 1))) * dl)
            if backward:
                val = jnp.where(rows == seq, 0.0, val)
            kfull[pl.ds(r0, chunk), :] = val
            return ss + jnp.sum(val * val, axis=0, keepdims=True)
        return fill

    ss = lax.fori_loop(0, seq // chunk, fill_half(w4f_ref, False), jnp.zeros((1, dl.shape[1]), F32))
    ss = lax.fori_loop(seq // chunk, n // chunk, fill_half(w4b_ref, True), ss)
    norm = lax.rsqrt(ss + EPS)

    _dft_stage1(kfull, e1_ref, bre, bim, n // DFT_N2, k1p, n2_major=True)

    kr_ref[live * DFT_N2:, :] = jnp.zeros(((k1p - live) * DFT_N2, kr_ref.shape[1]), F32)
    ki_ref[live * DFT_N2:, :] = jnp.zeros(((k1p - live) * DFT_N2, ki_ref.shape[1]), F32)
    f2 = f2_ref[...]

    def stage2(a, carry):
        src = pl.ds(a, DFT_N2, stride=k1p)
        slab = jnp.concatenate([bre[src, :], bim[src, :]], axis=0).astype(BF16)
        xf = jnp.dot(f2, slab, preferred_element_type=F32)
        rows = pl.ds(pl.multiple_of(a * DFT_N2, DFT_N2), DFT_N2)
        kr_ref[rows, :] = xf[0:DFT_N2] * norm
        ki_ref[rows, :] = xf[DFT_N2:] * norm
        return carry

    lax.fori_loop(0, live, stage2, 0, unroll=8)


def _filter_spec_call(h3, w4, deltas, e1f, f2f, seq):
    n = 2 * seq
    k1p = e1f.shape[1] // 2
    ct = LANES
    nct = CONV_WIDTH // ct
    fw = HYENA_FILTER_WIDTH
    spec_rows = k1p * DFT_N2
    out = pl.BlockSpec((spec_rows, ct), lambda j: (0, j))
    return pl.pallas_call(
        functools.partial(_filter_spec_kernel, seq, k1p),
        out_shape=(jax.ShapeDtypeStruct((spec_rows, CONV_WIDTH), F32),
                   jax.ShapeDtypeStruct((spec_rows, CONV_WIDTH), F32)),
        grid=(nct,),
        in_specs=[
            _resident((n, fw)),
            pl.BlockSpec((fw, ct), lambda j: (0, j)),
            pl.BlockSpec((fw, ct), lambda j: (0, nct + j)),
            pl.BlockSpec((1, ct), lambda j: (0, j)),
            _resident(e1f.shape),
            _resident((2 * DFT_N2, 2 * DFT_N2)),
        ],
        out_specs=(out, out),
        scratch_shapes=[pltpu.VMEM((n, ct), F32), pltpu.VMEM((spec_rows, ct), F32),
                        pltpu.VMEM((spec_rows, ct), F32)],
        compiler_params=_cparams(("arbitrary",)),
        name="hyena_filter_spectrum",
    )(h3, w4, w4, deltas, e1f, f2f)


def _long_conv_kernel(seq, k1p, z_ref, x0_ref, kr_ref, ki_ref, skip_ref, e1_ref, f2f_ref, f2i_ref,
                      einv_ref, o_ref, br, bi):
    n1h = seq // DFT_N2
    _dft_stage1(z_ref, e1_ref, br, bi, n1h, k1p)

    f2f = f2f_ref[...]
    f2i = f2i_ref[...]

    def spectral(a, carry):
        r0 = pl.multiple_of(a * DFT_N2, DFT_N2)
        rows = pl.ds(r0, DFT_N2)
        slab = jnp.concatenate([br[rows, :], bi[rows, :]], axis=0).astype(BF16)
        xf = jnp.dot(f2f, slab, preferred_element_type=F32)
        xr, xi = xf[0:DFT_N2], xf[DFT_N2:]
        kr, ki = kr_ref[rows, :], ki_ref[rows, :]
        y = jnp.concatenate([xr * kr - xi * ki, xr * ki + xi * kr], axis=0).astype(BF16)
        d = jnp.dot(f2i, y, preferred_element_type=F32)
        br[rows, :] = d[0:DFT_N2]
        bi[rows, :] = d[DFT_N2:]
        return carry

    lax.fori_loop(0, n1h + 1, spectral, 0, unroll=8)

    skip = skip_ref[...]

    def synth(n2, carry):
        ds_spec = pl.ds(n2, k1p, stride=DFT_N2)
        d = jnp.concatenate([br[ds_spec, :], bi[ds_spec, :]], axis=0).astype(BF16)
        o_ref[pl.ds(n2, n1h, stride=DFT_N2), :] = jnp.dot(einv_ref[n2], d, preferred_element_type=F32)
        return carry

    lax.fori_loop(0, DFT_N2, synth, 0, unroll=8)

    chunk = min(seq, 512)

    def gate(c, carry):
        rows = pl.ds(pl.multiple_of(c * chunk, chunk), chunk)
        o_ref[rows, :] = x0_ref[rows, :] * (o_ref[rows, :] + skip * z_ref[rows, :])
        return carry

    lax.fori_loop(0, seq // chunk, gate, 0)


def _long_conv_call(z, x0, kr, ki, skip, e1, f2f, f2i, einv, bsz, seq):
    ct = LANES
    nct = CONV_WIDTH // ct
    k1p = e1.shape[1] // 2
    spec_rows = k1p * DFT_N2
    big = lambda: pl.BlockSpec((seq, ct), lambda j, b: (b, j), pipeline_mode=pl.Buffered(1))
    spec = lambda: pl.BlockSpec((spec_rows, ct), lambda j, b: (0, j), pipeline_mode=pl.Buffered(1))
    return pl.pallas_call(
        functools.partial(_long_conv_kernel, seq, k1p),
        out_shape=jax.ShapeDtypeStruct((bsz * seq, CONV_WIDTH), F32),
        grid=(nct, bsz),
        in_specs=[
            pl.BlockSpec((seq, ct), lambda j, b: (b, j)), big(), spec(), spec(),
            pl.BlockSpec((1, ct), lambda j, b: (0, j)),
            _resident(e1.shape),
            _resident((2 * DFT_N2, 2 * DFT_N2)), _resident((2 * DFT_N2, 2 * DFT_N2)),
            _resident(einv.shape),
        ],
        out_specs=pl.BlockSpec((seq, ct), lambda j, b: (b, j)),
        scratch_shapes=[pltpu.VMEM((spec_rows, ct), F32), pltpu.VMEM((spec_rows, ct), F32)],
        compiler_params=_cparams(("arbitrary", "arbitrary")),
        name="hyena_long_conv",
    )(z, x0, kr, ki, skip, e1, f2f, f2i, einv)


def _rope_tables(seq):
    t = jnp.arange(seq)
    row = (t // GRID_W).astype(F32)
    col = (t % GRID_W).astype(F32)
    half = HEAD_DIM // 2
    inv = ROPE_THETA ** (-jnp.arange(0, half, 2, dtype=F32) / half)
    ang = jnp.concatenate([row[:, None] * inv, col[:, None] * inv], axis=-1)
    ang = jnp.repeat(ang, 2, axis=-1)
    sign = jnp.where(jnp.arange(HEAD_DIM) % 2 == 0, -1.0, 1.0).astype(F32)
    reps = LANES // HEAD_DIM
    return jnp.tile(jnp.cos(ang), (1, reps)), jnp.tile(jnp.sin(ang) * sign, (1, reps))


def _dft_tables(seq):
    n = 2 * seq
    n1 = n // DFT_N2
    k1 = n1 // 2 + 1
    k1p = -(-k1 // 8) * 8
    a = jnp.arange(k1p)
    live = (a < k1).astype(F32)
    ang1 = ((a[:, None] * jnp.arange(n1)[None, :]) % n1).astype(F32) * (2.0 * math.pi / n1)
    ang2 = (jnp.arange(DFT_N2)[:, None] * a[None, :]).astype(F32) * (2.0 * math.pi / n)
    c1, s1 = jnp.cos(ang1) * live[:, None], jnp.sin(ang1) * live[:, None]
    c2, s2 = jnp.cos(ang2), jnp.sin(ang2)
    cos_a = c1[None] * c2[:, :, None] - s1[None] * s2[:, :, None]
    sin_a = s1[None] * c2[:, :, None] + c1[None] * s2[:, :, None]
    e1_full = jnp.concatenate([cos_a, -sin_a], axis=1).astype(BF16)
    e1_half = e1_full[:, :, :n1 // 2]
    wgt = jnp.where((a == 0) | (a == n1 // 2), 1.0, 2.0) / n
    c1t, s1t = (c1 * wgt[:, None]).T[:n1 // 2], (s1 * wgt[:, None]).T[:n1 // 2]
    cos_s = c1t[None] * c2[:, None, :] - s1t[None] * s2[:, None, :]
    sin_s = s1t[None] * c2[:, None, :] + c1t[None] * s2[:, None, :]
    einv = jnp.concatenate([cos_s, -sin_s], axis=2).astype(BF16)
    kk = jnp.arange(DFT_N2)
    phi = ((kk[:, None] * kk[None, :]) % DFT_N2).astype(F32) * (2.0 * math.pi / DFT_N2)
    ci, si = jnp.cos(phi), jnp.sin(phi)
    f2f = jnp.block([[ci, si], [-si, ci]]).astype(BF16)
    f2i = jnp.block([[ci, -si], [si, ci]]).astype(BF16)
    return e1_full, e1_half, einv, f2f, f2i


def _filter_features_t(seq):
    j = jnp.arange(2 * seq)
    lag = jnp.where(j < seq, j, jnp.where(j == seq, 0, 2 * seq - j)).astype(F32)[None, :]
    t = lag / (seq - 1)
    bands = (HYENA_EMB - 1) // 2
    w = (2.0 * math.pi / seq) * lag
    fr = jnp.linspace(1e-4, bands - 1, bands, dtype=F32)[:, None]
    pad = jnp.zeros((HYENA_EMB_PAD - HYENA_EMB, 2 * seq), F32)
    return jnp.concatenate([t, jnp.cos(fr * w), -jnp.sin(fr * w), pad], axis=0)


def kernel(x, c, rel_table, norm_g, w_mod, b_mod, w_in, w_out, ffn_w1, ffn_w3, ffn_w2, a_qk_g, a_sink,
           b_conv_w, b_conv_b, c_qk_g, d_conv_w, d_conv_b, d_f_w1, d_f_b1, d_f_w2, d_f_b2, d_f_w3,
           d_f_b3, d_f_w4, d_f_freq, d_skip):
    bsz, seq, d = x.shape
    depth = w_mod.shape[0]
    m_rows = bsz * seq
    mod =_mod_call(c, w_mod, b_mod).reshape(depth, bsz, 3, 3, 1, d)
    e_blk = jnp.kron(jnp.eye(MXU_DIM // HEAD_DIM, dtype=F32), jnp.ones((HEAD_DIM, HEAD_DIM), F32)).astype(BF16)

    w1b, w3b = _to_bf16_from_transposed(ffn_w1), _to_bf16_from_transposed(ffn_w3)
    w2b, w_in_b, w_out_b = (w.astype(BF16) for w in (ffn_w2, w_in, w_out))

    def ffn(xf, l, which, sub, mixer=None):
        return _ffn_call(xf, norm_g[l, sub][None], mod[l, :, sub, 0], mod[l, :, sub, 1], mod[l, :, sub, 2],
                         w1b, w3b, w2b, l, which, seq, mixer)

    xf = x.reshape(m_rows, d)
    for l in range(depth):
        j = l // 2
        xf = ffn(xf, l, 0, 0)
        even = l % 2 == 0
        qk_g = a_qk_g[j] if even else c_qk_g[j]
        hyena = None if even else _rope_tables(seq) + (d_conv_w[j], d_conv_b[j][None])
        q, k, v, *conv_in = _inproj_call(
            xf, norm_g[l, 1][None], mod[l, :, 1, 0], mod[l, :, 1, 1], w_in_b, l,
            jnp.tile(qk_g[0], N_Q_HEADS)[None], jnp.tile(qk_g[1], N_KV_HEADS)[None], e_blk, seq, hyena)
        gate = mod[l, :, 1, 2]
        if even:
            (u,) = conv_in
            qk_bound = (1.01 * HEAD_DIM ** 0.5) * jnp.max(jnp.abs(qk_g[0])) * jnp.max(jnp.abs(qk_g[1]))
            shift = jnp.maximum(qk_bound + jnp.max(rel_table, axis=0), a_sink[j])
            worst_gap = jnp.max(shift + qk_bound - rel_table[0])
            bias, *row_tables = _bias_call(rel_table, a_sink[j], shift)
            mixer_ab = lambda fixed: _mixer_ab_call(xf, q, k, v, u, bias, row_tables, fixed, b_conv_w[j],
                                                    b_conv_b[j][None], w_out_b, l, gate, seq)
            xf = lax.cond(worst_gap <= MAX_FIXED_SHIFT_GAP, lambda: mixer_ab(True), lambda: mixer_ab(False))
            pending = None
        else:
            x0, z = conv_in
            logit_bound = (1.01 * LOG2_E * HEAD_DIM ** 0.5) * jnp.max(jnp.abs(qk_g[0])) * jnp.max(jnp.abs(qk_g[1]))
            att = _dense_attn_call(q, k, v, logit_bound, bsz, seq)
            e1_full, e1_half, einv, f2f, f2i = _dft_tables(seq)
            fw1 = jnp.pad(d_f_w1[j], ((0, HYENA_EMB_PAD - HYENA_EMB), (0, 0)))
            h3 = _filter_trunk_call(_filter_features_t(seq), fw1, d_f_b1[j], d_f_w2[j], d_f_b2[j],
                                    d_f_w3[j], d_f_b3[j], d_f_freq[j])
            deltas = jnp.abs(jnp.linspace(HYENA_MIN_DECAY, HYENA_MAX_DECAY, CONV_WIDTH, dtype=F32))[None]
            kr, ki = _filter_spec_call(h3, d_f_w4[j].astype(BF16), deltas, e1_full, f2f, seq)
            y = _long_conv_call(z, x0, kr, ki, d_skip[j][None], e1_half, f2f, f2i, einv, bsz, seq)
            pending = (att, y, w_out_b, gate)
        xf = ffn(xf, l, 1, 2, pending)
    return xf.reshape(bsz, seq, d)
```

```python
import functools
import math

import jax
import jax.numpy as jnp
import numpy as np
from jax import lax
from jax.experimental import pallas as pl
from jax.experimental.pallas import tpu as pltpu

D_MODEL = 1024
HEAD_DIM = 64
N_Q_HEADS = 8
N_KV_HEADS = 2
GQA_GROUP = N_Q_HEADS // N_KV_HEADS
ATTN_WIDTH = N_Q_HEADS * HEAD_DIM
KV_WIDTH = N_KV_HEADS * HEAD_DIM
QKV_COLS = ATTN_WIDTH + 2 * KV_WIDTH
CONV_WIDTH = D_MODEL - ATTN_WIDTH
IN_COLS = QKV_COLS + 3 * CONV_WIDTH
D_FF = 2752
BLOCK = 128
WINDOW = 128
N_BUCKETS = 32
MAX_DISTANCE = 128
GRID_W = 64
ROPE_THETA = 10000.0
HYENA_EMB = 33
HYENA_FILTER_WIDTH = 64
HYENA_EMB_PAD = 64
HYENA_MIN_DECAY = math.log(1e-2) / 0.3
HYENA_MAX_DECAY = math.log(1e-2) / 1.5
EPS = 1e-6
NEG_INF = -1e30
LOG2_E = 1.4426950408889634
MAX_FIXED_SHIFT_GAP = 80.0

LANES = 128
HALO = 16
MXU_DIM = 256
FF_CHUNK = MXU_DIM
VMEM_LIMIT = 56 * 1024 * 1024
DFT_N2 = 128
SPECTRUM_PITCH = DFT_N2 + 8

BF16 = jnp.bfloat16
F32 = jnp.float32


def _cparams(sem):
    return pltpu.CompilerParams(dimension_semantics=sem, vmem_limit_bytes=VMEM_LIMIT)


def _resident(shape):
    nd = len(shape)
    return pl.BlockSpec(shape, lambda *_: (0,) * nd, pipeline_mode=pl.Buffered(1))


def _layer_slice(shape, layer):
    return pl.BlockSpec((None,) + shape, lambda *_: (layer, 0, 0), pipeline_mode=pl.Buffered(1))


def _adaln(x, g, scale, shift):
    y = x * lax.rsqrt(jnp.mean(x * x, axis=-1, keepdims=True) + EPS) * g
    return y * (1.0 + scale) + shift


def _mod_kernel(c_ref, w_ref, b_ref, o_ref):
    w = w_ref[0]
    for b in range(c_ref.shape[0]):
        cc = c_ref[b]
        cond = cc * jax.nn.sigmoid(cc)
        o_ref[0, b:b + 1, :] = jnp.sum(w * cond, axis=0, keepdims=True) + b_ref[0]


def _mod_call(c, w_mod, b_mod):
    depth, d, n = w_mod.shape
    bsz = c.shape[0]
    tn = 1152 if n % 1152 == 0 else 512
    return pl.pallas_call(
        _mod_kernel,
        out_shape=jax.ShapeDtypeStruct((depth, bsz, n), F32),
        grid=(depth, n // tn),
        in_specs=[
            pl.BlockSpec((bsz, d, 1), lambda l, j: (0, 0, 0)),
            pl.BlockSpec((1, d, tn), lambda l, j: (l, 0, j)),
            pl.BlockSpec((1, 1, tn), lambda l, j: (l, 0, j)),
        ],
        out_specs=pl.BlockSpec((1, bsz, tn), lambda l, j: (l, 0, j)),
        compiler_params=_cparams(("arbitrary", "arbitrary")),
        name="adaln_modulation",
    )(c.reshape(bsz, d, 1), w_mod, b_mod.reshape(depth, 1, n))


def _cast_transpose_kernel(wt_ref, o_ref):
    cols = wt_ref.shape[0]
    starts = list(range(0, cols - MXU_DIM, MXU_DIM)) + [cols - MXU_DIM]
    for c0 in starts:
        o_ref[:, c0:c0 + MXU_DIM] = wt_ref[c0:c0 + MXU_DIM, :].T.astype(o_ref.dtype)


def _to_bf16_from_transposed(w):
    lead, (rows, cols) = w.shape[:-2], w.shape[-2:]
    none = (None,) * len(lead)
    return pl.pallas_call(
        _cast_transpose_kernel,
        out_shape=jax.ShapeDtypeStruct(w.shape, BF16),
        grid=lead,
        in_specs=[pl.BlockSpec(none + (cols, rows), lambda *idx: idx + (0, 0))],
        out_specs=pl.BlockSpec(none + (rows, cols), lambda *idx: idx + (0, 0)),
        compiler_params=_cparams(("arbitrary",) * len(lead)),
        name="weights_to_bf16_transposing",
    )(jnp.swapaxes(w, -1, -2))


def _ffn_kernel(mixer_update, x_ref, g_ref, shift_ref, scale_ref, gate_ref, w1_ref, w3_ref, w2_ref, *rest):
    x = x_ref[...]
    if mixer_update:
        att_ref, y_ref, wo_ref, mgate_ref, o_ref = rest
        upd = jnp.dot(att_ref[...], wo_ref[0:ATTN_WIDTH, :], preferred_element_type=F32)
        upd = upd + jnp.dot(y_ref[...].astype(BF16), wo_ref[ATTN_WIDTH:, :], preferred_element_type=F32)
        x = x + mgate_ref[0] * upd
    else:
        (o_ref,) = rest
    h = _adaln(x, g_ref[...], scale_ref[0], shift_ref[0]).astype(BF16)
    acc = jnp.zeros(x.shape, F32)
    for c0 in range(0, D_FF, FF_CHUNK):
        cols = slice(c0, min(c0 + FF_CHUNK, D_FF))
        a = jnp.dot(h, w1_ref[:, cols], preferred_element_type=F32)
        b = jnp.dot(h, w3_ref[:, cols], preferred_element_type=F32)
        act = (a * jax.nn.sigmoid(a) * b).astype(BF16)
        acc = acc + jnp.dot(act, w2_ref[cols, :], preferred_element_type=F32)
    o_ref[...] = x + (0.5 * gate_ref[0]) * acc


def _ffn_call(x, g, shift, scale, gate, w1, w3, w2, layer, which, rows_per_batch, mixer=None):
    m, d = x.shape
    tm = min(rows_per_batch, 1024)
    tpb = rows_per_batch // tm
    row = lambda w: pl.BlockSpec((tm, w), lambda i: (i, 0))
    vec = pl.BlockSpec((1, 1, d), lambda i: (i // tpb, 0, 0))
    weight = lambda r, c: pl.BlockSpec((None, None, r, c), lambda i: (layer, which, 0, 0),
                                       pipeline_mode=pl.Buffered(1))
    in_specs = [row(d), _resident((1, d)), vec, vec, vec, weight(d, D_FF), weight(d, D_FF), weight(D_FF, d)]
    args = [x, g, shift, scale, gate, w1, w3, w2]
    if mixer is not None:
        in_specs += [row(ATTN_WIDTH), row(CONV_WIDTH), _layer_slice((d, d), layer), vec]
        args += list(mixer)
    return pl.pallas_call(
        functools.partial(_ffn_kernel, mixer is not None),
        out_shape=jax.ShapeDtypeStruct((m, d), F32),
        grid=(m // tm,),
        in_specs=in_specs,
        out_specs=row(d),
        compiler_params=_cparams(("arbitrary",)),
        name="adaln_swiglu_ffn",
    )(*args)


def _group_norm_scale(v, e_ref):
    sq = v * v
    hi = sq.astype(BF16)
    lo = (sq - hi.astype(F32)).astype(BF16)
    w = min(v.shape[1], MXU_DIM)
    e = e_ref[...]
    ss = jnp.concatenate(
        [jnp.dot(hi[:, c:c + w], e[0:w, 0:w], preferred_element_type=F32)
         + jnp.dot(lo[:, c:c + w], e[0:w, 0:w], preferred_element_type=F32) for c in range(0, v.shape[1], w)],
        axis=1)
    return lax.rsqrt(ss * (1.0 / HEAD_DIM) + EPS)


def _rope128(v, cos, sin_signed, even_lane):
    partner = jnp.where(even_lane, pltpu.roll(v, LANES - 1, 1), pltpu.roll(v, 1, 1))
    return v * cos + partner * sin_signed


def _short_conv_rows(u, before, after, cw, cb):
    rows = u.shape[0]
    r = lax.broadcasted_iota(jnp.int32, u.shape, 0)
    um1 = jnp.where(r == 0, before, pltpu.roll(u, 1, 0))
    up1 = jnp.where(r == rows - 1, after, pltpu.roll(u, rows - 1, 0))
    return cw[0:1] * um1 + cw[1:2] * u + cw[2:3] * up1 + cb


def _inproj_kernel(rope, tpb, x_ref, g_ref, shift_ref, scale_ref, w_ref, qg_ref, kg_ref, e_ref, *rest):
    if rope:
        cos_ref, sin_ref, xp_ref, xn_ref, cw_ref, cb_ref, q_ref, k_ref, v_ref, x0_ref, z_ref = rest
    else:
        q_ref, k_ref, v_ref, u_ref = rest
    h = _adaln(x_ref[...], g_ref[...], scale_ref[0], shift_ref[0]).astype(BF16)
    q = jnp.dot(h, w_ref[:, 0:ATTN_WIDTH], preferred_element_type=F32)
    k = jnp.dot(h, w_ref[:, ATTN_WIDTH:ATTN_WIDTH + KV_WIDTH], preferred_element_type=F32)
    q = q * _group_norm_scale(q, e_ref) * qg_ref[...]
    k = k * _group_norm_scale(k, e_ref) * kg_ref[...]
    if rope:
        cos = cos_ref[...]
        sin = sin_ref[...]
        even = (lax.broadcasted_iota(jnp.int32, cos.shape, 1) % 2) == 0
        q = jnp.concatenate(
            [_rope128(q[:, j * LANES:(j + 1) * LANES], cos, sin, even) for j in range(ATTN_WIDTH // LANES)],
            axis=1)
        k = _rope128(k, cos, sin, even)
    q_ref[...] = (q * (HEAD_DIM ** -0.5 * (LOG2_E if rope else 1.0))).astype(BF16)
    v = jnp.dot(h, w_ref[:, ATTN_WIDTH + KV_WIDTH:QKV_COLS], preferred_element_type=F32)
    k_ref[0] = k.T.astype(BF16)
    low = lax.broadcasted_iota(jnp.int32, v.shape, 1) < HEAD_DIM
    v_ref[...] = jnp.concatenate(
        [jnp.where(low, v, 1.0), jnp.where(low, pltpu.roll(v, HEAD_DIM, 1), 1.0)], axis=1).astype(BF16)
    if not rope:
        u_ref[...] = jnp.dot(h, w_ref[:, QKV_COLS:IN_COLS], preferred_element_type=F32)
        return
    i = pl.program_id(0)
    h_prev = _adaln(xp_ref[...], g_ref[...], scale_ref[0], shift_ref[0]).astype(BF16)
    h_next = _adaln(xn_ref[...], g_ref[...], scale_ref[0], shift_ref[0]).astype(BF16)
    u_ext = jnp.dot(jnp.concatenate([h_prev, h, h_next], axis=0), w_ref[:, QKV_COLS:IN_COLS],
                    preferred_element_type=F32)
    tm = h.shape[0]
    before = jnp.where((i % tpb) == 0, 0.0, u_ext[HALO - 1:HALO])
    after = jnp.where((i % tpb) == tpb - 1, 0.0, u_ext[HALO + tm:HALO + tm + 1])
    t = _short_conv_rows(u_ext[HALO:HALO + tm], before, after, cw_ref[...], cb_ref[...])
    x0_ref[...] = t[:, 0:CONV_WIDTH]
    z_ref[...] = t[:, CONV_WIDTH:2 * CONV_WIDTH] * t[:, 2 * CONV_WIDTH:]


def _inproj_call(x, g, shift, scale, w_in, layer, qg, kg, e, rows_per_batch, hyena=None):
    m, d = x.shape
    tm = min(rows_per_batch, 1024)
    tpb = rows_per_batch // tm
    vec = pl.BlockSpec((1, 1, d), lambda i: (i // tpb, 0, 0))
    row = lambda w: pl.BlockSpec((tm, w), lambda i: (i, 0))
    in_specs = [
        row(d),
        _resident((1, d)), vec, vec,
        _layer_slice((d, IN_COLS), layer),
        _resident((1, ATTN_WIDTH)), _resident((1, KV_WIDTH)),
        _resident((MXU_DIM, MXU_DIM)),
    ]
    args = [x, g, shift, scale, w_in, qg, kg, e]
    uw = 3 * CONV_WIDTH
    if hyena is not None:
        cos, sin, conv_w, conv_b = hyena
        tab = pl.BlockSpec((tm, LANES), lambda i: (i % tpb, 0))
        hpt = tm // HALO
        last_halo = m // HALO - 1
        in_specs += [
            tab, tab,
            pl.BlockSpec((HALO, d), lambda i: (jnp.maximum(i * hpt - 1, 0), 0)),
            pl.BlockSpec((HALO, d), lambda i: (jnp.minimum((i + 1) * hpt, last_halo), 0)),
            _resident((3, uw)), _resident((1, uw)),
        ]
        args += [cos, sin, x, x, conv_w, conv_b]
        conv_shapes = [jax.ShapeDtypeStruct((m, CONV_WIDTH), F32)] * 2
        conv_specs = [row(CONV_WIDTH)] * 2
    else:
        conv_shapes = [jax.ShapeDtypeStruct((m, uw), F32)]
        conv_specs = [row(uw)]
    tk = _key_chunk(rows_per_batch)
    per = tk // tm
    return pl.pallas_call(
        functools.partial(_inproj_kernel, hyena is not None, tpb),
        out_shape=[
            jax.ShapeDtypeStruct((m, ATTN_WIDTH), BF16),
            jax.ShapeDtypeStruct((m // tk, KV_WIDTH, tk), BF16),
            jax.ShapeDtypeStruct((m, 2 * KV_WIDTH), BF16),
        ] + conv_shapes,
        grid=(m // tm,),
        in_specs=in_specs,
        out_specs=[row(ATTN_WIDTH), pl.BlockSpec((1, KV_WIDTH, tm), lambda i: (i // per, 0, i % per)),
                   row(2 * KV_WIDTH)] + conv_specs,
        compiler_params=_cparams(("arbitrary",)),
        name="adaln_in_projection",
    )(*args)


_T5_STEPS = (12, 16, 23, 32, 46, 64, 91)


def _bias_kernel(tab_ref, sink_ref, shift_ref, o_ref, sink_rows_ref, shift_rows_ref, sink_term_rows_ref):
    for h in range(N_Q_HEADS):
        rows = slice(h * BLOCK, (h + 1) * BLOCK)
        sink_rows_ref[rows, :] = jnp.full((BLOCK, LANES), sink_ref[h], F32)
        shift_rows_ref[rows, :] = jnp.full((BLOCK, LANES), shift_ref[h], F32)
        sink_term_rows_ref[rows, :] = jnp.exp(jnp.full((BLOCK, LANES), sink_ref[h] - shift_ref[h], F32))
    qi = lax.broadcasted_iota(jnp.int32, (BLOCK, 3 * BLOCK), 0)
    kj = lax.broadcasted_iota(jnp.int32, (BLOCK, 3 * BLOCK), 1)
    rel = kj - BLOCK - qi
    n = jnp.abs(rel)
    half = N_BUCKETS // 2
    max_exact = half // 2
    large = jnp.full(n.shape, max_exact, jnp.int32)
    for t in _T5_STEPS:
        large = large + (n >= t).astype(jnp.int32)
    bucket = jnp.where(rel > 0, half, 0) + jnp.where(n < max_exact, n, large)
    for h in range(N_Q_HEADS):
        bias = jnp.zeros(n.shape, F32)
        for b in range(N_BUCKETS):
            bias = jnp.where(bucket == b, tab_ref[b, h], bias)
        o_ref[h] = jnp.where(n <= WINDOW, bias, NEG_INF)


def _bias_call(rel_table, sink, shift):
    rows = jax.ShapeDtypeStruct((N_Q_HEADS * BLOCK, LANES), F32)
    smem = pl.BlockSpec(memory_space=pltpu.SMEM)
    return pl.pallas_call(
        _bias_kernel,
        out_shape=(jax.ShapeDtypeStruct((N_Q_HEADS, BLOCK, 3 * BLOCK), F32), rows, rows, rows),
        in_specs=[smem, smem, smem],
        name="t5_bias_tile",
    )(rel_table, sink, shift)


def _mixer_ab_kernel(tq, tpb, fixed_shift, x_ref, q_ref, kc_ref, kp_ref, kn_ref, vc_ref, vp_ref, vn_ref,
                     uc_ref, up_ref, un_ref, bias_ref, sink_ref, shift_ref, sink_term_ref, cw_ref, cb_ref,
                     wo_ref, gate_ref, o_ref, kbuf, vbuf, qs, att):
    i = pl.program_id(0)
    first = (i % tpb) == 0
    last = (i % tpb) == tpb - 1
    nblk = tq // BLOCK
    grows = GQA_GROUP * BLOCK
    kbuf[:, 0:BLOCK] = kp_ref[0]
    kbuf[:, BLOCK:BLOCK + tq] = kc_ref[0]
    kbuf[:, BLOCK + tq:] = kn_ref[0]
    vbuf[0:BLOCK] = vp_ref[...]
    vbuf[BLOCK:BLOCK + tq] = vc_ref[...]
    vbuf[BLOCK + tq:] = vn_ref[...]
    for n in range(nblk):
        for h in range(N_Q_HEADS):
            g, j = divmod(h, GQA_GROUP)
            qs[g, n, j * BLOCK:(j + 1) * BLOCK, :] = q_ref[n * BLOCK:(n + 1) * BLOCK, h * HEAD_DIM:(h + 1) * HEAD_DIM]
    col = lax.broadcasted_iota(jnp.int32, (grows, 3 * BLOCK), 1)
    for n in range(nblk):
        keys = slice(n * BLOCK, (n + 3) * BLOCK)
        for g in range(N_KV_HEADS):
            s = jnp.dot(qs[g, n], kbuf[g * HEAD_DIM:(g + 1) * HEAD_DIM, keys], preferred_element_type=F32)
            s = s + bias_ref[g * GQA_GROUP:(g + 1) * GQA_GROUP].reshape(grows, 3 * BLOCK)
            if n == 0:
                s = jnp.where(jnp.logical_and(first, col < BLOCK), NEG_INF, s)
            if n == nblk - 1:
                s = jnp.where(jnp.logical_and(last, col >= 2 * BLOCK), NEG_INF, s)
            grp = slice(g * grows, (g + 1) * grows)
            if fixed_shift:
                mx = shift_ref[grp, :]
                sink_term = sink_term_ref[grp, :]
            else:
                sk = sink_ref[grp, :]
                mx = jnp.maximum(jnp.broadcast_to(jnp.max(s, axis=-1, keepdims=True), sk.shape), sk)
                sink_term = jnp.exp(sk - mx)
            p = jnp.exp(s - jnp.tile(mx, (1, 3)))
            pv = jnp.dot(p.astype(BF16), vbuf[keys, g * LANES:(g + 1) * LANES], preferred_element_type=F32)
            o = pv / (pltpu.roll(pv, HEAD_DIM, 1) + sink_term)
            for j in range(GQA_GROUP):
                h = g * GQA_GROUP + j
                att[n * BLOCK:(n + 1) * BLOCK, h * HEAD_DIM:(h + 1) * HEAD_DIM] = o[j * BLOCK:(j + 1) * BLOCK, 0:HEAD_DIM]

    gb = uc_ref[:, 0:CONV_WIDTH]
    p = uc_ref[:, CONV_WIDTH:2 * CONV_WIDTH] * uc_ref[:, 2 * CONV_WIDTH:]
    p_before = jnp.where(first, 0.0, up_ref[7:8, CONV_WIDTH:2 * CONV_WIDTH] * up_ref[7:8, 2 * CONV_WIDTH:])
    p_after = jnp.where(last, 0.0, un_ref[0:1, CONV_WIDTH:2 * CONV_WIDTH] * un_ref[0:1, 2 * CONV_WIDTH:])
    conv = gb * _short_conv_rows(p, p_before, p_after, cw_ref[...], cb_ref[...])

    y = jnp.dot(att[...].astype(BF16), wo_ref[0:ATTN_WIDTH, :], preferred_element_type=F32)
    y = y + jnp.dot(conv.astype(BF16), wo_ref[ATTN_WIDTH:, :], preferred_element_type=F32)
    o_ref[...] = x_ref[...] + gate_ref[0] * y


def _mixer_ab_call(x, q, kt, v1, u, bias, row_tables, fixed_shift, conv_w, conv_b, w_out, layer, gate,
                   rows_per_batch):
    m, d = x.shape
    tq = min(rows_per_batch, 1024)
    tpb = rows_per_batch // tq
    r = tq // BLOCK
    nb = m // BLOCK
    n8 = m // 8
    cur = lambda w: pl.BlockSpec((tq, w), lambda i: (i, 0))
    prev_of = lambda i: jnp.maximum(i * r - 1, 0)
    next_of = lambda i: jnp.minimum((i + 1) * r, nb - 1)
    v_prev = pl.BlockSpec((BLOCK, 2 * KV_WIDTH), lambda i: (prev_of(i), 0))
    v_next = pl.BlockSpec((BLOCK, 2 * KV_WIDTH), lambda i: (next_of(i), 0))
    tk = kt.shape[2]
    k_cur = pl.BlockSpec((1, KV_WIDTH, tq), lambda i: (i // (tk // tq), 0, i % (tk // tq)))
    kpb = tk // BLOCK
    k_prev = pl.BlockSpec((1, KV_WIDTH, BLOCK), lambda i: (prev_of(i) // kpb, 0, prev_of(i) % kpb))
    k_next = pl.BlockSpec((1, KV_WIDTH, BLOCK), lambda i: (next_of(i) // kpb, 0, next_of(i) % kpb))
    uw = 3 * CONV_WIDTH
    row_table = _resident((N_Q_HEADS * BLOCK, LANES))
    return pl.pallas_call(
        functools.partial(_mixer_ab_kernel, tq, tpb, fixed_shift),
        out_shape=jax.ShapeDtypeStruct((m, d), F32),
        grid=(m // tq,),
        in_specs=[
            cur(d), cur(ATTN_WIDTH),
            k_cur, k_prev, k_next,
            cur(2 * KV_WIDTH), v_prev, v_next,
            cur(uw),
            pl.BlockSpec((8, uw), lambda i: (jnp.maximum(i * (tq // 8) - 1, 0), 0)),
            pl.BlockSpec((8, uw), lambda i: (jnp.minimum((i + 1) * (tq // 8), n8 - 1), 0)),
            _resident((N_Q_HEADS, BLOCK, 3 * BLOCK)),
            row_table, row_table, row_table,
            _resident((3, CONV_WIDTH)), _resident((1, CONV_WIDTH)),
            _layer_slice((d, d), layer),
            pl.BlockSpec((1, 1, d), lambda i: (i // tpb, 0, 0)),
        ],
        out_specs=cur(d),
        scratch_shapes=[
            pltpu.VMEM((KV_WIDTH, tq + 2 * BLOCK), BF16),
            pltpu.VMEM((tq + 2 * BLOCK, 2 * KV_WIDTH), BF16),
            pltpu.VMEM((N_KV_HEADS, r, GQA_GROUP * BLOCK, HEAD_DIM), BF16),
            pltpu.VMEM((tq, ATTN_WIDTH), F32),
        ],
        compiler_params=_cparams(("arbitrary",)),
        name="windowed_attn_shortconv_outproj",
    )(x, q, kt, kt, kt, v1, v1, v1, u, u, u, bias, *row_tables, conv_w, conv_b, w_out, gate)


def _key_chunk(seq):
    return min(seq, 1024)


def _dense_attn_kernel(tq, tk, q_ref, k_ref, v_ref, o_ref, qs, s_buf, p_buf, rmax_buf, alpha_buf,
                       m_ref, acc_ref):
    nc = k_ref.shape[0]
    for j in range(GQA_GROUP):
        qs[j * tq:(j + 1) * tq, :] = q_ref[:, j * HEAD_DIM:(j + 1) * HEAD_DIM]
    m_ref[...] = jnp.full(m_ref.shape, -jnp.inf, F32)
    acc_ref[...] = jnp.zeros(acc_ref.shape, F32)

    def scores(c):
        s = jnp.dot(qs[...], k_ref[c], preferred_element_type=F32)
        s_buf[...] = s
        rmax_buf[...] = jnp.broadcast_to(jnp.max(s, axis=-1, keepdims=True), rmax_buf.shape)

    def softmax():
        m_old = m_ref[...]
        m_new = jnp.maximum(m_old, rmax_buf[...])
        alpha_buf[...] = jnp.exp2(m_old - m_new)
        m_ref[...] = m_new
        p = jnp.exp2(s_buf[...] - jnp.tile(m_new, (1, tk // LANES)))
        p_buf[...] = p.astype(BF16)

    def weighted_values(c):
        start = c * tk if isinstance(c, int) else pl.multiple_of(c * tk, tk)
        pv = jnp.dot(p_buf[...], v_ref[pl.ds(start, tk), :], preferred_element_type=F32)
        acc_ref[...] = alpha_buf[...] * acc_ref[...] + pv

    def step(t):
        static = isinstance(t, int)
        weighted_values(t)
        if not static or t + 1 < nc:
            softmax()
        if not static or t + 2 < nc:
            scores(t + 2)

    scores(0)
    softmax()
    if nc > 1:
        scores(1)

    def steady(t, carry):
        step(t)
        return carry

    lax.fori_loop(0, max(nc - 2, 0), steady, 0)
    for t in range(max(nc - 2, 0), nc):
        step(t)

    acc = acc_ref[...]
    o = acc / pltpu.roll(acc, HEAD_DIM, 1)
    for j in range(GQA_GROUP):
        o_ref[:, j * HEAD_DIM:(j + 1) * HEAD_DIM] = o[j * tq:(j + 1) * tq, 0:HEAD_DIM].astype(o_ref.dtype)


def _dense_attn_bounded_kernel(tq, tk, bound_ref, q_ref, k_ref, v_ref, o_ref, qs, p_buf, acc_ref):
    nc = k_ref.shape[0]
    for j in range(GQA_GROUP):
        qs[j * tq:(j + 1) * tq, :] = q_ref[:, j * HEAD_DIM:(j + 1) * HEAD_DIM]
    shift = bound_ref[0]

    def probabilities(c):
        s = jnp.dot(qs[...], k_ref[c], preferred_element_type=F32)
        p_buf[...] = jnp.exp2(s - shift).astype(BF16)

    def weighted_values(c):
        start = c * tk if isinstance(c, int) else pl.multiple_of(c * tk, tk)
        return jnp.dot(p_buf[...], v_ref[pl.ds(start, tk), :], preferred_element_type=F32)

    probabilities(0)
    acc_ref[...] = weighted_values(0)
    if nc > 1:
        probabilities(1)

    def steady(t, carry):
        acc_ref[...] += weighted_values(t)
        probabilities(t + 1)
        return carry

    lax.fori_loop(1, nc - 1, steady, 0)
    if nc > 1:
        acc_ref[...] += weighted_values(nc - 1)

    acc = acc_ref[...]
    o = acc / pltpu.roll(acc, HEAD_DIM, 1)
    for j in range(GQA_GROUP):
        o_ref[:, j * HEAD_DIM:(j + 1) * HEAD_DIM] = o[j * tq:(j + 1) * tq, 0:HEAD_DIM].astype(o_ref.dtype)


MAX_FIXED_SHIFT = 50.0


def _dense_attn_call(q, kt, v1, logit_bound, bsz, seq):
    tq, tk = min(seq, 1024), _key_chunk(seq)
    nq = seq // tq
    nc = seq // tk
    rows = GQA_GROUP * tq
    gw = GQA_GROUP * HEAD_DIM
    stat = pltpu.VMEM((rows, LANES), F32)
    q_spec = pl.BlockSpec((tq, gw), lambda b, g, i: (b * nq + i, g))
    k_spec = pl.BlockSpec((nc, HEAD_DIM, tk), lambda b, g, i: (b, g, 0), pipeline_mode=pl.Buffered(1))
    v_spec = pl.BlockSpec((seq, LANES), lambda b, g, i: (b, g), pipeline_mode=pl.Buffered(1))
    common = dict(
        out_shape=jax.ShapeDtypeStruct((bsz * seq, ATTN_WIDTH), BF16),
        grid=(bsz, N_KV_HEADS, nq),
        out_specs=q_spec,
        compiler_params=_cparams(("arbitrary", "arbitrary", "arbitrary")),
    )

    def running_max():
        return pl.pallas_call(
            functools.partial(_dense_attn_kernel, tq, tk),
            in_specs=[q_spec, k_spec, v_spec],
            scratch_shapes=[
                pltpu.VMEM((rows, HEAD_DIM), BF16),
                pltpu.VMEM((rows, tk), F32), pltpu.VMEM((rows, tk), BF16),
                stat, stat, stat, stat,
            ],
            name="dense_gqa_attention", **common,
        )(q, kt, v1)

    def fixed_shift():
        return pl.pallas_call(
            functools.partial(_dense_attn_bounded_kernel, tq, tk),
            in_specs=[pl.BlockSpec(memory_space=pltpu.SMEM), q_spec, k_spec, v_spec],
            scratch_shapes=[pltpu.VMEM((rows, HEAD_DIM), BF16), pltpu.VMEM((rows, tk), BF16), stat],
            name="dense_gqa_attention_fixed_shift", **common,
        )(logit_bound.reshape(1), q, kt, v1)

    return lax.cond(logit_bound <= MAX_FIXED_SHIFT, fixed_shift, running_max)


def _hp_dot(a, b):
    return jnp.dot(a, b, preferred_element_type=F32, precision=lax.Precision.HIGHEST)


def _filter_trunk_kernel(z_ref, w1_ref, b1_ref, w2_ref, b2_ref, w3_ref, b3_ref, fq_ref, o_ref):
    fq = fq_ref[...]
    h = jnp.sin(fq * (_hp_dot(w1_ref[...], z_ref[...]) + b1_ref[...]))
    h = jnp.sin(fq * (_hp_dot(w2_ref[...], h) + b2_ref[...]))
    h = jnp.sin(fq * (_hp_dot(w3_ref[...], h) + b3_ref[...]))
    o_ref[...] = h.T


def _filter_trunk_call(zfeat_t, w1, b1, w2, b2, w3, b3, freq):
    rows = zfeat_t.shape[1]
    tr = min(rows, 2048)
    fw = HYENA_FILTER_WIDTH
    col = lambda v: v.reshape(fw, 1)
    return pl.pallas_call(
        _filter_trunk_kernel,
        out_shape=jax.ShapeDtypeStruct((rows, fw), F32),
        grid=(rows // tr,),
        in_specs=[
            pl.BlockSpec((HYENA_EMB_PAD, tr), lambda i: (0, i)),
            _resident((fw, HYENA_EMB_PAD)), _resident((fw, 1)),
            _resident((fw, fw)), _resident((fw, 1)),
            _resident((fw, fw)), _resident((fw, 1)),
            _resident((fw, 1)),
        ],
        out_specs=pl.BlockSpec((tr, fw), lambda i: (i, 0)),
        compiler_params=_cparams(("arbitrary",)),
        name="hyena_filter_trunk",
    )(zfeat_t, w1.T, col(b1), w2.T, col(b2), w3.T, col(b3), col(freq))


def _dft_stage1(src_ref, tab_ref, re_ref, im_ref, n_rows, k1p, n2_major=False, pitch=DFT_N2, src_pitch=DFT_N2):
    def body(n2, carry):
        xs = src_ref[pl.ds(n2, n_rows, stride=src_pitch), :].astype(BF16)
        res = jnp.dot(tab_ref[n2], xs, preferred_element_type=F32)
        dst = pl.ds(pl.multiple_of(n2 * k1p, 8), k1p) if n2_major else pl.ds(n2, k1p, stride=pitch)
        re_ref[dst, :] = res[0:k1p]
        im_ref[dst, :] = res[k1p:]
        return carry

    lax.fori_loop(0, DFT_N2, body, 0, unroll=8)


def _filter_spec_kernel(seq, k1p, h_ref, w4f_ref, w4b_ref, dl_ref, e1_ref, f2_ref,
                        kr_ref, ki_ref, kfull, bre, bim):
    n = 2 * seq
    chunk = min(seq, 1024)
    dl = dl_ref[...]
    live = seq // DFT_N2 + 1

    def fill_half(w_ref, backward):
        def fill(c, ss):
            r0 = pl.multiple_of(c * chunk, chunk)
            val = jnp.dot(h_ref[pl.ds(r0, chunk), :].astype(BF16), w_ref[...], preferred_element_type=F32)
            rows = r0 + lax.broadcasted_iota(jnp.int32, (chunk, 1), 0)
            lag = ((n - rows) if backward else rows).astype(F32)
            val = val * jnp.exp(-(lag * (1.0 / (seq - 1))) * dl)
            if backward:
                val = jnp.where(rows == seq, 0.0, val)
            for k in range(chunk // DFT_N2):
                dst = pl.multiple_of((c * (chunk // DFT_N2) + k) * SPECTRUM_PITCH, 8)
                kfull[pl.ds(dst, DFT_N2), :] = val[k * DFT_N2:(k + 1) * DFT_N2]
            return ss + jnp.sum(val * val, axis=0, keepdims=True)
        return fill

    ss = lax.fori_loop(0, seq // chunk, fill_half(w4f_ref, False), jnp.zeros((1, dl.shape[1]), F32))
    ss = lax.fori_loop(seq // chunk, n // chunk, fill_half(w4b_ref, True), ss)
    norm = lax.rsqrt(ss + EPS)

    _dft_stage1(kfull, e1_ref, bre, bim, n // DFT_N2, k1p, n2_major=True, src_pitch=SPECTRUM_PITCH)

    kr_ref[live * DFT_N2:, :] = jnp.zeros(((k1p - live) * DFT_N2, kr_ref.shape[1]), F32)
    ki_ref[live * DFT_N2:, :] = jnp.zeros(((k1p - live) * DFT_N2, ki_ref.shape[1]), F32)
    f2 = f2_ref[...]

    def stage2(a, carry):
        src = pl.ds(a, DFT_N2, stride=k1p)
        slab = jnp.concatenate([bre[src, :], bim[src, :]], axis=0).astype(BF16)
        xf = jnp.dot(f2, slab, preferred_element_type=F32)
        rows = pl.ds(pl.multiple_of(a * DFT_N2, DFT_N2), DFT_N2)
        kr_ref[rows, :] = xf[0:DFT_N2] * norm
        ki_ref[rows, :] = xf[DFT_N2:] * norm
        return carry

    lax.fori_loop(0, live, stage2, 0, unroll=8)


def _filter_spec_call(h3, w4, deltas, e1f, f2f, seq):
    n = 2 * seq
    k1p = e1f.shape[1] // 2
    ct = LANES
    nct = CONV_WIDTH // ct
    fw = HYENA_FILTER_WIDTH
    spec_rows = k1p * DFT_N2
    out = pl.BlockSpec((spec_rows, ct), lambda j: (0, j))
    return pl.pallas_call(
        functools.partial(_filter_spec_kernel, seq, k1p),
        out_shape=(jax.ShapeDtypeStruct((spec_rows, CONV_WIDTH), F32),
                   jax.ShapeDtypeStruct((spec_rows, CONV_WIDTH), F32)),
        grid=(nct,),
        in_specs=[
            _resident((n, fw)),
            pl.BlockSpec((fw, ct), lambda j: (0, j)),
            pl.BlockSpec((fw, ct), lambda j: (0, nct + j)),
            pl.BlockSpec((1, ct), lambda j: (0, j)),
            _resident(e1f.shape),
            _resident((2 * DFT_N2, 2 * DFT_N2)),
        ],
        out_specs=(out, out),
        scratch_shapes=[pltpu.VMEM((n // DFT_N2 * SPECTRUM_PITCH, ct), F32), pltpu.VMEM((spec_rows, ct), F32),
                        pltpu.VMEM((spec_rows, ct), F32)],
        compiler_params=_cparams(("arbitrary",)),
        name="hyena_filter_spectrum",
    )(h3, w4, w4, deltas, e1f, f2f)


def _long_conv_kernel(seq, k1p, z_ref, x0_ref, kr_ref, ki_ref, skip_ref, e1_ref, f2f_ref, f2i_ref,
                      einv_ref, o_ref, br, bi, cbuf):
    n1h = seq // DFT_N2
    pitch = br.shape[0] // k1p
    _dft_stage1(z_ref, e1_ref, br, bi, n1h, k1p, pitch=pitch)

    f2f = f2f_ref[...]
    f2i = f2i_ref[...]

    def spectral(a, carry):
        rows = pl.ds(pl.multiple_of(a * pitch, 8), DFT_N2)
        krows = pl.ds(pl.multiple_of(a * DFT_N2, DFT_N2), DFT_N2)
        slab = jnp.concatenate([br[rows, :], bi[rows, :]], axis=0).astype(BF16)
        xf = jnp.dot(f2f, slab, preferred_element_type=F32)
        xr, xi = xf[0:DFT_N2], xf[DFT_N2:]
        kr, ki = kr_ref[krows, :], ki_ref[krows, :]
        y = jnp.concatenate([xr * kr - xi * ki, xr * ki + xi * kr], axis=0).astype(BF16)
        d = jnp.dot(f2i, y, preferred_element_type=F32)
        br[rows, :] = d[0:DFT_N2]
        bi[rows, :] = d[DFT_N2:]
        return carry

    lax.fori_loop(0, n1h + 1, spectral, 0, unroll=8)

    skip = skip_ref[...]

    def synth(n2, carry):
        ds_spec = pl.ds(n2, k1p, stride=pitch)
        d = jnp.concatenate([br[ds_spec, :], bi[ds_spec, :]], axis=0).astype(BF16)
        cbuf[pl.ds(n2, n1h, stride=pitch), :] = jnp.dot(einv_ref[n2], d, preferred_element_type=F32)
        return carry

    lax.fori_loop(0, DFT_N2, synth, 0, unroll=8)

    def gate(n1, carry):
        rows = pl.ds(pl.multiple_of(n1 * DFT_N2, DFT_N2), DFT_N2)
        conv = cbuf[pl.ds(pl.multiple_of(n1 * pitch, 8), DFT_N2), :]
        o_ref[rows, :] = x0_ref[rows, :] * (conv + skip * z_ref[rows, :])
        return carry

    lax.fori_loop(0, n1h, gate, 0, unroll=4)


def _long_conv_call(z, x0, kr, ki, skip, e1, f2f, f2i, einv, bsz, seq):
    ct = LANES
    nct = CONV_WIDTH // ct
    k1p = e1.shape[1] // 2
    spec_rows = k1p * DFT_N2
    big = lambda: pl.BlockSpec((seq, ct), lambda j, b: (b, j), pipeline_mode=pl.Buffered(1))
    spec = lambda: pl.BlockSpec((spec_rows, ct), lambda j, b: (0, j), pipeline_mode=pl.Buffered(1))
    return pl.pallas_call(
        functools.partial(_long_conv_kernel, seq, k1p),
        out_shape=jax.ShapeDtypeStruct((bsz * seq, CONV_WIDTH), F32),
        grid=(nct, bsz),
        in_specs=[
            big(), big(), spec(), spec(),
            pl.BlockSpec((1, ct), lambda j, b: (0, j)),
            _resident(e1.shape),
            _resident((2 * DFT_N2, 2 * DFT_N2)), _resident((2 * DFT_N2, 2 * DFT_N2)),
            _resident(einv.shape),
        ],
        out_specs=pl.BlockSpec((seq, ct), lambda j, b: (b, j)),
        scratch_shapes=[pltpu.VMEM((k1p * SPECTRUM_PITCH, ct), F32), pltpu.VMEM((k1p * SPECTRUM_PITCH, ct), F32),
                        pltpu.VMEM((seq // DFT_N2 * SPECTRUM_PITCH, ct), F32)],
        compiler_params=_cparams(("arbitrary", "arbitrary")),
        name="hyena_long_conv",
    )(z, x0, kr, ki, skip, e1, f2f, f2i, einv)


def _rope_tables(seq):
    t = jnp.arange(seq)
    row = (t // GRID_W).astype(F32)
    col = (t % GRID_W).astype(F32)
    half = HEAD_DIM // 2
    inv = ROPE_THETA ** (-jnp.arange(0, half, 2, dtype=F32) / half)
    ang = jnp.concatenate([row[:, None] * inv, col[:, None] * inv], axis=-1)
    ang = jnp.repeat(ang, 2, axis=-1)
    sign = jnp.where(jnp.arange(HEAD_DIM) % 2 == 0, -1.0, 1.0).astype(F32)
    reps = LANES // HEAD_DIM
    return jnp.tile(jnp.cos(ang), (1, reps)), jnp.tile(jnp.sin(ang) * sign, (1, reps))


def _dft_tables(seq):
    n = 2 * seq
    n1 = n // DFT_N2
    k1 = n1 // 2 + 1
    k1p = -(-k1 // 8) * 8
    a = jnp.arange(k1p)
    live = (a < k1).astype(F32)
    ang1 = ((a[:, None] * jnp.arange(n1)[None, :]) % n1).astype(F32) * (2.0 * math.pi / n1)
    ang2 = (jnp.arange(DFT_N2)[:, None] * a[None, :]).astype(F32) * (2.0 * math.pi / n)
    c1, s1 = jnp.cos(ang1) * live[:, None], jnp.sin(ang1) * live[:, None]
    c2, s2 = jnp.cos(ang2), jnp.sin(ang2)
    cos_a = c1[None] * c2[:, :, None] - s1[None] * s2[:, :, None]
    sin_a = s1[None] * c2[:, :, None] + c1[None] * s2[:, :, None]
    e1_full = jnp.concatenate([cos_a, -sin_a], axis=1).astype(BF16)
    e1_half = e1_full[:, :, :n1 // 2]
    wgt = jnp.where((a == 0) | (a == n1 // 2), 1.0, 2.0) / n
    c1t, s1t = (c1 * wgt[:, None]).T[:n1 // 2], (s1 * wgt[:, None]).T[:n1 // 2]
    cos_s = c1t[None] * c2[:, None, :] - s1t[None] * s2[:, None, :]
    sin_s = s1t[None] * c2[:, None, :] + c1t[None] * s2[:, None, :]
    einv = jnp.concatenate([cos_s, -sin_s], axis=2).astype(BF16)
    kk = jnp.arange(DFT_N2)
    phi = ((kk[:, None] * kk[None, :]) % DFT_N2).astype(F32) * (2.0 * math.pi / DFT_N2)
    ci, si = jnp.cos(phi), jnp.sin(phi)
    f2f = jnp.block([[ci, si], [-si, ci]]).astype(BF16)
    f2i = jnp.block([[ci, -si], [si, ci]]).astype(BF16)
    return e1_full, e1_half, einv, f2f, f2i


def _filter_features_t(seq):
    j = jnp.arange(2 * seq)
    lag = jnp.where(j < seq, j, jnp.where(j == seq, 0, 2 * seq - j)).astype(F32)[None, :]
    t = lag / (seq - 1)
    bands = (HYENA_EMB - 1) // 2
    w = (2.0 * math.pi / seq) * lag
    fr = jnp.linspace(1e-4, bands - 1, bands, dtype=F32)[:, None]
    pad = jnp.zeros((HYENA_EMB_PAD - HYENA_EMB, 2 * seq), F32)
    return jnp.concatenate([t, jnp.cos(fr * w), -jnp.sin(fr * w), pad], axis=0)


def kernel(x, c, rel_table, norm_g, w_mod, b_mod, w_in, w_out, ffn_w1, ffn_w3, ffn_w2, a_qk_g, a_sink,
           b_conv_w, b_conv_b, c_qk_g, d_conv_w, d_conv_b, d_f_w1, d_f_b1, d_f_w2, d_f_b2, d_f_w3,
           d_f_b3, d_f_w4, d_f_freq, d_skip):
    bsz, seq, d = x.shape
    depth = w_mod.shape[0]
    m_rows = bsz * seq
    mod =_mod_call(c, w_mod, b_mod).reshape(depth, bsz, 3, 3, 1, d)
    e_blk = jnp.kron(jnp.eye(MXU_DIM // HEAD_DIM, dtype=F32), jnp.ones((HEAD_DIM, HEAD_DIM), F32)).astype(BF16)

    w1b, w3b = _to_bf16_from_transposed(ffn_w1), _to_bf16_from_transposed(ffn_w3)
    w2b, w_in_b, w_out_b = (w.astype(BF16) for w in (ffn_w2, w_in, w_out))

    def ffn(xf, l, which, sub, mixer=None):
        return _ffn_call(xf, norm_g[l, sub][None], mod[l, :, sub, 0], mod[l, :, sub, 1], mod[l, :, sub, 2],
                         w1b, w3b, w2b, l, which, seq, mixer)

    xf = x.reshape(m_rows, d)
    for l in range(depth):
        j = l // 2
        xf = ffn(xf, l, 0, 0)
        even = l % 2 == 0
        qk_g = a_qk_g[j] if even else c_qk_g[j]
        hyena = None if even else _rope_tables(seq) + (d_conv_w[j], d_conv_b[j][None])
        q, k, v, *conv_in = _inproj_call(
            xf, norm_g[l, 1][None], mod[l, :, 1, 0], mod[l, :, 1, 1], w_in_b, l,
            jnp.tile(qk_g[0], N_Q_HEADS)[None], jnp.tile(qk_g[1], N_KV_HEADS)[None], e_blk, seq, hyena)
        gate = mod[l, :, 1, 2]
        if even:
            (u,) = conv_in
            qk_bound = (1.01 * HEAD_DIM ** 0.5) * jnp.max(jnp.abs(qk_g[0])) * jnp.max(jnp.abs(qk_g[1]))
            shift = jnp.maximum(qk_bound + jnp.max(rel_table, axis=0), a_sink[j])
            worst_gap = jnp.max(shift + qk_bound - rel_table[0])
            bias, *row_tables = _bias_call(rel_table, a_sink[j], shift)
            mixer_ab = lambda fixed: _mixer_ab_call(xf, q, k, v, u, bias, row_tables, fixed, b_conv_w[j],
                                                    b_conv_b[j][None], w_out_b, l, gate, seq)
            xf = lax.cond(worst_gap <= MAX_FIXED_SHIFT_GAP, lambda: mixer_ab(True), lambda: mixer_ab(False))
            pending = None
        else:
            x0, z = conv_in
            logit_bound = (1.01 * LOG2_E * HEAD_DIM ** 0.5) * jnp.max(jnp.abs(qk_g[0])) * jnp.max(jnp.abs(qk_g[1]))
            att = _dense_attn_call(q, k, v, logit_bound, bsz, seq)
            e1_full, e1_half, einv, f2f, f2i = _dft_tables(seq)
            fw1 = jnp.pad(d_f_w1[j], ((0, HYENA_EMB_PAD - HYENA_EMB), (0, 0)))
            h3 = _filter_trunk_call(_filter_features_t(seq), fw1, d_f_b1[j], d_f_w2[j], d_f_b2[j],
                                    d_f_w3[j], d_f_b3[j], d_f_freq[j])
            deltas = jnp.abs(jnp.linspace(HYENA_MIN_DECAY, HYENA_MAX_DECAY, CONV_WIDTH, dtype=F32))[None]
            kr, ki = _filter_spec_call(h3, d_f_w4[j].astype(BF16), deltas, e1_full, f2f, seq)
            y = _long_conv_call(z, x0, kr, ki, d_skip[j][None], e1_half, f2f, f2i, einv, bsz, seq)
            pending = (att, y, w_out_b, gate)
        xf = ffn(xf, l, 1, 2, pending)
    return xf.reshape(bsz, seq, d)
```

```python
import functools
import math

import jax
import jax.numpy as jnp
import numpy as np
from jax import lax
from jax.experimental import pallas as pl
from jax.experimental.pallas import tpu as pltpu

D_MODEL = 1024
HEAD_DIM = 64
N_Q_HEADS = 8
N_KV_HEADS = 2
GQA_GROUP = N_Q_HEADS // N_KV_HEADS
ATTN_WIDTH = N_Q_HEADS * HEAD_DIM
KV_WIDTH = N_KV_HEADS * HEAD_DIM
QKV_COLS = ATTN_WIDTH + 2 * KV_WIDTH
CONV_WIDTH = D_MODEL - ATTN_WIDTH
IN_COLS = QKV_COLS + 3 * CONV_WIDTH
D_FF = 2752
BLOCK = 128
WINDOW = 128
N_BUCKETS = 32
MAX_DISTANCE = 128
GRID_W = 64
ROPE_THETA = 10000.0
HYENA_EMB = 33
HYENA_FILTER_WIDTH = 64
HYENA_EMB_PAD = 64
HYENA_MIN_DECAY = math.log(1e-2) / 0.3
HYENA_MAX_DECAY = math.log(1e-2) / 1.5
EPS = 1e-6
NEG_INF = -1e30
LOG2_E = 1.4426950408889634
MAX_FIXED_SHIFT_GAP = 80.0

LANES = 128
HALO = 16
MXU_DIM = 256
FF_CHUNK = MXU_DIM
VMEM_LIMIT = 56 * 1024 * 1024
DFT_N2 = 128
SPECTRUM_PITCH = DFT_N2 + 8

BF16 = jnp.bfloat16
F32 = jnp.float32


def _cparams(sem):
    return pltpu.CompilerParams(dimension_semantics=sem, vmem_limit_bytes=VMEM_LIMIT)


def _resident(shape):
    nd = len(shape)
    return pl.BlockSpec(shape, lambda *_: (0,) * nd, pipeline_mode=pl.Buffered(1))


def _layer_slice(shape, layer):
    return pl.BlockSpec((None,) + shape, lambda *_: (layer, 0, 0), pipeline_mode=pl.Buffered(1))


def _adaln(x, g, scale, shift):
    y = x * lax.rsqrt(jnp.mean(x * x, axis=-1, keepdims=True) + EPS) * g
    return y * (1.0 + scale) + shift


def _mod_kernel(c_ref, w_ref, b_ref, o_ref):
    w = w_ref[0]
    for b in range(c_ref.shape[0]):
        cc = c_ref[b]
        cond = cc * jax.nn.sigmoid(cc)
        o_ref[0, b:b + 1, :] = jnp.sum(w * cond, axis=0, keepdims=True) + b_ref[0]


def _mod_call(c, w_mod, b_mod):
    depth, d, n = w_mod.shape
    bsz = c.shape[0]
    tn = 1152 if n % 1152 == 0 else 512
    return pl.pallas_call(
        _mod_kernel,
        out_shape=jax.ShapeDtypeStruct((depth, bsz, n), F32),
        grid=(depth, n // tn),
        in_specs=[
            pl.BlockSpec((bsz, d, 1), lambda l, j: (0, 0, 0)),
            pl.BlockSpec((1, d, tn), lambda l, j: (l, 0, j)),
            pl.BlockSpec((1, 1, tn), lambda l, j: (l, 0, j)),
        ],
        out_specs=pl.BlockSpec((1, bsz, tn), lambda l, j: (l, 0, j)),
        compiler_params=_cparams(("arbitrary", "arbitrary")),
        name="adaln_modulation",
    )(c.reshape(bsz, d, 1), w_mod, b_mod.reshape(depth, 1, n))


def _cast_transpose_kernel(wt_ref, o_ref):
    cols = wt_ref.shape[0]
    starts = list(range(0, cols - MXU_DIM, MXU_DIM)) + [cols - MXU_DIM]
    for c0 in starts:
        o_ref[:, c0:c0 + MXU_DIM] = wt_ref[c0:c0 + MXU_DIM, :].T.astype(o_ref.dtype)


def _to_bf16_from_transposed(w):
    lead, (rows, cols) = w.shape[:-2], w.shape[-2:]
    none = (None,) * len(lead)
    return pl.pallas_call(
        _cast_transpose_kernel,
        out_shape=jax.ShapeDtypeStruct(w.shape, BF16),
        grid=lead,
        in_specs=[pl.BlockSpec(none + (cols, rows), lambda *idx: idx + (0, 0))],
        out_specs=pl.BlockSpec(none + (rows, cols), lambda *idx: idx + (0, 0)),
        compiler_params=_cparams(("arbitrary",) * len(lead)),
        name="weights_to_bf16_transposing",
    )(jnp.swapaxes(w, -1, -2))


def _ffn_kernel(mixer_update, x_ref, g_ref, shift_ref, scale_ref, gate_ref, w1_ref, w3_ref, w2_ref, *rest):
    x = x_ref[...]
    if mixer_update:
        att_ref, y_ref, wo_ref, mgate_ref, o_ref = rest
        upd = jnp.dot(att_ref[...], wo_ref[0:ATTN_WIDTH, :], preferred_element_type=F32)
        upd = upd + jnp.dot(y_ref[...].astype(BF16), wo_ref[ATTN_WIDTH:, :], preferred_element_type=F32)
        x = x + mgate_ref[0] * upd
    else:
        (o_ref,) = rest
    h = _adaln(x, g_ref[...], scale_ref[0], shift_ref[0]).astype(BF16)
    acc = jnp.zeros(x.shape, F32)
    for c0 in range(0, D_FF, FF_CHUNK):
        cols = slice(c0, min(c0 + FF_CHUNK, D_FF))
        a = jnp.dot(h, w1_ref[:, cols], preferred_element_type=F32)
        b = jnp.dot(h, w3_ref[:, cols], preferred_element_type=F32)
        act = (a * jax.nn.sigmoid(a) * b).astype(BF16)
        acc = acc + jnp.dot(act, w2_ref[cols, :], preferred_element_type=F32)
    o_ref[...] = x + (0.5 * gate_ref[0]) * acc


def _ffn_call(x, g, shift, scale, gate, w1, w3, w2, layer, which, rows_per_batch, mixer=None):
    m, d = x.shape
    tm = min(rows_per_batch, 1024)
    tpb = rows_per_batch // tm
    row = lambda w: pl.BlockSpec((tm, w), lambda i: (i, 0))
    vec = pl.BlockSpec((1, 1, d), lambda i: (i // tpb, 0, 0))
    weight = lambda r, c: pl.BlockSpec((None, None, r, c), lambda i: (layer, which, 0, 0),
                                       pipeline_mode=pl.Buffered(1))
    in_specs = [row(d), _resident((1, d)), vec, vec, vec, weight(d, D_FF), weight(d, D_FF), weight(D_FF, d)]
    args = [x, g, shift, scale, gate, w1, w3, w2]
    if mixer is not None:
        in_specs += [row(ATTN_WIDTH), row(CONV_WIDTH), _layer_slice((d, d), layer), vec]
        args += list(mixer)
    return pl.pallas_call(
        functools.partial(_ffn_kernel, mixer is not None),
        out_shape=jax.ShapeDtypeStruct((m, d), F32),
        grid=(m // tm,),
        in_specs=in_specs,
        out_specs=row(d),
        compiler_params=_cparams(("arbitrary",)),
        name="adaln_swiglu_ffn",
    )(*args)


def _group_norm_scale(v, e_ref):
    sq = v * v
    hi = sq.astype(BF16)
    lo = (sq - hi.astype(F32)).astype(BF16)
    w = min(v.shape[1], MXU_DIM)
    e = e_ref[...]
    ss = jnp.concatenate(
        [jnp.dot(hi[:, c:c + w], e[0:w, 0:w], preferred_element_type=F32)
         + jnp.dot(lo[:, c:c + w], e[0:w, 0:w], preferred_element_type=F32) for c in range(0, v.shape[1], w)],
        axis=1)
    return lax.rsqrt(ss * (1.0 / HEAD_DIM) + EPS)


def _rope128(v, cos, sin_signed, even_lane):
    partner = jnp.where(even_lane, pltpu.roll(v, LANES - 1, 1), pltpu.roll(v, 1, 1))
    return v * cos + partner * sin_signed


def _short_conv_rows(u, before, after, cw, cb):
    rows = u.shape[0]
    r = lax.broadcasted_iota(jnp.int32, u.shape, 0)
    um1 = jnp.where(r == 0, before, pltpu.roll(u, 1, 0))
    up1 = jnp.where(r == rows - 1, after, pltpu.roll(u, rows - 1, 0))
    return cw[0:1] * um1 + cw[1:2] * u + cw[2:3] * up1 + cb


def _inproj_kernel(rope, tpb, x_ref, g_ref, shift_ref, scale_ref, w_ref, qg_ref, kg_ref, e_ref, *rest):
    if rope:
        cos_ref, sin_ref, xp_ref, xn_ref, cw_ref, cb_ref, q_ref, k_ref, v_ref, x0_ref, z_ref = rest
    else:
        q_ref, k_ref, v_ref, u_ref = rest
    h = _adaln(x_ref[...], g_ref[...], scale_ref[0], shift_ref[0]).astype(BF16)
    q = jnp.dot(h, w_ref[:, 0:ATTN_WIDTH], preferred_element_type=F32)
    k = jnp.dot(h, w_ref[:, ATTN_WIDTH:ATTN_WIDTH + KV_WIDTH], preferred_element_type=F32)
    q = q * _group_norm_scale(q, e_ref) * qg_ref[...]
    k = k * _group_norm_scale(k, e_ref) * kg_ref[...]
    if rope:
        cos = cos_ref[...]
        sin = sin_ref[...]
        even = (lax.broadcasted_iota(jnp.int32, cos.shape, 1) % 2) == 0
        q = jnp.concatenate(
            [_rope128(q[:, j * LANES:(j + 1) * LANES], cos, sin, even) for j in range(ATTN_WIDTH // LANES)],
            axis=1)
        k = _rope128(k, cos, sin, even)
    q_ref[...] = (q * (HEAD_DIM ** -0.5 * (LOG2_E if rope else 1.0))).astype(BF16)
    v = jnp.dot(h, w_ref[:, ATTN_WIDTH + KV_WIDTH:QKV_COLS], preferred_element_type=F32)
    k_ref[0] = k.T.astype(BF16)
    low = lax.broadcasted_iota(jnp.int32, v.shape, 1) < HEAD_DIM
    v_ref[...] = jnp.concatenate(
        [jnp.where(low, v, 1.0), jnp.where(low, pltpu.roll(v, HEAD_DIM, 1), 1.0)], axis=1).astype(BF16)
    if not rope:
        u_ref[...] = jnp.dot(h, w_ref[:, QKV_COLS:IN_COLS], preferred_element_type=F32)
        return
    i = pl.program_id(0)
    h_prev = _adaln(xp_ref[...], g_ref[...], scale_ref[0], shift_ref[0]).astype(BF16)
    h_next = _adaln(xn_ref[...], g_ref[...], scale_ref[0], shift_ref[0]).astype(BF16)
    u_ext = jnp.dot(jnp.concatenate([h_prev, h, h_next], axis=0), w_ref[:, QKV_COLS:IN_COLS],
                    preferred_element_type=F32)
    tm = h.shape[0]
    before = jnp.where((i % tpb) == 0, 0.0, u_ext[HALO - 1:HALO])
    after = jnp.where((i % tpb) == tpb - 1, 0.0, u_ext[HALO + tm:HALO + tm + 1])
    t = _short_conv_rows(u_ext[HALO:HALO + tm], before, after, cw_ref[...], cb_ref[...])
    x0_ref[...] = t[:, 0:CONV_WIDTH]
    z_ref[...] = t[:, CONV_WIDTH:2 * CONV_WIDTH] * t[:, 2 * CONV_WIDTH:]


def _inproj_call(x, g, shift, scale, w_in, layer, qg, kg, e, rows_per_batch, hyena=None):
    m, d = x.shape
    tm = min(rows_per_batch, 1024)
    tpb = rows_per_batch // tm
    vec = pl.BlockSpec((1, 1, d), lambda i: (i // tpb, 0, 0))
    row = lambda w: pl.BlockSpec((tm, w), lambda i: (i, 0))
    in_specs = [
        row(d),
        _resident((1, d)), vec, vec,
        _layer_slice((d, IN_COLS), layer),
        _resident((1, ATTN_WIDTH)), _resident((1, KV_WIDTH)),
        _resident((MXU_DIM, MXU_DIM)),
    ]
    args = [x, g, shift, scale, w_in, qg, kg, e]
    uw = 3 * CONV_WIDTH
    if hyena is not None:
        cos, sin, conv_w, conv_b = hyena
        tab = pl.BlockSpec((tm, LANES), lambda i: (i % tpb, 0))
        hpt = tm // HALO
        last_halo = m // HALO - 1
        in_specs += [
            tab, tab,
            pl.BlockSpec((HALO, d), lambda i: (jnp.maximum(i * hpt - 1, 0), 0)),
            pl.BlockSpec((HALO, d), lambda i: (jnp.minimum((i + 1) * hpt, last_halo), 0)),
            _resident((3, uw)), _resident((1, uw)),
        ]
        args += [cos, sin, x, x, conv_w, conv_b]
        conv_shapes = [jax.ShapeDtypeStruct((m, CONV_WIDTH), F32)] * 2
        conv_specs = [row(CONV_WIDTH)] * 2
    else:
        conv_shapes = [jax.ShapeDtypeStruct((m, uw), F32)]
        conv_specs = [row(uw)]
    tk = _key_chunk(rows_per_batch)
    per = tk // tm
    return pl.pallas_call(
        functools.partial(_inproj_kernel, hyena is not None, tpb),
        out_shape=[
            jax.ShapeDtypeStruct((m, ATTN_WIDTH), BF16),
            jax.ShapeDtypeStruct((m // tk, KV_WIDTH, tk), BF16),
            jax.ShapeDtypeStruct((m, 2 * KV_WIDTH), BF16),
        ] + conv_shapes,
        grid=(m // tm,),
        in_specs=in_specs,
        out_specs=[row(ATTN_WIDTH), pl.BlockSpec((1, KV_WIDTH, tm), lambda i: (i // per, 0, i % per)),
                   row(2 * KV_WIDTH)] + conv_specs,
        compiler_params=_cparams(("arbitrary",)),
        name="adaln_in_projection",
    )(*args)


_T5_STEPS = (12, 16, 23, 32, 46, 64, 91)


def _bias_kernel(tab_ref, sink_ref, shift_ref, o_ref, sink_rows_ref, shift_rows_ref, sink_term_rows_ref):
    for h in range(N_Q_HEADS):
        rows = slice(h * BLOCK, (h + 1) * BLOCK)
        sink_rows_ref[rows, :] = jnp.full((BLOCK, LANES), sink_ref[h], F32)
        shift_rows_ref[rows, :] = jnp.full((BLOCK, LANES), shift_ref[h], F32)
        sink_term_rows_ref[rows, :] = jnp.exp(jnp.full((BLOCK, LANES), sink_ref[h] - shift_ref[h], F32))
    qi = lax.broadcasted_iota(jnp.int32, (BLOCK, 3 * BLOCK), 0)
    kj = lax.broadcasted_iota(jnp.int32, (BLOCK, 3 * BLOCK), 1)
    rel = kj - BLOCK - qi
    n = jnp.abs(rel)
    half = N_BUCKETS // 2
    max_exact = half // 2
    large = jnp.full(n.shape, max_exact, jnp.int32)
    for t in _T5_STEPS:
        large = large + (n >= t).astype(jnp.int32)
    bucket = jnp.where(rel > 0, half, 0) + jnp.where(n < max_exact, n, large)
    for h in range(N_Q_HEADS):
        bias = jnp.zeros(n.shape, F32)
        for b in range(N_BUCKETS):
            bias = jnp.where(bucket == b, tab_ref[b, h], bias)
        o_ref[h] = jnp.where(n <= WINDOW, bias, NEG_INF)


def _bias_call(rel_table, sink, shift):
    rows = jax.ShapeDtypeStruct((N_Q_HEADS * BLOCK, LANES), F32)
    smem = pl.BlockSpec(memory_space=pltpu.SMEM)
    return pl.pallas_call(
        _bias_kernel,
        out_shape=(jax.ShapeDtypeStruct((N_Q_HEADS, BLOCK, 3 * BLOCK), F32), rows, rows, rows),
        in_specs=[smem, smem, smem],
        name="t5_bias_tile",
    )(rel_table, sink, shift)


def _mixer_ab_kernel(tq, tpb, fixed_shift, x_ref, q_ref, kc_ref, kp_ref, kn_ref, vc_ref, vp_ref, vn_ref,
                     uc_ref, up_ref, un_ref, bias_ref, sink_ref, shift_ref, sink_term_ref, cw_ref, cb_ref,
                     wo_ref, gate_ref, o_ref, kbuf, vbuf, qs, att):
    i = pl.program_id(0)
    first = (i % tpb) == 0
    last = (i % tpb) == tpb - 1
    nblk = tq // BLOCK
    grows = GQA_GROUP * BLOCK
    kbuf[:, 0:BLOCK] = kp_ref[0]
    kbuf[:, BLOCK:BLOCK + tq] = kc_ref[0]
    kbuf[:, BLOCK + tq:] = kn_ref[0]
    vbuf[0:BLOCK] = vp_ref[...]
    vbuf[BLOCK:BLOCK + tq] = vc_ref[...]
    vbuf[BLOCK + tq:] = vn_ref[...]
    for n in range(nblk):
        for h in range(N_Q_HEADS):
            g, j = divmod(h, GQA_GROUP)
            qs[g, n, j * BLOCK:(j + 1) * BLOCK, :] = q_ref[n * BLOCK:(n + 1) * BLOCK, h * HEAD_DIM:(h + 1) * HEAD_DIM]
    col = lax.broadcasted_iota(jnp.int32, (grows, 3 * BLOCK), 1)
    for n in range(nblk):
        keys = slice(n * BLOCK, (n + 3) * BLOCK)
        for g in range(N_KV_HEADS):
            s = jnp.dot(qs[g, n], kbuf[g * HEAD_DIM:(g + 1) * HEAD_DIM, keys], preferred_element_type=F32)
            s = s + bias_ref[g * GQA_GROUP:(g + 1) * GQA_GROUP].reshape(grows, 3 * BLOCK)
            if n == 0:
                s = jnp.where(jnp.logical_and(first, col < BLOCK), NEG_INF, s)
            if n == nblk - 1:
                s = jnp.where(jnp.logical_and(last, col >= 2 * BLOCK), NEG_INF, s)
            grp = slice(g * grows, (g + 1) * grows)
            if fixed_shift:
                mx = shift_ref[grp, :]
                sink_term = sink_term_ref[grp, :]
            else:
                sk = sink_ref[grp, :]
                mx = jnp.maximum(jnp.broadcast_to(jnp.max(s, axis=-1, keepdims=True), sk.shape), sk)
                sink_term = jnp.exp(sk - mx)
            p = jnp.exp(s - jnp.tile(mx, (1, 3)))
            pv = jnp.dot(p.astype(BF16), vbuf[keys, g * LANES:(g + 1) * LANES], preferred_element_type=F32)
            o = pv / (pltpu.roll(pv, HEAD_DIM, 1) + sink_term)
            for j in range(GQA_GROUP):
                h = g * GQA_GROUP + j
                att[n * BLOCK:(n + 1) * BLOCK, h * HEAD_DIM:(h + 1) * HEAD_DIM] = o[j * BLOCK:(j + 1) * BLOCK, 0:HEAD_DIM]

    gb = uc_ref[:, 0:CONV_WIDTH]
    p = uc_ref[:, CONV_WIDTH:2 * CONV_WIDTH] * uc_ref[:, 2 * CONV_WIDTH:]
    p_before = jnp.where(first, 0.0, up_ref[7:8, CONV_WIDTH:2 * CONV_WIDTH] * up_ref[7:8, 2 * CONV_WIDTH:])
    p_after = jnp.where(last, 0.0, un_ref[0:1, CONV_WIDTH:2 * CONV_WIDTH] * un_ref[0:1, 2 * CONV_WIDTH:])
    conv = gb * _short_conv_rows(p, p_before, p_after, cw_ref[...], cb_ref[...])

    y = jnp.dot(att[...].astype(BF16), wo_ref[0:ATTN_WIDTH, :], preferred_element_type=F32)
    y = y + jnp.dot(conv.astype(BF16), wo_ref[ATTN_WIDTH:, :], preferred_element_type=F32)
    o_ref[...] = x_ref[...] + gate_ref[0] * y


def _mixer_ab_dispatch_kernel(tq, tpb, gap_ref, *refs):
    fixed = gap_ref[0] <= MAX_FIXED_SHIFT_GAP

    @pl.when(fixed)
    def _():
        _mixer_ab_kernel(tq, tpb, True, *refs)

    @pl.when(jnp.logical_not(fixed))
    def _():
        _mixer_ab_kernel(tq, tpb, False, *refs)


def _mixer_ab_call(x, q, kt, v1, u, bias, row_tables, worst_gap, conv_w, conv_b, w_out, layer, gate,
                   rows_per_batch):
    m, d = x.shape
    tq = min(rows_per_batch, 1024)
    tpb = rows_per_batch // tq
    r = tq // BLOCK
    nb = m // BLOCK
    n8 = m // 8
    cur = lambda w: pl.BlockSpec((tq, w), lambda i: (i, 0))
    prev_of = lambda i: jnp.maximum(i * r - 1, 0)
    next_of = lambda i: jnp.minimum((i + 1) * r, nb - 1)
    v_prev = pl.BlockSpec((BLOCK, 2 * KV_WIDTH), lambda i: (prev_of(i), 0))
    v_next = pl.BlockSpec((BLOCK, 2 * KV_WIDTH), lambda i: (next_of(i), 0))
    tk = kt.shape[2]
    k_cur = pl.BlockSpec((1, KV_WIDTH, tq), lambda i: (i // (tk // tq), 0, i % (tk // tq)))
    kpb = tk // BLOCK
    k_prev = pl.BlockSpec((1, KV_WIDTH, BLOCK), lambda i: (prev_of(i) // kpb, 0, prev_of(i) % kpb))
    k_next = pl.BlockSpec((1, KV_WIDTH, BLOCK), lambda i: (next_of(i) // kpb, 0, next_of(i) % kpb))
    uw = 3 * CONV_WIDTH
    row_table = _resident((N_Q_HEADS * BLOCK, LANES))
    return pl.pallas_call(
        functools.partial(_mixer_ab_dispatch_kernel, tq, tpb),
        out_shape=jax.ShapeDtypeStruct((m, d), F32),
        grid=(m // tq,),
        in_specs=[
            pl.BlockSpec(memory_space=pltpu.SMEM),
            cur(d), cur(ATTN_WIDTH),
            k_cur, k_prev, k_next,
            cur(2 * KV_WIDTH), v_prev, v_next,
            cur(uw),
            pl.BlockSpec((8, uw), lambda i: (jnp.maximum(i * (tq // 8) - 1, 0), 0)),
            pl.BlockSpec((8, uw), lambda i: (jnp.minimum((i + 1) * (tq // 8), n8 - 1), 0)),
            _resident((N_Q_HEADS, BLOCK, 3 * BLOCK)),
            row_table, row_table, row_table,
            _resident((3, CONV_WIDTH)), _resident((1, CONV_WIDTH)),
            _layer_slice((d, d), layer),
            pl.BlockSpec((1, 1, d), lambda i: (i // tpb, 0, 0)),
        ],
        out_specs=cur(d),
        scratch_shapes=[
            pltpu.VMEM((KV_WIDTH, tq + 2 * BLOCK), BF16),
            pltpu.VMEM((tq + 2 * BLOCK, 2 * KV_WIDTH), BF16),
            pltpu.VMEM((N_KV_HEADS, r, GQA_GROUP * BLOCK, HEAD_DIM), BF16),
            pltpu.VMEM((tq, ATTN_WIDTH), F32),
        ],
        compiler_params=_cparams(("arbitrary",)),
        name="windowed_attn_shortconv_outproj",
    )(worst_gap.reshape(1), x, q, kt, kt, kt, v1, v1, v1, u, u, u, bias, *row_tables, conv_w, conv_b, w_out, gate)


def _key_chunk(seq):
    return min(seq, 1024)


def _dense_attn_kernel(tq, tk, q_ref, k_ref, v_ref, o_ref, qs, s_buf, p_buf, rmax_buf, alpha_buf,
                       m_ref, acc_ref):
    nc = k_ref.shape[0]
    for j in range(GQA_GROUP):
        qs[j * tq:(j + 1) * tq, :] = q_ref[:, j * HEAD_DIM:(j + 1) * HEAD_DIM]
    m_ref[...] = jnp.full(m_ref.shape, -jnp.inf, F32)
    acc_ref[...] = jnp.zeros(acc_ref.shape, F32)

    def scores(c):
        s = jnp.dot(qs[...], k_ref[c], preferred_element_type=F32)
        s_buf[...] = s
        rmax_buf[...] = jnp.broadcast_to(jnp.max(s, axis=-1, keepdims=True), rmax_buf.shape)

    def softmax():
        m_old = m_ref[...]
        m_new = jnp.maximum(m_old, rmax_buf[...])
        alpha_buf[...] = jnp.exp2(m_old - m_new)
        m_ref[...] = m_new
        p = jnp.exp2(s_buf[...] - jnp.tile(m_new, (1, tk // LANES)))
        p_buf[...] = p.astype(BF16)

    def weighted_values(c):
        start = c * tk if isinstance(c, int) else pl.multiple_of(c * tk, tk)
        pv = jnp.dot(p_buf[...], v_ref[pl.ds(start, tk), :], preferred_element_type=F32)
        acc_ref[...] = alpha_buf[...] * acc_ref[...] + pv

    def step(t):
        static = isinstance(t, int)
        weighted_values(t)
        if not static or t + 1 < nc:
            softmax()
        if not static or t + 2 < nc:
            scores(t + 2)

    scores(0)
    softmax()
    if nc > 1:
        scores(1)

    def steady(t, carry):
        step(t)
        return carry

    lax.fori_loop(0, max(nc - 2, 0), steady, 0)
    for t in range(max(nc - 2, 0), nc):
        step(t)

    acc = acc_ref[...]
    o = acc / pltpu.roll(acc, HEAD_DIM, 1)
    for j in range(GQA_GROUP):
        o_ref[:, j * HEAD_DIM:(j + 1) * HEAD_DIM] = o[j * tq:(j + 1) * tq, 0:HEAD_DIM].astype(o_ref.dtype)


def _dense_attn_bounded_kernel(tq, tk, bound_ref, q_ref, k_ref, v_ref, o_ref, qs, p_buf, acc_ref):
    nc = k_ref.shape[0]
    for j in range(GQA_GROUP):
        qs[j * tq:(j + 1) * tq, :] = q_ref[:, j * HEAD_DIM:(j + 1) * HEAD_DIM]
    shift = bound_ref[0]

    def probabilities(c):
        s = jnp.dot(qs[...], k_ref[c], preferred_element_type=F32)
        p_buf[...] = jnp.exp2(s - shift).astype(BF16)

    def weighted_values(c):
        start = c * tk if isinstance(c, int) else pl.multiple_of(c * tk, tk)
        return jnp.dot(p_buf[...], v_ref[pl.ds(start, tk), :], preferred_element_type=F32)

    probabilities(0)
    acc_ref[...] = weighted_values(0)
    if nc > 1:
        probabilities(1)

    def steady(t, carry):
        acc_ref[...] += weighted_values(t)
        probabilities(t + 1)
        return carry

    lax.fori_loop(1, nc - 1, steady, 0)
    if nc > 1:
        acc_ref[...] += weighted_values(nc - 1)

    acc = acc_ref[...]
    o = acc / pltpu.roll(acc, HEAD_DIM, 1)
    for j in range(GQA_GROUP):
        o_ref[:, j * HEAD_DIM:(j + 1) * HEAD_DIM] = o[j * tq:(j + 1) * tq, 0:HEAD_DIM].astype(o_ref.dtype)


MAX_FIXED_SHIFT = 50.0


def _dense_attn_dispatch_kernel(tq, tk, bound_ref, q_ref, k_ref, v_ref, o_ref, qs, s_buf, p_buf, rmax_buf,
                                alpha_buf, m_ref, acc_ref):
    fixed = bound_ref[0] <= MAX_FIXED_SHIFT

    @pl.when(fixed)
    def _():
        _dense_attn_bounded_kernel(tq, tk, bound_ref, q_ref, k_ref, v_ref, o_ref, qs, p_buf, acc_ref)

    @pl.when(jnp.logical_not(fixed))
    def _():
        _dense_attn_kernel(tq, tk, q_ref, k_ref, v_ref, o_ref, qs, s_buf, p_buf, rmax_buf, alpha_buf,
                           m_ref, acc_ref)


def _dense_attn_call(q, kt, v1, logit_bound, bsz, seq):
    tq, tk = min(seq, 1024), _key_chunk(seq)
    nq = seq // tq
    nc = seq // tk
    rows = GQA_GROUP * tq
    gw = GQA_GROUP * HEAD_DIM
    stat = pltpu.VMEM((rows, LANES), F32)
    q_spec = pl.BlockSpec((tq, gw), lambda b, g, i: (b * nq + i, g))
    return pl.pallas_call(
        functools.partial(_dense_attn_dispatch_kernel, tq, tk),
        out_shape=jax.ShapeDtypeStruct((bsz * seq, ATTN_WIDTH), BF16),
        grid=(bsz, N_KV_HEADS, nq),
        in_specs=[
            pl.BlockSpec(memory_space=pltpu.SMEM),
            q_spec,
            pl.BlockSpec((nc, HEAD_DIM, tk), lambda b, g, i: (b, g, 0), pipeline_mode=pl.Buffered(1)),
            pl.BlockSpec((seq, LANES), lambda b, g, i: (b, g), pipeline_mode=pl.Buffered(1)),
        ],
        out_specs=q_spec,
        scratch_shapes=[
            pltpu.VMEM((rows, HEAD_DIM), BF16),
            pltpu.VMEM((rows, tk), F32), pltpu.VMEM((rows, tk), BF16),
            stat, stat, stat, stat,
        ],
        compiler_params=_cparams(("arbitrary", "arbitrary", "arbitrary")),
        name="dense_gqa_attention",
    )(logit_bound.reshape(1), q, kt, v1)


def _hp_dot(a, b):
    return jnp.dot(a, b, preferred_element_type=F32, precision=lax.Precision.HIGHEST)


def _filter_trunk_kernel(z_ref, w1_ref, b1_ref, w2_ref, b2_ref, w3_ref, b3_ref, fq_ref, o_ref):
    fq = fq_ref[...]
    h = jnp.sin(fq * (_hp_dot(w1_ref[...], z_ref[...]) + b1_ref[...]))
    h = jnp.sin(fq * (_hp_dot(w2_ref[...], h) + b2_ref[...]))
    h = jnp.sin(fq * (_hp_dot(w3_ref[...], h) + b3_ref[...]))
    o_ref[...] = h.T


def _filter_trunk_call(zfeat_t, w1, b1, w2, b2, w3, b3, freq):
    rows = zfeat_t.shape[1]
    tr = min(rows, 2048)
    fw = HYENA_FILTER_WIDTH
    col = lambda v: v.reshape(fw, 1)
    return pl.pallas_call(
        _filter_trunk_kernel,
        out_shape=jax.ShapeDtypeStruct((rows, fw), F32),
        grid=(rows // tr,),
        in_specs=[
            pl.BlockSpec((HYENA_EMB_PAD, tr), lambda i: (0, i)),
            _resident((fw, HYENA_EMB_PAD)), _resident((fw, 1)),
            _resident((fw, fw)), _resident((fw, 1)),
            _resident((fw, fw)), _resident((fw, 1)),
            _resident((fw, 1)),
        ],
        out_specs=pl.BlockSpec((tr, fw), lambda i: (i, 0)),
        compiler_params=_cparams(("arbitrary",)),
        name="hyena_filter_trunk",
    )(zfeat_t, w1.T, col(b1), w2.T, col(b2), w3.T, col(b3), col(freq))


def _dft_stage1(src_ref, tab_ref, re_ref, im_ref, n_rows, k1p, n2_major=False, pitch=DFT_N2, src_pitch=DFT_N2):
    def body(n2, carry):
        xs = src_ref[pl.ds(n2, n_rows, stride=src_pitch), :].astype(BF16)
        res = jnp.dot(tab_ref[n2], xs, preferred_element_type=F32)
        dst = pl.ds(pl.multiple_of(n2 * k1p, 8), k1p) if n2_major else pl.ds(n2, k1p, stride=pitch)
        re_ref[dst, :] = res[0:k1p]
        im_ref[dst, :] = res[k1p:]
        return carry

    lax.fori_loop(0, DFT_N2, body, 0, unroll=8)


def _filter_spec_kernel(seq, k1p, h_ref, w4f_ref, w4b_ref, dl_ref, e1_ref, f2_ref,
                        kr_ref, ki_ref, kfull, bre, bim):
    n = 2 * seq
    chunk = min(seq, 1024)
    dl = dl_ref[...]
    live = seq // DFT_N2 + 1

    def fill_half(w_ref, backward):
        def fill(c, ss):
            r0 = pl.multiple_of(c * chunk, chunk)
            val = jnp.dot(h_ref[pl.ds(r0, chunk), :].astype(BF16), w_ref[...], preferred_element_type=F32)
            rows = r0 + lax.broadcasted_iota(jnp.int32, (chunk, 1), 0)
            lag = ((n - rows) if backward else rows).astype(F32)
            val = val * jnp.exp(-(lag * (1.0 / (seq - 1))) * dl)
            if backward:
                val = jnp.where(rows == seq, 0.0, val)
            for k in range(chunk // DFT_N2):
                dst = pl.multiple_of((c * (chunk // DFT_N2) + k) * SPECTRUM_PITCH, 8)
                kfull[pl.ds(dst, DFT_N2), :] = val[k * DFT_N2:(k + 1) * DFT_N2]
            return ss + jnp.sum(val * val, axis=0, keepdims=True)
        return fill

    ss = lax.fori_loop(0, seq // chunk, fill_half(w4f_ref, False), jnp.zeros((1, dl.shape[1]), F32))
    ss = lax.fori_loop(seq // chunk, n // chunk, fill_half(w4b_ref, True), ss)
    norm = lax.rsqrt(ss + EPS)

    _dft_stage1(kfull, e1_ref, bre, bim, n // DFT_N2, k1p, n2_major=True, src_pitch=SPECTRUM_PITCH)

    kr_ref[live * DFT_N2:, :] = jnp.zeros(((k1p - live) * DFT_N2, kr_ref.shape[1]), F32)
    ki_ref[live * DFT_N2:, :] = jnp.zeros(((k1p - live) * DFT_N2, ki_ref.shape[1]), F32)
    f2 = f2_ref[...]

    def stage2(a, carry):
        src = pl.ds(a, DFT_N2, stride=k1p)
        slab = jnp.concatenate([bre[src, :], bim[src, :]], axis=0).astype(BF16)
        xf = jnp.dot(f2, slab, preferred_element_type=F32)
        rows = pl.ds(pl.multiple_of(a * DFT_N2, DFT_N2), DFT_N2)
        kr_ref[rows, :] = xf[0:DFT_N2] * norm
        ki_ref[rows, :] = xf[DFT_N2:] * norm
        return carry

    lax.fori_loop(0, live, stage2, 0, unroll=8)


def _filter_spec_call(h3, w4, deltas, e1f, f2f, seq):
    n = 2 * seq
    k1p = e1f.shape[1] // 2
    ct = LANES
    nct = CONV_WIDTH // ct
    fw = HYENA_FILTER_WIDTH
    spec_rows = k1p * DFT_N2
    out = pl.BlockSpec((spec_rows, ct), lambda j: (0, j))
    return pl.pallas_call(
        functools.partial(_filter_spec_kernel, seq, k1p),
        out_shape=(jax.ShapeDtypeStruct((spec_rows, CONV_WIDTH), F32),
                   jax.ShapeDtypeStruct((spec_rows, CONV_WIDTH), F32)),
        grid=(nct,),
        in_specs=[
            _resident((n, fw)),
            pl.BlockSpec((fw, ct), lambda j: (0, j)),
            pl.BlockSpec((fw, ct), lambda j: (0, nct + j)),
            pl.BlockSpec((1, ct), lambda j: (0, j)),
            _resident(e1f.shape),
            _resident((2 * DFT_N2, 2 * DFT_N2)),
        ],
        out_specs=(out, out),
        scratch_shapes=[pltpu.VMEM((n // DFT_N2 * SPECTRUM_PITCH, ct), F32), pltpu.VMEM((spec_rows, ct), F32),
                        pltpu.VMEM((spec_rows, ct), F32)],
        compiler_params=_cparams(("arbitrary",)),
        name="hyena_filter_spectrum",
    )(h3, w4, w4, deltas, e1f, f2f)


def _long_conv_kernel(seq, k1p, z_ref, x0_ref, kr_ref, ki_ref, skip_ref, e1_ref, f2f_ref, f2i_ref,
                      einv_ref, o_ref, br, bi, cbuf):
    n1h = seq // DFT_N2
    pitch = br.shape[0] // k1p
    _dft_stage1(z_ref, e1_ref, br, bi, n1h, k1p, pitch=pitch)

    f2f = f2f_ref[...]
    f2i = f2i_ref[...]

    def spectral(a, carry):
        rows = pl.ds(pl.multiple_of(a * pitch, 8), DFT_N2)
        krows = pl.ds(pl.multiple_of(a * DFT_N2, DFT_N2), DFT_N2)
        slab = jnp.concatenate([br[rows, :], bi[rows, :]], axis=0).astype(BF16)
        xf = jnp.dot(f2f, slab, preferred_element_type=F32)
        xr, xi = xf[0:DFT_N2], xf[DFT_N2:]
        kr, ki = kr_ref[krows, :], ki_ref[krows, :]
        y = jnp.concatenate([xr * kr - xi * ki, xr * ki + xi * kr], axis=0).astype(BF16)
        d = jnp.dot(f2i, y, preferred_element_type=F32)
        br[rows, :] = d[0:DFT_N2]
        bi[rows, :] = d[DFT_N2:]
        return carry

    lax.fori_loop(0, n1h + 1, spectral, 0, unroll=8)

    skip = skip_ref[...]

    def synth(n2, carry):
        ds_spec = pl.ds(n2, k1p, stride=pitch)
        d = jnp.concatenate([br[ds_spec, :], bi[ds_spec, :]], axis=0).astype(BF16)
        cbuf[pl.ds(n2, n1h, stride=pitch), :] = jnp.dot(einv_ref[n2], d, preferred_element_type=F32)
        return carry

    lax.fori_loop(0, DFT_N2, synth, 0, unroll=8)

    def gate(n1, carry):
        rows = pl.ds(pl.multiple_of(n1 * DFT_N2, DFT_N2), DFT_N2)
        conv = cbuf[pl.ds(pl.multiple_of(n1 * pitch, 8), DFT_N2), :]
        o_ref[rows, :] = x0_ref[rows, :] * (conv + skip * z_ref[rows, :])
        return carry

    lax.fori_loop(0, n1h, gate, 0, unroll=4)


def _long_conv_call(z, x0, kr, ki, skip, e1, f2f, f2i, einv, bsz, seq):
    ct = LANES
    nct = CONV_WIDTH // ct
    k1p = e1.shape[1] // 2
    spec_rows = k1p * DFT_N2
    big = lambda: pl.BlockSpec((seq, ct), lambda j, b: (b, j), pipeline_mode=pl.Buffered(1))
    spec = lambda: pl.BlockSpec((spec_rows, ct), lambda j, b: (0, j), pipeline_mode=pl.Buffered(1))
    return pl.pallas_call(
        functools.partial(_long_conv_kernel, seq, k1p),
        out_shape=jax.ShapeDtypeStruct((bsz * seq, CONV_WIDTH), F32),
        grid=(nct, bsz),
        in_specs=[
            big(), big(), spec(), spec(),
            pl.BlockSpec((1, ct), lambda j, b: (0, j)),
            _resident(e1.shape),
            _resident((2 * DFT_N2, 2 * DFT_N2)), _resident((2 * DFT_N2, 2 * DFT_N2)),
            _resident(einv.shape),
        ],
        out_specs=pl.BlockSpec((seq, ct), lambda j, b: (b, j)),
        scratch_shapes=[pltpu.VMEM((k1p * SPECTRUM_PITCH, ct), F32), pltpu.VMEM((k1p * SPECTRUM_PITCH, ct), F32),
                        pltpu.VMEM((seq // DFT_N2 * SPECTRUM_PITCH, ct), F32)],
        compiler_params=_cparams(("arbitrary", "arbitrary")),
        name="hyena_long_conv",
    )(z, x0, kr, ki, skip, e1, f2f, f2i, einv)


def _rope_tables(seq):
    t = jnp.arange(seq)
    row = (t // GRID_W).astype(F32)
    col = (t % GRID_W).astype(F32)
    half = HEAD_DIM // 2
    inv = ROPE_THETA ** (-jnp.arange(0, half, 2, dtype=F32) / half)
    ang = jnp.concatenate([row[:, None] * inv, col[:, None] * inv], axis=-1)
    ang = jnp.repeat(ang, 2, axis=-1)
    sign = jnp.where(jnp.arange(HEAD_DIM) % 2 == 0, -1.0, 1.0).astype(F32)
    reps = LANES // HEAD_DIM
    return jnp.tile(jnp.cos(ang), (1, reps)), jnp.tile(jnp.sin(ang) * sign, (1, reps))


def _dft_tables(seq):
    n = 2 * seq
    n1 = n // DFT_N2
    k1 = n1 // 2 + 1
    k1p = -(-k1 // 8) * 8
    a = jnp.arange(k1p)
    live = (a < k1).astype(F32)
    ang1 = ((a[:, None] * jnp.arange(n1)[None, :]) % n1).astype(F32) * (2.0 * math.pi / n1)
    ang2 = (jnp.arange(DFT_N2)[:, None] * a[None, :]).astype(F32) * (2.0 * math.pi / n)
    c1, s1 = jnp.cos(ang1) * live[:, None], jnp.sin(ang1) * live[:, None]
    c2, s2 = jnp.cos(ang2), jnp.sin(ang2)
    cos_a = c1[None] * c2[:, :, None] - s1[None] * s2[:, :, None]
    sin_a = s1[None] * c2[:, :, None] + c1[None] * s2[:, :, None]
    e1_full = jnp.concatenate([cos_a, -sin_a], axis=1).astype(BF16)
    e1_half = e1_full[:, :, :n1 // 2]
    wgt = jnp.where((a == 0) | (a == n1 // 2), 1.0, 2.0) / n
    c1t, s1t = (c1 * wgt[:, None]).T[:n1 // 2], (s1 * wgt[:, None]).T[:n1 // 2]
    cos_s = c1t[None] * c2[:, None, :] - s1t[None] * s2[:, None, :]
    sin_s = s1t[None] * c2[:, None, :] + c1t[None] * s2[:, None, :]
    einv = jnp.concatenate([cos_s, -sin_s], axis=2).astype(BF16)
    kk = jnp.arange(DFT_N2)
    phi = ((kk[:, None] * kk[None, :]) % DFT_N2).astype(F32) * (2.0 * math.pi / DFT_N2)
    ci, si = jnp.cos(phi), jnp.sin(phi)
    f2f = jnp.block([[ci, si], [-si, ci]]).astype(BF16)
    f2i = jnp.block([[ci, -si], [si, ci]]).astype(BF16)
    return e1_full, e1_half, einv, f2f, f2i


def _filter_features_t(seq):
    j = jnp.arange(2 * seq)
    lag = jnp.where(j < seq, j, jnp.where(j == seq, 0, 2 * seq - j)).astype(F32)[None, :]
    t = lag / (seq - 1)
    bands = (HYENA_EMB - 1) // 2
    w = (2.0 * math.pi / seq) * lag
    fr = jnp.linspace(1e-4, bands - 1, bands, dtype=F32)[:, None]
    pad = jnp.zeros((HYENA_EMB_PAD - HYENA_EMB, 2 * seq), F32)
    return jnp.concatenate([t, jnp.cos(fr * w), -jnp.sin(fr * w), pad], axis=0)


def kernel(x, c, rel_table, norm_g, w_mod, b_mod, w_in, w_out, ffn_w1, ffn_w3, ffn_w2, a_qk_g, a_sink,
           b_conv_w, b_conv_b, c_qk_g, d_conv_w, d_conv_b, d_f_w1, d_f_b1, d_f_w2, d_f_b2, d_f_w3,
           d_f_b3, d_f_w4, d_f_freq, d_skip):
    bsz, seq, d = x.shape
    depth = w_mod.shape[0]
    m_rows = bsz * seq
    mod =_mod_call(c, w_mod, b_mod).reshape(depth, bsz, 3, 3, 1, d)
    e_blk = jnp.kron(jnp.eye(MXU_DIM // HEAD_DIM, dtype=F32), jnp.ones((HEAD_DIM, HEAD_DIM), F32)).astype(BF16)

    w1b, w3b = _to_bf16_from_transposed(ffn_w1), _to_bf16_from_transposed(ffn_w3)
    w2b, w_in_b, w_out_b = (w.astype(BF16) for w in (ffn_w2, w_in, w_out))

    def ffn(xf, l, which, sub, mixer=None):
        return _ffn_call(xf, norm_g[l, sub][None], mod[l, :, sub, 0], mod[l, :, sub, 1], mod[l, :, sub, 2],
                         w1b, w3b, w2b, l, which, seq, mixer)

    xf = x.reshape(m_rows, d)
    for l in range(depth):
        j = l // 2
        xf = ffn(xf, l, 0, 0)
        even = l % 2 == 0
        qk_g = a_qk_g[j] if even else c_qk_g[j]
        hyena = None if even else _rope_tables(seq) + (d_conv_w[j], d_conv_b[j][None])
        q, k, v, *conv_in = _inproj_call(
            xf, norm_g[l, 1][None], mod[l, :, 1, 0], mod[l, :, 1, 1], w_in_b, l,
            jnp.tile(qk_g[0], N_Q_HEADS)[None], jnp.tile(qk_g[1], N_KV_HEADS)[None], e_blk, seq, hyena)
        gate = mod[l, :, 1, 2]
        if even:
            (u,) = conv_in
            qk_bound = (1.01 * HEAD_DIM ** 0.5) * jnp.max(jnp.abs(qk_g[0])) * jnp.max(jnp.abs(qk_g[1]))
            shift = jnp.maximum(qk_bound + jnp.max(rel_table, axis=0), a_sink[j])
            worst_gap = jnp.max(shift + qk_bound - rel_table[0])
            bias, *row_tables = _bias_call(rel_table, a_sink[j], shift)
            xf = _mixer_ab_call(xf, q, k, v, u, bias, row_tables, worst_gap, b_conv_w[j], b_conv_b[j][None],
                                w_out_b, l, gate, seq)
            pending = None
        else:
            x0, z = conv_in
            logit_bound = (1.01 * LOG2_E * HEAD_DIM ** 0.5) * jnp.max(jnp.abs(qk_g[0])) * jnp.max(jnp.abs(qk_g[1]))
            att = _dense_attn_call(q, k, v, logit_bound, bsz, seq)
            e1_full, e1_half, einv, f2f, f2i = _dft_tables(seq)
            fw1 = jnp.pad(d_f_w1[j], ((0, HYENA_EMB_PAD - HYENA_EMB), (0, 0)))
            h3 = _filter_trunk_call(_filter_features_t(seq), fw1, d_f_b1[j], d_f_w2[j], d_f_b2[j],
                                    d_f_w3[j], d_f_b3[j], d_f_freq[j])
            deltas = jnp.abs(jnp.linspace(HYENA_MIN_DECAY, HYENA_MAX_DECAY, CONV_WIDTH, dtype=F32))[None]
            kr, ki = _filter_spec_call(h3, d_f_w4[j].astype(BF16), deltas, e1_full, f2f, seq)
            y = _long_conv_call(z, x0, kr, ki, d_skip[j][None], e1_half, f2f, f2i, einv, bsz, seq)
            pending = (att, y, w_out_b, gate)
        xf = ffn(xf, l, 1, 2, pending)
    return xf.reshape(bsz, seq, d)
```

```python
import functools
import math

import jax
import jax.numpy as jnp
import numpy as np
from jax import lax
from jax.experimental import pallas as pl
from jax.experimental.pallas import tpu as pltpu

D_MODEL = 1024
HEAD_DIM = 64
N_Q_HEADS = 8
N_KV_HEADS = 2
GQA_GROUP = N_Q_HEADS // N_KV_HEADS
ATTN_WIDTH = N_Q_HEADS * HEAD_DIM
KV_WIDTH = N_KV_HEADS * HEAD_DIM
QKV_COLS = ATTN_WIDTH + 2 * KV_WIDTH
CONV_WIDTH = D_MODEL - ATTN_WIDTH
IN_COLS = QKV_COLS + 3 * CONV_WIDTH
D_FF = 2752
BLOCK = 128
WINDOW = 128
N_BUCKETS = 32
MAX_DISTANCE = 128
GRID_W = 64
ROPE_THETA = 10000.0
HYENA_EMB = 33
HYENA_FILTER_WIDTH = 64
HYENA_EMB_PAD = 64
HYENA_MIN_DECAY = math.log(1e-2) / 0.3
HYENA_MAX_DECAY = math.log(1e-2) / 1.5
EPS = 1e-6
NEG_INF = -1e30
LOG2_E = 1.4426950408889634
MAX_FIXED_SHIFT_GAP = 80.0

LANES = 128
HALO = 16
MXU_DIM = 256
FF_CHUNK = MXU_DIM
VMEM_LIMIT = 56 * 1024 * 1024
DFT_N2 = 128
SPECTRUM_PITCH = DFT_N2 + 8

BF16 = jnp.bfloat16
F32 = jnp.float32


def _cparams(sem):
    return pltpu.CompilerParams(dimension_semantics=sem, vmem_limit_bytes=VMEM_LIMIT)


def _resident(shape):
    nd = len(shape)
    return pl.BlockSpec(shape, lambda *_: (0,) * nd, pipeline_mode=pl.Buffered(1))


def _layer_slice(shape, layer):
    return pl.BlockSpec((None,) + shape, lambda *_: (layer, 0, 0), pipeline_mode=pl.Buffered(1))


def _adaln(x, g, scale, shift):
    y = x * lax.rsqrt(jnp.mean(x * x, axis=-1, keepdims=True) + EPS) * g
    return y * (1.0 + scale) + shift


def _mod_kernel(c_ref, w_ref, b_ref, o_ref):
    w = w_ref[0]
    for b in range(c_ref.shape[0]):
        cc = c_ref[b]
        cond = cc * jax.nn.sigmoid(cc)
        o_ref[0, b:b + 1, :] = jnp.sum(w * cond, axis=0, keepdims=True) + b_ref[0]


def _mod_call(c, w_mod, b_mod):
    depth, d, n = w_mod.shape
    bsz = c.shape[0]
    tn = 1152 if n % 1152 == 0 else 512
    return pl.pallas_call(
        _mod_kernel,
        out_shape=jax.ShapeDtypeStruct((depth, bsz, n), F32),
        grid=(depth, n // tn),
        in_specs=[
            pl.BlockSpec((bsz, d, 1), lambda l, j: (0, 0, 0)),
            pl.BlockSpec((1, d, tn), lambda l, j: (l, 0, j)),
            pl.BlockSpec((1, 1, tn), lambda l, j: (l, 0, j)),
        ],
        out_specs=pl.BlockSpec((1, bsz, tn), lambda l, j: (l, 0, j)),
        compiler_params=_cparams(("arbitrary", "arbitrary")),
        name="adaln_modulation",
    )(c.reshape(bsz, d, 1), w_mod, b_mod.reshape(depth, 1, n))


def _cast_transpose_kernel(wt_ref, o_ref):
    cols = wt_ref.shape[0]
    starts = list(range(0, cols - MXU_DIM, MXU_DIM)) + [cols - MXU_DIM]
    for c0 in starts:
        o_ref[:, c0:c0 + MXU_DIM] = wt_ref[c0:c0 + MXU_DIM, :].T.astype(o_ref.dtype)


def _to_bf16_from_transposed(w):
    lead, (rows, cols) = w.shape[:-2], w.shape[-2:]
    none = (None,) * len(lead)
    return pl.pallas_call(
        _cast_transpose_kernel,
        out_shape=jax.ShapeDtypeStruct(w.shape, BF16),
        grid=lead,
        in_specs=[pl.BlockSpec(none + (cols, rows), lambda *idx: idx + (0, 0))],
        out_specs=pl.BlockSpec(none + (rows, cols), lambda *idx: idx + (0, 0)),
        compiler_params=_cparams(("arbitrary",) * len(lead)),
        name="weights_to_bf16_transposing",
    )(jnp.swapaxes(w, -1, -2))


def _ffn_kernel(mixer_update, x_ref, g_ref, shift_ref, scale_ref, gate_ref, w1_ref, w3_ref, w2_ref, *rest):
    x = x_ref[...]
    if mixer_update:
        att_ref, y_ref, wo_ref, mgate_ref, o_ref = rest
        upd = jnp.dot(att_ref[...], wo_ref[0:ATTN_WIDTH, :], preferred_element_type=F32)
        upd = upd + jnp.dot(y_ref[...].astype(BF16), wo_ref[ATTN_WIDTH:, :], preferred_element_type=F32)
        x = x + mgate_ref[0] * upd
    else:
        (o_ref,) = rest
    h = _adaln(x, g_ref[...], scale_ref[0], shift_ref[0]).astype(BF16)
    acc = jnp.zeros(x.shape, F32)
    for c0 in range(0, D_FF, FF_CHUNK):
        cols = slice(c0, min(c0 + FF_CHUNK, D_FF))
        a = jnp.dot(h, w1_ref[:, cols], preferred_element_type=F32)
        b = jnp.dot(h, w3_ref[:, cols], preferred_element_type=F32)
        act = (a * jax.nn.sigmoid(a) * b).astype(BF16)
        acc = acc + jnp.dot(act, w2_ref[cols, :], preferred_element_type=F32)
    o_ref[...] = x + (0.5 * gate_ref[0]) * acc


def _ffn_call(x, g, shift, scale, gate, w1, w3, w2, layer, which, rows_per_batch, mixer=None):
    m, d = x.shape
    tm = min(rows_per_batch, 1024)
    tpb = rows_per_batch // tm
    row = lambda w: pl.BlockSpec((tm, w), lambda i: (i, 0))
    vec = pl.BlockSpec((1, 1, d), lambda i: (i // tpb, 0, 0))
    weight = lambda r, c: pl.BlockSpec((None, None, r, c), lambda i: (layer, which, 0, 0),
                                       pipeline_mode=pl.Buffered(1))
    in_specs = [row(d), _resident((1, d)), vec, vec, vec, weight(d, D_FF), weight(d, D_FF), weight(D_FF, d)]
    args = [x, g, shift, scale, gate, w1, w3, w2]
    if mixer is not None:
        in_specs += [row(ATTN_WIDTH), row(CONV_WIDTH), _layer_slice((d, d), layer), vec]
        args += list(mixer)
    return pl.pallas_call(
        functools.partial(_ffn_kernel, mixer is not None),
        out_shape=jax.ShapeDtypeStruct((m, d), F32),
        grid=(m // tm,),
        in_specs=in_specs,
        out_specs=row(d),
        compiler_params=_cparams(("arbitrary",)),
        name="adaln_swiglu_ffn",
    )(*args)


def _group_norm_scale(v, e_ref):
    sq = v * v
    hi = sq.astype(BF16)
    lo = (sq - hi.astype(F32)).astype(BF16)
    w = min(v.shape[1], MXU_DIM)
    e = e_ref[...]
    ss = jnp.concatenate(
        [jnp.dot(hi[:, c:c + w], e[0:w, 0:w], preferred_element_type=F32)
         + jnp.dot(lo[:, c:c + w], e[0:w, 0:w], preferred_element_type=F32) for c in range(0, v.shape[1], w)],
        axis=1)
    return lax.rsqrt(ss * (1.0 / HEAD_DIM) + EPS)


def _rope128(v, cos, sin_signed, even_lane):
    partner = jnp.where(even_lane, pltpu.roll(v, LANES - 1, 1), pltpu.roll(v, 1, 1))
    return v * cos + partner * sin_signed


def _short_conv_rows(u, before, after, cw, cb):
    rows = u.shape[0]
    r = lax.broadcasted_iota(jnp.int32, u.shape, 0)
    um1 = jnp.where(r == 0, before, pltpu.roll(u, 1, 0))
    up1 = jnp.where(r == rows - 1, after, pltpu.roll(u, rows - 1, 0))
    return cw[0:1] * um1 + cw[1:2] * u + cw[2:3] * up1 + cb


def _inproj_kernel(rope, tpb, x_ref, g_ref, shift_ref, scale_ref, w_ref, qg_ref, kg_ref, e_ref, *rest):
    if rope:
        cos_ref, sin_ref, xp_ref, xn_ref, cw_ref, cb_ref, q_ref, k_ref, v_ref, x0_ref, z_ref = rest
    else:
        q_ref, k_ref, v_ref, u_ref = rest
    h = _adaln(x_ref[...], g_ref[...], scale_ref[0], shift_ref[0]).astype(BF16)
    q = jnp.dot(h, w_ref[:, 0:ATTN_WIDTH], preferred_element_type=F32)
    k = jnp.dot(h, w_ref[:, ATTN_WIDTH:ATTN_WIDTH + KV_WIDTH], preferred_element_type=F32)
    q = q * _group_norm_scale(q, e_ref) * qg_ref[...]
    k = k * _group_norm_scale(k, e_ref) * kg_ref[...]
    if rope:
        cos = cos_ref[...]
        sin = sin_ref[...]
        even = (lax.broadcasted_iota(jnp.int32, cos.shape, 1) % 2) == 0
        q = jnp.concatenate(
            [_rope128(q[:, j * LANES:(j + 1) * LANES], cos, sin, even) for j in range(ATTN_WIDTH // LANES)],
            axis=1)
        k = _rope128(k, cos, sin, even)
    q_ref[...] = (q * (HEAD_DIM ** -0.5 * (LOG2_E if rope else 1.0))).astype(BF16)
    v = jnp.dot(h, w_ref[:, ATTN_WIDTH + KV_WIDTH:QKV_COLS], preferred_element_type=F32)
    k_ref[0] = k.T.astype(BF16)
    low = lax.broadcasted_iota(jnp.int32, v.shape, 1) < HEAD_DIM
    v_ref[...] = jnp.concatenate(
        [jnp.where(low, v, 1.0), jnp.where(low, pltpu.roll(v, HEAD_DIM, 1), 1.0)], axis=1).astype(BF16)
    if not rope:
        u_ref[...] = jnp.dot(h, w_ref[:, QKV_COLS:IN_COLS], preferred_element_type=F32)
        return
    i = pl.program_id(0)
    h_prev = _adaln(xp_ref[...], g_ref[...], scale_ref[0], shift_ref[0]).astype(BF16)
    h_next = _adaln(xn_ref[...], g_ref[...], scale_ref[0], shift_ref[0]).astype(BF16)
    u_ext = jnp.dot(jnp.concatenate([h_prev, h, h_next], axis=0), w_ref[:, QKV_COLS:IN_COLS],
                    preferred_element_type=F32)
    tm = h.shape[0]
    before = jnp.where((i % tpb) == 0, 0.0, u_ext[HALO - 1:HALO])
    after = jnp.where((i % tpb) == tpb - 1, 0.0, u_ext[HALO + tm:HALO + tm + 1])
    t = _short_conv_rows(u_ext[HALO:HALO + tm], before, after, cw_ref[...], cb_ref[...])
    x0_ref[...] = t[:, 0:CONV_WIDTH]
    z_ref[...] = t[:, CONV_WIDTH:2 * CONV_WIDTH] * t[:, 2 * CONV_WIDTH:]


def _inproj_call(x, g, shift, scale, w_in, layer, qg, kg, e, rows_per_batch, hyena=None):
    m, d = x.shape
    tm = min(rows_per_batch, 1024)
    tpb = rows_per_batch // tm
    vec = pl.BlockSpec((1, 1, d), lambda i: (i // tpb, 0, 0))
    row = lambda w: pl.BlockSpec((tm, w), lambda i: (i, 0))
    in_specs = [
        row(d),
        _resident((1, d)), vec, vec,
        _layer_slice((d, IN_COLS), layer),
        _resident((1, ATTN_WIDTH)), _resident((1, KV_WIDTH)),
        _resident((MXU_DIM, MXU_DIM)),
    ]
    args = [x, g, shift, scale, w_in, qg, kg, e]
    uw = 3 * CONV_WIDTH
    if hyena is not None:
        cos, sin, conv_w, conv_b = hyena
        tab = pl.BlockSpec((tm, LANES), lambda i: (i % tpb, 0))
        hpt = tm // HALO
        last_halo = m // HALO - 1
        in_specs += [
            tab, tab,
            pl.BlockSpec((HALO, d), lambda i: (jnp.maximum(i * hpt - 1, 0), 0)),
            pl.BlockSpec((HALO, d), lambda i: (jnp.minimum((i + 1) * hpt, last_halo), 0)),
            _resident((3, uw)), _resident((1, uw)),
        ]
        args += [cos, sin, x, x, conv_w, conv_b]
        conv_shapes = [jax.ShapeDtypeStruct((m, CONV_WIDTH), F32)] * 2
        conv_specs = [row(CONV_WIDTH)] * 2
    else:
        conv_shapes = [jax.ShapeDtypeStruct((m, uw), F32)]
        conv_specs = [row(uw)]
    tk = _key_chunk(rows_per_batch)
    per = tk // tm
    return pl.pallas_call(
        functools.partial(_inproj_kernel, hyena is not None, tpb),
        out_shape=[
            jax.ShapeDtypeStruct((m, ATTN_WIDTH), BF16),
            jax.ShapeDtypeStruct((m // tk, KV_WIDTH, tk), BF16),
            jax.ShapeDtypeStruct((m, 2 * KV_WIDTH), BF16),
        ] + conv_shapes,
        grid=(m // tm,),
        in_specs=in_specs,
        out_specs=[row(ATTN_WIDTH), pl.BlockSpec((1, KV_WIDTH, tm), lambda i: (i // per, 0, i % per)),
                   row(2 * KV_WIDTH)] + conv_specs,
        compiler_params=_cparams(("arbitrary",)),
        name="adaln_in_projection",
    )(*args)


_T5_STEPS = (12, 16, 23, 32, 46, 64, 91)


def _bias_kernel(tab_ref, sink_ref, shift_ref, o_ref, sink_rows_ref, shift_rows_ref, sink_term_rows_ref):
    for h in range(N_Q_HEADS):
        rows = slice(h * BLOCK, (h + 1) * BLOCK)
        sink_rows_ref[rows, :] = jnp.full((BLOCK, LANES), sink_ref[h], F32)
        shift_rows_ref[rows, :] = jnp.full((BLOCK, LANES), shift_ref[h], F32)
        sink_term_rows_ref[rows, :] = jnp.exp(jnp.full((BLOCK, LANES), sink_ref[h] - shift_ref[h], F32))
    qi = lax.broadcasted_iota(jnp.int32, (BLOCK, 3 * BLOCK), 0)
    kj = lax.broadcasted_iota(jnp.int32, (BLOCK, 3 * BLOCK), 1)
    rel = kj - BLOCK - qi
    n = jnp.abs(rel)
    half = N_BUCKETS // 2
    max_exact = half // 2
    large = jnp.full(n.shape, max_exact, jnp.int32)
    for t in _T5_STEPS:
        large = large + (n >= t).astype(jnp.int32)
    bucket = jnp.where(rel > 0, half, 0) + jnp.where(n < max_exact, n, large)
    for h in range(N_Q_HEADS):
        bias = jnp.zeros(n.shape, F32)
        for b in range(N_BUCKETS):
            bias = jnp.where(bucket == b, tab_ref[b, h], bias)
        o_ref[h] = jnp.where(n <= WINDOW, bias, NEG_INF)


def _bias_call(rel_table, sink, shift):
    rows = jax.ShapeDtypeStruct((N_Q_HEADS * BLOCK, LANES), F32)
    smem = pl.BlockSpec(memory_space=pltpu.SMEM)
    return pl.pallas_call(
        _bias_kernel,
        out_shape=(jax.ShapeDtypeStruct((N_Q_HEADS, BLOCK, 3 * BLOCK), F32), rows, rows, rows),
        in_specs=[smem, smem, smem],
        name="t5_bias_tile",
    )(rel_table, sink, shift)


def _mixer_ab_kernel(tq, tpb, fixed_shift, x_ref, q_ref, kc_ref, kp_ref, kn_ref, vc_ref, vp_ref, vn_ref,
                     uc_ref, up_ref, un_ref, bias_ref, sink_ref, shift_ref, sink_term_ref, cw_ref, cb_ref,
                     wo_ref, gate_ref, o_ref, kbuf, vbuf, qs, att):
    i = pl.program_id(0)
    first = (i % tpb) == 0
    last = (i % tpb) == tpb - 1
    nblk = tq // BLOCK
    grows = GQA_GROUP * BLOCK
    kbuf[:, 0:BLOCK] = kp_ref[0]
    kbuf[:, BLOCK:BLOCK + tq] = kc_ref[0]
    kbuf[:, BLOCK + tq:] = kn_ref[0]
    vbuf[0:BLOCK] = vp_ref[...]
    vbuf[BLOCK:BLOCK + tq] = vc_ref[...]
    vbuf[BLOCK + tq:] = vn_ref[...]
    for n in range(nblk):
        for h in range(N_Q_HEADS):
            g, j = divmod(h, GQA_GROUP)
            qs[g, n, j * BLOCK:(j + 1) * BLOCK, :] = q_ref[n * BLOCK:(n + 1) * BLOCK, h * HEAD_DIM:(h + 1) * HEAD_DIM]
    col = lax.broadcasted_iota(jnp.int32, (grows, 3 * BLOCK), 1)
    for n in range(nblk):
        keys = slice(n * BLOCK, (n + 3) * BLOCK)
        for g in range(N_KV_HEADS):
            s = jnp.dot(qs[g, n], kbuf[g * HEAD_DIM:(g + 1) * HEAD_DIM, keys], preferred_element_type=F32)
            s = s + bias_ref[g * GQA_GROUP:(g + 1) * GQA_GROUP].reshape(grows, 3 * BLOCK)
            if n == 0:
                s = jnp.where(jnp.logical_and(first, col < BLOCK), NEG_INF, s)
            if n == nblk - 1:
                s = jnp.where(jnp.logical_and(last, col >= 2 * BLOCK), NEG_INF, s)
            grp = slice(g * grows, (g + 1) * grows)
            if fixed_shift:
                mx = shift_ref[grp, :]
                sink_term = sink_term_ref[grp, :]
            else:
                sk = sink_ref[grp, :]
                mx = jnp.maximum(jnp.broadcast_to(jnp.max(s, axis=-1, keepdims=True), sk.shape), sk)
                sink_term = jnp.exp(sk - mx)
            p = jnp.exp(s - jnp.tile(mx, (1, 3)))
            pv = jnp.dot(p.astype(BF16), vbuf[keys, g * LANES:(g + 1) * LANES], preferred_element_type=F32)
            o = pv / (pltpu.roll(pv, HEAD_DIM, 1) + sink_term)
            for j in range(GQA_GROUP):
                h = g * GQA_GROUP + j
                att[n * BLOCK:(n + 1) * BLOCK, h * HEAD_DIM:(h + 1) * HEAD_DIM] = o[j * BLOCK:(j + 1) * BLOCK, 0:HEAD_DIM]

    gb = uc_ref[:, 0:CONV_WIDTH]
    p = uc_ref[:, CONV_WIDTH:2 * CONV_WIDTH] * uc_ref[:, 2 * CONV_WIDTH:]
    p_before = jnp.where(first, 0.0, up_ref[7:8, CONV_WIDTH:2 * CONV_WIDTH] * up_ref[7:8, 2 * CONV_WIDTH:])
    p_after = jnp.where(last, 0.0, un_ref[0:1, CONV_WIDTH:2 * CONV_WIDTH] * un_ref[0:1, 2 * CONV_WIDTH:])
    conv = gb * _short_conv_rows(p, p_before, p_after, cw_ref[...], cb_ref[...])

    y = jnp.dot(att[...].astype(BF16), wo_ref[0:ATTN_WIDTH, :], preferred_element_type=F32)
    y = y + jnp.dot(conv.astype(BF16), wo_ref[ATTN_WIDTH:, :], preferred_element_type=F32)
    o_ref[...] = x_ref[...] + gate_ref[0] * y


def _mixer_ab_call(x, q, kt, v1, u, bias, row_tables, fixed_shift, conv_w, conv_b, w_out, layer, gate,
                   rows_per_batch):
    m, d = x.shape
    tq = min(rows_per_batch, 1024)
    tpb = rows_per_batch // tq
    r = tq // BLOCK
    nb = m // BLOCK
    n8 = m // 8
    cur = lambda w: pl.BlockSpec((tq, w), lambda i: (i, 0))
    prev_of = lambda i: jnp.maximum(i * r - 1, 0)
    next_of = lambda i: jnp.minimum((i + 1) * r, nb - 1)
    v_prev = pl.BlockSpec((BLOCK, 2 * KV_WIDTH), lambda i: (prev_of(i), 0))
    v_next = pl.BlockSpec((BLOCK, 2 * KV_WIDTH), lambda i: (next_of(i), 0))
    tk = kt.shape[2]
    k_cur = pl.BlockSpec((1, KV_WIDTH, tq), lambda i: (i // (tk // tq), 0, i % (tk // tq)))
    kpb = tk // BLOCK
    k_prev = pl.BlockSpec((1, KV_WIDTH, BLOCK), lambda i: (prev_of(i) // kpb, 0, prev_of(i) % kpb))
    k_next = pl.BlockSpec((1, KV_WIDTH, BLOCK), lambda i: (next_of(i) // kpb, 0, next_of(i) % kpb))
    uw = 3 * CONV_WIDTH
    row_table = _resident((N_Q_HEADS * BLOCK, LANES))
    return pl.pallas_call(
        functools.partial(_mixer_ab_kernel, tq, tpb, fixed_shift),
        out_shape=jax.ShapeDtypeStruct((m, d), F32),
        grid=(m // tq,),
        in_specs=[
            cur(d), cur(ATTN_WIDTH),
            k_cur, k_prev, k_next,
            cur(2 * KV_WIDTH), v_prev, v_next,
            cur(uw),
            pl.BlockSpec((8, uw), lambda i: (jnp.maximum(i * (tq // 8) - 1, 0), 0)),
            pl.BlockSpec((8, uw), lambda i: (jnp.minimum((i + 1) * (tq // 8), n8 - 1), 0)),
            _resident((N_Q_HEADS, BLOCK, 3 * BLOCK)),
            row_table, row_table, row_table,
            _resident((3, CONV_WIDTH)), _resident((1, CONV_WIDTH)),
            _layer_slice((d, d), layer),
            pl.BlockSpec((1, 1, d), lambda i: (i // tpb, 0, 0)),
        ],
        out_specs=cur(d),
        scratch_shapes=[
            pltpu.VMEM((KV_WIDTH, tq + 2 * BLOCK), BF16),
            pltpu.VMEM((tq + 2 * BLOCK, 2 * KV_WIDTH), BF16),
            pltpu.VMEM((N_KV_HEADS, r, GQA_GROUP * BLOCK, HEAD_DIM), BF16),
            pltpu.VMEM((tq, ATTN_WIDTH), F32),
        ],
        compiler_params=_cparams(("arbitrary",)),
        name="windowed_attn_shortconv_outproj",
    )(x, q, kt, kt, kt, v1, v1, v1, u, u, u, bias, *row_tables, conv_w, conv_b, w_out, gate)


def _key_chunk(seq):
    return min(seq, 1024)


def _dense_attn_kernel(tq, tk, q_ref, k_ref, v_ref, o_ref, qs, s_buf, p_buf, rmax_buf, alpha_buf,
                       m_ref, acc_ref):
    nc = k_ref.shape[0]
    for j in range(GQA_GROUP):
        qs[j * tq:(j + 1) * tq, :] = q_ref[:, j * HEAD_DIM:(j + 1) * HEAD_DIM]
    m_ref[...] = jnp.full(m_ref.shape, -jnp.inf, F32)
    acc_ref[...] = jnp.zeros(acc_ref.shape, F32)

    def scores(c):
        s = jnp.dot(qs[...], k_ref[c], preferred_element_type=F32)
        s_buf[...] = s
        rmax_buf[...] = jnp.broadcast_to(jnp.max(s, axis=-1, keepdims=True), rmax_buf.shape)

    def softmax():
        m_old = m_ref[...]
        m_new = jnp.maximum(m_old, rmax_buf[...])
        alpha_buf[...] = jnp.exp2(m_old - m_new)
        m_ref[...] = m_new
        p = jnp.exp2(s_buf[...] - jnp.tile(m_new, (1, tk // LANES)))
        p_buf[...] = p.astype(BF16)

    def weighted_values(c):
        start = c * tk if isinstance(c, int) else pl.multiple_of(c * tk, tk)
        pv = jnp.dot(p_buf[...], v_ref[pl.ds(start, tk), :], preferred_element_type=F32)
        acc_ref[...] = alpha_buf[...] * acc_ref[...] + pv

    def step(t):
        static = isinstance(t, int)
        weighted_values(t)
        if not static or t + 1 < nc:
            softmax()
        if not static or t + 2 < nc:
            scores(t + 2)

    scores(0)
    softmax()
    if nc > 1:
        scores(1)

    def steady(t, carry):
        step(t)
        return carry

    lax.fori_loop(0, max(nc - 2, 0), steady, 0)
    for t in range(max(nc - 2, 0), nc):
        step(t)

    acc = acc_ref[...]
    o = acc / pltpu.roll(acc, HEAD_DIM, 1)
    for j in range(GQA_GROUP):
        o_ref[:, j * HEAD_DIM:(j + 1) * HEAD_DIM] = o[j * tq:(j + 1) * tq, 0:HEAD_DIM].astype(o_ref.dtype)


def _dense_attn_bounded_kernel(tq, tk, bound_ref, q_ref, k_ref, v_ref, o_ref, qs, p_buf, acc_ref):
    nc = k_ref.shape[0]
    for j in range(GQA_GROUP):
        qs[j * tq:(j + 1) * tq, :] = q_ref[:, j * HEAD_DIM:(j + 1) * HEAD_DIM]
    shift = bound_ref[0]

    def probabilities(c):
        s = jnp.dot(qs[...], k_ref[c], preferred_element_type=F32)
        p_buf[...] = jnp.exp2(s - shift).astype(BF16)

    def weighted_values(c):
        start = c * tk if isinstance(c, int) else pl.multiple_of(c * tk, tk)
        return jnp.dot(p_buf[...], v_ref[pl.ds(start, tk), :], preferred_element_type=F32)

    probabilities(0)
    acc_ref[...] = weighted_values(0)
    if nc > 1:
        probabilities(1)

    def steady(t, carry):
        acc_ref[...] += weighted_values(t)
        probabilities(t + 1)
        return carry

    lax.fori_loop(1, nc - 1, steady, 0)
    if nc > 1:
        acc_ref[...] += weighted_values(nc - 1)

    acc = acc_ref[...]
    o = acc / pltpu.roll(acc, HEAD_DIM, 1)
    for j in range(GQA_GROUP):
        o_ref[:, j * HEAD_DIM:(j + 1) * HEAD_DIM] = o[j * tq:(j + 1) * tq, 0:HEAD_DIM].astype(o_ref.dtype)


MAX_FIXED_SHIFT = 50.0


def _dense_attn_call(q, kt, v1, logit_bound, bsz, seq):
    tq, tk = min(seq, 1024), _key_chunk(seq)
    nq = seq // tq
    nc = seq // tk
    rows = GQA_GROUP * tq
    gw = GQA_GROUP * HEAD_DIM
    stat = pltpu.VMEM((rows, LANES), F32)
    q_spec = pl.BlockSpec((tq, gw), lambda b, g, i: (b * nq + i, g))
    k_spec = pl.BlockSpec((nc, HEAD_DIM, tk), lambda b, g, i: (b, g, 0), pipeline_mode=pl.Buffered(1))
    v_spec = pl.BlockSpec((seq, LANES), lambda b, g, i: (b, g), pipeline_mode=pl.Buffered(1))
    common = dict(
        out_shape=jax.ShapeDtypeStruct((bsz * seq, ATTN_WIDTH), BF16),
        grid=(bsz, N_KV_HEADS, nq),
        out_specs=q_spec,
        compiler_params=_cparams(("arbitrary", "arbitrary", "arbitrary")),
    )

    def running_max():
        return pl.pallas_call(
            functools.partial(_dense_attn_kernel, tq, tk),
            in_specs=[q_spec, k_spec, v_spec],
            scratch_shapes=[
                pltpu.VMEM((rows, HEAD_DIM), BF16),
                pltpu.VMEM((rows, tk), F32), pltpu.VMEM((rows, tk), BF16),
                stat, stat, stat, stat,
            ],
            name="dense_gqa_attention", **common,
        )(q, kt, v1)

    def fixed_shift():
        return pl.pallas_call(
            functools.partial(_dense_attn_bounded_kernel, tq, tk),
            in_specs=[pl.BlockSpec(memory_space=pltpu.SMEM), q_spec, k_spec, v_spec],
            scratch_shapes=[pltpu.VMEM((rows, HEAD_DIM), BF16), pltpu.VMEM((rows, tk), BF16), stat],
            name="dense_gqa_attention_fixed_shift", **common,
        )(logit_bound.reshape(1), q, kt, v1)

    return lax.cond(logit_bound <= MAX_FIXED_SHIFT, fixed_shift, running_max)


def _hp_dot(a, b):
    return jnp.dot(a, b, preferred_element_type=F32, precision=lax.Precision.HIGHEST)


def _filter_trunk_kernel(z_ref, w1_ref, b1_ref, w2_ref, b2_ref, w3_ref, b3_ref, fq_ref, o_ref):
    fq = fq_ref[...]
    h = jnp.sin(fq * (_hp_dot(w1_ref[...], z_ref[...]) + b1_ref[...]))
    h = jnp.sin(fq * (_hp_dot(w2_ref[...], h) + b2_ref[...]))
    h = jnp.sin(fq * (_hp_dot(w3_ref[...], h) + b3_ref[...]))
    o_ref[...] = h.T


def _filter_trunk_call(zfeat_t, w1, b1, w2, b2, w3, b3, freq):
    rows = zfeat_t.shape[1]
    tr = min(rows, 2048)
    fw = HYENA_FILTER_WIDTH
    col = lambda v: v.reshape(fw, 1)
    return pl.pallas_call(
        _filter_trunk_kernel,
        out_shape=jax.ShapeDtypeStruct((rows, fw), F32),
        grid=(rows // tr,),
        in_specs=[
            pl.BlockSpec((HYENA_EMB_PAD, tr), lambda i: (0, i)),
            _resident((fw, HYENA_EMB_PAD)), _resident((fw, 1)),
            _resident((fw, fw)), _resident((fw, 1)),
            _resident((fw, fw)), _resident((fw, 1)),
            _resident((fw, 1)),
        ],
        out_specs=pl.BlockSpec((tr, fw), lambda i: (i, 0)),
        compiler_params=_cparams(("arbitrary",)),
        name="hyena_filter_trunk",
    )(zfeat_t, w1.T, col(b1), w2.T, col(b2), w3.T, col(b3), col(freq))


def _dft_stage1(src_ref, tab_ref, re_ref, im_ref, n_rows, k1p, n2_major=False, pitch=DFT_N2, src_pitch=DFT_N2):
    def body(n2, carry):
        xs = src_ref[pl.ds(n2, n_rows, stride=src_pitch), :].astype(BF16)
        res = jnp.dot(tab_ref[n2], xs, preferred_element_type=F32)
        dst = pl.ds(pl.multiple_of(n2 * k1p, 8), k1p) if n2_major else pl.ds(n2, k1p, stride=pitch)
        re_ref[dst, :] = res[0:k1p]
        im_ref[dst, :] = res[k1p:]
        return carry

    lax.fori_loop(0, DFT_N2, body, 0, unroll=8)


def _filter_spec_kernel(seq, k1p, h_ref, w4f_ref, w4b_ref, dl_ref, e1_ref, f2_ref,
                        kr_ref, ki_ref, kfull, bre, bim):
    n = 2 * seq
    chunk = min(seq, 1024)
    dl = dl_ref[...]
    live = seq // DFT_N2 + 1

    def fill_half(w_ref, backward):
        def fill(c, ss):
            r0 = pl.multiple_of(c * chunk, chunk)
            val = jnp.dot(h_ref[pl.ds(r0, chunk), :].astype(BF16), w_ref[...], preferred_element_type=F32)
            rows = r0 + lax.broadcasted_iota(jnp.int32, (chunk, 1), 0)
            lag = ((n - rows) if backward else rows).astype(F32)
            val = val * jnp.exp(-(lag * (1.0 / (seq - 1))) * dl)
            if backward:
                val = jnp.where(rows == seq, 0.0, val)
            for k in range(chunk // DFT_N2):
                dst = pl.multiple_of((c * (chunk // DFT_N2) + k) * SPECTRUM_PITCH, 8)
                kfull[pl.ds(dst, DFT_N2), :] = val[k * DFT_N2:(k + 1) * DFT_N2]
            return ss + jnp.sum(val * val, axis=0, keepdims=True)
        return fill

    ss = lax.fori_loop(0, seq // chunk, fill_half(w4f_ref, False), jnp.zeros((1, dl.shape[1]), F32))
    ss = lax.fori_loop(seq // chunk, n // chunk, fill_half(w4b_ref, True), ss)
    norm = lax.rsqrt(ss + EPS)

    _dft_stage1(kfull, e1_ref, bre, bim, n // DFT_N2, k1p, n2_major=True, src_pitch=SPECTRUM_PITCH)

    kr_ref[live * DFT_N2:, :] = jnp.zeros(((k1p - live) * DFT_N2, kr_ref.shape[1]), F32)
    ki_ref[live * DFT_N2:, :] = jnp.zeros(((k1p - live) * DFT_N2, ki_ref.shape[1]), F32)
    f2 = f2_ref[...]

    def stage2(a, carry):
        src = pl.ds(a, DFT_N2, stride=k1p)
        slab = jnp.concatenate([bre[src, :], bim[src, :]], axis=0).astype(BF16)
        xf = jnp.dot(f2, slab, preferred_element_type=F32)
        rows = pl.ds(pl.multiple_of(a * DFT_N2, DFT_N2), DFT_N2)
        kr_ref[rows, :] = xf[0:DFT_N2] * norm
        ki_ref[rows, :] = xf[DFT_N2:] * norm
        return carry

    lax.fori_loop(0, live, stage2, 0, unroll=8)


def _filter_spec_call(h3, w4, deltas, e1f, f2f, seq):
    n = 2 * seq
    k1p = e1f.shape[1] // 2
    ct = LANES
    nct = CONV_WIDTH // ct
    fw = HYENA_FILTER_WIDTH
    spec_rows = k1p * DFT_N2
    out = pl.BlockSpec((spec_rows, ct), lambda j: (0, j))
    return pl.pallas_call(
        functools.partial(_filter_spec_kernel, seq, k1p),
        out_shape=(jax.ShapeDtypeStruct((spec_rows, CONV_WIDTH), F32),
                   jax.ShapeDtypeStruct((spec_rows, CONV_WIDTH), F32)),
        grid=(nct,),
        in_specs=[
            _resident((n, fw)),
            pl.BlockSpec((fw, ct), lambda j: (0, j)),
            pl.BlockSpec((fw, ct), lambda j: (0, nct + j)),
            pl.BlockSpec((1, ct), lambda j: (0, j)),
            _resident(e1f.shape),
            _resident((2 * DFT_N2, 2 * DFT_N2)),
        ],
        out_specs=(out, out),
        scratch_shapes=[pltpu.VMEM((n // DFT_N2 * SPECTRUM_PITCH, ct), F32), pltpu.VMEM((spec_rows, ct), F32),
                        pltpu.VMEM((spec_rows, ct), F32)],
        compiler_params=_cparams(("arbitrary",)),
        name="hyena_filter_spectrum",
    )(h3, w4, w4, deltas, e1f, f2f)


def _long_conv_kernel(seq, k1p, z_ref, x0_ref, kr_ref, ki_ref, skip_ref, e1_ref, f2f_ref, f2i_ref,
                      einv_ref, o_ref, br, bi, cbuf):
    n1h = seq // DFT_N2
    pitch = br.shape[0] // k1p
    _dft_stage1(z_ref, e1_ref, br, bi, n1h, k1p, pitch=pitch)

    f2f = f2f_ref[...]
    f2i = f2i_ref[...]

    ct = br.shape[1]

    def spectral(a0, count):
        def start(x, m):
            return x if isinstance(x, int) else pl.multiple_of(x, m)
        rows = [pl.ds(start((a0 + t) * pitch, 8), DFT_N2) for t in range(count)]
        krows = [pl.ds(start((a0 + t) * DFT_N2, DFT_N2), DFT_N2) for t in range(count)]
        lanes = lambda parts: jnp.concatenate(parts, axis=1)
        slab = lanes([jnp.concatenate([br[r, :], bi[r, :]], axis=0) for r in rows]).astype(BF16)
        xf = jnp.dot(f2f, slab, preferred_element_type=F32)
        xr, xi = xf[0:DFT_N2], xf[DFT_N2:]
        kr, ki = lanes([kr_ref[r, :] for r in krows]), lanes([ki_ref[r, :] for r in krows])
        y = jnp.concatenate([xr * kr - xi * ki, xr * ki + xi * kr], axis=0).astype(BF16)
        d = jnp.dot(f2i, y, preferred_element_type=F32)
        for t, r in enumerate(rows):
            br[r, :] = d[0:DFT_N2, t * ct:(t + 1) * ct]
            bi[r, :] = d[DFT_N2:, t * ct:(t + 1) * ct]

    live = n1h + 1
    pair = MXU_DIM // ct

    def spectral_pairs(p, carry):
        spectral(p * pair, pair)
        return carry

    lax.fori_loop(0, live // pair, spectral_pairs, 0, unroll=4)
    for a in range(live // pair * pair, live):
        spectral(a, 1)

    skip = skip_ref[...]

    def synth(n2, carry):
        ds_spec = pl.ds(n2, k1p, stride=pitch)
        d = jnp.concatenate([br[ds_spec, :], bi[ds_spec, :]], axis=0).astype(BF16)
        cbuf[pl.ds(n2, n1h, stride=pitch), :] = jnp.dot(einv_ref[n2], d, preferred_element_type=F32)
        return carry

    lax.fori_loop(0, DFT_N2, synth, 0, unroll=8)

    def gate(n1, carry):
        rows = pl.ds(pl.multiple_of(n1 * DFT_N2, DFT_N2), DFT_N2)
        conv = cbuf[pl.ds(pl.multiple_of(n1 * pitch, 8), DFT_N2), :]
        o_ref[rows, :] = x0_ref[rows, :] * (conv + skip * z_ref[rows, :])
        return carry

    lax.fori_loop(0, n1h, gate, 0, unroll=4)


def _long_conv_call(z, x0, kr, ki, skip, e1, f2f, f2i, einv, bsz, seq):
    ct = LANES
    nct = CONV_WIDTH // ct
    k1p = e1.shape[1] // 2
    spec_rows = k1p * DFT_N2
    big = lambda: pl.BlockSpec((seq, ct), lambda j, b: (b, j), pipeline_mode=pl.Buffered(1))
    spec = lambda: pl.BlockSpec((spec_rows, ct), lambda j, b: (0, j), pipeline_mode=pl.Buffered(1))
    return pl.pallas_call(
        functools.partial(_long_conv_kernel, seq, k1p),
        out_shape=jax.ShapeDtypeStruct((bsz * seq, CONV_WIDTH), F32),
        grid=(nct, bsz),
        in_specs=[
            big(), big(), spec(), spec(),
            pl.BlockSpec((1, ct), lambda j, b: (0, j)),
            _resident(e1.shape),
            _resident((2 * DFT_N2, 2 * DFT_N2)), _resident((2 * DFT_N2, 2 * DFT_N2)),
            _resident(einv.shape),
        ],
        out_specs=pl.BlockSpec((seq, ct), lambda j, b: (b, j)),
        scratch_shapes=[pltpu.VMEM((k1p * SPECTRUM_PITCH, ct), F32), pltpu.VMEM((k1p * SPECTRUM_PITCH, ct), F32),
                        pltpu.VMEM((seq // DFT_N2 * SPECTRUM_PITCH, ct), F32)],
        compiler_params=_cparams(("arbitrary", "arbitrary")),
        name="hyena_long_conv",
    )(z, x0, kr, ki, skip, e1, f2f, f2i, einv)


def _rope_tables(seq):
    t = jnp.arange(seq)
    row = (t // GRID_W).astype(F32)
    col = (t % GRID_W).astype(F32)
    half = HEAD_DIM // 2
    inv = ROPE_THETA ** (-jnp.arange(0, half, 2, dtype=F32) / half)
    ang = jnp.concatenate([row[:, None] * inv, col[:, None] * inv], axis=-1)
    ang = jnp.repeat(ang, 2, axis=-1)
    sign = jnp.where(jnp.arange(HEAD_DIM) % 2 == 0, -1.0, 1.0).astype(F32)
    reps = LANES // HEAD_DIM
    return jnp.tile(jnp.cos(ang), (1, reps)), jnp.tile(jnp.sin(ang) * sign, (1, reps))


def _dft_tables(seq):
    n = 2 * seq
    n1 = n // DFT_N2
    k1 = n1 // 2 + 1
    k1p = -(-k1 // 8) * 8
    a = jnp.arange(k1p)
    live = (a < k1).astype(F32)
    ang1 = ((a[:, None] * jnp.arange(n1)[None, :]) % n1).astype(F32) * (2.0 * math.pi / n1)
    ang2 = (jnp.arange(DFT_N2)[:, None] * a[None, :]).astype(F32) * (2.0 * math.pi / n)
    c1, s1 = jnp.cos(ang1) * live[:, None], jnp.sin(ang1) * live[:, None]
    c2, s2 = jnp.cos(ang2), jnp.sin(ang2)
    cos_a = c1[None] * c2[:, :, None] - s1[None] * s2[:, :, None]
    sin_a = s1[None] * c2[:, :, None] + c1[None] * s2[:, :, None]
    e1_full = jnp.concatenate([cos_a, -sin_a], axis=1).astype(BF16)
    e1_half = e1_full[:, :, :n1 // 2]
    wgt = jnp.where((a == 0) | (a == n1 // 2), 1.0, 2.0) / n
    c1t, s1t = (c1 * wgt[:, None]).T[:n1 // 2], (s1 * wgt[:, None]).T[:n1 // 2]
    cos_s = c1t[None] * c2[:, None, :] - s1t[None] * s2[:, None, :]
    sin_s = s1t[None] * c2[:, None, :] + c1t[None] * s2[:, None, :]
    einv = jnp.concatenate([cos_s, -sin_s], axis=2).astype(BF16)
    kk = jnp.arange(DFT_N2)
    phi = ((kk[:, None] * kk[None, :]) % DFT_N2).astype(F32) * (2.0 * math.pi / DFT_N2)
    ci, si = jnp.cos(phi), jnp.sin(phi)
    f2f = jnp.block([[ci, si], [-si, ci]]).astype(BF16)
    f2i = jnp.block([[ci, -si], [si, ci]]).astype(BF16)
    return e1_full, e1_half, einv, f2f, f2i


def _filter_features_t(seq):
    j = jnp.arange(2 * seq)
    lag = jnp.where(j < seq, j, jnp.where(j == seq, 0, 2 * seq - j)).astype(F32)[None, :]
    t = lag / (seq - 1)
    bands = (HYENA_EMB - 1) // 2
    w = (2.0 * math.pi / seq) * lag
    fr = jnp.linspace(1e-4, bands - 1, bands, dtype=F32)[:, None]
    pad = jnp.zeros((HYENA_EMB_PAD - HYENA_EMB, 2 * seq), F32)
    return jnp.concatenate([t, jnp.cos(fr * w), -jnp.sin(fr * w), pad], axis=0)


def kernel(x, c, rel_table, norm_g, w_mod, b_mod, w_in, w_out, ffn_w1, ffn_w3, ffn_w2, a_qk_g, a_sink,
           b_conv_w, b_conv_b, c_qk_g, d_conv_w, d_conv_b, d_f_w1, d_f_b1, d_f_w2, d_f_b2, d_f_w3,
           d_f_b3, d_f_w4, d_f_freq, d_skip):
    bsz, seq, d = x.shape
    depth = w_mod.shape[0]
    m_rows = bsz * seq
    mod =_mod_call(c, w_mod, b_mod).reshape(depth, bsz, 3, 3, 1, d)
    e_blk = jnp.kron(jnp.eye(MXU_DIM // HEAD_DIM, dtype=F32), jnp.ones((HEAD_DIM, HEAD_DIM), F32)).astype(BF16)

    w1b, w3b = _to_bf16_from_transposed(ffn_w1), _to_bf16_from_transposed(ffn_w3)
    w2b, w_in_b, w_out_b = (w.astype(BF16) for w in (ffn_w2, w_in, w_out))

    def ffn(xf, l, which, sub, mixer=None):
        return _ffn_call(xf, norm_g[l, sub][None], mod[l, :, sub, 0], mod[l, :, sub, 1], mod[l, :, sub, 2],
                         w1b, w3b, w2b, l, which, seq, mixer)

    xf = x.reshape(m_rows, d)
    for l in range(depth):
        j = l // 2
        xf = ffn(xf, l, 0, 0)
        even = l % 2 == 0
        qk_g = a_qk_g[j] if even else c_qk_g[j]
        hyena = None if even else _rope_tables(seq) + (d_conv_w[j], d_conv_b[j][None])
        q, k, v, *conv_in = _inproj_call(
            xf, norm_g[l, 1][None], mod[l, :, 1, 0], mod[l, :, 1, 1], w_in_b, l,
            jnp.tile(qk_g[0], N_Q_HEADS)[None], jnp.tile(qk_g[1], N_KV_HEADS)[None], e_blk, seq, hyena)
        gate = mod[l, :, 1, 2]
        if even:
            (u,) = conv_in
            qk_bound = (1.01 * HEAD_DIM ** 0.5) * jnp.max(jnp.abs(qk_g[0])) * jnp.max(jnp.abs(qk_g[1]))
            shift = jnp.maximum(qk_bound + jnp.max(rel_table, axis=0), a_sink[j])
            worst_gap = jnp.max(shift + qk_bound - rel_table[0])
            bias, *row_tables = _bias_call(rel_table, a_sink[j], shift)
            mixer_ab = lambda fixed: _mixer_ab_call(xf, q, k, v, u, bias, row_tables, fixed, b_conv_w[j],
                                                    b_conv_b[j][None], w_out_b, l, gate, seq)
            xf = lax.cond(worst_gap <= MAX_FIXED_SHIFT_GAP, lambda: mixer_ab(True), lambda: mixer_ab(False))
            pending = None
        else:
            x0, z = conv_in
            logit_bound = (1.01 * LOG2_E * HEAD_DIM ** 0.5) * jnp.max(jnp.abs(qk_g[0])) * jnp.max(jnp.abs(qk_g[1]))
            att = _dense_attn_call(q, k, v, logit_bound, bsz, seq)
            e1_full, e1_half, einv, f2f, f2i = _dft_tables(seq)
            fw1 = jnp.pad(d_f_w1[j], ((0, HYENA_EMB_PAD - HYENA_EMB), (0, 0)))
            h3 = _filter_trunk_call(_filter_features_t(seq), fw1, d_f_b1[j], d_f_w2[j], d_f_b2[j],
                                    d_f_w3[j], d_f_b3[j], d_f_freq[j])
            deltas = jnp.abs(jnp.linspace(HYENA_MIN_DECAY, HYENA_MAX_DECAY, CONV_WIDTH, dtype=F32))[None]
            kr, ki = _filter_spec_call(h3, d_f_w4[j].astype(BF16), deltas, e1_full, f2f, seq)
            y = _long_conv_call(z, x0, kr, ki, d_skip[j][None], e1_half, f2f, f2i, einv, bsz, seq)
            pending = (att, y, w_out_b, gate)
        xf = ffn(xf, l, 1, 2, pending)
    return xf.reshape(bsz, seq, d)
```

```python
import functools
import math

import jax
import jax.numpy as jnp
import numpy as np
from jax import lax
from jax.experimental import pallas as pl
from jax.experimental.pallas import tpu as pltpu

D_MODEL = 1024
HEAD_DIM = 64
N_Q_HEADS = 8
N_KV_HEADS = 2
GQA_GROUP = N_Q_HEADS // N_KV_HEADS
ATTN_WIDTH = N_Q_HEADS * HEAD_DIM
KV_WIDTH = N_KV_HEADS * HEAD_DIM
QKV_COLS = ATTN_WIDTH + 2 * KV_WIDTH
CONV_WIDTH = D_MODEL - ATTN_WIDTH
IN_COLS = QKV_COLS + 3 * CONV_WIDTH
D_FF = 2752
BLOCK = 128
WINDOW = 128
N_BUCKETS = 32
MAX_DISTANCE = 128
GRID_W = 64
ROPE_THETA = 10000.0
HYENA_EMB = 33
HYENA_FILTER_WIDTH = 64
HYENA_EMB_PAD = 64
HYENA_MIN_DECAY = math.log(1e-2) / 0.3
HYENA_MAX_DECAY = math.log(1e-2) / 1.5
EPS = 1e-6
NEG_INF = -1e30
LOG2_E = 1.4426950408889634
MAX_FIXED_SHIFT_GAP = 80.0

LANES = 128
HALO = 16
MXU_DIM = 256
FF_CHUNK = MXU_DIM
VMEM_LIMIT = 56 * 1024 * 1024
DFT_N2 = 128
SPECTRUM_PITCH = DFT_N2 + 8

BF16 = jnp.bfloat16
F32 = jnp.float32


def _cparams(sem):
    return pltpu.CompilerParams(dimension_semantics=sem, vmem_limit_bytes=VMEM_LIMIT)


def _resident(shape):
    nd = len(shape)
    return pl.BlockSpec(shape, lambda *_: (0,) * nd, pipeline_mode=pl.Buffered(1))


def _layer_slice(shape, layer):
    return pl.BlockSpec((None,) + shape, lambda *_: (layer, 0, 0), pipeline_mode=pl.Buffered(1))


def _adaln(x, g, scale, shift):
    y = x * lax.rsqrt(jnp.mean(x * x, axis=-1, keepdims=True) + EPS) * g
    return y * (1.0 + scale) + shift


def _mod_kernel(c_ref, w_ref, b_ref, o_ref):
    w = w_ref[0]
    for b in range(c_ref.shape[0]):
        cc = c_ref[b]
        cond = cc * jax.nn.sigmoid(cc)
        o_ref[0, b:b + 1, :] = jnp.sum(w * cond, axis=0, keepdims=True) + b_ref[0]


def _mod_call(c, w_mod, b_mod):
    depth, d, n = w_mod.shape
    bsz = c.shape[0]
    tn = 1152 if n % 1152 == 0 else 512
    return pl.pallas_call(
        _mod_kernel,
        out_shape=jax.ShapeDtypeStruct((depth, bsz, n), F32),
        grid=(depth, n // tn),
        in_specs=[
            pl.BlockSpec((bsz, d, 1), lambda l, j: (0, 0, 0)),
            pl.BlockSpec((1, d, tn), lambda l, j: (l, 0, j)),
            pl.BlockSpec((1, 1, tn), lambda l, j: (l, 0, j)),
        ],
        out_specs=pl.BlockSpec((1, bsz, tn), lambda l, j: (l, 0, j)),
        compiler_params=_cparams(("arbitrary", "arbitrary")),
        name="adaln_modulation",
    )(c.reshape(bsz, d, 1), w_mod, b_mod.reshape(depth, 1, n))


def _cast_transpose_kernel(wt_ref, o_ref):
    cols = wt_ref.shape[0]
    starts = list(range(0, cols - MXU_DIM, MXU_DIM)) + [cols - MXU_DIM]
    for c0 in starts:
        o_ref[:, c0:c0 + MXU_DIM] = wt_ref[c0:c0 + MXU_DIM, :].T.astype(o_ref.dtype)


def _to_bf16_from_transposed(w):
    lead, (rows, cols) = w.shape[:-2], w.shape[-2:]
    none = (None,) * len(lead)
    return pl.pallas_call(
        _cast_transpose_kernel,
        out_shape=jax.ShapeDtypeStruct(w.shape, BF16),
        grid=lead,
        in_specs=[pl.BlockSpec(none + (cols, rows), lambda *idx: idx + (0, 0))],
        out_specs=pl.BlockSpec(none + (rows, cols), lambda *idx: idx + (0, 0)),
        compiler_params=_cparams(("arbitrary",) * len(lead)),
        name="weights_to_bf16_transposing",
    )(jnp.swapaxes(w, -1, -2))


def _ffn_kernel(mixer_update, x_ref, g_ref, shift_ref, scale_ref, gate_ref, w1_ref, w3_ref, w2_ref, *rest):
    x = x_ref[...]
    if mixer_update:
        att_ref, y_ref, wo_ref, mgate_ref, o_ref = rest
        upd = jnp.dot(att_ref[...], wo_ref[0:ATTN_WIDTH, :], preferred_element_type=F32)
        upd = upd + jnp.dot(y_ref[...].astype(BF16), wo_ref[ATTN_WIDTH:, :], preferred_element_type=F32)
        x = x + mgate_ref[0] * upd
    else:
        (o_ref,) = rest
    h = _adaln(x, g_ref[...], scale_ref[0], shift_ref[0]).astype(BF16)
    acc = jnp.zeros(x.shape, F32)
    for c0 in range(0, D_FF, FF_CHUNK):
        cols = slice(c0, min(c0 + FF_CHUNK, D_FF))
        a = jnp.dot(h, w1_ref[:, cols], preferred_element_type=F32)
        b = jnp.dot(h, w3_ref[:, cols], preferred_element_type=F32)
        act = (a * jax.nn.sigmoid(a) * b).astype(BF16)
        acc = acc + jnp.dot(act, w2_ref[cols, :], preferred_element_type=F32)
    o_ref[...] = x + (0.5 * gate_ref[0]) * acc


def _ffn_call(x, g, shift, scale, gate, w1, w3, w2, layer, which, rows_per_batch, mixer=None):
    m, d = x.shape
    tm = min(rows_per_batch, 1024)
    tpb = rows_per_batch // tm
    row = lambda w: pl.BlockSpec((tm, w), lambda i: (i, 0))
    vec = pl.BlockSpec((1, 1, d), lambda i: (i // tpb, 0, 0))
    weight = lambda r, c: pl.BlockSpec((None, None, r, c), lambda i: (layer, which, 0, 0),
                                       pipeline_mode=pl.Buffered(1))
    in_specs = [row(d), _resident((1, d)), vec, vec, vec, weight(d, D_FF), weight(d, D_FF), weight(D_FF, d)]
    args = [x, g, shift, scale, gate, w1, w3, w2]
    if mixer is not None:
        in_specs += [row(ATTN_WIDTH), row(CONV_WIDTH), _layer_slice((d, d), layer), vec]
        args += list(mixer)
    return pl.pallas_call(
        functools.partial(_ffn_kernel, mixer is not None),
        out_shape=jax.ShapeDtypeStruct((m, d), F32),
        grid=(m // tm,),
        in_specs=in_specs,
        out_specs=row(d),
        compiler_params=_cparams(("arbitrary",)),
        name="adaln_swiglu_ffn",
    )(*args)


def _group_norm_scale(v, e_ref):
    sq = v * v
    hi = sq.astype(BF16)
    lo = (sq - hi.astype(F32)).astype(BF16)
    w = min(v.shape[1], MXU_DIM)
    e = e_ref[...]
    ss = jnp.concatenate(
        [jnp.dot(hi[:, c:c + w], e[0:w, 0:w], preferred_element_type=F32)
         + jnp.dot(lo[:, c:c + w], e[0:w, 0:w], preferred_element_type=F32) for c in range(0, v.shape[1], w)],
        axis=1)
    return lax.rsqrt(ss * (1.0 / HEAD_DIM) + EPS)


def _rope128(v, cos, sin_signed, even_lane):
    partner = jnp.where(even_lane, pltpu.roll(v, LANES - 1, 1), pltpu.roll(v, 1, 1))
    return v * cos + partner * sin_signed


def _short_conv_rows(u, before, after, cw, cb):
    rows = u.shape[0]
    r = lax.broadcasted_iota(jnp.int32, u.shape, 0)
    um1 = jnp.where(r == 0, before, pltpu.roll(u, 1, 0))
    up1 = jnp.where(r == rows - 1, after, pltpu.roll(u, rows - 1, 0))
    return cw[0:1] * um1 + cw[1:2] * u + cw[2:3] * up1 + cb


def _inproj_kernel(rope, tpb, x_ref, g_ref, shift_ref, scale_ref, w_ref, qg_ref, kg_ref, e_ref, *rest):
    if rope:
        cos_ref, sin_ref, xp_ref, xn_ref, cw_ref, cb_ref, q_ref, k_ref, v_ref, x0_ref, z_ref = rest
    else:
        q_ref, k_ref, v_ref, u_ref = rest
    h = _adaln(x_ref[...], g_ref[...], scale_ref[0], shift_ref[0]).astype(BF16)
    q = jnp.dot(h, w_ref[:, 0:ATTN_WIDTH], preferred_element_type=F32)
    k = jnp.dot(h, w_ref[:, ATTN_WIDTH:ATTN_WIDTH + KV_WIDTH], preferred_element_type=F32)
    q = q * _group_norm_scale(q, e_ref) * qg_ref[...]
    k = k * _group_norm_scale(k, e_ref) * kg_ref[...]
    if rope:
        cos = cos_ref[...]
        sin = sin_ref[...]
        even = (lax.broadcasted_iota(jnp.int32, cos.shape, 1) % 2) == 0
        q = jnp.concatenate(
            [_rope128(q[:, j * LANES:(j + 1) * LANES], cos, sin, even) for j in range(ATTN_WIDTH // LANES)],
            axis=1)
        k = _rope128(k, cos, sin, even)
    q_ref[...] = (q * (HEAD_DIM ** -0.5 * (LOG2_E if rope else 1.0))).astype(BF16)
    v = jnp.dot(h, w_ref[:, ATTN_WIDTH + KV_WIDTH:QKV_COLS], preferred_element_type=F32)
    k_ref[0] = k.T.astype(BF16)
    low = lax.broadcasted_iota(jnp.int32, v.shape, 1) < HEAD_DIM
    v_ref[...] = jnp.concatenate(
        [jnp.where(low, v, 1.0), jnp.where(low, pltpu.roll(v, HEAD_DIM, 1), 1.0)], axis=1).astype(BF16)
    if not rope:
        u_ref[...] = jnp.dot(h, w_ref[:, QKV_COLS:IN_COLS], preferred_element_type=F32)
        return
    i = pl.program_id(0)
    h_prev = _adaln(xp_ref[...], g_ref[...], scale_ref[0], shift_ref[0]).astype(BF16)
    h_next = _adaln(xn_ref[...], g_ref[...], scale_ref[0], shift_ref[0]).astype(BF16)
    u_ext = jnp.dot(jnp.concatenate([h_prev, h, h_next], axis=0), w_ref[:, QKV_COLS:IN_COLS],
                    preferred_element_type=F32)
    tm = h.shape[0]
    before = jnp.where((i % tpb) == 0, 0.0, u_ext[HALO - 1:HALO])
    after = jnp.where((i % tpb) == tpb - 1, 0.0, u_ext[HALO + tm:HALO + tm + 1])
    t = _short_conv_rows(u_ext[HALO:HALO + tm], before, after, cw_ref[...], cb_ref[...])
    x0_ref[...] = t[:, 0:CONV_WIDTH]
    z_ref[...] = t[:, CONV_WIDTH:2 * CONV_WIDTH] * t[:, 2 * CONV_WIDTH:]


def _inproj_call(x, g, shift, scale, w_in, layer, qg, kg, e, rows_per_batch, hyena=None):
    m, d = x.shape
    tm = min(rows_per_batch, 1024)
    tpb = rows_per_batch // tm
    vec = pl.BlockSpec((1, 1, d), lambda i: (i // tpb, 0, 0))
    row = lambda w: pl.BlockSpec((tm, w), lambda i: (i, 0))
    in_specs = [
        row(d),
        _resident((1, d)), vec, vec,
        _layer_slice((d, IN_COLS), layer),
        _resident((1, ATTN_WIDTH)), _resident((1, KV_WIDTH)),
        _resident((MXU_DIM, MXU_DIM)),
    ]
    args = [x, g, shift, scale, w_in, qg, kg, e]
    uw = 3 * CONV_WIDTH
    if hyena is not None:
        cos, sin, conv_w, conv_b = hyena
        tab = pl.BlockSpec((tm, LANES), lambda i: (i % tpb, 0))
        hpt = tm // HALO
        last_halo = m // HALO - 1
        in_specs += [
            tab, tab,
            pl.BlockSpec((HALO, d), lambda i: (jnp.maximum(i * hpt - 1, 0), 0)),
            pl.BlockSpec((HALO, d), lambda i: (jnp.minimum((i + 1) * hpt, last_halo), 0)),
            _resident((3, uw)), _resident((1, uw)),
        ]
        args += [cos, sin, x, x, conv_w, conv_b]
        conv_shapes = [jax.ShapeDtypeStruct((m, CONV_WIDTH), F32)] * 2
        conv_specs = [row(CONV_WIDTH)] * 2
    else:
        conv_shapes = [jax.ShapeDtypeStruct((m, uw), F32)]
        conv_specs = [row(uw)]
    tk = _key_chunk(rows_per_batch)
    per = tk // tm
    return pl.pallas_call(
        functools.partial(_inproj_kernel, hyena is not None, tpb),
        out_shape=[
            jax.ShapeDtypeStruct((m, ATTN_WIDTH), BF16),
            jax.ShapeDtypeStruct((m // tk, KV_WIDTH, tk), BF16),
            jax.ShapeDtypeStruct((m, 2 * KV_WIDTH), BF16),
        ] + conv_shapes,
        grid=(m // tm,),
        in_specs=in_specs,
        out_specs=[row(ATTN_WIDTH), pl.BlockSpec((1, KV_WIDTH, tm), lambda i: (i // per, 0, i % per)),
                   row(2 * KV_WIDTH)] + conv_specs,
        compiler_params=_cparams(("arbitrary",)),
        name="adaln_in_projection",
    )(*args)


_T5_STEPS = (12, 16, 23, 32, 46, 64, 91)


def _bias_kernel(tab_ref, sink_ref, shift_ref, o_ref, sink_rows_ref, shift_rows_ref, sink_term_rows_ref):
    for h in range(N_Q_HEADS):
        rows = slice(h * BLOCK, (h + 1) * BLOCK)
        sink_rows_ref[rows, :] = jnp.full((BLOCK, LANES), sink_ref[h], F32)
        shift_rows_ref[rows, :] = jnp.full((BLOCK, LANES), shift_ref[h], F32)
        sink_term_rows_ref[rows, :] = jnp.exp(jnp.full((BLOCK, LANES), sink_ref[h] - shift_ref[h], F32))
    qi = lax.broadcasted_iota(jnp.int32, (BLOCK, 3 * BLOCK), 0)
    kj = lax.broadcasted_iota(jnp.int32, (BLOCK, 3 * BLOCK), 1)
    rel = kj - BLOCK - qi
    n = jnp.abs(rel)
    half = N_BUCKETS // 2
    max_exact = half // 2
    large = jnp.full(n.shape, max_exact, jnp.int32)
    for t in _T5_STEPS:
        large = large + (n >= t).astype(jnp.int32)
    bucket = jnp.where(rel > 0, half, 0) + jnp.where(n < max_exact, n, large)
    for h in range(N_Q_HEADS):
        bias = jnp.zeros(n.shape, F32)
        for b in range(N_BUCKETS):
            bias = jnp.where(bucket == b, tab_ref[b, h], bias)
        o_ref[h] = jnp.where(n <= WINDOW, bias, NEG_INF)


def _bias_call(rel_table, sink, shift):
    rows = jax.ShapeDtypeStruct((N_Q_HEADS * BLOCK, LANES), F32)
    smem = pl.BlockSpec(memory_space=pltpu.SMEM)
    return pl.pallas_call(
        _bias_kernel,
        out_shape=(jax.ShapeDtypeStruct((N_Q_HEADS, BLOCK, 3 * BLOCK), F32), rows, rows, rows),
        in_specs=[smem, smem, smem],
        name="t5_bias_tile",
    )(rel_table, sink, shift)


def _mixer_ab_kernel(tq, tpb, fixed_shift, x_ref, q_ref, kc_ref, kp_ref, kn_ref, vc_ref, vp_ref, vn_ref,
                     uc_ref, up_ref, un_ref, bias_ref, sink_ref, shift_ref, sink_term_ref, cw_ref, cb_ref,
                     wo_ref, gate_ref, o_ref, kbuf, vbuf, qs, att):
    i = pl.program_id(0)
    first = (i % tpb) == 0
    last = (i % tpb) == tpb - 1
    nblk = tq // BLOCK
    grows = GQA_GROUP * BLOCK
    kbuf[:, 0:BLOCK] = kp_ref[0]
    kbuf[:, BLOCK:BLOCK + tq] = kc_ref[0]
    kbuf[:, BLOCK + tq:] = kn_ref[0]
    vbuf[0:BLOCK] = vp_ref[...]
    vbuf[BLOCK:BLOCK + tq] = vc_ref[...]
    vbuf[BLOCK + tq:] = vn_ref[...]
    for n in range(nblk):
        for h in range(N_Q_HEADS):
            g, j = divmod(h, GQA_GROUP)
            qs[g, n, j * BLOCK:(j + 1) * BLOCK, :] = q_ref[n * BLOCK:(n + 1) * BLOCK, h * HEAD_DIM:(h + 1) * HEAD_DIM]
    col = lax.broadcasted_iota(jnp.int32, (grows, 3 * BLOCK), 1)
    for n in range(nblk):
        keys = slice(n * BLOCK, (n + 3) * BLOCK)
        for g in range(N_KV_HEADS):
            s = jnp.dot(qs[g, n], kbuf[g * HEAD_DIM:(g + 1) * HEAD_DIM, keys], preferred_element_type=F32)
            s = s + bias_ref[g * GQA_GROUP:(g + 1) * GQA_GROUP].reshape(grows, 3 * BLOCK)
            if n == 0:
                s = jnp.where(jnp.logical_and(first, col < BLOCK), NEG_INF, s)
            if n == nblk - 1:
                s = jnp.where(jnp.logical_and(last, col >= 2 * BLOCK), NEG_INF, s)
            grp = slice(g * grows, (g + 1) * grows)
            if fixed_shift:
                mx = shift_ref[grp, :]
                sink_term = sink_term_ref[grp, :]
            else:
                sk = sink_ref[grp, :]
                mx = jnp.maximum(jnp.broadcast_to(jnp.max(s, axis=-1, keepdims=True), sk.shape), sk)
                sink_term = jnp.exp(sk - mx)
            p = jnp.exp(s - jnp.tile(mx, (1, 3)))
            pv = jnp.dot(p.astype(BF16), vbuf[keys, g * LANES:(g + 1) * LANES], preferred_element_type=F32)
            o = pv / (pltpu.roll(pv, HEAD_DIM, 1) + sink_term)
            for j in range(GQA_GROUP):
                h = g * GQA_GROUP + j
                att[n * BLOCK:(n + 1) * BLOCK, h * HEAD_DIM:(h + 1) * HEAD_DIM] = o[j * BLOCK:(j + 1) * BLOCK, 0:HEAD_DIM]

    gb = uc_ref[:, 0:CONV_WIDTH]
    p = uc_ref[:, CONV_WIDTH:2 * CONV_WIDTH] * uc_ref[:, 2 * CONV_WIDTH:]
    p_before = jnp.where(first, 0.0, up_ref[7:8, CONV_WIDTH:2 * CONV_WIDTH] * up_ref[7:8, 2 * CONV_WIDTH:])
    p_after = jnp.where(last, 0.0, un_ref[0:1, CONV_WIDTH:2 * CONV_WIDTH] * un_ref[0:1, 2 * CONV_WIDTH:])
    conv = gb * _short_conv_rows(p, p_before, p_after, cw_ref[...], cb_ref[...])

    y = jnp.dot(att[...].astype(BF16), wo_ref[0:ATTN_WIDTH, :], preferred_element_type=F32)
    y = y + jnp.dot(conv.astype(BF16), wo_ref[ATTN_WIDTH:, :], preferred_element_type=F32)
    o_ref[...] = x_ref[...] + gate_ref[0] * y


def _mixer_ab_call(x, q, kt, v1, u, bias, row_tables, fixed_shift, conv_w, conv_b, w_out, layer, gate,
                   rows_per_batch):
    m, d = x.shape
    tq = min(rows_per_batch, 1024)
    tpb = rows_per_batch // tq
    r = tq // BLOCK
    nb = m // BLOCK
    n8 = m // 8
    cur = lambda w: pl.BlockSpec((tq, w), lambda i: (i, 0))
    prev_of = lambda i: jnp.maximum(i * r - 1, 0)
    next_of = lambda i: jnp.minimum((i + 1) * r, nb - 1)
    v_prev = pl.BlockSpec((BLOCK, 2 * KV_WIDTH), lambda i: (prev_of(i), 0))
    v_next = pl.BlockSpec((BLOCK, 2 * KV_WIDTH), lambda i: (next_of(i), 0))
    tk = kt.shape[2]
    k_cur = pl.BlockSpec((1, KV_WIDTH, tq), lambda i: (i // (tk // tq), 0, i % (tk // tq)))
    kpb = tk // BLOCK
    k_prev = pl.BlockSpec((1, KV_WIDTH, BLOCK), lambda i: (prev_of(i) // kpb, 0, prev_of(i) % kpb))
    k_next = pl.BlockSpec((1, KV_WIDTH, BLOCK), lambda i: (next_of(i) // kpb, 0, next_of(i) % kpb))
    uw = 3 * CONV_WIDTH
    row_table = _resident((N_Q_HEADS * BLOCK, LANES))
    return pl.pallas_call(
        functools.partial(_mixer_ab_kernel, tq, tpb, fixed_shift),
        out_shape=jax.ShapeDtypeStruct((m, d), F32),
        grid=(m // tq,),
        in_specs=[
            cur(d), cur(ATTN_WIDTH),
            k_cur, k_prev, k_next,
            cur(2 * KV_WIDTH), v_prev, v_next,
            cur(uw),
            pl.BlockSpec((8, uw), lambda i: (jnp.maximum(i * (tq // 8) - 1, 0), 0)),
            pl.BlockSpec((8, uw), lambda i: (jnp.minimum((i + 1) * (tq // 8), n8 - 1), 0)),
            _resident((N_Q_HEADS, BLOCK, 3 * BLOCK)),
            row_table, row_table, row_table,
            _resident((3, CONV_WIDTH)), _resident((1, CONV_WIDTH)),
            _layer_slice((d, d), layer),
            pl.BlockSpec((1, 1, d), lambda i: (i // tpb, 0, 0)),
        ],
        out_specs=cur(d),
        scratch_shapes=[
            pltpu.VMEM((KV_WIDTH, tq + 2 * BLOCK), BF16),
            pltpu.VMEM((tq + 2 * BLOCK, 2 * KV_WIDTH), BF16),
            pltpu.VMEM((N_KV_HEADS, r, GQA_GROUP * BLOCK, HEAD_DIM), BF16),
            pltpu.VMEM((tq, ATTN_WIDTH), F32),
        ],
        compiler_params=_cparams(("arbitrary",)),
        name="windowed_attn_shortconv_outproj",
    )(x, q, kt, kt, kt, v1, v1, v1, u, u, u, bias, *row_tables, conv_w, conv_b, w_out, gate)


def _key_chunk(seq):
    return min(seq, 1024)


def _dense_attn_kernel(tq, tk, q_ref, k_ref, v_ref, o_ref, qs, s_buf, p_buf, rmax_buf, alpha_buf,
                       m_ref, acc_ref):
    nc = k_ref.shape[0]
    for j in range(GQA_GROUP):
        qs[j * tq:(j + 1) * tq, :] = q_ref[:, j * HEAD_DIM:(j + 1) * HEAD_DIM]
    m_ref[...] = jnp.full(m_ref.shape, -jnp.inf, F32)
    acc_ref[...] = jnp.zeros(acc_ref.shape, F32)

    def scores(c):
        s = jnp.dot(qs[...], k_ref[c], preferred_element_type=F32)
        s_buf[...] = s
        rmax_buf[...] = jnp.broadcast_to(jnp.max(s, axis=-1, keepdims=True), rmax_buf.shape)

    def softmax():
        m_old = m_ref[...]
        m_new = jnp.maximum(m_old, rmax_buf[...])
        alpha_buf[...] = jnp.exp2(m_old - m_new)
        m_ref[...] = m_new
        p = jnp.exp2(s_buf[...] - jnp.tile(m_new, (1, tk // LANES)))
        p_buf[...] = p.astype(BF16)

    def weighted_values(c):
        start = c * tk if isinstance(c, int) else pl.multiple_of(c * tk, tk)
        pv = jnp.dot(p_buf[...], v_ref[pl.ds(start, tk), :], preferred_element_type=F32)
        acc_ref[...] = alpha_buf[...] * acc_ref[...] + pv

    def step(t):
        static = isinstance(t, int)
        weighted_values(t)
        if not static or t + 1 < nc:
            softmax()
        if not static or t + 2 < nc:
            scores(t + 2)

    scores(0)
    softmax()
    if nc > 1:
        scores(1)

    def steady(t, carry):
        step(t)
        return carry

    lax.fori_loop(0, max(nc - 2, 0), steady, 0)
    for t in range(max(nc - 2, 0), nc):
        step(t)

    acc = acc_ref[...]
    o = acc / pltpu.roll(acc, HEAD_DIM, 1)
    for j in range(GQA_GROUP):
        o_ref[:, j * HEAD_DIM:(j + 1) * HEAD_DIM] = o[j * tq:(j + 1) * tq, 0:HEAD_DIM].astype(o_ref.dtype)


def _dense_attn_bounded_kernel(tq, tk, bound_ref, q_ref, k_ref, v_ref, o_ref, qs, p_buf, acc_ref):
    nc = k_ref.shape[0]
    for j in range(GQA_GROUP):
        qs[j * tq:(j + 1) * tq, :] = q_ref[:, j * HEAD_DIM:(j + 1) * HEAD_DIM]
    shift = bound_ref[0]

    def probabilities(c):
        s = jnp.dot(qs[...], k_ref[c], preferred_element_type=F32)
        p_buf[...] = jnp.exp2(s - shift).astype(BF16)

    def weighted_values(c):
        start = c * tk if isinstance(c, int) else pl.multiple_of(c * tk, tk)
        return jnp.dot(p_buf[...], v_ref[pl.ds(start, tk), :], preferred_element_type=F32)

    probabilities(0)
    acc_ref[...] = weighted_values(0)
    if nc > 1:
        probabilities(1)

    def steady(t, carry):
        acc_ref[...] += weighted_values(t)
        probabilities(t + 1)
        return carry

    lax.fori_loop(1, nc - 1, steady, 0)
    if nc > 1:
        acc_ref[...] += weighted_values(nc - 1)

    acc = acc_ref[...]
    o = acc / pltpu.roll(acc, HEAD_DIM, 1)
    for j in range(GQA_GROUP):
        o_ref[:, j * HEAD_DIM:(j + 1) * HEAD_DIM] = o[j * tq:(j + 1) * tq, 0:HEAD_DIM].astype(o_ref.dtype)


MAX_FIXED_SHIFT = 50.0


def _dense_attn_call(q, kt, v1, logit_bound, bsz, seq):
    tq, tk = min(seq, 1024), _key_chunk(seq)
    nq = seq // tq
    nc = seq // tk
    rows = GQA_GROUP * tq
    gw = GQA_GROUP * HEAD_DIM
    stat = pltpu.VMEM((rows, LANES), F32)
    q_spec = pl.BlockSpec((tq, gw), lambda b, g, i: (b * nq + i, g))
    k_spec = pl.BlockSpec((nc, HEAD_DIM, tk), lambda b, g, i: (b, g, 0), pipeline_mode=pl.Buffered(1))
    v_spec = pl.BlockSpec((seq, LANES), lambda b, g, i: (b, g), pipeline_mode=pl.Buffered(1))
    common = dict(
        out_shape=jax.ShapeDtypeStruct((bsz * seq, ATTN_WIDTH), BF16),
        grid=(bsz, N_KV_HEADS, nq),
        out_specs=q_spec,
        compiler_params=_cparams(("arbitrary", "arbitrary", "arbitrary")),
    )

    def running_max():
        return pl.pallas_call(
            functools.partial(_dense_attn_kernel, tq, tk),
            in_specs=[q_spec, k_spec, v_spec],
            scratch_shapes=[
                pltpu.VMEM((rows, HEAD_DIM), BF16),
                pltpu.VMEM((rows, tk), F32), pltpu.VMEM((rows, tk), BF16),
                stat, stat, stat, stat,
            ],
            name="dense_gqa_attention", **common,
        )(q, kt, v1)

    def fixed_shift():
        return pl.pallas_call(
            functools.partial(_dense_attn_bounded_kernel, tq, tk),
            in_specs=[pl.BlockSpec(memory_space=pltpu.SMEM), q_spec, k_spec, v_spec],
            scratch_shapes=[pltpu.VMEM((rows, HEAD_DIM), BF16), pltpu.VMEM((rows, tk), BF16), stat],
            name="dense_gqa_attention_fixed_shift", **common,
        )(logit_bound.reshape(1), q, kt, v1)

    return lax.cond(logit_bound <= MAX_FIXED_SHIFT, fixed_shift, running_max)


def _hp_dot(a, b):
    return jnp.dot(a, b, preferred_element_type=F32, precision=lax.Precision.HIGHEST)


def _filter_trunk_kernel(z_ref, w1_ref, b1_ref, w2_ref, b2_ref, w3_ref, b3_ref, fq_ref, o_ref):
    fq = fq_ref[...]
    h = jnp.sin(fq * (_hp_dot(w1_ref[...], z_ref[...]) + b1_ref[...]))
    h = jnp.sin(fq * (_hp_dot(w2_ref[...], h) + b2_ref[...]))
    h = jnp.sin(fq * (_hp_dot(w3_ref[...], h) + b3_ref[...]))
    o_ref[...] = h.T


def _filter_trunk_call(zfeat_t, w1, b1, w2, b2, w3, b3, freq):
    rows = zfeat_t.shape[1]
    tr = min(rows, 2048)
    fw = HYENA_FILTER_WIDTH
    col = lambda v: v.reshape(fw, 1)
    return pl.pallas_call(
        _filter_trunk_kernel,
        out_shape=jax.ShapeDtypeStruct((rows, fw), F32),
        grid=(rows // tr,),
        in_specs=[
            pl.BlockSpec((HYENA_EMB_PAD, tr), lambda i: (0, i)),
            _resident((fw, HYENA_EMB_PAD)), _resident((fw, 1)),
            _resident((fw, fw)), _resident((fw, 1)),
            _resident((fw, fw)), _resident((fw, 1)),
            _resident((fw, 1)),
        ],
        out_specs=pl.BlockSpec((tr, fw), lambda i: (i, 0)),
        compiler_params=_cparams(("arbitrary",)),
        name="hyena_filter_trunk",
    )(zfeat_t, w1.T, col(b1), w2.T, col(b2), w3.T, col(b3), col(freq))


def _dft_stage1(src_ref, tab_ref, re_ref, im_ref, n_rows, k1p, n2_major=False, pitch=DFT_N2, src_pitch=DFT_N2):
    def body(n2, carry):
        xs = src_ref[pl.ds(n2, n_rows, stride=src_pitch), :].astype(BF16)
        res = jnp.dot(tab_ref[n2], xs, preferred_element_type=F32)
        dst = pl.ds(pl.multiple_of(n2 * k1p, 8), k1p) if n2_major else pl.ds(n2, k1p, stride=pitch)
        re_ref[dst, :] = res[0:k1p]
        im_ref[dst, :] = res[k1p:]
        return carry

    lax.fori_loop(0, DFT_N2, body, 0, unroll=8)


def _filter_spec_kernel(seq, k1p, h_ref, w4f_ref, w4b_ref, dl_ref, e1_ref, f2_ref,
                        kr_ref, ki_ref, kfull, bre, bim):
    n = 2 * seq
    chunk = min(seq, 1024)
    dl = dl_ref[...]
    live = seq // DFT_N2 + 1

    def fill_half(w_ref, backward):
        def fill(c, ss):
            r0 = pl.multiple_of(c * chunk, chunk)
            val = jnp.dot(h_ref[pl.ds(r0, chunk), :].astype(BF16), w_ref[...], preferred_element_type=F32)
            rows = r0 + lax.broadcasted_iota(jnp.int32, (chunk, 1), 0)
            lag = ((n - rows) if backward else rows).astype(F32)
            val = val * jnp.exp(-(lag * (1.0 / (seq - 1))) * dl)
            if backward:
                val = jnp.where(rows == seq, 0.0, val)
            for k in range(chunk // DFT_N2):
                dst = pl.multiple_of((c * (chunk // DFT_N2) + k) * SPECTRUM_PITCH, 8)
                kfull[pl.ds(dst, DFT_N2), :] = val[k * DFT_N2:(k + 1) * DFT_N2]
            return ss + jnp.sum(val * val, axis=0, keepdims=True)
        return fill

    ss = lax.fori_loop(0, seq // chunk, fill_half(w4f_ref, False), jnp.zeros((1, dl.shape[1]), F32))
    ss = lax.fori_loop(seq // chunk, n // chunk, fill_half(w4b_ref, True), ss)
    norm = lax.rsqrt(ss + EPS)

    _dft_stage1(kfull, e1_ref, bre, bim, n // DFT_N2, k1p, n2_major=True, src_pitch=SPECTRUM_PITCH)

    kr_ref[live * DFT_N2:, :] = jnp.zeros(((k1p - live) * DFT_N2, kr_ref.shape[1]), F32)
    ki_ref[live * DFT_N2:, :] = jnp.zeros(((k1p - live) * DFT_N2, ki_ref.shape[1]), F32)
    f2 = f2_ref[...]

    def stage2(a, carry):
        src = pl.ds(a, DFT_N2, stride=k1p)
        slab = jnp.concatenate([bre[src, :], bim[src, :]], axis=0).astype(BF16)
        xf = jnp.dot(f2, slab, preferred_element_type=F32)
        rows = pl.ds(pl.multiple_of(a * DFT_N2, DFT_N2), DFT_N2)
        kr_ref[rows, :] = xf[0:DFT_N2] * norm
        ki_ref[rows, :] = xf[DFT_N2:] * norm
        return carry

    lax.fori_loop(0, live, stage2, 0, unroll=8)


def _filter_spec_call(h3, w4, deltas, e1f, f2f, seq):
    n = 2 * seq
    k1p = e1f.shape[1] // 2
    ct = LANES
    nct = CONV_WIDTH // ct
    fw = HYENA_FILTER_WIDTH
    spec_rows = k1p * DFT_N2
    out = pl.BlockSpec((spec_rows, ct), lambda j: (0, j))
    return pl.pallas_call(
        functools.partial(_filter_spec_kernel, seq, k1p),
        out_shape=(jax.ShapeDtypeStruct((spec_rows, CONV_WIDTH), F32),
                   jax.ShapeDtypeStruct((spec_rows, CONV_WIDTH), F32)),
        grid=(nct,),
        in_specs=[
            _resident((n, fw)),
            pl.BlockSpec((fw, ct), lambda j: (0, j)),
            pl.BlockSpec((fw, ct), lambda j: (0, nct + j)),
            pl.BlockSpec((1, ct), lambda j: (0, j)),
            _resident(e1f.shape),
            _resident((2 * DFT_N2, 2 * DFT_N2)),
        ],
        out_specs=(out, out),
        scratch_shapes=[pltpu.VMEM((n // DFT_N2 * SPECTRUM_PITCH, ct), F32), pltpu.VMEM((spec_rows, ct), F32),
                        pltpu.VMEM((spec_rows, ct), F32)],
        compiler_params=_cparams(("arbitrary",)),
        name="hyena_filter_spectrum",
    )(h3, w4, w4, deltas, e1f, f2f)


def _long_conv_kernel(seq, k1p, z_ref, x0_ref, kr_ref, ki_ref, skip_ref, e1_ref, f2f_ref, f2i_ref,
                      einv_ref, o_ref, br, bi, cbuf):
    n1h = seq // DFT_N2
    pitch = br.shape[0] // k1p
    _dft_stage1(z_ref, e1_ref, br, bi, n1h, k1p, pitch=pitch)

    f2f = f2f_ref[...]
    f2i = f2i_ref[...]

    ct = br.shape[1]

    def spectral(a0, count):
        def start(x, m):
            return x if isinstance(x, int) else pl.multiple_of(x, m)
        rows = [pl.ds(start((a0 + t) * pitch, 8), DFT_N2) for t in range(count)]
        krows = [pl.ds(start((a0 + t) * DFT_N2, DFT_N2), DFT_N2) for t in range(count)]
        lanes = lambda parts: jnp.concatenate(parts, axis=1)
        slab = lanes([jnp.concatenate([br[r, :], bi[r, :]], axis=0) for r in rows]).astype(BF16)
        xf = jnp.dot(f2f, slab, preferred_element_type=F32)
        xr, xi = xf[0:DFT_N2], xf[DFT_N2:]
        kr, ki = lanes([kr_ref[r, :] for r in krows]), lanes([ki_ref[r, :] for r in krows])
        y = jnp.concatenate([xr * kr - xi * ki, xr * ki + xi * kr], axis=0).astype(BF16)
        d = jnp.dot(f2i, y, preferred_element_type=F32)
        for t, r in enumerate(rows):
            br[r, :] = d[0:DFT_N2, t * ct:(t + 1) * ct]
            bi[r, :] = d[DFT_N2:, t * ct:(t + 1) * ct]

    live = n1h + 1
    pair = MXU_DIM // ct

    def spectral_pairs(p, carry):
        spectral(p * pair, pair)
        return carry

    lax.fori_loop(0, live // pair, spectral_pairs, 0, unroll=4)
    for a in range(live // pair * pair, live):
        spectral(a, 1)

    skip = skip_ref[...]

    def synth(n2, carry):
        ds_spec = pl.ds(n2, k1p, stride=pitch)
        d = jnp.concatenate([br[ds_spec, :], bi[ds_spec, :]], axis=0).astype(BF16)
        cbuf[pl.ds(n2, n1h, stride=pitch), :] = jnp.dot(einv_ref[n2], d, preferred_element_type=F32)
        return carry

    lax.fori_loop(0, DFT_N2, synth, 0, unroll=8)

    def gate(n1, carry):
        rows = pl.ds(pl.multiple_of(n1 * DFT_N2, DFT_N2), DFT_N2)
        conv = cbuf[pl.ds(pl.multiple_of(n1 * pitch, 8), DFT_N2), :]
        o_ref[rows, :] = x0_ref[rows, :] * (conv + skip * z_ref[rows, :])
        return carry

    lax.fori_loop(0, n1h, gate, 0, unroll=4)


def _long_conv_call(z, x0, kr, ki, skip, e1, f2f, f2i, einv, bsz, seq):
    ct = LANES
    nct = CONV_WIDTH // ct
    k1p = e1.shape[1] // 2
    spec_rows = k1p * DFT_N2
    big = lambda: pl.BlockSpec((seq, ct), lambda j, b: (b, j), pipeline_mode=pl.Buffered(1))
    spec = lambda: pl.BlockSpec((spec_rows, ct), lambda j, b: (0, j), pipeline_mode=pl.Buffered(1))
    return pl.pallas_call(
        functools.partial(_long_conv_kernel, seq, k1p),
        out_shape=jax.ShapeDtypeStruct((bsz * seq, CONV_WIDTH), F32),
        grid=(nct, bsz),
        in_specs=[
            pl.BlockSpec((seq, ct), lambda j, b: (b, j)), big(), spec(), spec(),
            pl.BlockSpec((1, ct), lambda j, b: (0, j)),
            _resident(e1.shape),
            _resident((2 * DFT_N2, 2 * DFT_N2)), _resident((2 * DFT_N2, 2 * DFT_N2)),
            _resident(einv.shape),
        ],
        out_specs=pl.BlockSpec((seq, ct), lambda j, b: (b, j)),
        scratch_shapes=[pltpu.VMEM((k1p * SPECTRUM_PITCH, ct), F32), pltpu.VMEM((k1p * SPECTRUM_PITCH, ct), F32),
                        pltpu.VMEM((seq // DFT_N2 * SPECTRUM_PITCH, ct), F32)],
        compiler_params=_cparams(("arbitrary", "arbitrary")),
        name="hyena_long_conv",
    )(z, x0, kr, ki, skip, e1, f2f, f2i, einv)


def _rope_tables(seq):
    t = jnp.arange(seq)
    row = (t // GRID_W).astype(F32)
    col = (t % GRID_W).astype(F32)
    half = HEAD_DIM // 2
    inv = ROPE_THETA ** (-jnp.arange(0, half, 2, dtype=F32) / half)
    ang = jnp.concatenate([row[:, None] * inv, col[:, None] * inv], axis=-1)
    ang = jnp.repeat(ang, 2, axis=-1)
    sign = jnp.where(jnp.arange(HEAD_DIM) % 2 == 0, -1.0, 1.0).astype(F32)
    reps = LANES // HEAD_DIM
    return jnp.tile(jnp.cos(ang), (1, reps)), jnp.tile(jnp.sin(ang) * sign, (1, reps))


def _dft_tables(seq):
    n = 2 * seq
    n1 = n // DFT_N2
    k1 = n1 // 2 + 1
    k1p = -(-k1 // 8) * 8
    a = jnp.arange(k1p)
    live = (a < k1).astype(F32)
    ang1 = ((a[:, None] * jnp.arange(n1)[None, :]) % n1).astype(F32) * (2.0 * math.pi / n1)
    ang2 = (jnp.arange(DFT_N2)[:, None] * a[None, :]).astype(F32) * (2.0 * math.pi / n)
    c1, s1 = jnp.cos(ang1) * live[:, None], jnp.sin(ang1) * live[:, None]
    c2, s2 = jnp.cos(ang2), jnp.sin(ang2)
    cos_a = c1[None] * c2[:, :, None] - s1[None] * s2[:, :, None]
    sin_a = s1[None] * c2[:, :, None] + c1[None] * s2[:, :, None]
    e1_full = jnp.concatenate([cos_a, -sin_a], axis=1).astype(BF16)
    e1_half = e1_full[:, :, :n1 // 2]
    wgt = jnp.where((a == 0) | (a == n1 // 2), 1.0, 2.0) / n
    c1t, s1t = (c1 * wgt[:, None]).T[:n1 // 2], (s1 * wgt[:, None]).T[:n1 // 2]
    cos_s = c1t[None] * c2[:, None, :] - s1t[None] * s2[:, None, :]
    sin_s = s1t[None] * c2[:, None, :] + c1t[None] * s2[:, None, :]
    einv = jnp.concatenate([cos_s, -sin_s], axis=2).astype(BF16)
    kk = jnp.arange(DFT_N2)
    phi = ((kk[:, None] * kk[None, :]) % DFT_N2).astype(F32) * (2.0 * math.pi / DFT_N2)
    ci, si = jnp.cos(phi), jnp.sin(phi)
    f2f = jnp.block([[ci, si], [-si, ci]]).astype(BF16)
    f2i = jnp.block([[ci, -si], [si, ci]]).astype(BF16)
    return e1_full, e1_half, einv, f2f, f2i


def _filter_features_t(seq):
    j = jnp.arange(2 * seq)
    lag = jnp.where(j < seq, j, jnp.where(j == seq, 0, 2 * seq - j)).astype(F32)[None, :]
    t = lag / (seq - 1)
    bands = (HYENA_EMB - 1) // 2
    w = (2.0 * math.pi / seq) * lag
    fr = jnp.linspace(1e-4, bands - 1, bands, dtype=F32)[:, None]
    pad = jnp.zeros((HYENA_EMB_PAD - HYENA_EMB, 2 * seq), F32)
    return jnp.concatenate([t, jnp.cos(fr * w), -jnp.sin(fr * w), pad], axis=0)


def kernel(x, c, rel_table, norm_g, w_mod, b_mod, w_in, w_out, ffn_w1, ffn_w3, ffn_w2, a_qk_g, a_sink,
           b_conv_w, b_conv_b, c_qk_g, d_conv_w, d_conv_b, d_f_w1, d_f_b1, d_f_w2, d_f_b2, d_f_w3,
           d_f_b3, d_f_w4, d_f_freq, d_skip):
    bsz, seq, d = x.shape
    depth = w_mod.shape[0]
    m_rows = bsz * seq
    mod =_mod_call(c, w_mod, b_mod).reshape(depth, bsz, 3, 3, 1, d)
    e_blk = jnp.kron(jnp.eye(MXU_DIM // HEAD_DIM, dtype=F32), jnp.ones((HEAD_DIM, HEAD_DIM), F32)).astype(BF16)

    w1b, w3b = _to_bf16_from_transposed(ffn_w1), _to_bf16_from_transposed(ffn_w3)
    w2b, w_in_b, w_out_b = (w.astype(BF16) for w in (ffn_w2, w_in, w_out))

    def ffn(xf, l, which, sub, mixer=None):
        return _ffn_call(xf, norm_g[l, sub][None], mod[l, :, sub, 0], mod[l, :, sub, 1], mod[l, :, sub, 2],
                         w1b, w3b, w2b, l, which, seq, mixer)

    xf = x.reshape(m_rows, d)
    for l in range(depth):
        j = l // 2
        xf = ffn(xf, l, 0, 0)
        even = l % 2 == 0
        qk_g = a_qk_g[j] if even else c_qk_g[j]
        hyena = None if even else _rope_tables(seq) + (d_conv_w[j], d_conv_b[j][None])
        q, k, v, *conv_in = _inproj_call(
            xf, norm_g[l, 1][None], mod[l, :, 1, 0], mod[l, :, 1, 1], w_in_b, l,
            jnp.tile(qk_g[0], N_Q_HEADS)[None], jnp.tile(qk_g[1], N_KV_HEADS)[None], e_blk, seq, hyena)
        gate = mod[l, :, 1, 2]
        if even:
            (u,) = conv_in
            qk_bound = (1.01 * HEAD_DIM ** 0.5) * jnp.max(jnp.abs(qk_g[0])) * jnp.max(jnp.abs(qk_g[1]))
            shift = jnp.maximum(qk_bound + jnp.max(rel_table, axis=0), a_sink[j])
            worst_gap = jnp.max(shift + qk_bound - rel_table[0])
            bias, *row_tables = _bias_call(rel_table, a_sink[j], shift)
            mixer_ab = lambda fixed: _mixer_ab_call(xf, q, k, v, u, bias, row_tables, fixed, b_conv_w[j],
                                                    b_conv_b[j][None], w_out_b, l, gate, seq)
            xf = lax.cond(worst_gap <= MAX_FIXED_SHIFT_GAP, lambda: mixer_ab(True), lambda: mixer_ab(False))
            pending = None
        else:
            x0, z = conv_in
            logit_bound = (1.01 * LOG2_E * HEAD_DIM ** 0.5) * jnp.max(jnp.abs(qk_g[0])) * jnp.max(jnp.abs(qk_g[1]))
            att = _dense_attn_call(q, k, v, logit_bound, bsz, seq)
            e1_full, e1_half, einv, f2f, f2i = _dft_tables(seq)
            fw1 = jnp.pad(d_f_w1[j], ((0, HYENA_EMB_PAD - HYENA_EMB), (0, 0)))
            h3 = _filter_trunk_call(_filter_features_t(seq), fw1, d_f_b1[j], d_f_w2[j], d_f_b2[j],
                                    d_f_w3[j], d_f_b3[j], d_f_freq[j])
            deltas = jnp.abs(jnp.linspace(HYENA_MIN_DECAY, HYENA_MAX_DECAY, CONV_WIDTH, dtype=F32))[None]
            kr, ki = _filter_spec_call(h3, d_f_w4[j].astype(BF16), deltas, e1_full, f2f, seq)
            y = _long_conv_call(z, x0, kr, ki, d_skip[j][None], e1_half, f2f, f2i, einv, bsz, seq)
            pending = (att, y, w_out_b, gate)
        xf = ffn(xf, l, 1, 2, pending)
    return xf.reshape(bsz, seq, d)
```

```python
import functools
import math

import jax
import jax.numpy as jnp
import numpy as np
from jax import lax
from jax.experimental import pallas as pl
from jax.experimental.pallas import tpu as pltpu

D_MODEL = 1024
HEAD_DIM = 64
N_Q_HEADS = 8
N_KV_HEADS = 2
GQA_GROUP = N_Q_HEADS // N_KV_HEADS
ATTN_WIDTH = N_Q_HEADS * HEAD_DIM
KV_WIDTH = N_KV_HEADS * HEAD_DIM
QKV_COLS = ATTN_WIDTH + 2 * KV_WIDTH
CONV_WIDTH = D_MODEL - ATTN_WIDTH
IN_COLS = QKV_COLS + 3 * CONV_WIDTH
D_FF = 2752
BLOCK = 128
WINDOW = 128
N_BUCKETS = 32
MAX_DISTANCE = 128
GRID_W = 64
ROPE_THETA = 10000.0
HYENA_EMB = 33
HYENA_FILTER_WIDTH = 64
HYENA_EMB_PAD = 64
HYENA_MIN_DECAY = math.log(1e-2) / 0.3
HYENA_MAX_DECAY = math.log(1e-2) / 1.5
EPS = 1e-6
NEG_INF = -1e30
LOG2_E = 1.4426950408889634
MAX_FIXED_SHIFT_GAP = 80.0

LANES = 128
HALO = 16
MXU_DIM = 256
FF_CHUNK = MXU_DIM
VMEM_LIMIT = 56 * 1024 * 1024
DFT_N2 = 128
SPECTRUM_PITCH = DFT_N2 + 8

BF16 = jnp.bfloat16
F32 = jnp.float32


def _cparams(sem):
    return pltpu.CompilerParams(dimension_semantics=sem, vmem_limit_bytes=VMEM_LIMIT)


def _resident(shape):
    nd = len(shape)
    return pl.BlockSpec(shape, lambda *_: (0,) * nd, pipeline_mode=pl.Buffered(1))


def _layer_slice(shape, layer):
    return pl.BlockSpec((None,) + shape, lambda *_: (layer, 0, 0), pipeline_mode=pl.Buffered(1))


def _adaln(x, g, scale, shift):
    y = x * lax.rsqrt(jnp.mean(x * x, axis=-1, keepdims=True) + EPS) * g
    return y * (1.0 + scale) + shift


def _mod_kernel(c_ref, w_ref, b_ref, o_ref):
    w = w_ref[0]
    for b in range(c_ref.shape[0]):
        cc = c_ref[b]
        cond = cc * jax.nn.sigmoid(cc)
        o_ref[0, b:b + 1, :] = jnp.sum(w * cond, axis=0, keepdims=True) + b_ref[0]


def _mod_call(c, w_mod, b_mod):
    depth, d, n = w_mod.shape
    bsz = c.shape[0]
    tn = 1152 if n % 1152 == 0 else 512
    return pl.pallas_call(
        _mod_kernel,
        out_shape=jax.ShapeDtypeStruct((depth, bsz, n), F32),
        grid=(depth, n // tn),
        in_specs=[
            pl.BlockSpec((bsz, d, 1), lambda l, j: (0, 0, 0)),
            pl.BlockSpec((1, d, tn), lambda l, j: (l, 0, j)),
            pl.BlockSpec((1, 1, tn), lambda l, j: (l, 0, j)),
        ],
        out_specs=pl.BlockSpec((1, bsz, tn), lambda l, j: (l, 0, j)),
        compiler_params=_cparams(("arbitrary", "arbitrary")),
        name="adaln_modulation",
    )(c.reshape(bsz, d, 1), w_mod, b_mod.reshape(depth, 1, n))


def _cast_transpose_kernel(wt_ref, o_ref):
    cols = wt_ref.shape[0]
    starts = list(range(0, cols - MXU_DIM, MXU_DIM)) + [cols - MXU_DIM]
    for c0 in starts:
        o_ref[:, c0:c0 + MXU_DIM] = wt_ref[c0:c0 + MXU_DIM, :].T.astype(o_ref.dtype)


def _to_bf16_from_transposed(w):
    lead, (rows, cols) = w.shape[:-2], w.shape[-2:]
    none = (None,) * len(lead)
    return pl.pallas_call(
        _cast_transpose_kernel,
        out_shape=jax.ShapeDtypeStruct(w.shape, BF16),
        grid=lead,
        in_specs=[pl.BlockSpec(none + (cols, rows), lambda *idx: idx + (0, 0))],
        out_specs=pl.BlockSpec(none + (rows, cols), lambda *idx: idx + (0, 0)),
        compiler_params=_cparams(("arbitrary",) * len(lead)),
        name="weights_to_bf16_transposing",
    )(jnp.swapaxes(w, -1, -2))


def _ffn_kernel(mixer_update, x_ref, g_ref, shift_ref, scale_ref, gate_ref, w1_ref, w3_ref, w2_ref, *rest):
    x = x_ref[...]
    if mixer_update:
        att_ref, y_ref, wo_ref, mgate_ref, o_ref = rest
        upd = jnp.dot(att_ref[...], wo_ref[0:ATTN_WIDTH, :], preferred_element_type=F32)
        upd = upd + jnp.dot(y_ref[...].astype(BF16), wo_ref[ATTN_WIDTH:, :], preferred_element_type=F32)
        x = x + mgate_ref[0] * upd
    else:
        (o_ref,) = rest
    h = _adaln(x, g_ref[...], scale_ref[0], shift_ref[0]).astype(BF16)
    acc = jnp.zeros(x.shape, F32)
    for c0 in range(0, D_FF, FF_CHUNK):
        cols = slice(c0, min(c0 + FF_CHUNK, D_FF))
        a = jnp.dot(h, w1_ref[:, cols], preferred_element_type=F32)
        b = jnp.dot(h, w3_ref[:, cols], preferred_element_type=F32)
        act = (a * jax.nn.sigmoid(a) * b).astype(BF16)
        acc = acc + jnp.dot(act, w2_ref[cols, :], preferred_element_type=F32)
    o_ref[...] = x + (0.5 * gate_ref[0]) * acc


def _ffn_call(x, g, shift, scale, gate, w1, w3, w2, layer, which, rows_per_batch, mixer=None):
    m, d = x.shape
    tm = min(rows_per_batch, 1024)
    tpb = rows_per_batch // tm
    row = lambda w: pl.BlockSpec((tm, w), lambda i: (i, 0))
    vec = pl.BlockSpec((1, 1, d), lambda i: (i // tpb, 0, 0))
    weight = lambda r, c: pl.BlockSpec((None, None, r, c), lambda i: (layer, which, 0, 0),
                                       pipeline_mode=pl.Buffered(1))
    in_specs = [row(d), _resident((1, d)), vec, vec, vec, weight(d, D_FF), weight(d, D_FF), weight(D_FF, d)]
    args = [x, g, shift, scale, gate, w1, w3, w2]
    if mixer is not None:
        in_specs += [row(ATTN_WIDTH), row(CONV_WIDTH), _layer_slice((d, d), layer), vec]
        args += list(mixer)
    return pl.pallas_call(
        functools.partial(_ffn_kernel, mixer is not None),
        out_shape=jax.ShapeDtypeStruct((m, d), F32),
        grid=(m // tm,),
        in_specs=in_specs,
        out_specs=row(d),
        compiler_params=_cparams(("arbitrary",)),
        name="adaln_swiglu_ffn",
    )(*args)


def _group_norm_scale(v, e_ref):
    sq = (v * v).astype(BF16)
    w = min(v.shape[1], MXU_DIM)
    e = e_ref[...]
    ss = jnp.concatenate(
        [jnp.dot(sq[:, c:c + w], e[0:w, 0:w], preferred_element_type=F32) for c in range(0, v.shape[1], w)],
        axis=1)
    return lax.rsqrt(ss * (1.0 / HEAD_DIM) + EPS)


def _rope128(v, cos, sin_signed, even_lane):
    partner = jnp.where(even_lane, pltpu.roll(v, LANES - 1, 1), pltpu.roll(v, 1, 1))
    return v * cos + partner * sin_signed


def _short_conv_rows(u, before, after, cw, cb):
    rows = u.shape[0]
    r = lax.broadcasted_iota(jnp.int32, u.shape, 0)
    um1 = jnp.where(r == 0, before, pltpu.roll(u, 1, 0))
    up1 = jnp.where(r == rows - 1, after, pltpu.roll(u, rows - 1, 0))
    return cw[0:1] * um1 + cw[1:2] * u + cw[2:3] * up1 + cb


def _inproj_kernel(rope, tpb, x_ref, g_ref, shift_ref, scale_ref, w_ref, qg_ref, kg_ref, e_ref, *rest):
    if rope:
        cos_ref, sin_ref, xp_ref, xn_ref, cw_ref, cb_ref, q_ref, k_ref, v_ref, x0_ref, z_ref = rest
    else:
        q_ref, k_ref, v_ref, u_ref = rest
    h = _adaln(x_ref[...], g_ref[...], scale_ref[0], shift_ref[0]).astype(BF16)
    q = jnp.dot(h, w_ref[:, 0:ATTN_WIDTH], preferred_element_type=F32)
    k = jnp.dot(h, w_ref[:, ATTN_WIDTH:ATTN_WIDTH + KV_WIDTH], preferred_element_type=F32)
    q = q * _group_norm_scale(q, e_ref) * qg_ref[...]
    k = k * _group_norm_scale(k, e_ref) * kg_ref[...]
    if rope:
        cos = cos_ref[...]
        sin = sin_ref[...]
        even = (lax.broadcasted_iota(jnp.int32, cos.shape, 1) % 2) == 0
        q = jnp.concatenate(
            [_rope128(q[:, j * LANES:(j + 1) * LANES], cos, sin, even) for j in range(ATTN_WIDTH // LANES)],
            axis=1)
        k = _rope128(k, cos, sin, even)
    q_ref[...] = (q * (HEAD_DIM ** -0.5 * (LOG2_E if rope else 1.0))).astype(BF16)
    v = jnp.dot(h, w_ref[:, ATTN_WIDTH + KV_WIDTH:QKV_COLS], preferred_element_type=F32)
    k_ref[0] = k.T.astype(BF16)
    low = lax.broadcasted_iota(jnp.int32, v.shape, 1) < HEAD_DIM
    v_ref[...] = jnp.concatenate(
        [jnp.where(low, v, 1.0), jnp.where(low, pltpu.roll(v, HEAD_DIM, 1), 1.0)], axis=1).astype(BF16)
    if not rope:
        u_ref[...] = jnp.dot(h, w_ref[:, QKV_COLS:IN_COLS], preferred_element_type=F32)
        return
    i = pl.program_id(0)
    h_prev = _adaln(xp_ref[...], g_ref[...], scale_ref[0], shift_ref[0]).astype(BF16)
    h_next = _adaln(xn_ref[...], g_ref[...], scale_ref[0], shift_ref[0]).astype(BF16)
    u_ext = jnp.dot(jnp.concatenate([h_prev, h, h_next], axis=0), w_ref[:, QKV_COLS:IN_COLS],
                    preferred_element_type=F32)
    tm = h.shape[0]
    before = jnp.where((i % tpb) == 0, 0.0, u_ext[HALO - 1:HALO])
    after = jnp.where((i % tpb) == tpb - 1, 0.0, u_ext[HALO + tm:HALO + tm + 1])
    t = _short_conv_rows(u_ext[HALO:HALO + tm], before, after, cw_ref[...], cb_ref[...])
    x0_ref[...] = t[:, 0:CONV_WIDTH]
    z_ref[...] = t[:, CONV_WIDTH:2 * CONV_WIDTH] * t[:, 2 * CONV_WIDTH:]


def _inproj_call(x, g, shift, scale, w_in, layer, qg, kg, e, rows_per_batch, hyena=None):
    m, d = x.shape
    tm = min(rows_per_batch, 1024)
    tpb = rows_per_batch // tm
    vec = pl.BlockSpec((1, 1, d), lambda i: (i // tpb, 0, 0))
    row = lambda w: pl.BlockSpec((tm, w), lambda i: (i, 0))
    in_specs = [
        row(d),
        _resident((1, d)), vec, vec,
        _layer_slice((d, IN_COLS), layer),
        _resident((1, ATTN_WIDTH)), _resident((1, KV_WIDTH)),
        _resident((MXU_DIM, MXU_DIM)),
    ]
    args = [x, g, shift, scale, w_in, qg, kg, e]
    uw = 3 * CONV_WIDTH
    if hyena is not None:
        cos, sin, conv_w, conv_b = hyena
        tab = pl.BlockSpec((tm, LANES), lambda i: (i % tpb, 0))
        hpt = tm // HALO
        last_halo = m // HALO - 1
        in_specs += [
            tab, tab,
            pl.BlockSpec((HALO, d), lambda i: (jnp.maximum(i * hpt - 1, 0), 0)),
            pl.BlockSpec((HALO, d), lambda i: (jnp.minimum((i + 1) * hpt, last_halo), 0)),
            _resident((3, uw)), _resident((1, uw)),
        ]
        args += [cos, sin, x, x, conv_w, conv_b]
        conv_shapes = [jax.ShapeDtypeStruct((m, CONV_WIDTH), F32)] * 2
        conv_specs = [row(CONV_WIDTH)] * 2
    else:
        conv_shapes = [jax.ShapeDtypeStruct((m, uw), F32)]
        conv_specs = [row(uw)]
    tk = _key_chunk(rows_per_batch)
    per = tk // tm
    return pl.pallas_call(
        functools.partial(_inproj_kernel, hyena is not None, tpb),
        out_shape=[
            jax.ShapeDtypeStruct((m, ATTN_WIDTH), BF16),
            jax.ShapeDtypeStruct((m // tk, KV_WIDTH, tk), BF16),
            jax.ShapeDtypeStruct((m, 2 * KV_WIDTH), BF16),
        ] + conv_shapes,
        grid=(m // tm,),
        in_specs=in_specs,
        out_specs=[row(ATTN_WIDTH), pl.BlockSpec((1, KV_WIDTH, tm), lambda i: (i // per, 0, i % per)),
                   row(2 * KV_WIDTH)] + conv_specs,
        compiler_params=_cparams(("arbitrary",)),
        name="adaln_in_projection",
    )(*args)


_T5_STEPS = (12, 16, 23, 32, 46, 64, 91)


def _bias_kernel(tab_ref, sink_ref, shift_ref, o_ref, sink_rows_ref, shift_rows_ref, sink_term_rows_ref):
    for h in range(N_Q_HEADS):
        rows = slice(h * BLOCK, (h + 1) * BLOCK)
        sink_rows_ref[rows, :] = jnp.full((BLOCK, LANES), sink_ref[h], F32)
        shift_rows_ref[rows, :] = jnp.full((BLOCK, LANES), shift_ref[h], F32)
        sink_term_rows_ref[rows, :] = jnp.exp(jnp.full((BLOCK, LANES), sink_ref[h] - shift_ref[h], F32))
    qi = lax.broadcasted_iota(jnp.int32, (BLOCK, 3 * BLOCK), 0)
    kj = lax.broadcasted_iota(jnp.int32, (BLOCK, 3 * BLOCK), 1)
    rel = kj - BLOCK - qi
    n = jnp.abs(rel)
    half = N_BUCKETS // 2
    max_exact = half // 2
    large = jnp.full(n.shape, max_exact, jnp.int32)
    for t in _T5_STEPS:
        large = large + (n >= t).astype(jnp.int32)
    bucket = jnp.where(rel > 0, half, 0) + jnp.where(n < max_exact, n, large)
    for h in range(N_Q_HEADS):
        bias = jnp.zeros(n.shape, F32)
        for b in range(N_BUCKETS):
            bias = jnp.where(bucket == b, tab_ref[b, h], bias)
        o_ref[h] = jnp.where(n <= WINDOW, bias, NEG_INF)


def _bias_call(rel_table, sink, shift):
    rows = jax.ShapeDtypeStruct((N_Q_HEADS * BLOCK, LANES), F32)
    smem = pl.BlockSpec(memory_space=pltpu.SMEM)
    return pl.pallas_call(
        _bias_kernel,
        out_shape=(jax.ShapeDtypeStruct((N_Q_HEADS, BLOCK, 3 * BLOCK), F32), rows, rows, rows),
        in_specs=[smem, smem, smem],
        name="t5_bias_tile",
    )(rel_table, sink, shift)


def _mixer_ab_kernel(tq, tpb, fixed_shift, x_ref, q_ref, kc_ref, kp_ref, kn_ref, vc_ref, vp_ref, vn_ref,
                     uc_ref, up_ref, un_ref, bias_ref, sink_ref, shift_ref, sink_term_ref, cw_ref, cb_ref,
                     wo_ref, gate_ref, o_ref, kbuf, vbuf, qs, att):
    i = pl.program_id(0)
    first = (i % tpb) == 0
    last = (i % tpb) == tpb - 1
    nblk = tq // BLOCK
    grows = GQA_GROUP * BLOCK
    kbuf[:, 0:BLOCK] = kp_ref[0]
    kbuf[:, BLOCK:BLOCK + tq] = kc_ref[0]
    kbuf[:, BLOCK + tq:] = kn_ref[0]
    vbuf[0:BLOCK] = vp_ref[...]
    vbuf[BLOCK:BLOCK + tq] = vc_ref[...]
    vbuf[BLOCK + tq:] = vn_ref[...]
    for n in range(nblk):
        for h in range(N_Q_HEADS):
            g, j = divmod(h, GQA_GROUP)
            qs[g, n, j * BLOCK:(j + 1) * BLOCK, :] = q_ref[n * BLOCK:(n + 1) * BLOCK, h * HEAD_DIM:(h + 1) * HEAD_DIM]
    col = lax.broadcasted_iota(jnp.int32, (grows, 3 * BLOCK), 1)
    for n in range(nblk):
        keys = slice(n * BLOCK, (n + 3) * BLOCK)
        for g in range(N_KV_HEADS):
            s = jnp.dot(qs[g, n], kbuf[g * HEAD_DIM:(g + 1) * HEAD_DIM, keys], preferred_element_type=F32)
            s = s + bias_ref[g * GQA_GROUP:(g + 1) * GQA_GROUP].reshape(grows, 3 * BLOCK)
            if n == 0:
                s = jnp.where(jnp.logical_and(first, col < BLOCK), NEG_INF, s)
            if n == nblk - 1:
                s = jnp.where(jnp.logical_and(last, col >= 2 * BLOCK), NEG_INF, s)
            grp = slice(g * grows, (g + 1) * grows)
            if fixed_shift:
                mx = shift_ref[grp, :]
                sink_term = sink_term_ref[grp, :]
            else:
                sk = sink_ref[grp, :]
                mx = jnp.maximum(jnp.broadcast_to(jnp.max(s, axis=-1, keepdims=True), sk.shape), sk)
                sink_term = jnp.exp(sk - mx)
            p = jnp.exp(s - jnp.tile(mx, (1, 3)))
            pv = jnp.dot(p.astype(BF16), vbuf[keys, g * LANES:(g + 1) * LANES], preferred_element_type=F32)
            o = pv / (pltpu.roll(pv, HEAD_DIM, 1) + sink_term)
            for j in range(GQA_GROUP):
                h = g * GQA_GROUP + j
                att[n * BLOCK:(n + 1) * BLOCK, h * HEAD_DIM:(h + 1) * HEAD_DIM] = o[j * BLOCK:(j + 1) * BLOCK, 0:HEAD_DIM]

    gb = uc_ref[:, 0:CONV_WIDTH]
    p = uc_ref[:, CONV_WIDTH:2 * CONV_WIDTH] * uc_ref[:, 2 * CONV_WIDTH:]
    p_before = jnp.where(first, 0.0, up_ref[7:8, CONV_WIDTH:2 * CONV_WIDTH] * up_ref[7:8, 2 * CONV_WIDTH:])
    p_after = jnp.where(last, 0.0, un_ref[0:1, CONV_WIDTH:2 * CONV_WIDTH] * un_ref[0:1, 2 * CONV_WIDTH:])
    conv = gb * _short_conv_rows(p, p_before, p_after, cw_ref[...], cb_ref[...])

    y = jnp.dot(att[...].astype(BF16), wo_ref[0:ATTN_WIDTH, :], preferred_element_type=F32)
    y = y + jnp.dot(conv.astype(BF16), wo_ref[ATTN_WIDTH:, :], preferred_element_type=F32)
    o_ref[...] = x_ref[...] + gate_ref[0] * y


def _mixer_ab_call(x, q, kt, v1, u, bias, row_tables, fixed_shift, conv_w, conv_b, w_out, layer, gate,
                   rows_per_batch):
    m, d = x.shape
    tq = min(rows_per_batch, 1024)
    tpb = rows_per_batch // tq
    r = tq // BLOCK
    nb = m // BLOCK
    n8 = m // 8
    cur = lambda w: pl.BlockSpec((tq, w), lambda i: (i, 0))
    prev_of = lambda i: jnp.maximum(i * r - 1, 0)
    next_of = lambda i: jnp.minimum((i + 1) * r, nb - 1)
    v_prev = pl.BlockSpec((BLOCK, 2 * KV_WIDTH), lambda i: (prev_of(i), 0))
    v_next = pl.BlockSpec((BLOCK, 2 * KV_WIDTH), lambda i: (next_of(i), 0))
    tk = kt.shape[2]
    k_cur = pl.BlockSpec((1, KV_WIDTH, tq), lambda i: (i // (tk // tq), 0, i % (tk // tq)))
    kpb = tk // BLOCK
    k_prev = pl.BlockSpec((1, KV_WIDTH, BLOCK), lambda i: (prev_of(i) // kpb, 0, prev_of(i) % kpb))
    k_next = pl.BlockSpec((1, KV_WIDTH, BLOCK), lambda i: (next_of(i) // kpb, 0, next_of(i) % kpb))
    uw = 3 * CONV_WIDTH
    row_table = _resident((N_Q_HEADS * BLOCK, LANES))
    return pl.pallas_call(
        functools.partial(_mixer_ab_kernel, tq, tpb, fixed_shift),
        out_shape=jax.ShapeDtypeStruct((m, d), F32),
        grid=(m // tq,),
        in_specs=[
            cur(d), cur(ATTN_WIDTH),
            k_cur, k_prev, k_next,
            cur(2 * KV_WIDTH), v_prev, v_next,
            cur(uw),
            pl.BlockSpec((8, uw), lambda i: (jnp.maximum(i * (tq // 8) - 1, 0), 0)),
            pl.BlockSpec((8, uw), lambda i: (jnp.minimum((i + 1) * (tq // 8), n8 - 1), 0)),
            _resident((N_Q_HEADS, BLOCK, 3 * BLOCK)),
            row_table, row_table, row_table,
            _resident((3, CONV_WIDTH)), _resident((1, CONV_WIDTH)),
            _layer_slice((d, d), layer),
            pl.BlockSpec((1, 1, d), lambda i: (i // tpb, 0, 0)),
        ],
        out_specs=cur(d),
        scratch_shapes=[
            pltpu.VMEM((KV_WIDTH, tq + 2 * BLOCK), BF16),
            pltpu.VMEM((tq + 2 * BLOCK, 2 * KV_WIDTH), BF16),
            pltpu.VMEM((N_KV_HEADS, r, GQA_GROUP * BLOCK, HEAD_DIM), BF16),
            pltpu.VMEM((tq, ATTN_WIDTH), F32),
        ],
        compiler_params=_cparams(("arbitrary",)),
        name="windowed_attn_shortconv_outproj",
    )(x, q, kt, kt, kt, v1, v1, v1, u, u, u, bias, *row_tables, conv_w, conv_b, w_out, gate)


def _key_chunk(seq):
    return min(seq, 1024)


def _dense_attn_kernel(tq, tk, q_ref, k_ref, v_ref, o_ref, qs, s_buf, p_buf, rmax_buf, alpha_buf,
                       m_ref, acc_ref):
    nc = k_ref.shape[0]
    for j in range(GQA_GROUP):
        qs[j * tq:(j + 1) * tq, :] = q_ref[:, j * HEAD_DIM:(j + 1) * HEAD_DIM]
    m_ref[...] = jnp.full(m_ref.shape, -jnp.inf, F32)
    acc_ref[...] = jnp.zeros(acc_ref.shape, F32)

    def scores(c):
        s = jnp.dot(qs[...], k_ref[c], preferred_element_type=F32)
        s_buf[...] = s
        rmax_buf[...] = jnp.broadcast_to(jnp.max(s, axis=-1, keepdims=True), rmax_buf.shape)

    def softmax():
        m_old = m_ref[...]
        m_new = jnp.maximum(m_old, rmax_buf[...])
        alpha_buf[...] = jnp.exp2(m_old - m_new)
        m_ref[...] = m_new
        p = jnp.exp2(s_buf[...] - jnp.tile(m_new, (1, tk // LANES)))
        p_buf[...] = p.astype(BF16)

    def weighted_values(c):
        start = c * tk if isinstance(c, int) else pl.multiple_of(c * tk, tk)
        pv = jnp.dot(p_buf[...], v_ref[pl.ds(start, tk), :], preferred_element_type=F32)
        acc_ref[...] = alpha_buf[...] * acc_ref[...] + pv

    def step(t):
        static = isinstance(t, int)
        weighted_values(t)
        if not static or t + 1 < nc:
            softmax()
        if not static or t + 2 < nc:
            scores(t + 2)

    scores(0)
    softmax()
    if nc > 1:
        scores(1)

    def steady(t, carry):
        step(t)
        return carry

    lax.fori_loop(0, max(nc - 2, 0), steady, 0)
    for t in range(max(nc - 2, 0), nc):
        step(t)

    acc = acc_ref[...]
    o = acc / pltpu.roll(acc, HEAD_DIM, 1)
    for j in range(GQA_GROUP):
        o_ref[:, j * HEAD_DIM:(j + 1) * HEAD_DIM] = o[j * tq:(j + 1) * tq, 0:HEAD_DIM].astype(o_ref.dtype)


def _dense_attn_bounded_kernel(tq, tk, bound_ref, q_ref, k_ref, v_ref, o_ref, qs, p_buf, acc_ref):
    nc = k_ref.shape[0]
    for j in range(GQA_GROUP):
        qs[j * tq:(j + 1) * tq, :] = q_ref[:, j * HEAD_DIM:(j + 1) * HEAD_DIM]
    shift = bound_ref[0]

    def probabilities(c):
        s = jnp.dot(qs[...], k_ref[c], preferred_element_type=F32)
        p_buf[...] = jnp.exp2(s - shift).astype(BF16)

    def weighted_values(c):
        start = c * tk if isinstance(c, int) else pl.multiple_of(c * tk, tk)
        return jnp.dot(p_buf[...], v_ref[pl.ds(start, tk), :], preferred_element_type=F32)

    probabilities(0)
    acc_ref[...] = weighted_values(0)
    if nc > 1:
        probabilities(1)

    def steady(t, carry):
        acc_ref[...] += weighted_values(t)
        probabilities(t + 1)
        return carry

    lax.fori_loop(1, nc - 1, steady, 0)
    if nc > 1:
        acc_ref[...] += weighted_values(nc - 1)

    acc = acc_ref[...]
    o = acc / pltpu.roll(acc, HEAD_DIM, 1)
    for j in range(GQA_GROUP):
        o_ref[:, j * HEAD_DIM:(j + 1) * HEAD_DIM] = o[j * tq:(j + 1) * tq, 0:HEAD_DIM].astype(o_ref.dtype)


MAX_FIXED_SHIFT = 50.0


def _dense_attn_call(q, kt, v1, logit_bound, bsz, seq):
    tq, tk = min(seq, 1024), _key_chunk(seq)
    nq = seq // tq
    nc = seq // tk
    rows = GQA_GROUP * tq
    gw = GQA_GROUP * HEAD_DIM
    stat = pltpu.VMEM((rows, LANES), F32)
    q_spec = pl.BlockSpec((tq, gw), lambda b, g, i: (b * nq + i, g))
    k_spec = pl.BlockSpec((nc, HEAD_DIM, tk), lambda b, g, i: (b, g, 0), pipeline_mode=pl.Buffered(1))
    v_spec = pl.BlockSpec((seq, LANES), lambda b, g, i: (b, g), pipeline_mode=pl.Buffered(1))
    common = dict(
        out_shape=jax.ShapeDtypeStruct((bsz * seq, ATTN_WIDTH), BF16),
        grid=(bsz, N_KV_HEADS, nq),
        out_specs=q_spec,
        compiler_params=_cparams(("arbitrary", "arbitrary", "arbitrary")),
    )

    def running_max():
        return pl.pallas_call(
            functools.partial(_dense_attn_kernel, tq, tk),
            in_specs=[q_spec, k_spec, v_spec],
            scratch_shapes=[
                pltpu.VMEM((rows, HEAD_DIM), BF16),
                pltpu.VMEM((rows, tk), F32), pltpu.VMEM((rows, tk), BF16),
                stat, stat, stat, stat,
            ],
            name="dense_gqa_attention", **common,
        )(q, kt, v1)

    def fixed_shift():
        return pl.pallas_call(
            functools.partial(_dense_attn_bounded_kernel, tq, tk),
            in_specs=[pl.BlockSpec(memory_space=pltpu.SMEM), q_spec, k_spec, v_spec],
            scratch_shapes=[pltpu.VMEM((rows, HEAD_DIM), BF16), pltpu.VMEM((rows, tk), BF16), stat],
            name="dense_gqa_attention_fixed_shift", **common,
        )(logit_bound.reshape(1), q, kt, v1)

    return lax.cond(logit_bound <= MAX_FIXED_SHIFT, fixed_shift, running_max)


def _hp_dot(a, b):
    return jnp.dot(a, b, preferred_element_type=F32, precision=lax.Precision.HIGHEST)


def _filter_trunk_kernel(z_ref, w1_ref, b1_ref, w2_ref, b2_ref, w3_ref, b3_ref, fq_ref, o_ref):
    fq = fq_ref[...]
    h = jnp.sin(fq * (_hp_dot(w1_ref[...], z_ref[...]) + b1_ref[...]))
    h = jnp.sin(fq * (_hp_dot(w2_ref[...], h) + b2_ref[...]))
    h = jnp.sin(fq * (_hp_dot(w3_ref[...], h) + b3_ref[...]))
    o_ref[...] = h.T


def _filter_trunk_call(zfeat_t, w1, b1, w2, b2, w3, b3, freq):
    rows = zfeat_t.shape[1]
    tr = min(rows, 2048)
    fw = HYENA_FILTER_WIDTH
    col = lambda v: v.reshape(fw, 1)
    return pl.pallas_call(
        _filter_trunk_kernel,
        out_shape=jax.ShapeDtypeStruct((rows, fw), F32),
        grid=(rows // tr,),
        in_specs=[
            pl.BlockSpec((HYENA_EMB_PAD, tr), lambda i: (0, i)),
            _resident((fw, HYENA_EMB_PAD)), _resident((fw, 1)),
            _resident((fw, fw)), _resident((fw, 1)),
            _resident((fw, fw)), _resident((fw, 1)),
            _resident((fw, 1)),
        ],
        out_specs=pl.BlockSpec((tr, fw), lambda i: (i, 0)),
        compiler_params=_cparams(("arbitrary",)),
        name="hyena_filter_trunk",
    )(zfeat_t, w1.T, col(b1), w2.T, col(b2), w3.T, col(b3), col(freq))


def _dft_stage1(src_ref, tab_ref, re_ref, im_ref, n_rows, k1p, n2_major=False, pitch=DFT_N2, src_pitch=DFT_N2):
    def body(n2, carry):
        xs = src_ref[pl.ds(n2, n_rows, stride=src_pitch), :].astype(BF16)
        res = jnp.dot(tab_ref[n2], xs, preferred_element_type=F32)
        dst = pl.ds(pl.multiple_of(n2 * k1p, 8), k1p) if n2_major else pl.ds(n2, k1p, stride=pitch)
        re_ref[dst, :] = res[0:k1p]
        im_ref[dst, :] = res[k1p:]
        return carry

    lax.fori_loop(0, DFT_N2, body, 0, unroll=8)


def _filter_spec_kernel(seq, k1p, h_ref, w4f_ref, w4b_ref, dl_ref, e1_ref, f2_ref,
                        kr_ref, ki_ref, kfull, bre, bim):
    n = 2 * seq
    chunk = min(seq, 1024)
    dl = dl_ref[...]
    live = seq // DFT_N2 + 1

    def fill_half(w_ref, backward):
        def fill(c, ss):
            r0 = pl.multiple_of(c * chunk, chunk)
            val = jnp.dot(h_ref[pl.ds(r0, chunk), :].astype(BF16), w_ref[...], preferred_element_type=F32)
            rows = r0 + lax.broadcasted_iota(jnp.int32, (chunk, 1), 0)
            lag = ((n - rows) if backward else rows).astype(F32)
            val = val * jnp.exp(-(lag * (1.0 / (seq - 1))) * dl)
            if backward:
                val = jnp.where(rows == seq, 0.0, val)
            for k in range(chunk // DFT_N2):
                dst = pl.multiple_of((c * (chunk // DFT_N2) + k) * SPECTRUM_PITCH, 8)
                kfull[pl.ds(dst, DFT_N2), :] = val[k * DFT_N2:(k + 1) * DFT_N2]
            return ss + jnp.sum(val * val, axis=0, keepdims=True)
        return fill

    ss = lax.fori_loop(0, seq // chunk, fill_half(w4f_ref, False), jnp.zeros((1, dl.shape[1]), F32))
    ss = lax.fori_loop(seq // chunk, n // chunk, fill_half(w4b_ref, True), ss)
    norm = lax.rsqrt(ss + EPS)

    _dft_stage1(kfull, e1_ref, bre, bim, n // DFT_N2, k1p, n2_major=True, src_pitch=SPECTRUM_PITCH)

    kr_ref[live * DFT_N2:, :] = jnp.zeros(((k1p - live) * DFT_N2, kr_ref.shape[1]), F32)
    ki_ref[live * DFT_N2:, :] = jnp.zeros(((k1p - live) * DFT_N2, ki_ref.shape[1]), F32)
    f2 = f2_ref[...]

    def stage2(a, carry):
        src = pl.ds(a, DFT_N2, stride=k1p)
        slab = jnp.concatenate([bre[src, :], bim[src, :]], axis=0).astype(BF16)
        xf = jnp.dot(f2, slab, preferred_element_type=F32)
        rows = pl.ds(pl.multiple_of(a * DFT_N2, DFT_N2), DFT_N2)
        kr_ref[rows, :] = xf[0:DFT_N2] * norm
        ki_ref[rows, :] = xf[DFT_N2:] * norm
        return carry

    lax.fori_loop(0, live, stage2, 0, unroll=8)


def _filter_spec_call(h3, w4, deltas, e1f, f2f, seq):
    n = 2 * seq
    k1p = e1f.shape[1] // 2
    ct = LANES
    nct = CONV_WIDTH // ct
    fw = HYENA_FILTER_WIDTH
    spec_rows = k1p * DFT_N2
    out = pl.BlockSpec((spec_rows, ct), lambda j: (0, j))
    return pl.pallas_call(
        functools.partial(_filter_spec_kernel, seq, k1p),
        out_shape=(jax.ShapeDtypeStruct((spec_rows, CONV_WIDTH), F32),
                   jax.ShapeDtypeStruct((spec_rows, CONV_WIDTH), F32)),
        grid=(nct,),
        in_specs=[
            _resident((n, fw)),
            pl.BlockSpec((fw, ct), lambda j: (0, j)),
            pl.BlockSpec((fw, ct), lambda j: (0, nct + j)),
            pl.BlockSpec((1, ct), lambda j: (0, j)),
            _resident(e1f.shape),
            _resident((2 * DFT_N2, 2 * DFT_N2)),
        ],
        out_specs=(out, out),
        scratch_shapes=[pltpu.VMEM((n // DFT_N2 * SPECTRUM_PITCH, ct), F32), pltpu.VMEM((spec_rows, ct), F32),
                        pltpu.VMEM((spec_rows, ct), F32)],
        compiler_params=_cparams(("arbitrary",)),
        name="hyena_filter_spectrum",
    )(h3, w4, w4, deltas, e1f, f2f)


def _long_conv_kernel(seq, k1p, z_ref, x0_ref, kr_ref, ki_ref, skip_ref, e1_ref, f2f_ref, f2i_ref,
                      einv_ref, o_ref, br, bi, cbuf):
    n1h = seq // DFT_N2
    pitch = br.shape[0] // k1p
    _dft_stage1(z_ref, e1_ref, br, bi, n1h, k1p, pitch=pitch)

    f2f = f2f_ref[...]
    f2i = f2i_ref[...]

    ct = br.shape[1]

    def spectral(a0, count):
        def start(x, m):
            return x if isinstance(x, int) else pl.multiple_of(x, m)
        rows = [pl.ds(start((a0 + t) * pitch, 8), DFT_N2) for t in range(count)]
        krows = [pl.ds(start((a0 + t) * DFT_N2, DFT_N2), DFT_N2) for t in range(count)]
        lanes = lambda parts: jnp.concatenate(parts, axis=1)
        slab = lanes([jnp.concatenate([br[r, :], bi[r, :]], axis=0) for r in rows]).astype(BF16)
        xf = jnp.dot(f2f, slab, preferred_element_type=F32)
        xr, xi = xf[0:DFT_N2], xf[DFT_N2:]
        kr, ki = lanes([kr_ref[r, :] for r in krows]), lanes([ki_ref[r, :] for r in krows])
        y = jnp.concatenate([xr * kr - xi * ki, xr * ki + xi * kr], axis=0).astype(BF16)
        d = jnp.dot(f2i, y, preferred_element_type=F32)
        for t, r in enumerate(rows):
            br[r, :] = d[0:DFT_N2, t * ct:(t + 1) * ct]
            bi[r, :] = d[DFT_N2:, t * ct:(t + 1) * ct]

    live = n1h + 1
    pair = MXU_DIM // ct

    def spectral_pairs(p, carry):
        spectral(p * pair, pair)
        return carry

    lax.fori_loop(0, live // pair, spectral_pairs, 0, unroll=4)
    for a in range(live // pair * pair, live):
        spectral(a, 1)

    skip = skip_ref[...]

    def synth(n2, carry):
        ds_spec = pl.ds(n2, k1p, stride=pitch)
        d = jnp.concatenate([br[ds_spec, :], bi[ds_spec, :]], axis=0).astype(BF16)
        cbuf[pl.ds(n2, n1h, stride=pitch), :] = jnp.dot(einv_ref[n2], d, preferred_element_type=F32)
        return carry

    lax.fori_loop(0, DFT_N2, synth, 0, unroll=8)

    def gate(n1, carry):
        rows = pl.ds(pl.multiple_of(n1 * DFT_N2, DFT_N2), DFT_N2)
        conv = cbuf[pl.ds(pl.multiple_of(n1 * pitch, 8), DFT_N2), :]
        o_ref[rows, :] = x0_ref[rows, :] * (conv + skip * z_ref[rows, :])
        return carry

    lax.fori_loop(0, n1h, gate, 0, unroll=4)


def _long_conv_call(z, x0, kr, ki, skip, e1, f2f, f2i, einv, bsz, seq):
    ct = LANES
    nct = CONV_WIDTH // ct
    k1p = e1.shape[1] // 2
    spec_rows = k1p * DFT_N2
    big = lambda: pl.BlockSpec((seq, ct), lambda j, b: (b, j), pipeline_mode=pl.Buffered(1))
    spec = lambda: pl.BlockSpec((spec_rows, ct), lambda j, b: (0, j), pipeline_mode=pl.Buffered(1))
    return pl.pallas_call(
        functools.partial(_long_conv_kernel, seq, k1p),
        out_shape=jax.ShapeDtypeStruct((bsz * seq, CONV_WIDTH), F32),
        grid=(nct, bsz),
        in_specs=[
            pl.BlockSpec((seq, ct), lambda j, b: (b, j)), big(), spec(), spec(),
            pl.BlockSpec((1, ct), lambda j, b: (0, j)),
            _resident(e1.shape),
            _resident((2 * DFT_N2, 2 * DFT_N2)), _resident((2 * DFT_N2, 2 * DFT_N2)),
            _resident(einv.shape),
        ],
        out_specs=pl.BlockSpec((seq, ct), lambda j, b: (b, j)),
        scratch_shapes=[pltpu.VMEM((k1p * SPECTRUM_PITCH, ct), F32), pltpu.VMEM((k1p * SPECTRUM_PITCH, ct), F32),
                        pltpu.VMEM((seq // DFT_N2 * SPECTRUM_PITCH, ct), F32)],
        compiler_params=_cparams(("arbitrary", "arbitrary")),
        name="hyena_long_conv",
    )(z, x0, kr, ki, skip, e1, f2f, f2i, einv)


def _rope_tables(seq):
    t = jnp.arange(seq)
    row = (t // GRID_W).astype(F32)
    col = (t % GRID_W).astype(F32)
    half = HEAD_DIM // 2
    inv = ROPE_THETA ** (-jnp.arange(0, half, 2, dtype=F32) / half)
    ang = jnp.concatenate([row[:, None] * inv, col[:, None] * inv], axis=-1)
    ang = jnp.repeat(ang, 2, axis=-1)
    sign = jnp.where(jnp.arange(HEAD_DIM) % 2 == 0, -1.0, 1.0).astype(F32)
    reps = LANES // HEAD_DIM
    return jnp.tile(jnp.cos(ang), (1, reps)), jnp.tile(jnp.sin(ang) * sign, (1, reps))


def _dft_tables(seq):
    n = 2 * seq
    n1 = n // DFT_N2
    k1 = n1 // 2 + 1
    k1p = -(-k1 // 8) * 8
    a = jnp.arange(k1p)
    live = (a < k1).astype(F32)
    ang1 = ((a[:, None] * jnp.arange(n1)[None, :]) % n1).astype(F32) * (2.0 * math.pi / n1)
    ang2 = (jnp.arange(DFT_N2)[:, None] * a[None, :]).astype(F32) * (2.0 * math.pi / n)
    c1, s1 = jnp.cos(ang1) * live[:, None], jnp.sin(ang1) * live[:, None]
    c2, s2 = jnp.cos(ang2), jnp.sin(ang2)
    cos_a = c1[None] * c2[:, :, None] - s1[None] * s2[:, :, None]
    sin_a = s1[None] * c2[:, :, None] + c1[None] * s2[:, :, None]
    e1_full = jnp.concatenate([cos_a, -sin_a], axis=1).astype(BF16)
    e1_half = e1_full[:, :, :n1 // 2]
    wgt = jnp.where((a == 0) | (a == n1 // 2), 1.0, 2.0) / n
    c1t, s1t = (c1 * wgt[:, None]).T[:n1 // 2], (s1 * wgt[:, None]).T[:n1 // 2]
    cos_s = c1t[None] * c2[:, None, :] - s1t[None] * s2[:, None, :]
    sin_s = s1t[None] * c2[:, None, :] + c1t[None] * s2[:, None, :]
    einv = jnp.concatenate([cos_s, -sin_s], axis=2).astype(BF16)
    kk = jnp.arange(DFT_N2)
    phi = ((kk[:, None] * kk[None, :]) % DFT_N2).astype(F32) * (2.0 * math.pi / DFT_N2)
    ci, si = jnp.cos(phi), jnp.sin(phi)
    f2f = jnp.block([[ci, si], [-si, ci]]).astype(BF16)
    f2i = jnp.block([[ci, -si], [si, ci]]).astype(BF16)
    return e1_full, e1_half, einv, f2f, f2i


def _filter_features_t(seq):
    j = jnp.arange(2 * seq)
    lag = jnp.where(j < seq, j, jnp.where(j == seq, 0, 2 * seq - j)).astype(F32)[None, :]
    t = lag / (seq - 1)
    bands = (HYENA_EMB - 1) // 2
    w = (2.0 * math.pi / seq) * lag
    fr = jnp.linspace(1e-4, bands - 1, bands, dtype=F32)[:, None]
    pad = jnp.zeros((HYENA_EMB_PAD - HYENA_EMB, 2 * seq), F32)
    return jnp.concatenate([t, jnp.cos(fr * w), -jnp.sin(fr * w), pad], axis=0)


def kernel(x, c, rel_table, norm_g, w_mod, b_mod, w_in, w_out, ffn_w1, ffn_w3, ffn_w2, a_qk_g, a_sink,
           b_conv_w, b_conv_b, c_qk_g, d_conv_w, d_conv_b, d_f_w1, d_f_b1, d_f_w2, d_f_b2, d_f_w3,
           d_f_b3, d_f_w4, d_f_freq, d_skip):
    bsz, seq, d = x.shape
    depth = w_mod.shape[0]
    m_rows = bsz * seq
    mod =_mod_call(c, w_mod, b_mod).reshape(depth, bsz, 3, 3, 1, d)
    e_blk = jnp.kron(jnp.eye(MXU_DIM // HEAD_DIM, dtype=F32), jnp.ones((HEAD_DIM, HEAD_DIM), F32)).astype(BF16)

    w1b, w3b = _to_bf16_from_transposed(ffn_w1), _to_bf16_from_transposed(ffn_w3)
    w2b, w_in_b, w_out_b = (w.astype(BF16) for w in (ffn_w2, w_in, w_out))

    def ffn(xf, l, which, sub, mixer=None):
        return _ffn_call(xf, norm_g[l, sub][None], mod[l, :, sub, 0], mod[l, :, sub, 1], mod[l, :, sub, 2],
                         w1b, w3b, w2b, l, which, seq, mixer)

    xf = x.reshape(m_rows, d)
    for l in range(depth):
        j = l // 2
        xf = ffn(xf, l, 0, 0)
        even = l % 2 == 0
        qk_g = a_qk_g[j] if even else c_qk_g[j]
        hyena = None if even else _rope_tables(seq) + (d_conv_w[j], d_conv_b[j][None])
        q, k, v, *conv_in = _inproj_call(
            xf, norm_g[l, 1][None], mod[l, :, 1, 0], mod[l, :, 1, 1], w_in_b, l,
            jnp.tile(qk_g[0], N_Q_HEADS)[None], jnp.tile(qk_g[1], N_KV_HEADS)[None], e_blk, seq, hyena)
        gate = mod[l, :, 1, 2]
        if even:
            (u,) = conv_in
            qk_bound = (1.01 * HEAD_DIM ** 0.5) * jnp.max(jnp.abs(qk_g[0])) * jnp.max(jnp.abs(qk_g[1]))
            shift = jnp.maximum(qk_bound + jnp.max(rel_table, axis=0), a_sink[j])
            worst_gap = jnp.max(shift + qk_bound - rel_table[0])
            bias, *row_tables = _bias_call(rel_table, a_sink[j], shift)
            mixer_ab = lambda fixed: _mixer_ab_call(xf, q, k, v, u, bias, row_tables, fixed, b_conv_w[j],
                                                    b_conv_b[j][None], w_out_b, l, gate, seq)
            xf = lax.cond(worst_gap <= MAX_FIXED_SHIFT_GAP, lambda: mixer_ab(True), lambda: mixer_ab(False))
            pending = None
        else:
            x0, z = conv_in
            logit_bound = (1.01 * LOG2_E * HEAD_DIM ** 0.5) * jnp.max(jnp.abs(qk_g[0])) * jnp.max(jnp.abs(qk_g[1]))
            att = _dense_attn_call(q, k, v, logit_bound, bsz, seq)
            e1_full, e1_half, einv, f2f, f2i = _dft_tables(seq)
            fw1 = jnp.pad(d_f_w1[j], ((0, HYENA_EMB_PAD - HYENA_EMB), (0, 0)))
            h3 = _filter_trunk_call(_filter_features_t(seq), fw1, d_f_b1[j], d_f_w2[j], d_f_b2[j],
                                    d_f_w3[j], d_f_b3[j], d_f_freq[j])
            deltas = jnp.abs(jnp.linspace(HYENA_MIN_DECAY, HYENA_MAX_DECAY, CONV_WIDTH, dtype=F32))[None]
            kr, ki = _filter_spec_call(h3, d_f_w4[j].astype(BF16), deltas, e1_full, f2f, seq)
            y = _long_conv_call(z, x0, kr, ki, d_skip[j][None], e1_half, f2f, f2i, einv, bsz, seq)
            pending = (att, y, w_out_b, gate)
        xf = ffn(xf, l, 1, 2, pending)
    return xf.reshape(bsz, seq, d)
```
